```python
import math
import jax, jax.numpy as jnp
from jax import lax
import numpy as np

D_MODEL = 2048
BATCH = 8
SEQ = 8192
DEPTH = 1

N_META = 16
ATTN_HEADS = 8
ATTN_HEAD_DIM = 128
D_ATTN = ATTN_HEADS * ATTN_HEAD_DIM
D_SSM = D_MODEL // 2
SSM_GROUP = 16
SSM_GROUPS = D_SSM // SSM_GROUP
SSM_STATE = 64
D_FF = 5632
CONV_WIDTH = 3
Q_BLOCK = 128
EPS = 1e-6
IN_SPLITS = [D_ATTN, D_ATTN, D_ATTN, ATTN_HEADS, D_SSM, D_MODEL, D_MODEL]
N_IN = sum(IN_SPLITS)
IN_OFFSETS = [int(o) for o in np.cumsum(IN_SPLITS)[:-1]]

kernel_name = "hybrid_s5_forgetting_attn_convffn"


def rmsnorm(x, g):
    xf = x.astype(jnp.float32)
    y = xf * lax.rsqrt(jnp.mean(xf * xf, axis=-1, keepdims=True) + EPS)
    return (y * g.astype(jnp.float32)).astype(x.dtype)


def _fox_block(qb, Fq, qpos, k, v, Fk, kpos):
    s = jnp.einsum('bqhd,bkhd->bhqk', qb, k, preferred_element_type=jnp.float32) * (ATTN_HEAD_DIM ** -0.5)
    s = s + jnp.transpose(Fq, (0, 2, 1))[..., None] - jnp.transpose(Fk, (0, 2, 1))[:, :, None, :]
    mask = kpos[None, :] <= qpos[:, None]
    s = jnp.where(mask[None, None], s, -jnp.inf)
    p = jax.nn.softmax(s, axis=-1)
    return jnp.einsum('bhqk,bkhd->bqhd', p.astype(v.dtype), v)


def forgetting_attention(q, k, v, log_f):
    b, L, H, hd = q.shape
    F = jnp.cumsum(log_f, axis=1)
    pos = jnp.arange(L)
    out_meta = _fox_block(q[:, :N_META], F[:, :N_META], pos[:N_META],
                          k[:, :N_META], v[:, :N_META], F[:, :N_META], pos[:N_META])
    n_blk = (L - N_META) // Q_BLOCK
    qr = q[:, N_META:].reshape(b, n_blk, Q_BLOCK, H, hd).transpose(1, 0, 2, 3, 4)
    Fr = F[:, N_META:].reshape(b, n_blk, Q_BLOCK, H).transpose(1, 0, 2, 3)
    qpos = (N_META + jnp.arange(L - N_META)).reshape(n_blk, Q_BLOCK)
    out_r = lax.map(lambda a: _fox_block(a[0], a[1], a[2], k, v, F, pos), (qr, Fr, qpos))
    out_r = out_r.transpose(1, 0, 2, 3, 4).reshape(b, L - N_META, H, hd)
    return jnp.concatenate([out_meta, out_r], axis=1)


def s5_ssm(u, lam_re, lam_im, log_dt, b_re, b_im, c_re, c_im, d_skip):
    bsz, L, _ = u.shape
    f32 = jnp.float32
    uf = u.astype(f32).reshape(bsz, L, SSM_GROUPS, SSM_GROUP)
    dt = jnp.exp(log_dt.astype(f32))[:, None]
    lr = lam_re.astype(f32)
    li = lam_im.astype(f32)
    mag = jnp.exp(lr * dt)
    a_re = mag * jnp.cos(li * dt)
    a_im = mag * jnp.sin(li * dt)
    den = lr * lr + li * li
    nr = a_re - 1.0
    z_re = (nr * lr + a_im * li) / den
    z_im = (a_im * lr - nr * li) / den
    br = b_re.astype(f32)
    bi = b_im.astype(f32)
    bb_re = z_re[..., None] * br - z_im[..., None] * bi
    bb_im = z_re[..., None] * bi + z_im[..., None] * br
    bu_re = jnp.einsum('gpc,blgc->blgp', bb_re, uf)
    bu_im = jnp.einsum('gpc,blgc->blgp', bb_im, uf)
    at_re = jnp.broadcast_to(a_re, (1, L, SSM_GROUPS, SSM_STATE))
    at_im = jnp.broadcast_to(a_im, (1, L, SSM_GROUPS, SSM_STATE))

    def combine(e1, e2):
        ar1, ai1, br1, bi1 = e1
        ar2, ai2, br2, bi2 = e2
        return (ar2 * ar1 - ai2 * ai1,
                ar2 * ai1 + ai2 * ar1,
                ar2 * br1 - ai2 * bi1 + br2,
                ar2 * bi1 + ai2 * br1 + bi2)

    _, _, h_re, h_im = lax.associative_scan(combine, (at_re, at_im, bu_re, bu_im), axis=1)
    y = (jnp.einsum('gcp,blgp->blgc', c_re.astype(f32), h_re)
         - jnp.einsum('gcp,blgp->blgc', c_im.astype(f32), h_im))
    y = y.reshape(bsz, L, D_SSM) + d_skip.astype(f32) * u.astype(f32)
    return y.astype(u.dtype)


def conv_ffn(x, w_up, conv_w, conv_b, w_down):
    gu = x @ w_up
    g, u = jnp.split(gu, 2, axis=-1)
    L = g.shape[1]
    gp = jnp.pad(g, ((0, 0), (CONV_WIDTH - 1, 0), (0, 0)))
    gc = conv_b + conv_w[0] * gp[:, 0:L]
    for j in range(1, CONV_WIDTH):
        gc = gc + conv_w[j] * gp[:, j:j + L]
    return (jax.nn.silu(gc) * u) @ w_down


def mixer(n, w_in, b_f, lam_re, lam_im, log_dt, b_re, b_im, c_re, c_im, d_skip, w_glu, w_attn_o, w_out):
    bsz, L, _ = n.shape
    z = n @ w_in
    q, k, v, f, u, ga, gb = jnp.split(z, IN_OFFSETS, axis=-1)
    q = q.reshape(bsz, L, ATTN_HEADS, ATTN_HEAD_DIM)
    k = k.reshape(bsz, L, ATTN_HEADS, ATTN_HEAD_DIM)
    v = v.reshape(bsz, L, ATTN_HEADS, ATTN_HEAD_DIM)
    log_f = jax.nn.log_sigmoid(f.astype(jnp.float32) + b_f.astype(jnp.float32))
    attn = forgetting_attention(q, k, v, log_f).reshape(bsz, L, D_ATTN) @ w_attn_o
    y = s5_ssm(u, lam_re, lam_im, log_dt, b_re, b_im, c_re, c_im, d_skip)
    ya, yb = jnp.split(jax.nn.gelu(y) @ w_glu, 2, axis=-1)
    ssm_out = ya * jax.nn.sigmoid(yb)
    merged = jax.nn.sigmoid(ga) * ssm_out + jax.nn.sigmoid(gb) * attn
    return merged @ w_out


def _fwd_setup_inputs(seed: int = 0) -> dict:
    key = jax.random.key(seed)
    ks = jax.random.split(key, 24)
    nrm = lambda k, s, sc: jax.random.normal(k, s, jnp.float32) * sc
    Dp = DEPTH
    n_idx = jnp.arange(SSM_STATE, dtype=jnp.float32)
    return {
        "x": nrm(ks[0], (BATCH, SEQ, D_MODEL), 1.0),
        "meta": nrm(ks[1], (N_META, D_MODEL), 1.0),
        "g_mix": 1.0 + nrm(ks[2], (Dp, D_MODEL), 0.01),
        "w_in": nrm(ks[3], (Dp, D_MODEL, N_IN), D_MODEL ** -0.5),
        "b_f": jax.random.uniform(ks[4], (Dp, ATTN_HEADS), jnp.float32, 1.0, 6.0),
        "lam_re": -0.5 + nrm(ks[5], (Dp, SSM_GROUPS, SSM_STATE), 0.01),
        "lam_im": math.pi * n_idx + nrm(ks[6], (Dp, SSM_GROUPS, SSM_STATE), 0.01),
        "log_dt": jax.random.uniform(ks[7], (Dp, SSM_GROUPS), jnp.float32, math.log(1e-3), math.log(1e-1)),
        "b_re": nrm(ks[8], (Dp, SSM_GROUPS, SSM_STATE, SSM_GROUP), (2 * SSM_GROUP) ** -0.5),
        "b_im": nrm(ks[9], (Dp, SSM_GROUPS, SSM_STATE, SSM_GROUP), (2 * SSM_GROUP) ** -0.5),
        "c_re": nrm(ks[10], (Dp, SSM_GROUPS, SSM_GROUP, SSM_STATE), (2 * SSM_STATE) ** -0.5),
        "c_im": nrm(ks[11], (Dp, SSM_GROUPS, SSM_GROUP, SSM_STATE), (2 * SSM_STATE) ** -0.5),
        "d_skip": nrm(ks[12], (Dp, D_SSM), 1.0),
        "w_glu": nrm(ks[13], (Dp, D_SSM, 2 * D_MODEL), D_SSM ** -0.5),
        "w_attn_o": nrm(ks[14], (Dp, D_ATTN, D_MODEL), D_ATTN ** -0.5),
        "w_out": nrm(ks[15], (Dp, D_MODEL, D_MODEL), D_MODEL ** -0.5),
        "g_ffn": 1.0 + nrm(ks[16], (Dp, D_MODEL), 0.01),
        "w_up": nrm(ks[17], (Dp, D_MODEL, 2 * D_FF), D_MODEL ** -0.5),
        "conv_w": nrm(ks[18], (Dp, CONV_WIDTH, D_FF), CONV_WIDTH ** -0.5),
        "conv_b": nrm(ks[19], (Dp, D_FF), 0.01),
        "w_down": nrm(ks[20], (Dp, D_FF, D_MODEL), D_FF ** -0.5),
        "g_final": 1.0 + nrm(ks[21], (D_MODEL,), 0.01),
    }


def _fwd_reference(x, meta, g_mix, w_in, b_f, lam_re, lam_im, log_dt, b_re, b_im, c_re, c_im, d_skip,
              w_glu, w_attn_o, w_out, g_ffn, w_up, conv_w, conv_b, w_down, g_final):
    bsz = x.shape[0]
    m = jnp.broadcast_to(meta.astype(x.dtype)[None], (bsz, N_META, D_MODEL))
    h = jnp.concatenate([m, x], axis=1)
    for l in range(DEPTH):
        n = rmsnorm(h, g_mix[l])
        h = h + mixer(n, w_in[l], b_f[l], lam_re[l], lam_im[l], log_dt[l], b_re[l], b_im[l],
                      c_re[l], c_im[l], d_skip[l], w_glu[l], w_attn_o[l], w_out[l])
        n2 = rmsnorm(h, g_ffn[l])
        h = h + conv_ffn(n2, w_up[l], conv_w[l], conv_b[l], w_down[l])
    return rmsnorm(h, g_final)[:, N_META:]


import jax as _jax
import jax.numpy as _jnp

TWIN_FORMAT = 'train_step'
FWD_PARAMS = ['x', 'meta', 'g_mix', 'w_in', 'b_f', 'lam_re', 'lam_im', 'log_dt', 'b_re', 'b_im', 'c_re', 'c_im', 'd_skip', 'w_glu', 'w_attn_o', 'w_out', 'g_ffn', 'w_up', 'conv_w', 'conv_b', 'w_down', 'g_final']
TWIN_WEIGHTS = ['meta', 'g_mix', 'w_in', 'b_f', 'lam_re', 'lam_im', 'log_dt', 'b_re', 'b_im', 'c_re', 'c_im', 'd_skip', 'w_glu', 'w_attn_o', 'w_out', 'g_ffn', 'w_up', 'conv_w', 'conv_b', 'w_down', 'g_final']
TWIN_DIFF_INPUT = 'x'
TWIN_INPUTS = ['x', 'meta', 'g_mix', 'w_in', 'b_f', 'lam_re', 'lam_im', 'log_dt', 'b_re', 'b_im', 'c_re', 'c_im', 'd_skip', 'w_glu', 'w_attn_o', 'w_out', 'g_ffn', 'w_up', 'conv_w', 'conv_b', 'w_down', 'g_final', 'loss_target', 'm_meta', 'm_g_mix', 'm_w_in', 'm_b_f', 'm_lam_re', 'm_lam_im', 'm_log_dt', 'm_b_re', 'm_b_im', 'm_c_re', 'm_c_im', 'm_d_skip', 'm_w_glu', 'm_w_attn_o', 'm_w_out', 'm_g_ffn', 'm_w_up', 'm_conv_w', 'm_conv_b', 'm_w_down', 'm_g_final', 'v_meta', 'v_g_mix', 'v_w_in', 'v_b_f', 'v_lam_re', 'v_lam_im', 'v_log_dt', 'v_b_re', 'v_b_im', 'v_c_re', 'v_c_im', 'v_d_skip', 'v_w_glu', 'v_w_attn_o', 'v_w_out', 'v_g_ffn', 'v_w_up', 'v_conv_w', 'v_conv_b', 'v_w_down', 'v_g_final']
TWIN_OUTPUTS = ['loss', 'grad_x', 'grad_meta', 'grad_g_mix', 'grad_w_in', 'grad_b_f', 'grad_lam_re', 'grad_lam_im', 'grad_log_dt', 'grad_b_re', 'grad_b_im', 'grad_c_re', 'grad_c_im', 'grad_d_skip', 'grad_w_glu', 'grad_w_attn_o', 'grad_w_out', 'grad_g_ffn', 'grad_w_up', 'grad_conv_w', 'grad_conv_b', 'grad_w_down', 'grad_g_final', 'delta_meta', 'delta_g_mix', 'delta_w_in', 'delta_b_f', 'delta_lam_re', 'delta_lam_im', 'delta_log_dt', 'delta_b_re', 'delta_b_im', 'delta_c_re', 'delta_c_im', 'delta_d_skip', 'delta_w_glu', 'delta_w_attn_o', 'delta_w_out', 'delta_g_ffn', 'delta_w_up', 'delta_conv_w', 'delta_conv_b', 'delta_w_down', 'delta_g_final', 'new_m_meta', 'new_m_g_mix', 'new_m_w_in', 'new_m_b_f', 'new_m_lam_re', 'new_m_lam_im', 'new_m_log_dt', 'new_m_b_re', 'new_m_b_im', 'new_m_c_re', 'new_m_c_im', 'new_m_d_skip', 'new_m_w_glu', 'new_m_w_attn_o', 'new_m_w_out', 'new_m_g_ffn', 'new_m_w_up', 'new_m_conv_w', 'new_m_conv_b', 'new_m_w_down', 'new_m_g_final', 'new_v_meta', 'new_v_g_mix', 'new_v_w_in', 'new_v_b_f', 'new_v_lam_re', 'new_v_lam_im', 'new_v_log_dt', 'new_v_b_re', 'new_v_b_im', 'new_v_c_re', 'new_v_c_im', 'new_v_d_skip', 'new_v_w_glu', 'new_v_w_attn_o', 'new_v_w_out', 'new_v_g_ffn', 'new_v_w_up', 'new_v_conv_w', 'new_v_conv_b', 'new_v_w_down', 'new_v_g_final']
TWIN_LEAF_KINDS = {'loss': 'loss', 'grad_x': 'grad_x', 'grad_meta': 'grad_w', 'grad_g_mix': 'grad_w', 'grad_w_in': 'grad_w', 'grad_b_f': 'grad_w', 'grad_lam_re': 'grad_w', 'grad_lam_im': 'grad_w', 'grad_log_dt': 'grad_w', 'grad_b_re': 'grad_w', 'grad_b_im': 'grad_w', 'grad_c_re': 'grad_w', 'grad_c_im': 'grad_w', 'grad_d_skip': 'grad_w', 'grad_w_glu': 'grad_w', 'grad_w_attn_o': 'grad_w', 'grad_w_out': 'grad_w', 'grad_g_ffn': 'grad_w', 'grad_w_up': 'grad_w', 'grad_conv_w': 'grad_w', 'grad_conv_b': 'grad_w', 'grad_w_down': 'grad_w', 'grad_g_final': 'grad_w', 'delta_meta': 'delta_w', 'delta_g_mix': 'delta_w', 'delta_w_in': 'delta_w', 'delta_b_f': 'delta_w', 'delta_lam_re': 'delta_w', 'delta_lam_im': 'delta_w', 'delta_log_dt': 'delta_w', 'delta_b_re': 'delta_w', 'delta_b_im': 'delta_w', 'delta_c_re': 'delta_w', 'delta_c_im': 'delta_w', 'delta_d_skip': 'delta_w', 'delta_w_glu': 'delta_w', 'delta_w_attn_o': 'delta_w', 'delta_w_out': 'delta_w', 'delta_g_ffn': 'delta_w', 'delta_w_up': 'delta_w', 'delta_conv_w': 'delta_w', 'delta_conv_b': 'delta_w', 'delta_w_down': 'delta_w', 'delta_g_final': 'delta_w', 'new_m_meta': 'new_m', 'new_m_g_mix': 'new_m', 'new_m_w_in': 'new_m', 'new_m_b_f': 'new_m', 'new_m_lam_re': 'new_m', 'new_m_lam_im': 'new_m', 'new_m_log_dt': 'new_m', 'new_m_b_re': 'new_m', 'new_m_b_im': 'new_m', 'new_m_c_re': 'new_m', 'new_m_c_im': 'new_m', 'new_m_d_skip': 'new_m', 'new_m_w_glu': 'new_m', 'new_m_w_attn_o': 'new_m', 'new_m_w_out': 'new_m', 'new_m_g_ffn': 'new_m', 'new_m_w_up': 'new_m', 'new_m_conv_w': 'new_m', 'new_m_conv_b': 'new_m', 'new_m_w_down': 'new_m', 'new_m_g_final': 'new_m', 'new_v_meta': 'new_v', 'new_v_g_mix': 'new_v', 'new_v_w_in': 'new_v', 'new_v_b_f': 'new_v', 'new_v_lam_re': 'new_v', 'new_v_lam_im': 'new_v', 'new_v_log_dt': 'new_v', 'new_v_b_re': 'new_v', 'new_v_b_im': 'new_v', 'new_v_c_re': 'new_v', 'new_v_c_im': 'new_v', 'new_v_d_skip': 'new_v', 'new_v_w_glu': 'new_v', 'new_v_w_attn_o': 'new_v', 'new_v_w_out': 'new_v', 'new_v_g_ffn': 'new_v', 'new_v_w_up': 'new_v', 'new_v_conv_w': 'new_v', 'new_v_conv_b': 'new_v', 'new_v_w_down': 'new_v', 'new_v_g_final': 'new_v'}


def _forward(args):
    return _fwd_reference(*[args[k] for k in FWD_PARAMS])


def _output_shape():
    def fwd():
        inp = _fwd_setup_inputs(0)
        return _fwd_reference(*[inp[k] for k in FWD_PARAMS])
    out = _jax.eval_shape(fwd)
    return out.shape, out.dtype

N_MICROBATCH = 1
ADAM_LR = 0.001
ADAM_B1 = 0.9
ADAM_B2 = 0.999
ADAM_EPS = 1e-08
ADAM_WD = 0.01
ADAM_STEP = 10
PER_EXAMPLE_BATCH_AXIS = {'x': 0, 'loss_target': 0}
SHARED_INPUTS = []
_WEIGHT_DTYPES = {'meta': _jnp.float32, 'g_mix': _jnp.float32, 'w_in': _jnp.float32, 'b_f': _jnp.float32, 'lam_re': _jnp.float32, 'lam_im': _jnp.float32, 'log_dt': _jnp.float32, 'b_re': _jnp.float32, 'b_im': _jnp.float32, 'c_re': _jnp.float32, 'c_im': _jnp.float32, 'd_skip': _jnp.float32, 'w_glu': _jnp.float32, 'w_attn_o': _jnp.float32, 'w_out': _jnp.float32, 'g_ffn': _jnp.float32, 'w_up': _jnp.float32, 'conv_w': _jnp.float32, 'conv_b': _jnp.float32, 'w_down': _jnp.float32, 'g_final': _jnp.float32}
MOMENT_SCALE = {'meta': 1.829319e-03, 'g_mix': 4.930366e-02, 'w_in': 2.448697e-02, 'b_f': 2.103700e-01, 'lam_re': 1.950167e-03, 'lam_im': 1.948988e-03, 'log_dt': 2.218125e+00, 'b_re': 1.265488e-03, 'b_im': 1.279299e-03, 'c_re': 2.508617e-03, 'c_im': 2.545774e-03, 'd_skip': 4.281286e-02, 'w_glu': 1.913791e-02, 'w_attn_o': 2.550027e-02, 'w_out': 3.591177e-02, 'g_ffn': 9.138252e-02, 'w_up': 3.887933e-02, 'conv_w': 4.028945e-02, 'conv_b': 3.851895e-02, 'w_down': 6.343897e-02, 'g_final': 3.194339e+01}


def _to_microbatches(a, axis):
    t = _jnp.moveaxis(a, axis, 0)
    t = t.reshape((N_MICROBATCH, t.shape[0] // N_MICROBATCH) + t.shape[1:])
    return _jnp.moveaxis(t, 1, axis + 1)


def setup_inputs(seed: int = 0) -> dict:
    inp = _fwd_setup_inputs(seed)
    key = _jax.random.fold_in(_jax.random.key(seed), 7919)
    shape, _ = _output_shape()
    out = dict(inp)
    out["loss_target"] = _jax.random.normal(_jax.random.fold_in(key, 0), shape, _jnp.float32)
    for i, name in enumerate(TWIN_WEIGHTS):
        w = inp[name].astype(_jnp.float32)
        if MOMENT_SCALE is None:
            s = _jnp.sqrt(_jnp.mean(_jnp.square(w)) + 1e-30)
        else:
            s = MOMENT_SCALE[name]
        km, kv = _jax.random.split(_jax.random.fold_in(key, i + 1))
        out[name] = w
        out["m_" + name] = s * _jax.random.normal(km, w.shape, _jnp.float32)
        out["v_" + name] = (s * s) * _jax.random.uniform(kv, w.shape, _jnp.float32, 0.5, 1.5)
    if N_MICROBATCH > 1:
        for name, axis in PER_EXAMPLE_BATCH_AXIS.items():
            out[name] = _to_microbatches(out[name], axis)
    return {'x': out['x'], 'meta': out['meta'], 'g_mix': out['g_mix'], 'w_in': out['w_in'], 'b_f': out['b_f'], 'lam_re': out['lam_re'], 'lam_im': out['lam_im'], 'log_dt': out['log_dt'], 'b_re': out['b_re'], 'b_im': out['b_im'], 'c_re': out['c_re'], 'c_im': out['c_im'], 'd_skip': out['d_skip'], 'w_glu': out['w_glu'], 'w_attn_o': out['w_attn_o'], 'w_out': out['w_out'], 'g_ffn': out['g_ffn'], 'w_up': out['w_up'], 'conv_w': out['conv_w'], 'conv_b': out['conv_b'], 'w_down': out['w_down'], 'g_final': out['g_final'], 'loss_target': out['loss_target'], 'm_meta': out['m_meta'], 'm_g_mix': out['m_g_mix'], 'm_w_in': out['m_w_in'], 'm_b_f': out['m_b_f'], 'm_lam_re': out['m_lam_re'], 'm_lam_im': out['m_lam_im'], 'm_log_dt': out['m_log_dt'], 'm_b_re': out['m_b_re'], 'm_b_im': out['m_b_im'], 'm_c_re': out['m_c_re'], 'm_c_im': out['m_c_im'], 'm_d_skip': out['m_d_skip'], 'm_w_glu': out['m_w_glu'], 'm_w_attn_o': out['m_w_attn_o'], 'm_w_out': out['m_w_out'], 'm_g_ffn': out['m_g_ffn'], 'm_w_up': out['m_w_up'], 'm_conv_w': out['m_conv_w'], 'm_conv_b': out['m_conv_b'], 'm_w_down': out['m_w_down'], 'm_g_final': out['m_g_final'], 'v_meta': out['v_meta'], 'v_g_mix': out['v_g_mix'], 'v_w_in': out['v_w_in'], 'v_b_f': out['v_b_f'], 'v_lam_re': out['v_lam_re'], 'v_lam_im': out['v_lam_im'], 'v_log_dt': out['v_log_dt'], 'v_b_re': out['v_b_re'], 'v_b_im': out['v_b_im'], 'v_c_re': out['v_c_re'], 'v_c_im': out['v_c_im'], 'v_d_skip': out['v_d_skip'], 'v_w_glu': out['v_w_glu'], 'v_w_attn_o': out['v_w_attn_o'], 'v_w_out': out['v_w_out'], 'v_g_ffn': out['v_g_ffn'], 'v_w_up': out['v_w_up'], 'v_conv_w': out['v_conv_w'], 'v_conv_b': out['v_conv_b'], 'v_w_down': out['v_w_down'], 'v_g_final': out['v_g_final']}


def _loss(weights, diff, rest, loss_target):
    with _jax.named_scope("forward"):
        args = {**rest, TWIN_DIFF_INPUT: diff, **{k: w.astype(_WEIGHT_DTYPES[k]) for k, w in weights.items()}}
        y = _forward(args)
    with _jax.named_scope("loss_head"):
        err = _jnp.square(y.astype(_jnp.float32) - loss_target)
        return 0.5 * _jnp.sum(_jnp.mean(err, axis=-1)) if err.ndim else 0.5 * err


def _adamw(w, g, m, v):
    m = ADAM_B1 * m + (1.0 - ADAM_B1) * g
    v = ADAM_B2 * v + (1.0 - ADAM_B2) * _jnp.square(g)
    m_hat = m / (1.0 - ADAM_B1 ** ADAM_STEP)
    v_hat = v / (1.0 - ADAM_B2 ** ADAM_STEP)
    delta = -ADAM_LR * (m_hat / (_jnp.sqrt(v_hat) + ADAM_EPS) + ADAM_WD * w)
    return delta, m, v


def reference(x, meta, g_mix, w_in, b_f, lam_re, lam_im, log_dt, b_re, b_im, c_re, c_im, d_skip, w_glu, w_attn_o, w_out, g_ffn, w_up, conv_w, conv_b, w_down, g_final, loss_target, m_meta, m_g_mix, m_w_in, m_b_f, m_lam_re, m_lam_im, m_log_dt, m_b_re, m_b_im, m_c_re, m_c_im, m_d_skip, m_w_glu, m_w_attn_o, m_w_out, m_g_ffn, m_w_up, m_conv_w, m_conv_b, m_w_down, m_g_final, v_meta, v_g_mix, v_w_in, v_b_f, v_lam_re, v_lam_im, v_log_dt, v_b_re, v_b_im, v_c_re, v_c_im, v_d_skip, v_w_glu, v_w_attn_o, v_w_out, v_g_ffn, v_w_up, v_conv_w, v_conv_b, v_w_down, v_g_final):
    given = dict(x=x, meta=meta, g_mix=g_mix, w_in=w_in, b_f=b_f, lam_re=lam_re, lam_im=lam_im, log_dt=log_dt, b_re=b_re, b_im=b_im, c_re=c_re, c_im=c_im, d_skip=d_skip, w_glu=w_glu, w_attn_o=w_attn_o, w_out=w_out, g_ffn=g_ffn, w_up=w_up, conv_w=conv_w, conv_b=conv_b, w_down=w_down, g_final=g_final, loss_target=loss_target, m_meta=m_meta, m_g_mix=m_g_mix, m_w_in=m_w_in, m_b_f=m_b_f, m_lam_re=m_lam_re, m_lam_im=m_lam_im, m_log_dt=m_log_dt, m_b_re=m_b_re, m_b_im=m_b_im, m_c_re=m_c_re, m_c_im=m_c_im, m_d_skip=m_d_skip, m_w_glu=m_w_glu, m_w_attn_o=m_w_attn_o, m_w_out=m_w_out, m_g_ffn=m_g_ffn, m_w_up=m_w_up, m_conv_w=m_conv_w, m_conv_b=m_conv_b, m_w_down=m_w_down, m_g_final=m_g_final, v_meta=v_meta, v_g_mix=v_g_mix, v_w_in=v_w_in, v_b_f=v_b_f, v_lam_re=v_lam_re, v_lam_im=v_lam_im, v_log_dt=v_log_dt, v_b_re=v_b_re, v_b_im=v_b_im, v_c_re=v_c_re, v_c_im=v_c_im, v_d_skip=v_d_skip, v_w_glu=v_w_glu, v_w_attn_o=v_w_attn_o, v_w_out=v_w_out, v_g_ffn=v_g_ffn, v_w_up=v_w_up, v_conv_w=v_conv_w, v_conv_b=v_conv_b, v_w_down=v_w_down, v_g_final=v_g_final)
    weights = {n: given[n] for n in TWIN_WEIGHTS}
    shared = {n: given[n] for n in SHARED_INPUTS}
    per_example = {n: given[n] for n in ['x']}
    grad_fn = _jax.value_and_grad(_loss, argnums=(0, 1))

    def one_microbatch(ex, loss_target):
        ex = dict(ex)
        diff = ex.pop(TWIN_DIFF_INPUT)
        return grad_fn(weights, diff, {**shared, **ex}, loss_target)

    if N_MICROBATCH == 1:
        loss, (grad_w, grad_x) = one_microbatch(per_example, given["loss_target"])
    else:
        def body(carry, xs):
            loss_sum, grad_sum = carry
            l_k, (gw_k, gx_k) = one_microbatch(xs[0], xs[1])
            with _jax.named_scope("update"):
                return (loss_sum + l_k, _jax.tree.map(_jnp.add, grad_sum, gw_k)), gx_k

        init = (_jnp.zeros((), _jnp.float32), _jax.tree.map(_jnp.zeros_like, weights))
        (loss, grad_w), grad_x = _jax.lax.scan(body, init, (per_example, given["loss_target"]))
    with _jax.named_scope("update"):
        delta_w, new_m, new_v = {}, {}, {}
        for n in TWIN_WEIGHTS:
            delta_w[n], new_m[n], new_v[n] = _adamw(weights[n], grad_w[n], given["m_" + n], given["v_" + n])
    return (loss, grad_x, *[grad_w[n] for n in TWIN_WEIGHTS], *[delta_w[n] for n in TWIN_WEIGHTS],
            *[new_m[n] for n in TWIN_WEIGHTS], *[new_v[n] for n in TWIN_WEIGHTS])
```

```python
import functools
import math

import jax
import jax.numpy as jnp
from jax import lax
from jax.experimental import pallas as pl
from jax.experimental.pallas import tpu as pltpu

F32 = jnp.float32
BF16 = jnp.bfloat16

N_META = 16
EPS = 1e-6
HEAD_DIM = 128
SSM_GROUP = 16
SSM_STATE = 64
GROUPS_PER_SLAB = 8
SLAB_STATE = GROUPS_PER_SLAB * SSM_STATE
CONV_WIDTH = 3
N_DEV = 8

ADAM_LR = 0.001
ADAM_B1 = 0.9
ADAM_B2 = 0.999
ADAM_EPS = 1e-08
ADAM_WD = 0.01
ADAM_STEP = 10

LANES = 128
SUBLANES = 8
VMEM_LIMIT = 52 * 1024 * 1024

SEQ_BLOCK = 768
ATT_BLOCK = 384
ROW_BLOCK = 256
SSM_CHUNK = 128
ADAMW_BLOCK_ELEMS = 1 << 17
MASK_VALUE = -1e30


def _round_up(n, m):
    return (n + m - 1) // m * m


def _divisor(n, target, mult):
    if n <= target:
        return n
    best = None
    for d in range(mult, target + 1, mult):
        if n % d == 0:
            best = d
    assert best is not None, (n, target, mult)
    return best


def _params(sem):
    return pltpu.CompilerParams(dimension_semantics=sem, vmem_limit_bytes=VMEM_LIMIT)


def _matmul(a, b, *, name, trans_a=False, out_dtype=None, residual=None, tm=768, tn=1024, tk=2048):
    out_dtype = BF16 if out_dtype is None else out_dtype
    if trans_a:
        K, M = a.shape
    else:
        M, K = a.shape
    K2, N = b.shape
    assert K == K2, (a.shape, b.shape)
    tm = _divisor(M, tm, LANES if trans_a else SUBLANES)
    tn = _divisor(N, tn, LANES)
    tk = _divisor(K, tk, LANES if not trans_a else SUBLANES)
    nk = K // tk

    def body(*refs):
        if residual is None:
            a_ref, b_ref, o_ref, acc_ref = refs
        else:
            a_ref, b_ref, r_ref, o_ref, acc_ref = refs
        k = pl.program_id(2)

        @pl.when(k == 0)
        def _():
            acc_ref[...] = jnp.zeros_like(acc_ref)

        if trans_a:
            acc_ref[...] += lax.dot_general(a_ref[...], b_ref[...], (((0,), (0,)), ((), ())),
                                            preferred_element_type=F32)
        else:
            acc_ref[...] += jnp.dot(a_ref[...], b_ref[...], preferred_element_type=F32)

        @pl.when(k == nk - 1)
        def _():
            r = acc_ref[...]
            if residual is not None:
                r = r + r_ref[...]
            o_ref[...] = r.astype(o_ref.dtype)

    if trans_a:
        a_spec = pl.BlockSpec((tk, tm), lambda i, j, k: (k, i))
    else:
        a_spec = pl.BlockSpec((tm, tk), lambda i, j, k: (i, k))
    in_specs = [a_spec, pl.BlockSpec((tk, tn), lambda i, j, k: (k, j))]
    args = [a, b]
    if residual is not None:
        in_specs.append(pl.BlockSpec((tm, tn), lambda i, j, k: (i, j)))
        args.append(residual)
    return pl.pallas_call(
        body, name=name,
        grid=(M // tm, N // tn, nk),
        in_specs=in_specs,
        out_specs=pl.BlockSpec((tm, tn), lambda i, j, k: (i, j)),
        out_shape=jax.ShapeDtypeStruct((M, N), out_dtype),
        scratch_shapes=[pltpu.VMEM((tm, tn), F32)],
        compiler_params=_params(("parallel", "parallel", "arbitrary")),
    )(*args)


def _rms_fwd(h, g, *, name):
    T, D = h.shape
    tr = _divisor(T, ROW_BLOCK, SUBLANES)

    def body(h_ref, g_ref, o_ref):
        x = h_ref[...]
        r = lax.rsqrt(jnp.mean(x * x, axis=-1, keepdims=True) + EPS)
        o_ref[...] = (x * r * g_ref[...]).astype(o_ref.dtype)

    return pl.pallas_call(
        body, name=name, grid=(T // tr,),
        in_specs=[pl.BlockSpec((tr, D), lambda i: (i, 0)), pl.BlockSpec((1, D), lambda i: (0, 0))],
        out_specs=pl.BlockSpec((tr, D), lambda i: (i, 0)),
        out_shape=jax.ShapeDtypeStruct((T, D), BF16),
        compiler_params=_params(("parallel",)),
    )(h, g)


def _rms_bwd(dn, h, g, dres, *, name):
    T, D = h.shape
    tr = _divisor(T, ROW_BLOCK, SUBLANES)

    def body(dn_ref, h_ref, g_ref, dres_ref, dh_ref, dg_ref):
        i = pl.program_id(0)

        @pl.when(i == 0)
        def _():
            dg_ref[...] = jnp.zeros_like(dg_ref)

        x = h_ref[...]
        dn_v = dn_ref[...].astype(F32)
        r = lax.rsqrt(jnp.mean(x * x, axis=-1, keepdims=True) + EPS)
        xh = x * r
        dg_ref[...] += jnp.sum(dn_v * xh, axis=0, keepdims=True)
        dxh = dn_v * g_ref[...]
        dh_ref[...] = dres_ref[...] + r * (dxh - xh * jnp.mean(dxh * xh, axis=-1, keepdims=True))

    return pl.pallas_call(
        body, name=name, grid=(T // tr,),
        in_specs=[pl.BlockSpec((tr, D), lambda i: (i, 0)), pl.BlockSpec((tr, D), lambda i: (i, 0)),
                  pl.BlockSpec((1, D), lambda i: (0, 0)), pl.BlockSpec((tr, D), lambda i: (i, 0))],
        out_specs=[pl.BlockSpec((tr, D), lambda i: (i, 0)), pl.BlockSpec((1, D), lambda i: (0, 0))],
        out_shape=[jax.ShapeDtypeStruct((T, D), F32), jax.ShapeDtypeStruct((1, D), F32)],
        compiler_params=_params(("arbitrary",)),
    )(dn, h, g, dres)


def _final_loss(h, g, target, n_valid, *, name):
    T, D = h.shape
    tr = _divisor(T, ROW_BLOCK, SUBLANES)

    def body(h_ref, g_ref, t_ref, dh_ref, sq_ref, dg_ref):
        i = pl.program_id(0)

        @pl.when(i == 0)
        def _():
            sq_ref[...] = jnp.zeros_like(sq_ref)
            dg_ref[...] = jnp.zeros_like(dg_ref)

        x = h_ref[...]
        r = lax.rsqrt(jnp.mean(x * x, axis=-1, keepdims=True) + EPS)
        xh = x * r
        gv = g_ref[...]
        row = i * tr + lax.broadcasted_iota(jnp.int32, (tr, 1), 0)
        valid = (row >= N_META) & (row < N_META + n_valid)
        err = jnp.where(valid, xh * gv - t_ref[...], 0.0)
        sq_ref[...] += jnp.sum(err * err)
        dy = err * (1.0 / D)
        dg_ref[...] += jnp.sum(dy * xh, axis=0, keepdims=True)
        dxh = dy * gv
        dh_ref[...] = r * (dxh - xh * jnp.mean(dxh * xh, axis=-1, keepdims=True))

    return pl.pallas_call(
        body, name=name, grid=(T // tr,),
        in_specs=[pl.BlockSpec((tr, D), lambda i: (i, 0)), pl.BlockSpec((1, D), lambda i: (0, 0)),
                  pl.BlockSpec((tr, D), lambda i: (i, 0))],
        out_specs=[pl.BlockSpec((tr, D), lambda i: (i, 0)), pl.BlockSpec((SUBLANES, LANES), lambda i: (0, 0)),
                   pl.BlockSpec((1, D), lambda i: (0, 0))],
        out_shape=[jax.ShapeDtypeStruct((T, D), F32), jax.ShapeDtypeStruct((SUBLANES, LANES), F32),
                   jax.ShapeDtypeStruct((1, D), F32)],
        compiler_params=_params(("arbitrary",)),
    )(h, g, target)


def _prefix_sum_lanes(x):
    lane = lax.broadcasted_iota(jnp.int32, x.shape, 1)
    d = 1
    while d < LANES:
        x = x + jnp.where(lane >= d, pltpu.roll(x, d, axis=1), 0.0)
        d *= 2
    return x


def _forget_cumsum(ft, bf, *, name):
    H, T = ft.shape
    nb = T // LANES

    def body(f_ref, b_ref, o_ref):
        carry = jnp.zeros((H, 1), F32)
        for j in range(nb):
            sl = pl.ds(j * LANES, LANES)
            lf = jax.nn.log_sigmoid(f_ref[:, sl] + b_ref[...])
            c = _prefix_sum_lanes(lf) + carry
            o_ref[:, sl] = c
            carry = c[:, LANES - 1:LANES]

    return pl.pallas_call(
        body, name=name,
        in_specs=[pl.BlockSpec(memory_space=pltpu.VMEM), pl.BlockSpec(memory_space=pltpu.VMEM)],
        out_specs=pl.BlockSpec(memory_space=pltpu.VMEM),
        out_shape=jax.ShapeDtypeStruct((H, T), F32),
        compiler_params=pltpu.CompilerParams(vmem_limit_bytes=VMEM_LIMIT),
    )(ft, bf)


def _forget_bwd(dF, ft, bf, *, name):
    H, T = ft.shape
    nb = T // LANES

    def body(d_ref, f_ref, b_ref, o_ref, s_ref):
        carry = jnp.zeros((H, 1), F32)
        acc = jnp.zeros((H, LANES), F32)
        for j in reversed(range(nb)):
            sl = pl.ds(j * LANES, LANES)
            d = d_ref[:, sl]
            pre = _prefix_sum_lanes(d)
            tot = pre[:, LANES - 1:LANES]
            dlf = tot - pre + d + carry
            carry = carry + tot
            z = f_ref[:, sl] + b_ref[...]
            df = dlf * jax.nn.sigmoid(-z)
            o_ref[:, sl] = df
            acc = acc + df
        s_ref[...] = jnp.broadcast_to(jnp.sum(acc, axis=1, keepdims=True), (H, LANES))

    return pl.pallas_call(
        body, name=name,
        in_specs=[pl.BlockSpec(memory_space=pltpu.VMEM)] * 3,
        out_specs=[pl.BlockSpec(memory_space=pltpu.VMEM)] * 2,
        out_shape=[jax.ShapeDtypeStruct((H, T), F32), jax.ShapeDtypeStruct((H, LANES), F32)],
        compiler_params=pltpu.CompilerParams(vmem_limit_bytes=VMEM_LIMIT),
    )(dF, ft, bf)


def _scores(q_ref, k_ref, f_ref, fq_ref, h, i, j, bq, bk, scale):
    hs = pl.ds(h * HEAD_DIM, HEAD_DIM)
    s = lax.dot_general(q_ref[:, hs], k_ref[:, hs], (((1,), (1,)), ((), ())), preferred_element_type=F32)
    s = s * scale + (fq_ref[:, h:h + 1] - f_ref[h:h + 1, :])
    rows = i * bq + lax.broadcasted_iota(jnp.int32, (bq, bk), 0)
    cols = j * bk + lax.broadcasted_iota(jnp.int32, (bq, bk), 1)
    return jnp.where(cols <= rows, s, MASK_VALUE)


def _attn_fwd(zqkv, fcum, fcol, *, name):
    T = zqkv.shape[0]
    DA = zqkv.shape[1] // 3
    H = DA // HEAD_DIM
    blk = _divisor(T, ATT_BLOCK, LANES)
    nb = T // blk
    scale = HEAD_DIM ** -0.5

    def body(q_ref, k_ref, v_ref, f_ref, fq_ref, o_ref, lse_ref, m_ref, l_ref, acc_ref):
        i = pl.program_id(0)
        j = pl.program_id(1)

        @pl.when(j == 0)
        def _():
            m_ref[...] = jnp.full_like(m_ref, MASK_VALUE)
            l_ref[...] = jnp.zeros_like(l_ref)
            acc_ref[...] = jnp.zeros_like(acc_ref)

        @pl.when(j <= i)
        def _():
            for h in range(H):
                hs = pl.ds(h * HEAD_DIM, HEAD_DIM)
                s = _scores(q_ref, k_ref, f_ref, fq_ref, h, i, j, blk, blk, scale)
                m_prev = m_ref[h]
                m_new = jnp.maximum(m_prev, jnp.max(s, axis=1, keepdims=True))
                alpha = jnp.exp(m_prev - m_new)
                p = jnp.exp(s - m_new)
                l_ref[h] = alpha * l_ref[h] + jnp.sum(p, axis=1, keepdims=True)
                acc_ref[:, hs] = alpha * acc_ref[:, hs] + jnp.dot(p.astype(BF16), v_ref[:, hs],
                                                                  preferred_element_type=F32)
                m_ref[h] = m_new

        @pl.when(j == nb - 1)
        def _():
            for h in range(H):
                hs = pl.ds(h * HEAD_DIM, HEAD_DIM)
                l = l_ref[h]
                o_ref[:, hs] = (acc_ref[:, hs] / l).astype(o_ref.dtype)
                lse_ref[:, h:h + 1] = m_ref[h] + jnp.log(l)

    kv = lambda c: (lambda i, j: (jnp.minimum(j, i), c))
    return pl.pallas_call(
        body, name=name, grid=(nb, nb),
        in_specs=[pl.BlockSpec((blk, DA), lambda i, j: (i, 0)),
                  pl.BlockSpec((blk, DA), kv(1)), pl.BlockSpec((blk, DA), kv(2)),
                  pl.BlockSpec((H, blk), lambda i, j: (0, jnp.minimum(j, i))),
                  pl.BlockSpec((blk, H), lambda i, j: (i, 0))],
        out_specs=[pl.BlockSpec((blk, DA), lambda i, j: (i, 0)), pl.BlockSpec((blk, H), lambda i, j: (i, 0))],
        out_shape=[jax.ShapeDtypeStruct((T, DA), BF16), jax.ShapeDtypeStruct((T, H), F32)],
        scratch_shapes=[pltpu.VMEM((H, blk, 1), F32), pltpu.VMEM((H, blk, 1), F32), pltpu.VMEM((blk, DA), F32)],
        compiler_params=_params(("parallel", "arbitrary")),
    )(zqkv, zqkv, zqkv, fcum, fcol)


def _attn_delta(zqkv, fcum, fcol, do, lse, *, name):
    T = zqkv.shape[0]
    DA = zqkv.shape[1] // 3
    H = DA // HEAD_DIM
    blk = _divisor(T, ATT_BLOCK, LANES)
    nb = T // blk
    scale = HEAD_DIM ** -0.5

    def body(q_ref, k_ref, v_ref, f_ref, fq_ref, do_ref, lse_ref, d_ref):
        i = pl.program_id(0)
        j = pl.program_id(1)

        @pl.when(j == 0)
        def _():
            d_ref[...] = jnp.zeros_like(d_ref)

        @pl.when(j <= i)
        def _():
            for h in range(H):
                hs = pl.ds(h * HEAD_DIM, HEAD_DIM)
                s = _scores(q_ref, k_ref, f_ref, fq_ref, h, i, j, blk, blk, scale)
                p = jnp.exp(s - lse_ref[:, h:h + 1])
                dp = lax.dot_general(do_ref[:, hs], v_ref[:, hs], (((1,), (1,)), ((), ())),
                                     preferred_element_type=F32)
                d_ref[:, h:h + 1] += jnp.sum(p * dp, axis=1, keepdims=True)

    kv = lambda c: (lambda i, j: (jnp.minimum(j, i), c))
    row = lambda i, j: (i, 0)
    return pl.pallas_call(
        body, name=name, grid=(nb, nb),
        in_specs=[pl.BlockSpec((blk, DA), row), pl.BlockSpec((blk, DA), kv(1)), pl.BlockSpec((blk, DA), kv(2)),
                  pl.BlockSpec((H, blk), lambda i, j: (0, jnp.minimum(j, i))), pl.BlockSpec((blk, H), row),
                  pl.BlockSpec((blk, DA), row), pl.BlockSpec((blk, H), row)],
        out_specs=pl.BlockSpec((blk, H), row),
        out_shape=jax.ShapeDtypeStruct((T, H), F32),
        compiler_params=_params(("parallel", "arbitrary")),
    )(zqkv, zqkv, zqkv, fcum, fcol, do, lse)


def _attn_bwd_dq(zqkv, fcum, fcol, do, lse, delta, *, name):
    T = zqkv.shape[0]
    DA = zqkv.shape[1] // 3
    H = DA // HEAD_DIM
    blk = _divisor(T, ATT_BLOCK, LANES)
    nb = T // blk
    scale = HEAD_DIM ** -0.5

    def body(q_ref, k_ref, v_ref, f_ref, fq_ref, do_ref, lse_ref, dl_ref, dq_ref, acc_ref):
        i = pl.program_id(0)
        j = pl.program_id(1)

        @pl.when(j == 0)
        def _():
            acc_ref[...] = jnp.zeros_like(acc_ref)

        @pl.when(j <= i)
        def _():
            for h in range(H):
                hs = pl.ds(h * HEAD_DIM, HEAD_DIM)
                s = _scores(q_ref, k_ref, f_ref, fq_ref, h, i, j, blk, blk, scale)
                p = jnp.exp(s - lse_ref[:, h:h + 1])
                dp = lax.dot_general(do_ref[:, hs], v_ref[:, hs], (((1,), (1,)), ((), ())),
                                     preferred_element_type=F32)
                ds = p * (dp - dl_ref[:, h:h + 1])
                acc_ref[:, hs] += scale * jnp.dot(ds.astype(BF16), k_ref[:, hs], preferred_element_type=F32)

        @pl.when(j == nb - 1)
        def _():
            dq_ref[...] = acc_ref[...].astype(dq_ref.dtype)

    kv = lambda c: (lambda i, j: (jnp.minimum(j, i), c))
    row = lambda i, j: (i, 0)
    return pl.pallas_call(
        body, name=name, grid=(nb, nb),
        in_specs=[pl.BlockSpec((blk, DA), row), pl.BlockSpec((blk, DA), kv(1)), pl.BlockSpec((blk, DA), kv(2)),
                  pl.BlockSpec((H, blk), lambda i, j: (0, jnp.minimum(j, i))), pl.BlockSpec((blk, H), row),
                  pl.BlockSpec((blk, DA), row), pl.BlockSpec((blk, H), row), pl.BlockSpec((blk, H), row)],
        out_specs=pl.BlockSpec((blk, DA), row),
        out_shape=jax.ShapeDtypeStruct((T, DA), BF16),
        scratch_shapes=[pltpu.VMEM((blk, DA), F32)],
        compiler_params=_params(("parallel", "arbitrary")),
    )(zqkv, zqkv, zqkv, fcum, fcol, do, lse, delta)


def _attn_bwd_dkv(zqkv, fcum, fcol, do, lse, delta, *, name):
    T = zqkv.shape[0]
    DA = zqkv.shape[1] // 3
    H = DA // HEAD_DIM
    blk = _divisor(T, ATT_BLOCK, LANES)
    nb = T // blk
    scale = HEAD_DIM ** -0.5

    def body(q_ref, k_ref, v_ref, f_ref, fq_ref, do_ref, lse_ref, dl_ref, dk_ref, dv_ref, df_ref,
             dk_acc, dv_acc, df_acc):
        j = pl.program_id(0)
        i = pl.program_id(1)

        @pl.when(i == 0)
        def _():
            dk_acc[...] = jnp.zeros_like(dk_acc)
            dv_acc[...] = jnp.zeros_like(dv_acc)
            df_acc[...] = jnp.zeros_like(df_acc)

        @pl.when(i >= j)
        def _():
            for h in range(H):
                hs = pl.ds(h * HEAD_DIM, HEAD_DIM)
                s = _scores(q_ref, k_ref, f_ref, fq_ref, h, i, j, blk, blk, scale)
                p = jnp.exp(s - lse_ref[:, h:h + 1])
                dov = do_ref[:, hs]
                dv_acc[:, hs] += lax.dot_general(p.astype(BF16), dov, (((0,), (0,)), ((), ())),
                                                 preferred_element_type=F32)
                dp = lax.dot_general(dov, v_ref[:, hs], (((1,), (1,)), ((), ())), preferred_element_type=F32)
                ds = p * (dp - dl_ref[:, h:h + 1])
                dk_acc[:, hs] += scale * lax.dot_general(ds.astype(BF16), q_ref[:, hs], (((0,), (0,)), ((), ())),
                                                         preferred_element_type=F32)
                df_acc[h:h + 1, :] -= jnp.sum(ds, axis=0, keepdims=True)

        @pl.when(i == nb - 1)
        def _():
            dk_ref[...] = dk_acc[...].astype(dk_ref.dtype)
            dv_ref[...] = dv_acc[...].astype(dv_ref.dtype)
            df_ref[...] = df_acc[...]

    qrow = lambda j, i: (jnp.maximum(i, j), 0)
    kcol = lambda c: (lambda j, i: (j, c))
    return pl.pallas_call(
        body, name=name, grid=(nb, nb),
        in_specs=[pl.BlockSpec((blk, DA), qrow), pl.BlockSpec((blk, DA), kcol(1)), pl.BlockSpec((blk, DA), kcol(2)),
                  pl.BlockSpec((H, blk), lambda j, i: (0, j)), pl.BlockSpec((blk, H), qrow),
                  pl.BlockSpec((blk, DA), qrow), pl.BlockSpec((blk, H), qrow), pl.BlockSpec((blk, H), qrow)],
        out_specs=[pl.BlockSpec((blk, DA), kcol(0)), pl.BlockSpec((blk, DA), kcol(0)),
                   pl.BlockSpec((H, blk), lambda j, i: (0, j))],
        out_shape=[jax.ShapeDtypeStruct((T, DA), BF16), jax.ShapeDtypeStruct((T, DA), BF16),
                   jax.ShapeDtypeStruct((H, T), F32)],
        scratch_shapes=[pltpu.VMEM((blk, DA), F32), pltpu.VMEM((blk, DA), F32), pltpu.VMEM((H, blk), F32)],
        compiler_params=_params(("parallel", "arbitrary")),
    )(zqkv, zqkv, zqkv, fcum, fcol, do, lse, delta)


def _gelu(y):
    c = math.sqrt(2.0 / math.pi)
    return 0.5 * y * (1.0 + jnp.tanh(c * (y + 0.044715 * (y * y * y))))


def _gelu_grad(y):
    c = math.sqrt(2.0 / math.pi)
    th = jnp.tanh(c * (y + 0.044715 * (y * y * y)))
    return 0.5 * (1.0 + th) + 0.5 * y * (1.0 - th * th) * c * (1.0 + 3.0 * 0.044715 * y * y)


STATE_BLOCKS = SLAB_STATE // LANES


def _lane_blocks(ref, lead=()):
    return [ref[lead + (slice(None), pl.ds(b * LANES, LANES))] for b in range(2 * STATE_BLOCKS)]


def _put_lane_blocks(ref, blocks):
    for b, v in enumerate(blocks):
        ref[:, pl.ds(b * LANES, LANES)] = v


def _put_slab(x_ref, first, q, n_slab, chunk, value):
    for b in range(2 * STATE_BLOCKS):
        x_ref[b, pl.ds(first * n_slab + q, chunk, stride=n_slab), :] = value[:, b * LANES:(b + 1) * LANES]


def _get_slab(x_ref, first, q, n_slab, chunk):
    return jnp.concatenate([x_ref[b, pl.ds(first * n_slab + q, chunk, stride=n_slab), :]
                            for b in range(2 * STATE_BLOCKS)], axis=1)


def _ssm_scan_fwd(x_ref, a, h, chunk, n_slab, first=0):
    nb = STATE_BLOCKS

    def step(t, h):
        rows = pl.ds(pl.multiple_of((t + first) * n_slab, n_slab), n_slab)
        out = [None] * (2 * nb)
        for b in range(nb):
            n_re = a[b] * h[b] - a[nb + b] * h[nb + b] + x_ref[b, rows, :]
            n_im = a[b] * h[nb + b] + a[nb + b] * h[b] + x_ref[nb + b, rows, :]
            x_ref[b, rows, :] = n_re
            x_ref[nb + b, rows, :] = n_im
            out[b], out[nb + b] = n_re, n_im
        return tuple(out)

    return lax.fori_loop(0, chunk, step, tuple(h), unroll=4)


def _ssm_fwd(zu, w_b, w_c, a, d_skip, *, name):
    T, DS = zu.shape
    n_slab = DS // LANES
    chunk = _divisor(T, SSM_CHUNK, SUBLANES)
    n_chunk = T // chunk

    def body(u_ref, wb_ref, wc_ref, a_ref, ds_ref, y_ref, gy_ref, hin_ref, x_ref, h_ref):
        k = pl.program_id(0)

        @pl.when(k == 0)
        def _():
            h_ref[...] = jnp.zeros_like(h_ref)

        hin_ref[0] = h_ref[...]
        for q in range(n_slab):
            qs = pl.ds(q * LANES, LANES)
            _put_slab(x_ref, 0, q, n_slab, chunk, jnp.dot(u_ref[:, qs], wb_ref[q], preferred_element_type=F32))
        h = _ssm_scan_fwd(x_ref, _lane_blocks(a_ref), _lane_blocks(h_ref), chunk, n_slab)
        _put_lane_blocks(h_ref, h)
        for q in range(n_slab):
            qs = pl.ds(q * LANES, LANES)
            hq = _get_slab(x_ref, 0, q, n_slab, chunk).astype(BF16)
            y = jnp.dot(hq, wc_ref[q], preferred_element_type=F32) + ds_ref[:, qs] * u_ref[:, qs].astype(F32)
            y_ref[:, qs] = y
            gy_ref[:, qs] = _gelu(y).astype(gy_ref.dtype)

    whole = lambda shape: pl.BlockSpec(shape, lambda k: (0,) * len(shape))
    return pl.pallas_call(
        body, name=name, grid=(n_chunk,),
        in_specs=[pl.BlockSpec((chunk, DS), lambda k: (k, 0)), whole(w_b.shape), whole(w_c.shape),
                  whole(a.shape), whole(d_skip.shape)],
        out_specs=[pl.BlockSpec((chunk, DS), lambda k: (k, 0)), pl.BlockSpec((chunk, DS), lambda k: (k, 0)),
                   pl.BlockSpec((1, n_slab, 2 * SLAB_STATE), lambda k: (k, 0, 0))],
        out_shape=[jax.ShapeDtypeStruct((T, DS), F32), jax.ShapeDtypeStruct((T, DS), BF16),
                   jax.ShapeDtypeStruct((n_chunk, n_slab, 2 * SLAB_STATE), F32)],
        scratch_shapes=[pltpu.VMEM((2 * STATE_BLOCKS, chunk * n_slab, LANES), F32),
                        pltpu.VMEM((n_slab, 2 * SLAB_STATE), F32)],
        compiler_params=_params(("arbitrary",)),
    )(zu, w_b, w_c, a, d_skip)


def _ssm_bwd(zu, dgy, y, hin, w_b, w_bt, w_ct, a, d_skip, *, name):
    T, DS = zu.shape
    n_slab = DS // LANES
    chunk = _divisor(T, SSM_CHUNK, SUBLANES)
    n_chunk = T // chunk
    S = SLAB_STATE

    def body(u_ref, dgy_ref, y_ref, hin_ref, wb_ref, wbt_ref, wct_ref, a_ref, ds_ref,
             du_ref, dwb_ref, dwc_ref, da_ref, dds_ref, hb_ref, gb_ref, dy_ref, g_ref):
        k = pl.program_id(0)

        @pl.when(k == 0)
        def _():
            g_ref[...] = jnp.zeros_like(g_ref)
            dwb_ref[...] = jnp.zeros_like(dwb_ref)
            dwc_ref[...] = jnp.zeros_like(dwc_ref)
            da_ref[...] = jnp.zeros_like(da_ref)
            dds_ref[...] = jnp.zeros_like(dds_ref)

        nb = STATE_BLOCKS
        a = _lane_blocks(a_ref)
        hin = _lane_blocks(hin_ref, lead=(0,))

        for b in range(2 * nb):
            hb_ref[b, pl.ds(0, n_slab), :] = hin[b]
        dy_ref[...] = dgy_ref[...].astype(F32) * _gelu_grad(y_ref[...])
        for q in range(n_slab):
            qs = pl.ds(q * LANES, LANES)
            _put_slab(hb_ref, 1, q, n_slab, chunk, jnp.dot(u_ref[:, qs], wb_ref[q], preferred_element_type=F32))
            _put_slab(gb_ref, 0, q, n_slab, chunk,
                      jnp.dot(dy_ref[:, qs].astype(BF16), wct_ref[q], preferred_element_type=F32))
        _ssm_scan_fwd(hb_ref, a, hin, chunk, n_slab, first=1)

        def step(s, carry):
            g, da = carry[:2 * nb], carry[2 * nb:]
            t = chunk - 1 - s
            rows = pl.ds(pl.multiple_of(t * n_slab, n_slab), n_slab)
            g_out, da_out = [None] * (2 * nb), [None] * (2 * nb)
            for b in range(nb):
                n_re = gb_ref[b, rows, :] + a[b] * g[b] + a[nb + b] * g[nb + b]
                n_im = gb_ref[nb + b, rows, :] + a[b] * g[nb + b] - a[nb + b] * g[b]
                gb_ref[b, rows, :] = n_re
                gb_ref[nb + b, rows, :] = n_im
                p_re = hb_ref[b, rows, :]
                p_im = hb_ref[nb + b, rows, :]
                g_out[b], g_out[nb + b] = n_re, n_im
                da_out[b] = da[b] + n_re * p_re + n_im * p_im
                da_out[nb + b] = da[nb + b] + n_im * p_re - n_re * p_im
            return tuple(g_out) + tuple(da_out)

        zero = jnp.zeros((n_slab, LANES), F32)
        carry = lax.fori_loop(0, chunk, step, tuple(_lane_blocks(g_ref)) + (zero,) * (2 * nb), unroll=4)
        _put_lane_blocks(g_ref, carry[:2 * nb])
        for b in range(2 * nb):
            da_ref[:, pl.ds(b * LANES, LANES)] += carry[2 * nb + b]

        for q in range(n_slab):
            qs = pl.ds(q * LANES, LANES)
            uq = u_ref[:, qs]
            dy = dy_ref[:, qs]
            hq = _get_slab(hb_ref, 1, q, n_slab, chunk).astype(BF16)
            gq = _get_slab(gb_ref, 0, q, n_slab, chunk).astype(BF16)
            dwc_ref[q] += lax.dot_general(hq, dy.astype(BF16), (((0,), (0,)), ((), ())), preferred_element_type=F32)
            dwb_ref[q] += lax.dot_general(uq, gq, (((0,), (0,)), ((), ())), preferred_element_type=F32)
            du_ref[:, qs] = (jnp.dot(gq, wbt_ref[q], preferred_element_type=F32) + ds_ref[:, qs] * dy).astype(du_ref.dtype)
            dds_ref[:, qs] += jnp.sum(dy * uq.astype(F32), axis=0, keepdims=True)

    whole = lambda shape: pl.BlockSpec(shape, lambda k: (0,) * len(shape))
    rev = lambda k: (n_chunk - 1 - k, 0)
    return pl.pallas_call(
        body, name=name, grid=(n_chunk,),
        in_specs=[pl.BlockSpec((chunk, DS), rev), pl.BlockSpec((chunk, DS), rev), pl.BlockSpec((chunk, DS), rev),
                  pl.BlockSpec((1, n_slab, 2 * S), lambda k: (n_chunk - 1 - k, 0, 0)),
                  whole(w_b.shape), whole(w_bt.shape), whole(w_ct.shape), whole(a.shape), whole(d_skip.shape)],
        out_specs=[pl.BlockSpec((chunk, DS), rev), whole(w_b.shape), whole(w_bt.shape), whole(a.shape),
                   whole(d_skip.shape)],
        out_shape=[jax.ShapeDtypeStruct((T, DS), BF16), jax.ShapeDtypeStruct(w_b.shape, F32),
                   jax.ShapeDtypeStruct(w_bt.shape, F32), jax.ShapeDtypeStruct(a.shape, F32),
                   jax.ShapeDtypeStruct(d_skip.shape, F32)],
        scratch_shapes=[pltpu.VMEM((2 * STATE_BLOCKS, (chunk + 1) * n_slab, LANES), F32),
                        pltpu.VMEM((2 * STATE_BLOCKS, chunk * n_slab, LANES), F32),
                        pltpu.VMEM((chunk, DS), F32), pltpu.VMEM((n_slab, 2 * S), F32)],
        compiler_params=_params(("arbitrary",)),
    )(zu, dgy, y, hin, w_b, w_bt, w_ct, a, d_skip)


def _ssm_discretise(lam_re, lam_im, log_dt, b_re, b_im):
    dt = jnp.exp(log_dt)[:, None]
    mag = jnp.exp(lam_re * dt)
    a_re = mag * jnp.cos(lam_im * dt)
    a_im = mag * jnp.sin(lam_im * dt)
    den = lam_re * lam_re + lam_im * lam_im
    nr = a_re - 1.0
    z_re = (nr * lam_re + a_im * lam_im) / den
    z_im = (a_im * lam_re - nr * lam_im) / den
    bb_re = z_re[..., None] * b_re - z_im[..., None] * b_im
    bb_im = z_re[..., None] * b_im + z_im[..., None] * b_re
    return a_re, a_im, bb_re, bb_im


def _slab_in(m_re, m_im):
    G, P, C = m_re.shape
    n_slab = G // GROUPS_PER_SLAB
    eye = jnp.eye(GROUPS_PER_SLAB, dtype=m_re.dtype)

    def one(m):
        m = m.reshape(n_slab, GROUPS_PER_SLAB, P, C)
        w = jnp.einsum('sgpc,gh->sgchp', m, eye)
        return w.reshape(n_slab, GROUPS_PER_SLAB * C, GROUPS_PER_SLAB * P)

    return jnp.concatenate([one(m_re), one(m_im)], axis=2)


def _slab_in_grad(dw, G, P, C):
    n_slab = G // GROUPS_PER_SLAB
    eye = jnp.eye(GROUPS_PER_SLAB, dtype=dw.dtype)

    def one(w):
        w = w.reshape(n_slab, GROUPS_PER_SLAB, C, GROUPS_PER_SLAB, P)
        return jnp.einsum('sgchp,gh->sgpc', w, eye).reshape(G, P, C)

    return one(dw[:, :, :SLAB_STATE]), one(dw[:, :, SLAB_STATE:])


def _slab_out(c_re, c_im):
    G, C, P = c_re.shape
    n_slab = G // GROUPS_PER_SLAB
    eye = jnp.eye(GROUPS_PER_SLAB, dtype=c_re.dtype)

    def one(m):
        m = m.reshape(n_slab, GROUPS_PER_SLAB, C, P)
        w = jnp.einsum('sgcp,gh->shpgc', m, eye)
        return w.reshape(n_slab, GROUPS_PER_SLAB * P, GROUPS_PER_SLAB * C)

    return jnp.concatenate([one(c_re), one(-c_im)], axis=1)


def _slab_out_grad(dw, G, C, P):
    n_slab = G // GROUPS_PER_SLAB
    eye = jnp.eye(GROUPS_PER_SLAB, dtype=dw.dtype)

    def one(w):
        w = w.reshape(n_slab, GROUPS_PER_SLAB, P, GROUPS_PER_SLAB, C)
        return jnp.einsum('shpgc,gh->sgcp', w, eye).reshape(G, C, P)

    return one(dw[:, :SLAB_STATE, :]), -one(dw[:, SLAB_STATE:, :])


def _slab_diag(a_re, a_im):
    G, P = a_re.shape
    n_slab = G // GROUPS_PER_SLAB
    return jnp.concatenate([a_re.reshape(n_slab, SLAB_STATE), a_im.reshape(n_slab, SLAB_STATE)], axis=1)


def _merge_fwd(yab, zg, attn, *, name):
    T, D = attn.shape
    tr = _divisor(T, ROW_BLOCK, SUBLANES)

    def body(ya_ref, yb_ref, ga_ref, gb_ref, at_ref, o_ref):
        f = lambda r: r[...].astype(F32)
        ssm = f(ya_ref) * jax.nn.sigmoid(f(yb_ref))
        o_ref[...] = (jax.nn.sigmoid(f(ga_ref)) * ssm + jax.nn.sigmoid(f(gb_ref)) * f(at_ref)).astype(o_ref.dtype)

    lo = pl.BlockSpec((tr, D), lambda i: (i, 0))
    hi = pl.BlockSpec((tr, D), lambda i: (i, 1))
    return pl.pallas_call(
        body, name=name, grid=(T // tr,),
        in_specs=[lo, hi, lo, hi, lo],
        out_specs=lo,
        out_shape=jax.ShapeDtypeStruct((T, D), BF16),
        compiler_params=_params(("parallel",)),
    )(yab, yab, zg, zg, attn)


def _merge_bwd(dm, yab, zg, attn, *, name):
    T, D = attn.shape
    tr = _divisor(T, ROW_BLOCK, SUBLANES)

    def body(dm_ref, ya_ref, yb_ref, ga_ref, gb_ref, at_ref, dg_ref, dat_ref, dy_ref):
        f = lambda r: r[...].astype(F32)
        dmv, ya, at = f(dm_ref), f(ya_ref), f(at_ref)
        sa, sb, syb = jax.nn.sigmoid(f(ga_ref)), jax.nn.sigmoid(f(gb_ref)), jax.nn.sigmoid(f(yb_ref))
        ssm = ya * syb
        dssm = dmv * sa
        dg_ref[:, pl.ds(0, D)] = (dmv * ssm * sa * (1.0 - sa)).astype(dg_ref.dtype)
        dg_ref[:, pl.ds(D, D)] = (dmv * at * sb * (1.0 - sb)).astype(dg_ref.dtype)
        dat_ref[...] = (dmv * sb).astype(dat_ref.dtype)
        dy_ref[:, pl.ds(0, D)] = (dssm * syb).astype(dy_ref.dtype)
        dy_ref[:, pl.ds(D, D)] = (dssm * ya * syb * (1.0 - syb)).astype(dy_ref.dtype)

    lo = pl.BlockSpec((tr, D), lambda i: (i, 0))
    hi = pl.BlockSpec((tr, D), lambda i: (i, 1))
    both = pl.BlockSpec((tr, 2 * D), lambda i: (i, 0))
    return pl.pallas_call(
        body, name=name, grid=(T // tr,),
        in_specs=[lo, lo, hi, lo, hi, lo],
        out_specs=[both, lo, both],
        out_shape=[jax.ShapeDtypeStruct((T, 2 * D), BF16), jax.ShapeDtypeStruct((T, D), BF16),
                   jax.ShapeDtypeStruct((T, 2 * D), BF16)],
        compiler_params=_params(("parallel",)),
    )(dm, yab, yab, zg, zg, attn)


def _conv_taps(g_ref, halo_ref, i, tr):
    g0 = g_ref[...].astype(F32)
    halo = jnp.where(i > 0, halo_ref[...].astype(F32), 0.0)
    row = lax.broadcasted_iota(jnp.int32, g0.shape, 0)
    g1 = jnp.where(row == 0, halo[SUBLANES - 1:SUBLANES, :], pltpu.roll(g0, 1, axis=0))
    g2 = pltpu.roll(g0, 2, axis=0)
    g2 = jnp.where(row == 0, halo[SUBLANES - 2:SUBLANES - 1, :], g2)
    g2 = jnp.where(row == 1, halo[SUBLANES - 1:SUBLANES, :], g2)
    return g0, g1, g2


def _conv_blocks(T, FF):
    tr = _divisor(T, ROW_BLOCK, SUBLANES)
    tc = _divisor(FF, 1024, LANES)
    return tr, tc


def _conv_fwd(gu, conv_w, conv_b, *, name):
    T = gu.shape[0]
    FF = gu.shape[1] // 2
    tr, tc = _conv_blocks(T, FF)
    ncol = FF // tc

    def body(g_ref, halo_ref, u_ref, w_ref, b_ref, o_ref):
        i = pl.program_id(0)
        g0, g1, g2 = _conv_taps(g_ref, halo_ref, i, tr)
        gc = b_ref[...] + w_ref[0:1, :] * g2 + w_ref[1:2, :] * g1 + w_ref[2:3, :] * g0
        o_ref[...] = (gc * jax.nn.sigmoid(gc) * u_ref[...].astype(F32)).astype(o_ref.dtype)

    hb = tr // SUBLANES
    return pl.pallas_call(
        body, name=name, grid=(T // tr, ncol),
        in_specs=[pl.BlockSpec((tr, tc), lambda i, j: (i, j)),
                  pl.BlockSpec((SUBLANES, tc), lambda i, j: (jnp.maximum(i * hb - 1, 0), j)),
                  pl.BlockSpec((tr, tc), lambda i, j: (i, j + ncol)),
                  pl.BlockSpec((SUBLANES, tc), lambda i, j: (0, j)), pl.BlockSpec((1, tc), lambda i, j: (0, j))],
        out_specs=pl.BlockSpec((tr, tc), lambda i, j: (i, j)),
        out_shape=jax.ShapeDtypeStruct((T, FF), BF16),
        compiler_params=_params(("parallel", "parallel")),
    )(gu, gu, gu, conv_w, conv_b)


def _conv_bwd_gate(da, gu, conv_w, conv_b, *, name):
    T = gu.shape[0]
    FF = gu.shape[1] // 2
    tr, tc = _conv_blocks(T, FF)
    ncol = FF // tc

    def body(da_ref, g_ref, halo_ref, u_ref, w_ref, b_ref, dgc_ref, du_ref, s_ref):
        i = pl.program_id(1)

        @pl.when(i == 0)
        def _():
            s_ref[...] = jnp.zeros_like(s_ref)

        g0, g1, g2 = _conv_taps(g_ref, halo_ref, i, tr)
        gc = b_ref[...] + w_ref[0:1, :] * g2 + w_ref[1:2, :] * g1 + w_ref[2:3, :] * g0
        sg = jax.nn.sigmoid(gc)
        dav = da_ref[...].astype(F32)
        du_ref[...] = (dav * gc * sg).astype(du_ref.dtype)
        dgc = dav * u_ref[...].astype(F32) * (sg * (1.0 + gc * (1.0 - sg)))
        dgc_ref[...] = dgc.astype(dgc_ref.dtype)
        s_ref[0:1, :] += jnp.sum(dgc * g2, axis=0, keepdims=True)
        s_ref[1:2, :] += jnp.sum(dgc * g1, axis=0, keepdims=True)
        s_ref[2:3, :] += jnp.sum(dgc * g0, axis=0, keepdims=True)
        s_ref[3:4, :] += jnp.sum(dgc, axis=0, keepdims=True)

    hb = tr // SUBLANES
    blk = pl.BlockSpec((tr, tc), lambda j, i: (i, j))
    return pl.pallas_call(
        body, name=name, grid=(ncol, T // tr),
        in_specs=[blk, blk,
                  pl.BlockSpec((SUBLANES, tc), lambda j, i: (jnp.maximum(i * hb - 1, 0), j)),
                  pl.BlockSpec((tr, tc), lambda j, i: (i, j + ncol)),
                  pl.BlockSpec((SUBLANES, tc), lambda j, i: (0, j)), pl.BlockSpec((1, tc), lambda j, i: (0, j))],
        out_specs=[blk, blk, pl.BlockSpec((SUBLANES, tc), lambda j, i: (0, j))],
        out_shape=[jax.ShapeDtypeStruct((T, FF), BF16), jax.ShapeDtypeStruct((T, FF), BF16),
                   jax.ShapeDtypeStruct((SUBLANES, FF), F32)],
        compiler_params=_params(("parallel", "arbitrary")),
    )(da, gu, gu, gu, conv_w, conv_b)


def _conv_bwd_taps(dgc, conv_w, *, name):
    T, FF = dgc.shape
    tr, tc = _conv_blocks(T, FF)
    ncol = FF // tc
    nrow = T // tr

    def body(d_ref, next_ref, w_ref, o_ref):
        i = pl.program_id(0)
        d0 = d_ref[...].astype(F32)
        nxt = jnp.where(i < nrow - 1, next_ref[...].astype(F32), 0.0)
        row = lax.broadcasted_iota(jnp.int32, d0.shape, 0)
        d1 = jnp.where(row == tr - 1, nxt[0:1, :], pltpu.roll(d0, tr - 1, axis=0))
        d2 = pltpu.roll(d0, tr - 2, axis=0)
        d2 = jnp.where(row == tr - 2, nxt[0:1, :], d2)
        d2 = jnp.where(row == tr - 1, nxt[1:2, :], d2)
        dg = w_ref[2:3, :] * d0 + w_ref[1:2, :] * d1 + w_ref[0:1, :] * d2
        o_ref[...] = dg.astype(o_ref.dtype)

    hb = tr // SUBLANES
    last = T // SUBLANES - 1
    return pl.pallas_call(
        body, name=name, grid=(nrow, ncol),
        in_specs=[pl.BlockSpec((tr, tc), lambda i, j: (i, j)),
                  pl.BlockSpec((SUBLANES, tc), lambda i, j: (jnp.minimum((i + 1) * hb, last), j)),
                  pl.BlockSpec((SUBLANES, tc), lambda i, j: (0, j))],
        out_specs=pl.BlockSpec((tr, tc), lambda i, j: (i, j)),
        out_shape=jax.ShapeDtypeStruct((T, FF), BF16),
        compiler_params=_params(("parallel", "parallel")),
    )(dgc, dgc, conv_w)


def _mesh_pos():
    return lax.axis_index("x"), lax.axis_index("y"), lax.axis_index("c")


def _flip(pos, mask):
    x, y, c = pos
    return (x ^ ((mask >> 2) & 1), y ^ ((mask >> 1) & 1), c ^ (mask & 1))


def _index_of(pos):
    x, y, c = pos
    return 4 * x + 2 * y + c


def _all_gather(x, *, name):
    R, C = x.shape

    def body(x_ref, o_ref, send_sems, recv_sems, local_sem):
        me = _mesh_pos()
        mine = pltpu.make_async_copy(x_ref, o_ref.at[_index_of(me)], local_sem)
        mine.start()
        copies = []
        for k in range(1, N_DEV):
            cp = pltpu.make_async_remote_copy(
                src_ref=x_ref, dst_ref=o_ref.at[_index_of(me)],
                send_sem=send_sems.at[k - 1], recv_sem=recv_sems.at[k - 1],
                device_id=_flip(me, k), device_id_type=pl.DeviceIdType.MESH)
            cp.start()
            copies.append(cp)
        for k in range(1, N_DEV):
            src = _index_of(_flip(me, k))
            pltpu.make_async_remote_copy(
                src_ref=x_ref, dst_ref=o_ref.at[src],
                send_sem=send_sems.at[k - 1], recv_sem=recv_sems.at[k - 1],
                device_id=_flip(me, k), device_id_type=pl.DeviceIdType.MESH).wait_recv()
        for cp in copies:
            cp.wait_send()
        mine.wait()

    return pl.pallas_call(
        body, name=name,
        in_specs=[pl.BlockSpec(memory_space=pl.ANY)],
        out_specs=pl.BlockSpec(memory_space=pl.ANY),
        out_shape=jax.ShapeDtypeStruct((N_DEV, R, C), x.dtype),
        scratch_shapes=[pltpu.SemaphoreType.DMA((N_DEV - 1,)), pltpu.SemaphoreType.DMA((N_DEV - 1,)),
                        pltpu.SemaphoreType.DMA],
        compiler_params=pltpu.CompilerParams(has_side_effects=True),
    )(x)


def _exchange(parts, *, name):
    _, R, C = parts.shape

    def body(p_ref, o_ref, send_sems, recv_sems, local_sem):
        me = _mesh_pos()
        my = _index_of(me)
        mine = pltpu.make_async_copy(p_ref.at[my], o_ref.at[my], local_sem)
        mine.start()
        copies = []
        for k in range(1, N_DEV):
            peer = _flip(me, k)
            cp = pltpu.make_async_remote_copy(
                src_ref=p_ref.at[_index_of(peer)], dst_ref=o_ref.at[my],
                send_sem=send_sems.at[k - 1], recv_sem=recv_sems.at[k - 1],
                device_id=peer, device_id_type=pl.DeviceIdType.MESH)
            cp.start()
            copies.append(cp)
        for k in range(1, N_DEV):
            peer = _flip(me, k)
            pltpu.make_async_remote_copy(
                src_ref=p_ref.at[my], dst_ref=o_ref.at[_index_of(peer)],
                send_sem=send_sems.at[k - 1], recv_sem=recv_sems.at[k - 1],
                device_id=peer, device_id_type=pl.DeviceIdType.MESH).wait_recv()
        for cp in copies:
            cp.wait_send()
        mine.wait()

    return pl.pallas_call(
        body, name=name,
        in_specs=[pl.BlockSpec(memory_space=pl.ANY)],
        out_specs=pl.BlockSpec(memory_space=pl.ANY),
        out_shape=jax.ShapeDtypeStruct(parts.shape, parts.dtype),
        scratch_shapes=[pltpu.SemaphoreType.DMA((N_DEV - 1,)), pltpu.SemaphoreType.DMA((N_DEV - 1,)),
                        pltpu.SemaphoreType.DMA],
        compiler_params=pltpu.CompilerParams(has_side_effects=True),
    )(parts)


def _sum_adamw(parts, w, m, v, *, name):
    R, C = w.shape
    tr = _divisor(R, max(2 * SUBLANES, ADAMW_BLOCK_ELEMS // C // SUBLANES * SUBLANES), 2 * SUBLANES)
    c1 = 1.0 - ADAM_B1 ** ADAM_STEP
    c2 = 1.0 - ADAM_B2 ** ADAM_STEP

    def body(p_ref, w_ref, m_ref, v_ref, g_ref, d_ref, nm_ref, nv_ref):
        g = p_ref[0].astype(F32)
        for s in range(1, N_DEV):
            g = g + p_ref[s].astype(F32)
        nm = ADAM_B1 * m_ref[...] + (1.0 - ADAM_B1) * g
        nv = ADAM_B2 * v_ref[...] + (1.0 - ADAM_B2) * (g * g)
        g_ref[...] = g
        nm_ref[...] = nm
        nv_ref[...] = nv
        d_ref[...] = -ADAM_LR * ((nm / c1) / (jnp.sqrt(nv / c2) + ADAM_EPS) + ADAM_WD * w_ref[...])

    blk = pl.BlockSpec((tr, C), lambda i: (i, 0))
    return pl.pallas_call(
        body, name=name, grid=(R // tr,),
        in_specs=[pl.BlockSpec((N_DEV, tr, C), lambda i: (0, i, 0)), blk, blk, blk],
        out_specs=[blk] * 4,
        out_shape=[jax.ShapeDtypeStruct((R, C), F32)] * 4,
        compiler_params=_params(("parallel",)),
    )(parts, w, m, v)


def _pad2(a, rows, cols):
    return jnp.pad(a, ((0, rows - a.shape[0]), (0, cols - a.shape[1])))


def _gather_cols(w, dtype, *, name):
    R, c = w.shape
    rp, cp = _round_up(R, 2 * SUBLANES), _round_up(c, LANES)
    g = _all_gather(_pad2(w.astype(dtype), rp, cp), name=name)
    return jnp.transpose(g[:, :R, :c], (1, 0, 2)).reshape(R, N_DEV * c)


def _gather_rows(w, dtype, *, name):
    r, C = w.shape
    rp, cp = _round_up(r, 2 * SUBLANES), _round_up(C, LANES)
    g = _all_gather(_pad2(w.astype(dtype), rp, cp), name=name)
    return g[:, :r, :C].reshape(N_DEV * r, C)


def _update_cols(dw, w, m, v, *, name):
    R, c = w.shape
    rp, cp = _round_up(R, 2 * SUBLANES), _round_up(c, LANES)
    parts = jnp.transpose(dw.reshape(R, N_DEV, c), (1, 0, 2))
    parts = jnp.pad(parts, ((0, 0), (0, rp - R), (0, cp - c)))
    got = _exchange(parts, name=name + "_exchange")
    outs = _sum_adamw(got, _pad2(w, rp, cp), _pad2(m, rp, cp), _pad2(v, rp, cp), name=name + "_adamw")
    return [o[:R, :c] for o in outs]


def _update_rows(dw, w, m, v, *, name):
    r, C = w.shape
    rp, cp = _round_up(r, 2 * SUBLANES), _round_up(C, LANES)
    parts = jnp.pad(dw.reshape(N_DEV, r, C), ((0, 0), (0, rp - r), (0, cp - C)))
    got = _exchange(parts, name=name + "_exchange")
    outs = _sum_adamw(got, _pad2(w, rp, cp), _pad2(m, rp, cp), _pad2(v, rp, cp), name=name + "_adamw")
    return [o[:r, :C] for o in outs]


def _update_replicated(grads, ws, ms, vs, *, name):
    sizes = [int(g.size) for g in grads]
    total = sum(sizes)
    rows = _round_up(-(-total // LANES), 2 * SUBLANES)

    def pack(arrs):
        flat = jnp.concatenate([a.reshape(-1).astype(F32) for a in arrs])
        return jnp.pad(flat, (0, rows * LANES - total)).reshape(rows, LANES)

    got = _all_gather(pack(grads), name=name + "_gather")
    outs = _sum_adamw(got, pack(ws), pack(ms), pack(vs), name=name + "_adamw")
    result = []
    for o in outs:
        flat = o.reshape(-1)
        arrs, off = [], 0
        for w, n in zip(ws, sizes):
            arrs.append(flat[off:off + n].reshape(w.shape))
            off += n
        result.append(arrs)
    return result


def kernel(x, meta, g_mix, w_in, b_f, lam_re, lam_im, log_dt, b_re, b_im, c_re, c_im, d_skip, w_glu, w_attn_o, w_out, g_ffn, w_up, conv_w, conv_b, w_down, g_final, loss_target, m_meta, m_g_mix, m_w_in, m_b_f, m_lam_re, m_lam_im, m_log_dt, m_b_re, m_b_im, m_c_re, m_c_im, m_d_skip, m_w_glu, m_w_attn_o, m_w_out, m_g_ffn, m_w_up, m_conv_w, m_conv_b, m_w_down, m_g_final, v_meta, v_g_mix, v_w_in, v_b_f, v_lam_re, v_lam_im, v_log_dt, v_b_re, v_b_im, v_c_re, v_c_im, v_d_skip, v_w_glu, v_w_attn_o, v_w_out, v_g_ffn, v_w_up, v_conv_w, v_conv_b, v_w_down, v_g_final):
    seq, D = x.shape[1], x.shape[2]
    L = N_META + seq
    T = _round_up(L, SEQ_BLOCK) if L > SEQ_BLOCK else _round_up(L, LANES)
    DS = d_skip.shape[1]
    H = b_f.shape[1]
    DA = H * HEAD_DIM
    FF = conv_b.shape[1]
    G, P, C = b_re.shape[1:]

    meta_full = _gather_cols(meta, F32, name="gather_meta")
    w_in_full = _gather_cols(w_in[0], BF16, name="gather_w_in")
    w_glu_full = _gather_cols(w_glu[0], BF16, name="gather_w_glu")
    w_ao_full = _gather_cols(w_attn_o[0], BF16, name="gather_w_attn_o")
    w_out_full = _gather_rows(w_out[0], BF16, name="gather_w_out")
    w_up_full = _gather_cols(w_up[0], BF16, name="gather_w_up")
    conv_w_full = _gather_cols(conv_w[0], F32, name="gather_conv_w")
    w_down_full = _gather_rows(w_down[0], BF16, name="gather_w_down")

    o_f, o_u, o_g = 3 * DA, 3 * DA + H, 3 * DA + H + DS
    w_qkv = w_in_full[:, :o_f]
    w_f = jnp.pad(w_in_full[:, o_f:o_u], ((0, 0), (0, LANES - H)))
    w_u = w_in_full[:, o_u:o_g]
    w_g = w_in_full[:, o_g:]
    w_main_t = jnp.concatenate([w_qkv, w_u, w_g], axis=1).T
    conv_w8 = jnp.pad(conv_w_full, ((0, SUBLANES - CONV_WIDTH), (0, 0)))

    a_re, a_im, bb_re, bb_im = _ssm_discretise(lam_re[0], lam_im[0], log_dt[0], b_re[0], b_im[0])
    w_b = _slab_in(bb_re, bb_im)
    w_c = _slab_out(c_re[0], c_im[0])
    a_slab = _slab_diag(a_re, a_im)
    w_b16, w_c16 = w_b.astype(BF16), w_c.astype(BF16)
    w_bt16, w_ct16 = jnp.swapaxes(w_b16, 1, 2), jnp.swapaxes(w_c16, 1, 2)

    h0 = jnp.concatenate([meta_full, x[0], jnp.zeros((T - L, D), F32)], axis=0)
    target = jnp.pad(loss_target[0], ((N_META, T - L), (0, 0)))
    n1 = _rms_fwd(h0, g_mix, name="rms_mix")
    zqkv = _matmul(n1, w_qkv, name="mm_qkv")
    zu = _matmul(n1, w_u, name="mm_u")
    zg = _matmul(n1, w_g, name="mm_gates")
    zf = _matmul(n1, w_f, name="mm_forget", out_dtype=F32)
    f_t = zf[:, :H].T
    b_col = b_f.reshape(H, 1)
    fcum = _forget_cumsum(f_t, b_col, name="forget_cumsum")
    fcol = fcum.T
    o, lse = _attn_fwd(zqkv, fcum, fcol, name="attn_fwd")
    attn = _matmul(o, w_ao_full, name="mm_attn_o")
    y, gy, hin = _ssm_fwd(zu, w_b16, w_c16, a_slab, d_skip, name="ssm_fwd")
    yab = _matmul(gy, w_glu_full, name="mm_glu")
    merged = _merge_fwd(yab, zg, attn, name="merge_fwd")
    h1 = _matmul(merged, w_out_full, name="mm_out", out_dtype=F32, residual=h0)
    n2 = _rms_fwd(h1, g_ffn, name="rms_ffn")
    gu = _matmul(n2, w_up_full, name="mm_up")
    act = _conv_fwd(gu, conv_w8, conv_b, name="conv_fwd")
    h2 = _matmul(act, w_down_full, name="mm_down", out_dtype=F32, residual=h1)

    dh2, sq, dg_final = _final_loss(h2, g_final.reshape(1, D), target, seq, name="final_loss")
    loss = lax.psum(0.5 * sq[0, 0] / D, ("x", "y", "c"))
    dh2_16 = dh2.astype(BF16)
    d_act = _matmul(dh2_16, w_down_full.T, name="mm_down_dx")
    dw_down = _matmul(act, dh2_16, name="mm_down_dw", trans_a=True)
    dgc, du2, conv_sums = _conv_bwd_gate(d_act, gu, conv_w8, conv_b, name="conv_bwd_gate")
    dgu = jnp.concatenate([_conv_bwd_taps(dgc, conv_w8, name="conv_bwd_taps"), du2], axis=1)
    dn2 = _matmul(dgu, w_up_full.T, name="mm_up_dx")
    dw_up = _matmul(n2, dgu, name="mm_up_dw", trans_a=True)
    dh1, dg_ffn = _rms_bwd(dn2, h1, g_ffn, dh2, name="rms_ffn_bwd")
    dh1_16 = dh1.astype(BF16)
    dmerged = _matmul(dh1_16, w_out_full.T, name="mm_out_dx")
    dw_out = _matmul(merged, dh1_16, name="mm_out_dw", trans_a=True)
    dzg, dattn, dyab = _merge_bwd(dmerged, yab, zg, attn, name="merge_bwd")
    do = _matmul(dattn, w_ao_full.T, name="mm_attn_o_dx")
    dw_ao = _matmul(o, dattn, name="mm_attn_o_dw", trans_a=True)
    dgy = _matmul(dyab, w_glu_full.T, name="mm_glu_dx")
    dw_glu = _matmul(gy, dyab, name="mm_glu_dw", trans_a=True)
    dzu, dw_b, dw_c, da_slab, dd_skip = _ssm_bwd(zu, dgy, y, hin, w_b16, w_bt16, w_ct16, a_slab, d_skip,
                                                 name="ssm_bwd")
    delta = _attn_delta(zqkv, fcum, fcol, do, lse, name="attn_delta")
    dq = _attn_bwd_dq(zqkv, fcum, fcol, do, lse, delta, name="attn_bwd_dq")
    dk, dv, dfcum = _attn_bwd_dkv(zqkv, fcum, fcol, do, lse, delta, name="attn_bwd_dkv")
    df_t, db_f = _forget_bwd(dfcum, f_t, b_col, name="forget_bwd")
    dzf = jnp.pad(df_t.T, ((0, 0), (0, LANES - H))).astype(BF16)
    dz_main = jnp.concatenate([dq, dk, dv, dzu, dzg], axis=1)
    dn1 = _matmul(dzf, w_f.T, name="mm_forget_dx", out_dtype=F32)
    dn1 = _matmul(dz_main, w_main_t, name="mm_in_dx", out_dtype=F32, residual=dn1)
    dw_main = _matmul(n1, dz_main, name="mm_in_dw", trans_a=True)
    dw_f = _matmul(n1, dzf, name="mm_forget_dw", trans_a=True)
    dw_in = jnp.concatenate([dw_main[:, :o_f], dw_f[:, :H], dw_main[:, o_f:]], axis=1)
    dh0, dg_mix = _rms_bwd(dn1, h0, g_mix, dh1, name="rms_mix_bwd")
    grad_x = dh0[N_META:L][None]

    dbb_re, dbb_im = _slab_in_grad(dw_b, G, P, C)
    dc_re, dc_im = _slab_out_grad(dw_c, G, C, P)
    da_re = da_slab[:, :SLAB_STATE].reshape(G, P)
    da_im = da_slab[:, SLAB_STATE:].reshape(G, P)
    _, disc_vjp = jax.vjp(_ssm_discretise, lam_re[0], lam_im[0], log_dt[0], b_re[0], b_im[0])
    dlam_re, dlam_im, dlog_dt, db_re, db_im = disc_vjp((da_re, da_im, dbb_re, dbb_im))

    big = {}
    big["meta"] = _update_cols(dh0[:N_META], meta, m_meta, v_meta, name="meta")
    big["w_in"] = _update_cols(dw_in, w_in[0], m_w_in[0], v_w_in[0], name="w_in")
    big["w_glu"] = _update_cols(dw_glu, w_glu[0], m_w_glu[0], v_w_glu[0], name="w_glu")
    big["w_attn_o"] = _update_cols(dw_ao, w_attn_o[0], m_w_attn_o[0], v_w_attn_o[0], name="w_attn_o")
    big["w_out"] = _update_rows(dw_out, w_out[0], m_w_out[0], v_w_out[0], name="w_out")
    big["w_up"] = _update_cols(dw_up, w_up[0], m_w_up[0], v_w_up[0], name="w_up")
    big["conv_w"] = _update_cols(conv_sums[:CONV_WIDTH], conv_w[0], m_conv_w[0], v_conv_w[0], name="conv_w")
    big["w_down"] = _update_rows(dw_down, w_down[0], m_w_down[0], v_w_down[0], name="w_down")

    rep_names = ["g_mix", "b_f", "lam_re", "lam_im", "log_dt", "b_re", "b_im", "c_re", "c_im", "d_skip", "g_ffn",
                 "conv_b", "g_final"]
    rep_w = [g_mix, b_f, lam_re, lam_im, log_dt, b_re, b_im, c_re, c_im, d_skip, g_ffn, conv_b, g_final]
    rep_m = [m_g_mix, m_b_f, m_lam_re, m_lam_im, m_log_dt, m_b_re, m_b_im, m_c_re, m_c_im, m_d_skip, m_g_ffn,
             m_conv_b, m_g_final]
    rep_v = [v_g_mix, v_b_f, v_lam_re, v_lam_im, v_log_dt, v_b_re, v_b_im, v_c_re, v_c_im, v_d_skip, v_g_ffn,
             v_conv_b, v_g_final]
    rep_g = [dg_mix, db_f[:, 0], dlam_re, dlam_im, dlog_dt, db_re, db_im, dc_re, dc_im, dd_skip, dg_ffn,
             conv_sums[CONV_WIDTH], dg_final]
    rep = _update_replicated(rep_g, rep_w, rep_m, rep_v, name="replicated")
    rep_out = {n: [rep[k][i] for k in range(4)] for i, n in enumerate(rep_names)}

    order = ["meta", "g_mix", "w_in", "b_f", "lam_re", "lam_im", "log_dt", "b_re", "b_im", "c_re", "c_im", "d_skip",
             "w_glu", "w_attn_o", "w_out", "g_ffn", "w_up", "conv_w", "conv_b", "w_down", "g_final"]
    outs = [loss, grad_x]
    for kind in range(4):
        for n in order:
            if n in big:
                outs.append(big[n][kind] if n == "meta" else big[n][kind][None])
            else:
                outs.append(rep_out[n][kind])
    return tuple(outs)
```

```python
import functools
import math

import jax
import jax.numpy as jnp
from jax import lax
from jax.experimental import pallas as pl
from jax.experimental.pallas import tpu as pltpu

F32 = jnp.float32
BF16 = jnp.bfloat16

N_META = 16
EPS = 1e-6
HEAD_DIM = 128
SSM_GROUP = 16
SSM_STATE = 64
GROUPS_PER_SLAB = 8
SLAB_STATE = GROUPS_PER_SLAB * SSM_STATE
CONV_WIDTH = 3
N_DEV = 8

ADAM_LR = 0.001
ADAM_B1 = 0.9
ADAM_B2 = 0.999
ADAM_EPS = 1e-08
ADAM_WD = 0.01
ADAM_STEP = 10

LANES = 128
SUBLANES = 8
VMEM_LIMIT = 52 * 1024 * 1024

SEQ_BLOCK = 768
ATT_BLOCK = 384
ROW_BLOCK = 256
SSM_CHUNK = 128
ADAMW_BLOCK_ELEMS = 1 << 17
MASK_VALUE = -1e30


def _round_up(n, m):
    return (n + m - 1) // m * m


def _divisor(n, target, mult):
    if n <= target:
        return n
    best = None
    for d in range(mult, target + 1, mult):
        if n % d == 0:
            best = d
    assert best is not None, (n, target, mult)
    return best


def _params(sem):
    return pltpu.CompilerParams(dimension_semantics=sem, vmem_limit_bytes=VMEM_LIMIT)


def _matmul(a, b, *, name, trans_a=False, trans_b=False, out_dtype=None, residual=None, tm=768, tn=1024, tk=2048):
    out_dtype = BF16 if out_dtype is None else out_dtype
    assert not (trans_a and trans_b)
    if trans_a:
        K, M = a.shape
    else:
        M, K = a.shape
    if trans_b:
        N, K2 = b.shape
    else:
        K2, N = b.shape
    assert K == K2, (a.shape, b.shape)
    tm = _divisor(M, tm, LANES if trans_a else SUBLANES)
    tn = _divisor(N, tn, LANES)
    tk = _divisor(K, tk, LANES if not trans_a else SUBLANES)
    nk = K // tk

    def body(*refs):
        if residual is None:
            a_ref, b_ref, o_ref, acc_ref = refs
        else:
            a_ref, b_ref, r_ref, o_ref, acc_ref = refs
        k = pl.program_id(2)

        @pl.when(k == 0)
        def _():
            acc_ref[...] = jnp.zeros_like(acc_ref)

        contract = (0, 0) if trans_a else (1, 1) if trans_b else (1, 0)
        acc_ref[...] += lax.dot_general(a_ref[...], b_ref[...], (((contract[0],), (contract[1],)), ((), ())),
                                        preferred_element_type=F32)

        @pl.when(k == nk - 1)
        def _():
            r = acc_ref[...]
            if residual is not None:
                r = r + r_ref[...]
            o_ref[...] = r.astype(o_ref.dtype)

    if trans_a:
        a_spec = pl.BlockSpec((tk, tm), lambda i, j, k: (k, i))
    else:
        a_spec = pl.BlockSpec((tm, tk), lambda i, j, k: (i, k))
    if trans_b:
        b_spec = pl.BlockSpec((tn, tk), lambda i, j, k: (j, k))
    else:
        b_spec = pl.BlockSpec((tk, tn), lambda i, j, k: (k, j))
    in_specs = [a_spec, b_spec]
    args = [a, b]
    if residual is not None:
        in_specs.append(pl.BlockSpec((tm, tn), lambda i, j, k: (i, j)))
        args.append(residual)
    return pl.pallas_call(
        body, name=name,
        grid=(M // tm, N // tn, nk),
        in_specs=in_specs,
        out_specs=pl.BlockSpec((tm, tn), lambda i, j, k: (i, j)),
        out_shape=jax.ShapeDtypeStruct((M, N), out_dtype),
        scratch_shapes=[pltpu.VMEM((tm, tn), F32)],
        compiler_params=_params(("parallel", "parallel", "arbitrary")),
    )(*args)


def _rms_fwd(h, g, *, name):
    T, D = h.shape
    tr = _divisor(T, ROW_BLOCK, SUBLANES)

    def body(h_ref, g_ref, o_ref):
        x = h_ref[...]
        r = lax.rsqrt(jnp.mean(x * x, axis=-1, keepdims=True) + EPS)
        o_ref[...] = (x * r * g_ref[...]).astype(o_ref.dtype)

    return pl.pallas_call(
        body, name=name, grid=(T // tr,),
        in_specs=[pl.BlockSpec((tr, D), lambda i: (i, 0)), pl.BlockSpec((1, D), lambda i: (0, 0))],
        out_specs=pl.BlockSpec((tr, D), lambda i: (i, 0)),
        out_shape=jax.ShapeDtypeStruct((T, D), BF16),
        compiler_params=_params(("parallel",)),
    )(h, g)


def _rms_bwd(dn, h, g, dres, *, name):
    T, D = h.shape
    tr = _divisor(T, ROW_BLOCK, SUBLANES)

    def body(dn_ref, h_ref, g_ref, dres_ref, dh_ref, dg_ref):
        i = pl.program_id(0)

        @pl.when(i == 0)
        def _():
            dg_ref[...] = jnp.zeros_like(dg_ref)

        x = h_ref[...]
        dn_v = dn_ref[...].astype(F32)
        r = lax.rsqrt(jnp.mean(x * x, axis=-1, keepdims=True) + EPS)
        xh = x * r
        dg_ref[...] += jnp.sum(dn_v * xh, axis=0, keepdims=True)
        dxh = dn_v * g_ref[...]
        dh_ref[...] = dres_ref[...] + r * (dxh - xh * jnp.mean(dxh * xh, axis=-1, keepdims=True))

    return pl.pallas_call(
        body, name=name, grid=(T // tr,),
        in_specs=[pl.BlockSpec((tr, D), lambda i: (i, 0)), pl.BlockSpec((tr, D), lambda i: (i, 0)),
                  pl.BlockSpec((1, D), lambda i: (0, 0)), pl.BlockSpec((tr, D), lambda i: (i, 0))],
        out_specs=[pl.BlockSpec((tr, D), lambda i: (i, 0)), pl.BlockSpec((1, D), lambda i: (0, 0))],
        out_shape=[jax.ShapeDtypeStruct((T, D), F32), jax.ShapeDtypeStruct((1, D), F32)],
        compiler_params=_params(("arbitrary",)),
    )(dn, h, g, dres)


def _final_loss(h, g, target, n_valid, *, name):
    T, D = h.shape
    tr = _divisor(T, ROW_BLOCK, SUBLANES)

    def body(h_ref, g_ref, t_ref, dh_ref, sq_ref, dg_ref):
        i = pl.program_id(0)

        @pl.when(i == 0)
        def _():
            sq_ref[...] = jnp.zeros_like(sq_ref)
            dg_ref[...] = jnp.zeros_like(dg_ref)

        x = h_ref[...]
        r = lax.rsqrt(jnp.mean(x * x, axis=-1, keepdims=True) + EPS)
        xh = x * r
        gv = g_ref[...]
        row = i * tr + lax.broadcasted_iota(jnp.int32, (tr, 1), 0)
        valid = (row >= N_META) & (row < N_META + n_valid)
        err = jnp.where(valid, xh * gv - t_ref[...], 0.0)
        sq_ref[...] += jnp.sum(err * err)
        dy = err * (1.0 / D)
        dg_ref[...] += jnp.sum(dy * xh, axis=0, keepdims=True)
        dxh = dy * gv
        dh_ref[...] = r * (dxh - xh * jnp.mean(dxh * xh, axis=-1, keepdims=True))

    return pl.pallas_call(
        body, name=name, grid=(T // tr,),
        in_specs=[pl.BlockSpec((tr, D), lambda i: (i, 0)), pl.BlockSpec((1, D), lambda i: (0, 0)),
                  pl.BlockSpec((tr, D), lambda i: (i, 0))],
        out_specs=[pl.BlockSpec((tr, D), lambda i: (i, 0)), pl.BlockSpec((SUBLANES, LANES), lambda i: (0, 0)),
                   pl.BlockSpec((1, D), lambda i: (0, 0))],
        out_shape=[jax.ShapeDtypeStruct((T, D), F32), jax.ShapeDtypeStruct((SUBLANES, LANES), F32),
                   jax.ShapeDtypeStruct((1, D), F32)],
        compiler_params=_params(("arbitrary",)),
    )(h, g, target)


def _prefix_sum_lanes(x):
    lane = lax.broadcasted_iota(jnp.int32, x.shape, 1)
    d = 1
    while d < LANES:
        x = x + jnp.where(lane >= d, pltpu.roll(x, d, axis=1), 0.0)
        d *= 2
    return x


def _forget_cumsum(ft, bf, *, name):
    H, T = ft.shape
    nb = T // LANES

    def body(f_ref, b_ref, o_ref):
        carry = jnp.zeros((H, 1), F32)
        for j in range(nb):
            sl = pl.ds(j * LANES, LANES)
            lf = jax.nn.log_sigmoid(f_ref[:, sl] + b_ref[...])
            c = _prefix_sum_lanes(lf) + carry
            o_ref[:, sl] = c
            carry = c[:, LANES - 1:LANES]

    return pl.pallas_call(
        body, name=name,
        in_specs=[pl.BlockSpec(memory_space=pltpu.VMEM), pl.BlockSpec(memory_space=pltpu.VMEM)],
        out_specs=pl.BlockSpec(memory_space=pltpu.VMEM),
        out_shape=jax.ShapeDtypeStruct((H, T), F32),
        compiler_params=pltpu.CompilerParams(vmem_limit_bytes=VMEM_LIMIT),
    )(ft, bf)


def _forget_bwd(dF, ft, bf, *, name):
    H, T = ft.shape
    nb = T // LANES

    def body(d_ref, f_ref, b_ref, o_ref, s_ref):
        carry = jnp.zeros((H, 1), F32)
        acc = jnp.zeros((H, LANES), F32)
        for j in reversed(range(nb)):
            sl = pl.ds(j * LANES, LANES)
            d = d_ref[:, sl]
            pre = _prefix_sum_lanes(d)
            tot = pre[:, LANES - 1:LANES]
            dlf = tot - pre + d + carry
            carry = carry + tot
            z = f_ref[:, sl] + b_ref[...]
            df = dlf * jax.nn.sigmoid(-z)
            o_ref[:, sl] = df
            acc = acc + df
        s_ref[...] = jnp.broadcast_to(jnp.sum(acc, axis=1, keepdims=True), (H, LANES))

    return pl.pallas_call(
        body, name=name,
        in_specs=[pl.BlockSpec(memory_space=pltpu.VMEM)] * 3,
        out_specs=[pl.BlockSpec(memory_space=pltpu.VMEM)] * 2,
        out_shape=[jax.ShapeDtypeStruct((H, T), F32), jax.ShapeDtypeStruct((H, LANES), F32)],
        compiler_params=pltpu.CompilerParams(vmem_limit_bytes=VMEM_LIMIT),
    )(dF, ft, bf)


def _scores(q_ref, k_ref, f_ref, fq_ref, h, blk, scale, diagonal):
    hs = pl.ds(h * HEAD_DIM, HEAD_DIM)
    s = lax.dot_general(q_ref[:, hs], k_ref[:, hs], (((1,), (1,)), ((), ())), preferred_element_type=F32)
    s = s * scale + (fq_ref[:, h:h + 1] - f_ref[h:h + 1, :])
    if diagonal:
        rows = lax.broadcasted_iota(jnp.int32, (blk, blk), 0)
        cols = lax.broadcasted_iota(jnp.int32, (blk, blk), 1)
        s = jnp.where(cols <= rows, s, MASK_VALUE)
    return s


def _causal_blocks(i, j, compute):
    @pl.when(j < i)
    def _():
        compute(False)

    @pl.when(j == i)
    def _():
        compute(True)


def _attn_fwd(zqkv, fcum, fcol, *, name):
    T = zqkv.shape[0]
    DA = zqkv.shape[1] // 3
    H = DA // HEAD_DIM
    blk = _divisor(T, ATT_BLOCK, LANES)
    nb = T // blk
    scale = HEAD_DIM ** -0.5

    def body(q_ref, k_ref, v_ref, f_ref, fq_ref, o_ref, lse_ref, m_ref, l_ref, acc_ref):
        i = pl.program_id(0)
        j = pl.program_id(1)

        @pl.when(j == 0)
        def _():
            m_ref[...] = jnp.full_like(m_ref, MASK_VALUE)
            l_ref[...] = jnp.zeros_like(l_ref)
            acc_ref[...] = jnp.zeros_like(acc_ref)

        def compute(diagonal):
            for h in range(H):
                hs = pl.ds(h * HEAD_DIM, HEAD_DIM)
                s = _scores(q_ref, k_ref, f_ref, fq_ref, h, blk, scale, diagonal)
                m_prev = m_ref[h]
                m_new = jnp.maximum(m_prev, jnp.max(s, axis=1, keepdims=True))
                alpha = jnp.exp(m_prev - m_new)
                p = jnp.exp(s - m_new)
                l_ref[h] = alpha * l_ref[h] + jnp.sum(p, axis=1, keepdims=True)
                acc_ref[:, hs] = alpha * acc_ref[:, hs] + jnp.dot(p.astype(BF16), v_ref[:, hs],
                                                                  preferred_element_type=F32)
                m_ref[h] = m_new

        _causal_blocks(i, j, compute)

        @pl.when(j == nb - 1)
        def _():
            for h in range(H):
                hs = pl.ds(h * HEAD_DIM, HEAD_DIM)
                l = l_ref[h]
                o_ref[:, hs] = (acc_ref[:, hs] / l).astype(o_ref.dtype)
                lse_ref[:, h:h + 1] = m_ref[h] + jnp.log(l)

    kv = lambda c: (lambda i, j: (jnp.minimum(j, i), c))
    return pl.pallas_call(
        body, name=name, grid=(nb, nb),
        in_specs=[pl.BlockSpec((blk, DA), lambda i, j: (i, 0)),
                  pl.BlockSpec((blk, DA), kv(1)), pl.BlockSpec((blk, DA), kv(2)),
                  pl.BlockSpec((H, blk), lambda i, j: (0, jnp.minimum(j, i))),
                  pl.BlockSpec((blk, H), lambda i, j: (i, 0))],
        out_specs=[pl.BlockSpec((blk, DA), lambda i, j: (i, 0)), pl.BlockSpec((blk, H), lambda i, j: (i, 0))],
        out_shape=[jax.ShapeDtypeStruct((T, DA), BF16), jax.ShapeDtypeStruct((T, H), F32)],
        scratch_shapes=[pltpu.VMEM((H, blk, 1), F32), pltpu.VMEM((H, blk, 1), F32), pltpu.VMEM((blk, DA), F32)],
        compiler_params=_params(("parallel", "arbitrary")),
    )(zqkv, zqkv, zqkv, fcum, fcol)


def _attn_delta(zqkv, fcum, fcol, do, lse, *, name):
    T = zqkv.shape[0]
    DA = zqkv.shape[1] // 3
    H = DA // HEAD_DIM
    blk = _divisor(T, ATT_BLOCK, LANES)
    nb = T // blk
    scale = HEAD_DIM ** -0.5

    def body(q_ref, k_ref, v_ref, f_ref, fq_ref, do_ref, lse_ref, d_ref):
        i = pl.program_id(0)
        j = pl.program_id(1)

        @pl.when(j == 0)
        def _():
            d_ref[...] = jnp.zeros_like(d_ref)

        def compute(diagonal):
            for h in range(H):
                hs = pl.ds(h * HEAD_DIM, HEAD_DIM)
                s = _scores(q_ref, k_ref, f_ref, fq_ref, h, blk, scale, diagonal)
                p = jnp.exp(s - lse_ref[:, h:h + 1])
                dp = lax.dot_general(do_ref[:, hs], v_ref[:, hs], (((1,), (1,)), ((), ())),
                                     preferred_element_type=F32)
                d_ref[:, h:h + 1] += jnp.sum(p * dp, axis=1, keepdims=True)

        _causal_blocks(i, j, compute)

    kv = lambda c: (lambda i, j: (jnp.minimum(j, i), c))
    row = lambda i, j: (i, 0)
    return pl.pallas_call(
        body, name=name, grid=(nb, nb),
        in_specs=[pl.BlockSpec((blk, DA), row), pl.BlockSpec((blk, DA), kv(1)), pl.BlockSpec((blk, DA), kv(2)),
                  pl.BlockSpec((H, blk), lambda i, j: (0, jnp.minimum(j, i))), pl.BlockSpec((blk, H), row),
                  pl.BlockSpec((blk, DA), row), pl.BlockSpec((blk, H), row)],
        out_specs=pl.BlockSpec((blk, H), row),
        out_shape=jax.ShapeDtypeStruct((T, H), F32),
        compiler_params=_params(("parallel", "arbitrary")),
    )(zqkv, zqkv, zqkv, fcum, fcol, do, lse)


def _attn_bwd_dq(zqkv, fcum, fcol, do, lse, delta, *, name):
    T = zqkv.shape[0]
    DA = zqkv.shape[1] // 3
    H = DA // HEAD_DIM
    blk = _divisor(T, ATT_BLOCK, LANES)
    nb = T // blk
    scale = HEAD_DIM ** -0.5

    def body(q_ref, k_ref, v_ref, f_ref, fq_ref, do_ref, lse_ref, dl_ref, dq_ref, acc_ref):
        i = pl.program_id(0)
        j = pl.program_id(1)

        @pl.when(j == 0)
        def _():
            acc_ref[...] = jnp.zeros_like(acc_ref)

        def compute(diagonal):
            for h in range(H):
                hs = pl.ds(h * HEAD_DIM, HEAD_DIM)
                s = _scores(q_ref, k_ref, f_ref, fq_ref, h, blk, scale, diagonal)
                p = jnp.exp(s - lse_ref[:, h:h + 1])
                dp = lax.dot_general(do_ref[:, hs], v_ref[:, hs], (((1,), (1,)), ((), ())),
                                     preferred_element_type=F32)
                ds = p * (dp - dl_ref[:, h:h + 1])
                acc_ref[:, hs] += scale * jnp.dot(ds.astype(BF16), k_ref[:, hs], preferred_element_type=F32)

        _causal_blocks(i, j, compute)

        @pl.when(j == nb - 1)
        def _():
            dq_ref[...] = acc_ref[...].astype(dq_ref.dtype)

    kv = lambda c: (lambda i, j: (jnp.minimum(j, i), c))
    row = lambda i, j: (i, 0)
    return pl.pallas_call(
        body, name=name, grid=(nb, nb),
        in_specs=[pl.BlockSpec((blk, DA), row), pl.BlockSpec((blk, DA), kv(1)), pl.BlockSpec((blk, DA), kv(2)),
                  pl.BlockSpec((H, blk), lambda i, j: (0, jnp.minimum(j, i))), pl.BlockSpec((blk, H), row),
                  pl.BlockSpec((blk, DA), row), pl.BlockSpec((blk, H), row), pl.BlockSpec((blk, H), row)],
        out_specs=pl.BlockSpec((blk, DA), row),
        out_shape=jax.ShapeDtypeStruct((T, DA), BF16),
        scratch_shapes=[pltpu.VMEM((blk, DA), F32)],
        compiler_params=_params(("parallel", "arbitrary")),
    )(zqkv, zqkv, zqkv, fcum, fcol, do, lse, delta)


def _attn_bwd_dkv(zqkv, fcum, fcol, do, lse, delta, *, name):
    T = zqkv.shape[0]
    DA = zqkv.shape[1] // 3
    H = DA // HEAD_DIM
    blk = _divisor(T, ATT_BLOCK, LANES)
    nb = T // blk
    scale = HEAD_DIM ** -0.5

    def body(q_ref, k_ref, v_ref, f_ref, fq_ref, do_ref, lse_ref, dl_ref, dk_ref, dv_ref, df_ref,
             dk_acc, dv_acc, df_acc):
        j = pl.program_id(0)
        i = pl.program_id(1)

        @pl.when(i == 0)
        def _():
            dk_acc[...] = jnp.zeros_like(dk_acc)
            dv_acc[...] = jnp.zeros_like(dv_acc)
            df_acc[...] = jnp.zeros_like(df_acc)

        def compute(diagonal):
            for h in range(H):
                hs = pl.ds(h * HEAD_DIM, HEAD_DIM)
                s = _scores(q_ref, k_ref, f_ref, fq_ref, h, blk, scale, diagonal)
                p = jnp.exp(s - lse_ref[:, h:h + 1])
                dov = do_ref[:, hs]
                dv_acc[:, hs] += lax.dot_general(p.astype(BF16), dov, (((0,), (0,)), ((), ())),
                                                 preferred_element_type=F32)
                dp = lax.dot_general(dov, v_ref[:, hs], (((1,), (1,)), ((), ())), preferred_element_type=F32)
                ds = p * (dp - dl_ref[:, h:h + 1])
                dk_acc[:, hs] += scale * lax.dot_general(ds.astype(BF16), q_ref[:, hs], (((0,), (0,)), ((), ())),
                                                         preferred_element_type=F32)
                df_acc[h:h + 1, :] -= jnp.sum(ds, axis=0, keepdims=True)

        _causal_blocks(i, j, compute)

        @pl.when(i == nb - 1)
        def _():
            dk_ref[...] = dk_acc[...].astype(dk_ref.dtype)
            dv_ref[...] = dv_acc[...].astype(dv_ref.dtype)
            df_ref[...] = df_acc[...]

    qrow = lambda j, i: (jnp.maximum(i, j), 0)
    kcol = lambda c: (lambda j, i: (j, c))
    return pl.pallas_call(
        body, name=name, grid=(nb, nb),
        in_specs=[pl.BlockSpec((blk, DA), qrow), pl.BlockSpec((blk, DA), kcol(1)), pl.BlockSpec((blk, DA), kcol(2)),
                  pl.BlockSpec((H, blk), lambda j, i: (0, j)), pl.BlockSpec((blk, H), qrow),
                  pl.BlockSpec((blk, DA), qrow), pl.BlockSpec((blk, H), qrow), pl.BlockSpec((blk, H), qrow)],
        out_specs=[pl.BlockSpec((blk, DA), kcol(0)), pl.BlockSpec((blk, DA), kcol(0)),
                   pl.BlockSpec((H, blk), lambda j, i: (0, j))],
        out_shape=[jax.ShapeDtypeStruct((T, DA), BF16), jax.ShapeDtypeStruct((T, DA), BF16),
                   jax.ShapeDtypeStruct((H, T), F32)],
        scratch_shapes=[pltpu.VMEM((blk, DA), F32), pltpu.VMEM((blk, DA), F32), pltpu.VMEM((H, blk), F32)],
        compiler_params=_params(("parallel", "arbitrary")),
    )(zqkv, zqkv, zqkv, fcum, fcol, do, lse, delta)


def _gelu(y):
    c = math.sqrt(2.0 / math.pi)
    return 0.5 * y * (1.0 + jnp.tanh(c * (y + 0.044715 * (y * y * y))))


def _gelu_grad(y):
    c = math.sqrt(2.0 / math.pi)
    th = jnp.tanh(c * (y + 0.044715 * (y * y * y)))
    return 0.5 * (1.0 + th) + 0.5 * y * (1.0 - th * th) * c * (1.0 + 3.0 * 0.044715 * y * y)


STATE_BLOCKS = SLAB_STATE // LANES


def _lane_blocks(ref, lead=()):
    return [ref[lead + (slice(None), pl.ds(b * LANES, LANES))] for b in range(2 * STATE_BLOCKS)]


def _put_lane_blocks(ref, blocks):
    for b, v in enumerate(blocks):
        ref[:, pl.ds(b * LANES, LANES)] = v


def _put_slab(x_ref, first, q, n_slab, chunk, value):
    for b in range(2 * STATE_BLOCKS):
        x_ref[b, pl.ds(first * n_slab + q, chunk, stride=n_slab), :] = value[:, b * LANES:(b + 1) * LANES]


def _get_slab(x_ref, first, q, n_slab, chunk):
    return jnp.concatenate([x_ref[b, pl.ds(first * n_slab + q, chunk, stride=n_slab), :]
                            for b in range(2 * STATE_BLOCKS)], axis=1)


def _ssm_scan_fwd(x_ref, a, h, chunk, n_slab, first=0):
    nb = STATE_BLOCKS

    def step(t, h):
        rows = pl.ds(pl.multiple_of((t + first) * n_slab, n_slab), n_slab)
        out = [None] * (2 * nb)
        for b in range(nb):
            n_re = a[b] * h[b] - a[nb + b] * h[nb + b] + x_ref[b, rows, :]
            n_im = a[b] * h[nb + b] + a[nb + b] * h[b] + x_ref[nb + b, rows, :]
            x_ref[b, rows, :] = n_re
            x_ref[nb + b, rows, :] = n_im
            out[b], out[nb + b] = n_re, n_im
        return tuple(out)

    return lax.fori_loop(0, chunk, step, tuple(h), unroll=4)


def _ssm_fwd(zu, w_b, w_c, a, d_skip, *, name):
    T, DS = zu.shape
    n_slab = DS // LANES
    chunk = _divisor(T, SSM_CHUNK, SUBLANES)
    n_chunk = T // chunk

    def body(u_ref, wb_ref, wc_ref, a_ref, ds_ref, y_ref, gy_ref, hin_ref, x_ref, h_ref):
        k = pl.program_id(0)

        @pl.when(k == 0)
        def _():
            h_ref[...] = jnp.zeros_like(h_ref)

        hin_ref[0] = h_ref[...]
        for q in range(n_slab):
            qs = pl.ds(q * LANES, LANES)
            _put_slab(x_ref, 0, q, n_slab, chunk, jnp.dot(u_ref[:, qs], wb_ref[q], preferred_element_type=F32))
        h = _ssm_scan_fwd(x_ref, _lane_blocks(a_ref), _lane_blocks(h_ref), chunk, n_slab)
        _put_lane_blocks(h_ref, h)
        for q in range(n_slab):
            qs = pl.ds(q * LANES, LANES)
            hq = _get_slab(x_ref, 0, q, n_slab, chunk).astype(BF16)
            y = jnp.dot(hq, wc_ref[q], preferred_element_type=F32) + ds_ref[:, qs] * u_ref[:, qs].astype(F32)
            y_ref[:, qs] = y
            gy_ref[:, qs] = _gelu(y).astype(gy_ref.dtype)

    whole = lambda shape: pl.BlockSpec(shape, lambda k: (0,) * len(shape))
    return pl.pallas_call(
        body, name=name, grid=(n_chunk,),
        in_specs=[pl.BlockSpec((chunk, DS), lambda k: (k, 0)), whole(w_b.shape), whole(w_c.shape),
                  whole(a.shape), whole(d_skip.shape)],
        out_specs=[pl.BlockSpec((chunk, DS), lambda k: (k, 0)), pl.BlockSpec((chunk, DS), lambda k: (k, 0)),
                   pl.BlockSpec((1, n_slab, 2 * SLAB_STATE), lambda k: (k, 0, 0))],
        out_shape=[jax.ShapeDtypeStruct((T, DS), F32), jax.ShapeDtypeStruct((T, DS), BF16),
                   jax.ShapeDtypeStruct((n_chunk, n_slab, 2 * SLAB_STATE), F32)],
        scratch_shapes=[pltpu.VMEM((2 * STATE_BLOCKS, chunk * n_slab, LANES), F32),
                        pltpu.VMEM((n_slab, 2 * SLAB_STATE), F32)],
        compiler_params=_params(("arbitrary",)),
    )(zu, w_b, w_c, a, d_skip)


def _ssm_bwd(zu, dgy, y, hin, w_b, w_bt, w_ct, a, d_skip, *, name):
    T, DS = zu.shape
    n_slab = DS // LANES
    chunk = _divisor(T, SSM_CHUNK, SUBLANES)
    n_chunk = T // chunk
    S = SLAB_STATE

    def body(u_ref, dgy_ref, y_ref, hin_ref, wb_ref, wbt_ref, wct_ref, a_ref, ds_ref,
             du_ref, dwb_ref, dwc_ref, da_ref, dds_ref, hb_ref, gb_ref, dy_ref, g_ref):
        k = pl.program_id(0)

        @pl.when(k == 0)
        def _():
            g_ref[...] = jnp.zeros_like(g_ref)
            dwb_ref[...] = jnp.zeros_like(dwb_ref)
            dwc_ref[...] = jnp.zeros_like(dwc_ref)
            da_ref[...] = jnp.zeros_like(da_ref)
            dds_ref[...] = jnp.zeros_like(dds_ref)

        nb = STATE_BLOCKS
        a = _lane_blocks(a_ref)
        hin = _lane_blocks(hin_ref, lead=(0,))

        for b in range(2 * nb):
            hb_ref[b, pl.ds(0, n_slab), :] = hin[b]
        dy_ref[...] = dgy_ref[...].astype(F32) * _gelu_grad(y_ref[...])
        for q in range(n_slab):
            qs = pl.ds(q * LANES, LANES)
            _put_slab(hb_ref, 1, q, n_slab, chunk, jnp.dot(u_ref[:, qs], wb_ref[q], preferred_element_type=F32))
            _put_slab(gb_ref, 0, q, n_slab, chunk,
                      jnp.dot(dy_ref[:, qs].astype(BF16), wct_ref[q], preferred_element_type=F32))
        _ssm_scan_fwd(hb_ref, a, hin, chunk, n_slab, first=1)

        def step(s, carry):
            g, da = carry[:2 * nb], carry[2 * nb:]
            t = chunk - 1 - s
            rows = pl.ds(pl.multiple_of(t * n_slab, n_slab), n_slab)
            g_out, da_out = [None] * (2 * nb), [None] * (2 * nb)
            for b in range(nb):
                n_re = gb_ref[b, rows, :] + a[b] * g[b] + a[nb + b] * g[nb + b]
                n_im = gb_ref[nb + b, rows, :] + a[b] * g[nb + b] - a[nb + b] * g[b]
                gb_ref[b, rows, :] = n_re
                gb_ref[nb + b, rows, :] = n_im
                p_re = hb_ref[b, rows, :]
                p_im = hb_ref[nb + b, rows, :]
                g_out[b], g_out[nb + b] = n_re, n_im
                da_out[b] = da[b] + n_re * p_re + n_im * p_im
                da_out[nb + b] = da[nb + b] + n_im * p_re - n_re * p_im
            return tuple(g_out) + tuple(da_out)

        zero = jnp.zeros((n_slab, LANES), F32)
        carry = lax.fori_loop(0, chunk, step, tuple(_lane_blocks(g_ref)) + (zero,) * (2 * nb), unroll=4)
        _put_lane_blocks(g_ref, carry[:2 * nb])
        for b in range(2 * nb):
            da_ref[:, pl.ds(b * LANES, LANES)] += carry[2 * nb + b]

        for q in range(n_slab):
            qs = pl.ds(q * LANES, LANES)
            uq = u_ref[:, qs]
            dy = dy_ref[:, qs]
            hq = _get_slab(hb_ref, 1, q, n_slab, chunk).astype(BF16)
            gq = _get_slab(gb_ref, 0, q, n_slab, chunk).astype(BF16)
            dwc_ref[q] += lax.dot_general(hq, dy.astype(BF16), (((0,), (0,)), ((), ())), preferred_element_type=F32)
            dwb_ref[q] += lax.dot_general(uq, gq, (((0,), (0,)), ((), ())), preferred_element_type=F32)
            du_ref[:, qs] = (jnp.dot(gq, wbt_ref[q], preferred_element_type=F32) + ds_ref[:, qs] * dy).astype(du_ref.dtype)
            dds_ref[:, qs] += jnp.sum(dy * uq.astype(F32), axis=0, keepdims=True)

    whole = lambda shape: pl.BlockSpec(shape, lambda k: (0,) * len(shape))
    rev = lambda k: (n_chunk - 1 - k, 0)
    return pl.pallas_call(
        body, name=name, grid=(n_chunk,),
        in_specs=[pl.BlockSpec((chunk, DS), rev), pl.BlockSpec((chunk, DS), rev), pl.BlockSpec((chunk, DS), rev),
                  pl.BlockSpec((1, n_slab, 2 * S), lambda k: (n_chunk - 1 - k, 0, 0)),
                  whole(w_b.shape), whole(w_bt.shape), whole(w_ct.shape), whole(a.shape), whole(d_skip.shape)],
        out_specs=[pl.BlockSpec((chunk, DS), rev), whole(w_b.shape), whole(w_bt.shape), whole(a.shape),
                   whole(d_skip.shape)],
        out_shape=[jax.ShapeDtypeStruct((T, DS), BF16), jax.ShapeDtypeStruct(w_b.shape, F32),
                   jax.ShapeDtypeStruct(w_bt.shape, F32), jax.ShapeDtypeStruct(a.shape, F32),
                   jax.ShapeDtypeStruct(d_skip.shape, F32)],
        scratch_shapes=[pltpu.VMEM((2 * STATE_BLOCKS, (chunk + 1) * n_slab, LANES), F32),
                        pltpu.VMEM((2 * STATE_BLOCKS, chunk * n_slab, LANES), F32),
                        pltpu.VMEM((chunk, DS), F32), pltpu.VMEM((n_slab, 2 * S), F32)],
        compiler_params=_params(("arbitrary",)),
    )(zu, dgy, y, hin, w_b, w_bt, w_ct, a, d_skip)


def _ssm_discretise(lam_re, lam_im, log_dt, b_re, b_im):
    dt = jnp.exp(log_dt)[:, None]
    mag = jnp.exp(lam_re * dt)
    a_re = mag * jnp.cos(lam_im * dt)
    a_im = mag * jnp.sin(lam_im * dt)
    den = lam_re * lam_re + lam_im * lam_im
    nr = a_re - 1.0
    z_re = (nr * lam_re + a_im * lam_im) / den
    z_im = (a_im * lam_re - nr * lam_im) / den
    bb_re = z_re[..., None] * b_re - z_im[..., None] * b_im
    bb_im = z_re[..., None] * b_im + z_im[..., None] * b_re
    return a_re, a_im, bb_re, bb_im


def _slab_in(m_re, m_im):
    G, P, C = m_re.shape
    n_slab = G // GROUPS_PER_SLAB
    eye = jnp.eye(GROUPS_PER_SLAB, dtype=m_re.dtype)

    def one(m):
        m = m.reshape(n_slab, GROUPS_PER_SLAB, P, C)
        w = jnp.einsum('sgpc,gh->sgchp', m, eye)
        return w.reshape(n_slab, GROUPS_PER_SLAB * C, GROUPS_PER_SLAB * P)

    return jnp.concatenate([one(m_re), one(m_im)], axis=2)


def _slab_in_grad(dw, G, P, C):
    n_slab = G // GROUPS_PER_SLAB
    eye = jnp.eye(GROUPS_PER_SLAB, dtype=dw.dtype)

    def one(w):
        w = w.reshape(n_slab, GROUPS_PER_SLAB, C, GROUPS_PER_SLAB, P)
        return jnp.einsum('sgchp,gh->sgpc', w, eye).reshape(G, P, C)

    return one(dw[:, :, :SLAB_STATE]), one(dw[:, :, SLAB_STATE:])


def _slab_out(c_re, c_im):
    G, C, P = c_re.shape
    n_slab = G // GROUPS_PER_SLAB
    eye = jnp.eye(GROUPS_PER_SLAB, dtype=c_re.dtype)

    def one(m):
        m = m.reshape(n_slab, GROUPS_PER_SLAB, C, P)
        w = jnp.einsum('sgcp,gh->shpgc', m, eye)
        return w.reshape(n_slab, GROUPS_PER_SLAB * P, GROUPS_PER_SLAB * C)

    return jnp.concatenate([one(c_re), one(-c_im)], axis=1)


def _slab_out_grad(dw, G, C, P):
    n_slab = G // GROUPS_PER_SLAB
    eye = jnp.eye(GROUPS_PER_SLAB, dtype=dw.dtype)

    def one(w):
        w = w.reshape(n_slab, GROUPS_PER_SLAB, P, GROUPS_PER_SLAB, C)
        return jnp.einsum('shpgc,gh->sgcp', w, eye).reshape(G, C, P)

    return one(dw[:, :SLAB_STATE, :]), -one(dw[:, SLAB_STATE:, :])


def _slab_diag(a_re, a_im):
    G, P = a_re.shape
    n_slab = G // GROUPS_PER_SLAB
    return jnp.concatenate([a_re.reshape(n_slab, SLAB_STATE), a_im.reshape(n_slab, SLAB_STATE)], axis=1)


def _merge_fwd(yab, zg, attn, *, name):
    T, D = attn.shape
    tr = _divisor(T, ROW_BLOCK, SUBLANES)

    def body(ya_ref, yb_ref, ga_ref, gb_ref, at_ref, o_ref):
        f = lambda r: r[...].astype(F32)
        ssm = f(ya_ref) * jax.nn.sigmoid(f(yb_ref))
        o_ref[...] = (jax.nn.sigmoid(f(ga_ref)) * ssm + jax.nn.sigmoid(f(gb_ref)) * f(at_ref)).astype(o_ref.dtype)

    lo = pl.BlockSpec((tr, D), lambda i: (i, 0))
    hi = pl.BlockSpec((tr, D), lambda i: (i, 1))
    return pl.pallas_call(
        body, name=name, grid=(T // tr,),
        in_specs=[lo, hi, lo, hi, lo],
        out_specs=lo,
        out_shape=jax.ShapeDtypeStruct((T, D), BF16),
        compiler_params=_params(("parallel",)),
    )(yab, yab, zg, zg, attn)


def _merge_bwd(dm, yab, zg, attn, *, name):
    T, D = attn.shape
    tr = _divisor(T, ROW_BLOCK, SUBLANES)

    def body(dm_ref, ya_ref, yb_ref, ga_ref, gb_ref, at_ref, dg_ref, dat_ref, dy_ref):
        f = lambda r: r[...].astype(F32)
        dmv, ya, at = f(dm_ref), f(ya_ref), f(at_ref)
        sa, sb, syb = jax.nn.sigmoid(f(ga_ref)), jax.nn.sigmoid(f(gb_ref)), jax.nn.sigmoid(f(yb_ref))
        ssm = ya * syb
        dssm = dmv * sa
        dg_ref[:, pl.ds(0, D)] = (dmv * ssm * sa * (1.0 - sa)).astype(dg_ref.dtype)
        dg_ref[:, pl.ds(D, D)] = (dmv * at * sb * (1.0 - sb)).astype(dg_ref.dtype)
        dat_ref[...] = (dmv * sb).astype(dat_ref.dtype)
        dy_ref[:, pl.ds(0, D)] = (dssm * syb).astype(dy_ref.dtype)
        dy_ref[:, pl.ds(D, D)] = (dssm * ya * syb * (1.0 - syb)).astype(dy_ref.dtype)

    lo = pl.BlockSpec((tr, D), lambda i: (i, 0))
    hi = pl.BlockSpec((tr, D), lambda i: (i, 1))
    both = pl.BlockSpec((tr, 2 * D), lambda i: (i, 0))
    return pl.pallas_call(
        body, name=name, grid=(T // tr,),
        in_specs=[lo, lo, hi, lo, hi, lo],
        out_specs=[both, lo, both],
        out_shape=[jax.ShapeDtypeStruct((T, 2 * D), BF16), jax.ShapeDtypeStruct((T, D), BF16),
                   jax.ShapeDtypeStruct((T, 2 * D), BF16)],
        compiler_params=_params(("parallel",)),
    )(dm, yab, yab, zg, zg, attn)


def _conv_taps(g_ref, halo_ref, i, tr):
    g0 = g_ref[...].astype(F32)
    halo = jnp.where(i > 0, halo_ref[...].astype(F32), 0.0)
    row = lax.broadcasted_iota(jnp.int32, g0.shape, 0)
    g1 = jnp.where(row == 0, halo[SUBLANES - 1:SUBLANES, :], pltpu.roll(g0, 1, axis=0))
    g2 = pltpu.roll(g0, 2, axis=0)
    g2 = jnp.where(row == 0, halo[SUBLANES - 2:SUBLANES - 1, :], g2)
    g2 = jnp.where(row == 1, halo[SUBLANES - 1:SUBLANES, :], g2)
    return g0, g1, g2


def _conv_blocks(T, FF):
    tr = _divisor(T, ROW_BLOCK, SUBLANES)
    tc = _divisor(FF, 1024, LANES)
    return tr, tc


def _conv_fwd(gu, conv_w, conv_b, *, name):
    T = gu.shape[0]
    FF = gu.shape[1] // 2
    tr, tc = _conv_blocks(T, FF)
    ncol = FF // tc

    def body(g_ref, halo_ref, u_ref, w_ref, b_ref, o_ref):
        i = pl.program_id(0)
        g0, g1, g2 = _conv_taps(g_ref, halo_ref, i, tr)
        gc = b_ref[...] + w_ref[0:1, :] * g2 + w_ref[1:2, :] * g1 + w_ref[2:3, :] * g0
        o_ref[...] = (gc * jax.nn.sigmoid(gc) * u_ref[...].astype(F32)).astype(o_ref.dtype)

    hb = tr // SUBLANES
    return pl.pallas_call(
        body, name=name, grid=(T // tr, ncol),
        in_specs=[pl.BlockSpec((tr, tc), lambda i, j: (i, j)),
                  pl.BlockSpec((SUBLANES, tc), lambda i, j: (jnp.maximum(i * hb - 1, 0), j)),
                  pl.BlockSpec((tr, tc), lambda i, j: (i, j + ncol)),
                  pl.BlockSpec((SUBLANES, tc), lambda i, j: (0, j)), pl.BlockSpec((1, tc), lambda i, j: (0, j))],
        out_specs=pl.BlockSpec((tr, tc), lambda i, j: (i, j)),
        out_shape=jax.ShapeDtypeStruct((T, FF), BF16),
        compiler_params=_params(("parallel", "parallel")),
    )(gu, gu, gu, conv_w, conv_b)


def _conv_bwd_gate(da, gu, conv_w, conv_b, *, name):
    T = gu.shape[0]
    FF = gu.shape[1] // 2
    tr, tc = _conv_blocks(T, FF)
    ncol = FF // tc

    def body(da_ref, g_ref, halo_ref, u_ref, w_ref, b_ref, dgc_ref, du_ref, s_ref):
        i = pl.program_id(1)

        @pl.when(i == 0)
        def _():
            s_ref[...] = jnp.zeros_like(s_ref)

        g0, g1, g2 = _conv_taps(g_ref, halo_ref, i, tr)
        gc = b_ref[...] + w_ref[0:1, :] * g2 + w_ref[1:2, :] * g1 + w_ref[2:3, :] * g0
        sg = jax.nn.sigmoid(gc)
        dav = da_ref[...].astype(F32)
        du_ref[...] = (dav * gc * sg).astype(du_ref.dtype)
        dgc = dav * u_ref[...].astype(F32) * (sg * (1.0 + gc * (1.0 - sg)))
        dgc_ref[...] = dgc.astype(dgc_ref.dtype)
        s_ref[0:1, :] += jnp.sum(dgc * g2, axis=0, keepdims=True)
        s_ref[1:2, :] += jnp.sum(dgc * g1, axis=0, keepdims=True)
        s_ref[2:3, :] += jnp.sum(dgc * g0, axis=0, keepdims=True)
        s_ref[3:4, :] += jnp.sum(dgc, axis=0, keepdims=True)

    hb = tr // SUBLANES
    blk = pl.BlockSpec((tr, tc), lambda j, i: (i, j))
    return pl.pallas_call(
        body, name=name, grid=(ncol, T // tr),
        in_specs=[blk, blk,
                  pl.BlockSpec((SUBLANES, tc), lambda j, i: (jnp.maximum(i * hb - 1, 0), j)),
                  pl.BlockSpec((tr, tc), lambda j, i: (i, j + ncol)),
                  pl.BlockSpec((SUBLANES, tc), lambda j, i: (0, j)), pl.BlockSpec((1, tc), lambda j, i: (0, j))],
        out_specs=[blk, blk, pl.BlockSpec((SUBLANES, tc), lambda j, i: (0, j))],
        out_shape=[jax.ShapeDtypeStruct((T, FF), BF16), jax.ShapeDtypeStruct((T, FF), BF16),
                   jax.ShapeDtypeStruct((SUBLANES, FF), F32)],
        compiler_params=_params(("parallel", "arbitrary")),
    )(da, gu, gu, gu, conv_w, conv_b)


def _conv_bwd_taps(dgc, conv_w, *, name):
    T, FF = dgc.shape
    tr, tc = _conv_blocks(T, FF)
    ncol = FF // tc
    nrow = T // tr

    def body(d_ref, next_ref, w_ref, o_ref):
        i = pl.program_id(0)
        d0 = d_ref[...].astype(F32)
        nxt = jnp.where(i < nrow - 1, next_ref[...].astype(F32), 0.0)
        row = lax.broadcasted_iota(jnp.int32, d0.shape, 0)
        d1 = jnp.where(row == tr - 1, nxt[0:1, :], pltpu.roll(d0, tr - 1, axis=0))
        d2 = pltpu.roll(d0, tr - 2, axis=0)
        d2 = jnp.where(row == tr - 2, nxt[0:1, :], d2)
        d2 = jnp.where(row == tr - 1, nxt[1:2, :], d2)
        dg = w_ref[2:3, :] * d0 + w_ref[1:2, :] * d1 + w_ref[0:1, :] * d2
        o_ref[...] = dg.astype(o_ref.dtype)

    hb = tr // SUBLANES
    last = T // SUBLANES - 1
    return pl.pallas_call(
        body, name=name, grid=(nrow, ncol),
        in_specs=[pl.BlockSpec((tr, tc), lambda i, j: (i, j)),
                  pl.BlockSpec((SUBLANES, tc), lambda i, j: (jnp.minimum((i + 1) * hb, last), j)),
                  pl.BlockSpec((SUBLANES, tc), lambda i, j: (0, j))],
        out_specs=pl.BlockSpec((tr, tc), lambda i, j: (i, j)),
        out_shape=jax.ShapeDtypeStruct((T, FF), BF16),
        compiler_params=_params(("parallel", "parallel")),
    )(dgc, dgc, conv_w)


def _mesh_pos():
    return lax.axis_index("x"), lax.axis_index("y"), lax.axis_index("c")


def _flip(pos, mask):
    x, y, c = pos
    return (x ^ ((mask >> 2) & 1), y ^ ((mask >> 1) & 1), c ^ (mask & 1))


def _index_of(pos):
    x, y, c = pos
    return 4 * x + 2 * y + c


def _all_gather(x, *, name):
    R, C = x.shape

    def body(x_ref, o_ref, send_sems, recv_sems, local_sem):
        me = _mesh_pos()
        mine = pltpu.make_async_copy(x_ref, o_ref.at[_index_of(me)], local_sem)
        mine.start()
        copies = []
        for k in range(1, N_DEV):
            cp = pltpu.make_async_remote_copy(
                src_ref=x_ref, dst_ref=o_ref.at[_index_of(me)],
                send_sem=send_sems.at[k - 1], recv_sem=recv_sems.at[k - 1],
                device_id=_flip(me, k), device_id_type=pl.DeviceIdType.MESH)
            cp.start()
            copies.append(cp)
        for k in range(1, N_DEV):
            src = _index_of(_flip(me, k))
            pltpu.make_async_remote_copy(
                src_ref=x_ref, dst_ref=o_ref.at[src],
                send_sem=send_sems.at[k - 1], recv_sem=recv_sems.at[k - 1],
                device_id=_flip(me, k), device_id_type=pl.DeviceIdType.MESH).wait_recv()
        for cp in copies:
            cp.wait_send()
        mine.wait()

    return pl.pallas_call(
        body, name=name,
        in_specs=[pl.BlockSpec(memory_space=pl.ANY)],
        out_specs=pl.BlockSpec(memory_space=pl.ANY),
        out_shape=jax.ShapeDtypeStruct((N_DEV, R, C), x.dtype),
        scratch_shapes=[pltpu.SemaphoreType.DMA((N_DEV - 1,)), pltpu.SemaphoreType.DMA((N_DEV - 1,)),
                        pltpu.SemaphoreType.DMA],
        compiler_params=pltpu.CompilerParams(has_side_effects=True),
    )(x)


def _exchange(parts, *, name):
    _, R, C = parts.shape

    def body(p_ref, o_ref, send_sems, recv_sems, local_sem):
        me = _mesh_pos()
        my = _index_of(me)
        mine = pltpu.make_async_copy(p_ref.at[my], o_ref.at[my], local_sem)
        mine.start()
        copies = []
        for k in range(1, N_DEV):
            peer = _flip(me, k)
            cp = pltpu.make_async_remote_copy(
                src_ref=p_ref.at[_index_of(peer)], dst_ref=o_ref.at[my],
                send_sem=send_sems.at[k - 1], recv_sem=recv_sems.at[k - 1],
                device_id=peer, device_id_type=pl.DeviceIdType.MESH)
            cp.start()
            copies.append(cp)
        for k in range(1, N_DEV):
            peer = _flip(me, k)
            pltpu.make_async_remote_copy(
                src_ref=p_ref.at[my], dst_ref=o_ref.at[_index_of(peer)],
                send_sem=send_sems.at[k - 1], recv_sem=recv_sems.at[k - 1],
                device_id=peer, device_id_type=pl.DeviceIdType.MESH).wait_recv()
        for cp in copies:
            cp.wait_send()
        mine.wait()

    return pl.pallas_call(
        body, name=name,
        in_specs=[pl.BlockSpec(memory_space=pl.ANY)],
        out_specs=pl.BlockSpec(memory_space=pl.ANY),
        out_shape=jax.ShapeDtypeStruct(parts.shape, parts.dtype),
        scratch_shapes=[pltpu.SemaphoreType.DMA((N_DEV - 1,)), pltpu.SemaphoreType.DMA((N_DEV - 1,)),
                        pltpu.SemaphoreType.DMA],
        compiler_params=pltpu.CompilerParams(has_side_effects=True),
    )(parts)


def _sum_adamw(parts, w, m, v, *, name):
    R, C = w.shape
    tr = _divisor(R, max(2 * SUBLANES, ADAMW_BLOCK_ELEMS // C // SUBLANES * SUBLANES), 2 * SUBLANES)
    c1 = 1.0 - ADAM_B1 ** ADAM_STEP
    c2 = 1.0 - ADAM_B2 ** ADAM_STEP

    def body(p_ref, w_ref, m_ref, v_ref, g_ref, d_ref, nm_ref, nv_ref):
        g = p_ref[0].astype(F32)
        for s in range(1, N_DEV):
            g = g + p_ref[s].astype(F32)
        nm = ADAM_B1 * m_ref[...] + (1.0 - ADAM_B1) * g
        nv = ADAM_B2 * v_ref[...] + (1.0 - ADAM_B2) * (g * g)
        g_ref[...] = g
        nm_ref[...] = nm
        nv_ref[...] = nv
        d_ref[...] = -ADAM_LR * ((nm / c1) / (jnp.sqrt(nv / c2) + ADAM_EPS) + ADAM_WD * w_ref[...])

    blk = pl.BlockSpec((tr, C), lambda i: (i, 0))
    return pl.pallas_call(
        body, name=name, grid=(R // tr,),
        in_specs=[pl.BlockSpec((N_DEV, tr, C), lambda i: (0, i, 0)), blk, blk, blk],
        out_specs=[blk] * 4,
        out_shape=[jax.ShapeDtypeStruct((R, C), F32)] * 4,
        compiler_params=_params(("parallel",)),
    )(parts, w, m, v)


def _pad2(a, rows, cols):
    return jnp.pad(a, ((0, rows - a.shape[0]), (0, cols - a.shape[1])))


def _gather_cols(w, dtype, *, name):
    R, c = w.shape
    rp, cp = _round_up(R, 2 * SUBLANES), _round_up(c, LANES)
    g = _all_gather(_pad2(w.astype(dtype), rp, cp), name=name)
    return jnp.transpose(g[:, :R, :c], (1, 0, 2)).reshape(R, N_DEV * c)


def _update_cols(dw, w, m, v, *, name):
    R, c = w.shape
    rp, cp = _round_up(R, 2 * SUBLANES), _round_up(c, LANES)
    parts = jnp.transpose(dw.reshape(R, N_DEV, c), (1, 0, 2))
    parts = jnp.pad(parts, ((0, 0), (0, rp - R), (0, cp - c)))
    got = _exchange(parts, name=name + "_exchange")
    outs = _sum_adamw(got, _pad2(w, rp, cp), _pad2(m, rp, cp), _pad2(v, rp, cp), name=name + "_adamw")
    return [o[:R, :c] for o in outs]


def _gather_rows(w, dtype, *, name):
    r, C = w.shape
    rp, cp = _round_up(r, 2 * SUBLANES), _round_up(C, LANES)
    g = _all_gather(_pad2(w.astype(dtype), rp, cp), name=name)
    return g[:, :r, :C].reshape(N_DEV * r, C)


def _update_rows(dw, w, m, v, *, name):
    r, C = w.shape
    rp, cp = _round_up(r, 2 * SUBLANES), _round_up(C, LANES)
    parts = jnp.pad(dw.reshape(N_DEV, r, C), ((0, 0), (0, rp - r), (0, cp - C)))
    got = _exchange(parts, name=name + "_exchange")
    outs = _sum_adamw(got, _pad2(w, rp, cp), _pad2(m, rp, cp), _pad2(v, rp, cp), name=name + "_adamw")
    return [o[:r, :C] for o in outs]


def _update_replicated(grads, ws, ms, vs, *, name):
    sizes = [int(g.size) for g in grads]
    total = sum(sizes)
    rows = _round_up(-(-total // LANES), 2 * SUBLANES)

    def pack(arrs):
        flat = jnp.concatenate([a.reshape(-1).astype(F32) for a in arrs])
        return jnp.pad(flat, (0, rows * LANES - total)).reshape(rows, LANES)

    got = _all_gather(pack(grads), name=name + "_gather")
    outs = _sum_adamw(got, pack(ws), pack(ms), pack(vs), name=name + "_adamw")
    result = []
    for o in outs:
        flat = o.reshape(-1)
        arrs, off = [], 0
        for w, n in zip(ws, sizes):
            arrs.append(flat[off:off + n].reshape(w.shape))
            off += n
        result.append(arrs)
    return result


def kernel(x, meta, g_mix, w_in, b_f, lam_re, lam_im, log_dt, b_re, b_im, c_re, c_im, d_skip, w_glu, w_attn_o, w_out, g_ffn, w_up, conv_w, conv_b, w_down, g_final, loss_target, m_meta, m_g_mix, m_w_in, m_b_f, m_lam_re, m_lam_im, m_log_dt, m_b_re, m_b_im, m_c_re, m_c_im, m_d_skip, m_w_glu, m_w_attn_o, m_w_out, m_g_ffn, m_w_up, m_conv_w, m_conv_b, m_w_down, m_g_final, v_meta, v_g_mix, v_w_in, v_b_f, v_lam_re, v_lam_im, v_log_dt, v_b_re, v_b_im, v_c_re, v_c_im, v_d_skip, v_w_glu, v_w_attn_o, v_w_out, v_g_ffn, v_w_up, v_conv_w, v_conv_b, v_w_down, v_g_final):
    seq, D = x.shape[1], x.shape[2]
    L = N_META + seq
    T = _round_up(L, SEQ_BLOCK) if L > SEQ_BLOCK else _round_up(L, LANES)
    DS = d_skip.shape[1]
    H = b_f.shape[1]
    DA = H * HEAD_DIM
    FF = conv_b.shape[1]
    G, P, C = b_re.shape[1:]

    meta_full = _gather_cols(meta, F32, name="gather_meta")
    conv_w_full = _gather_cols(conv_w[0], F32, name="gather_conv_w")
    w_in_full = _gather_cols(w_in[0], BF16, name="gather_w_in")
    w_ao_full = _gather_cols(w_attn_o[0], BF16, name="gather_w_attn_o")
    w_glu_full = _gather_cols(w_glu[0], BF16, name="gather_w_glu")
    w_out_full = _gather_rows(w_out[0], BF16, name="gather_w_out")
    w_up_full = _gather_cols(w_up[0], BF16, name="gather_w_up")
    w_down_full = _gather_rows(w_down[0], BF16, name="gather_w_down")
    conv_w8 = jnp.pad(conv_w_full, ((0, SUBLANES - CONV_WIDTH), (0, 0)))

    a_re, a_im, bb_re, bb_im = _ssm_discretise(lam_re[0], lam_im[0], log_dt[0], b_re[0], b_im[0])
    w_b = _slab_in(bb_re, bb_im)
    w_c = _slab_out(c_re[0], c_im[0])
    a_slab = _slab_diag(a_re, a_im)
    w_b16, w_c16 = w_b.astype(BF16), w_c.astype(BF16)
    w_bt16, w_ct16 = jnp.swapaxes(w_b16, 1, 2), jnp.swapaxes(w_c16, 1, 2)

    h0 = jnp.concatenate([meta_full, x[0], jnp.zeros((T - L, D), F32)], axis=0)
    target = jnp.pad(loss_target[0], ((N_META, T - L), (0, 0)))
    n1 = _rms_fwd(h0, g_mix, name="rms_mix")
    o_f, o_u, o_g = 3 * DA, 3 * DA + H, 3 * DA + H + DS
    w_qkv = w_in_full[:, :o_f]
    w_f = jnp.pad(w_in_full[:, o_f:o_u], ((0, 0), (0, LANES - H)))
    w_u = w_in_full[:, o_u:o_g]
    w_g = w_in_full[:, o_g:]
    w_main = jnp.concatenate([w_qkv, w_u, w_g], axis=1)
    zqkv = _matmul(n1, w_qkv, name="mm_qkv")
    zu = _matmul(n1, w_u, name="mm_u")
    zg = _matmul(n1, w_g, name="mm_gates")
    zf = _matmul(n1, w_f, name="mm_forget", out_dtype=F32)
    f_t = zf[:, :H].T
    b_col = b_f.reshape(H, 1)
    fcum = _forget_cumsum(f_t, b_col, name="forget_cumsum")
    fcol = fcum.T
    o, lse = _attn_fwd(zqkv, fcum, fcol, name="attn_fwd")
    attn = _matmul(o, w_ao_full, name="mm_attn_o")
    y, gy, hin = _ssm_fwd(zu, w_b16, w_c16, a_slab, d_skip, name="ssm_fwd")
    yab = _matmul(gy, w_glu_full, name="mm_glu")
    merged = _merge_fwd(yab, zg, attn, name="merge_fwd")
    h1 = _matmul(merged, w_out_full, name="mm_out", out_dtype=F32, residual=h0)
    n2 = _rms_fwd(h1, g_ffn, name="rms_ffn")
    gu = _matmul(n2, w_up_full, name="mm_up")
    act = _conv_fwd(gu, conv_w8, conv_b, name="conv_fwd")
    h2 = _matmul(act, w_down_full, name="mm_down", out_dtype=F32, residual=h1)

    dh2, sq, dg_final = _final_loss(h2, g_final.reshape(1, D), target, seq, name="final_loss")
    loss = lax.psum(0.5 * sq[0, 0] / D, ("x", "y", "c"))
    dh2_16 = dh2.astype(BF16)
    d_act = _matmul(dh2_16, w_down_full, name="mm_down_dx", trans_b=True)
    dw_down = _matmul(act, dh2_16, name="mm_down_dw", trans_a=True)
    dgc, du2, conv_sums = _conv_bwd_gate(d_act, gu, conv_w8, conv_b, name="conv_bwd_gate")
    dgu = jnp.concatenate([_conv_bwd_taps(dgc, conv_w8, name="conv_bwd_taps"), du2], axis=1)
    dn2 = _matmul(dgu, w_up_full, name="mm_up_dx", trans_b=True)
    dw_up = _matmul(n2, dgu, name="mm_up_dw", trans_a=True)
    dh1, dg_ffn = _rms_bwd(dn2, h1, g_ffn, dh2, name="rms_ffn_bwd")
    dh1_16 = dh1.astype(BF16)
    dmerged = _matmul(dh1_16, w_out_full, name="mm_out_dx", trans_b=True)
    dw_out = _matmul(merged, dh1_16, name="mm_out_dw", trans_a=True)
    dzg, dattn, dyab = _merge_bwd(dmerged, yab, zg, attn, name="merge_bwd")
    do = _matmul(dattn, w_ao_full, name="mm_attn_o_dx", trans_b=True)
    dw_ao = _matmul(o, dattn, name="mm_attn_o_dw", trans_a=True)
    dgy = _matmul(dyab, w_glu_full, name="mm_glu_dx", trans_b=True)
    dw_glu = _matmul(gy, dyab, name="mm_glu_dw", trans_a=True)
    dzu, dw_b, dw_c, da_slab, dd_skip = _ssm_bwd(zu, dgy, y, hin, w_b16, w_bt16, w_ct16, a_slab, d_skip,
                                                 name="ssm_bwd")
    delta = _attn_delta(zqkv, fcum, fcol, do, lse, name="attn_delta")
    dq = _attn_bwd_dq(zqkv, fcum, fcol, do, lse, delta, name="attn_bwd_dq")
    dk, dv, dfcum = _attn_bwd_dkv(zqkv, fcum, fcol, do, lse, delta, name="attn_bwd_dkv")
    df_t, db_f = _forget_bwd(dfcum, f_t, b_col, name="forget_bwd")
    dzf = jnp.pad(df_t.T, ((0, 0), (0, LANES - H))).astype(BF16)
    dz_main = jnp.concatenate([dq, dk, dv, dzu, dzg], axis=1)
    dw_main = _matmul(n1, dz_main, name="mm_in_dw", trans_a=True)
    dw_f = _matmul(n1, dzf, name="mm_forget_dw", trans_a=True)
    dw_in = jnp.concatenate([dw_main[:, :o_f], dw_f[:, :H], dw_main[:, o_f:]], axis=1)
    dn1 = _matmul(dzf, w_f, name="mm_forget_dx", out_dtype=F32, trans_b=True)
    dn1 = _matmul(dz_main, w_main, name="mm_in_dx", out_dtype=F32, residual=dn1, trans_b=True)
    dh0, dg_mix = _rms_bwd(dn1, h0, g_mix, dh1, name="rms_mix_bwd")
    grad_x = dh0[N_META:L][None]

    dbb_re, dbb_im = _slab_in_grad(dw_b, G, P, C)
    dc_re, dc_im = _slab_out_grad(dw_c, G, C, P)
    da_re = da_slab[:, :SLAB_STATE].reshape(G, P)
    da_im = da_slab[:, SLAB_STATE:].reshape(G, P)
    _, disc_vjp = jax.vjp(_ssm_discretise, lam_re[0], lam_im[0], log_dt[0], b_re[0], b_im[0])
    dlam_re, dlam_im, dlog_dt, db_re, db_im = disc_vjp((da_re, da_im, dbb_re, dbb_im))

    big = {}
    big["meta"] = _update_cols(dh0[:N_META], meta, m_meta, v_meta, name="meta")
    big["conv_w"] = _update_cols(conv_sums[:CONV_WIDTH], conv_w[0], m_conv_w[0], v_conv_w[0], name="conv_w")
    big["w_down"] = _update_rows(dw_down, w_down[0], m_w_down[0], v_w_down[0], name="w_down")
    big["w_up"] = _update_cols(dw_up, w_up[0], m_w_up[0], v_w_up[0], name="w_up")
    big["w_out"] = _update_rows(dw_out, w_out[0], m_w_out[0], v_w_out[0], name="w_out")
    big["w_attn_o"] = _update_cols(dw_ao, w_attn_o[0], m_w_attn_o[0], v_w_attn_o[0], name="w_attn_o")
    big["w_glu"] = _update_cols(dw_glu, w_glu[0], m_w_glu[0], v_w_glu[0], name="w_glu")
    big["w_in"] = _update_cols(dw_in, w_in[0], m_w_in[0], v_w_in[0], name="w_in")

    rep_names = ["g_mix", "b_f", "lam_re", "lam_im", "log_dt", "b_re", "b_im", "c_re", "c_im", "d_skip", "g_ffn",
                 "conv_b", "g_final"]
    rep_w = [g_mix, b_f, lam_re, lam_im, log_dt, b_re, b_im, c_re, c_im, d_skip, g_ffn, conv_b, g_final]
    rep_m = [m_g_mix, m_b_f, m_lam_re, m_lam_im, m_log_dt, m_b_re, m_b_im, m_c_re, m_c_im, m_d_skip, m_g_ffn,
             m_conv_b, m_g_final]
    rep_v = [v_g_mix, v_b_f, v_lam_re, v_lam_im, v_log_dt, v_b_re, v_b_im, v_c_re, v_c_im, v_d_skip, v_g_ffn,
             v_conv_b, v_g_final]
    rep_g = [dg_mix, db_f[:, 0], dlam_re, dlam_im, dlog_dt, db_re, db_im, dc_re, dc_im, dd_skip, dg_ffn,
             conv_sums[CONV_WIDTH], dg_final]
    rep = _update_replicated(rep_g, rep_w, rep_m, rep_v, name="replicated")
    rep_out = {n: [rep[k][i] for k in range(4)] for i, n in enumerate(rep_names)}

    order = ["meta", "g_mix", "w_in", "b_f", "lam_re", "lam_im", "log_dt", "b_re", "b_im", "c_re", "c_im", "d_skip",
             "w_glu", "w_attn_o", "w_out", "g_ffn", "w_up", "conv_w", "conv_b", "w_down", "g_final"]
    outs = [loss, grad_x]
    for kind in range(4):
        for n in order:
            if n in big:
                outs.append(big[n][kind] if n == "meta" else big[n][kind][None])
            else:
                outs.append(rep_out[n][kind])
    return tuple(outs)
```

```python
import collections
import functools
import math

import jax
import jax.numpy as jnp
from jax import lax
from jax.experimental import pallas as pl
from jax.experimental.pallas import tpu as pltpu

F32 = jnp.float32
BF16 = jnp.bfloat16

N_META = 16
EPS = 1e-6
HEAD_DIM = 128
SSM_GROUP = 16
SSM_STATE = 64
GROUPS_PER_SLAB = 8
SLAB_STATE = GROUPS_PER_SLAB * SSM_STATE
CONV_WIDTH = 3
N_DEV = 8

ADAM_LR = 0.001
ADAM_B1 = 0.9
ADAM_B2 = 0.999
ADAM_EPS = 1e-08
ADAM_WD = 0.01
ADAM_STEP = 10

LANES = 128
SUBLANES = 8
VMEM_LIMIT = 52 * 1024 * 1024

SEQ_BLOCK = 768
ATT_BLOCK = 384
ROW_BLOCK = 256
SSM_CHUNK = 128
ADAMW_BLOCK_ELEMS = 1 << 17
MASK_VALUE = -1e30


def _round_up(n, m):
    return (n + m - 1) // m * m


def _divisor(n, target, mult):
    if n <= target:
        return n
    best = None
    for d in range(mult, target + 1, mult):
        if n % d == 0:
            best = d
    assert best is not None, (n, target, mult)
    return best


def _params(sem):
    return pltpu.CompilerParams(dimension_semantics=sem, vmem_limit_bytes=VMEM_LIMIT)


def _matmul(a, b, *, name, trans_a=False, trans_b=False, out_dtype=None, residual=None, riders=(),
            tm=768, tn=1024, tk=2048):
    out_dtype = BF16 if out_dtype is None else out_dtype
    assert not (trans_a and trans_b)
    if trans_a:
        K, M = a.shape
    else:
        M, K = a.shape
    if trans_b:
        N, K2 = b.shape
    else:
        K2, N = b.shape
    assert K == K2, (a.shape, b.shape)
    tm = _divisor(M, tm, LANES if trans_a else SUBLANES)
    tn = _divisor(N, tn, LANES)
    tk = _divisor(K, tk, LANES if not trans_a else SUBLANES)
    nk = K // tk

    def body(*refs):
        if residual is None:
            a_ref, b_ref, o_ref, acc_ref = refs
        else:
            a_ref, b_ref, r_ref, o_ref, acc_ref = refs
        k = pl.program_id(2)

        @pl.when(k == 0)
        def _():
            acc_ref[...] = jnp.zeros_like(acc_ref)

        contract = (0, 0) if trans_a else (1, 1) if trans_b else (1, 0)
        acc_ref[...] += lax.dot_general(a_ref[...], b_ref[...], (((contract[0],), (contract[1],)), ((), ())),
                                        preferred_element_type=F32)

        @pl.when(k == nk - 1)
        def _():
            r = acc_ref[...]
            if residual is not None:
                r = r + r_ref[...]
            o_ref[...] = r.astype(o_ref.dtype)

    if trans_a:
        a_spec = pl.BlockSpec((tk, tm), lambda i, j, k: (k, i))
    else:
        a_spec = pl.BlockSpec((tm, tk), lambda i, j, k: (i, k))
    if trans_b:
        b_spec = pl.BlockSpec((tn, tk), lambda i, j, k: (j, k))
    else:
        b_spec = pl.BlockSpec((tk, tn), lambda i, j, k: (k, j))
    in_specs = [a_spec, b_spec]
    args = [a, b]
    if residual is not None:
        in_specs.append(pl.BlockSpec((tm, tn), lambda i, j, k: (i, j)))
        args.append(residual)
    (out,), carried = _gridded_call(
        name, (M // tm, N // tn, nk), ("parallel", "parallel", "arbitrary"), body, in_specs,
        [pl.BlockSpec((tm, tn), lambda i, j, k: (i, j))], [jax.ShapeDtypeStruct((M, N), out_dtype)],
        [pltpu.VMEM((tm, tn), F32)], args, riders)
    return (out, carried) if riders else out


def _rms_fwd(h, g, *, name):
    T, D = h.shape
    tr = _divisor(T, ROW_BLOCK, SUBLANES)

    def body(h_ref, g_ref, o_ref):
        x = h_ref[...]
        r = lax.rsqrt(jnp.mean(x * x, axis=-1, keepdims=True) + EPS)
        o_ref[...] = (x * r * g_ref[...]).astype(o_ref.dtype)

    return pl.pallas_call(
        body, name=name, grid=(T // tr,),
        in_specs=[pl.BlockSpec((tr, D), lambda i: (i, 0)), pl.BlockSpec((1, D), lambda i: (0, 0))],
        out_specs=pl.BlockSpec((tr, D), lambda i: (i, 0)),
        out_shape=jax.ShapeDtypeStruct((T, D), BF16),
        compiler_params=_params(("parallel",)),
    )(h, g)


def _rms_bwd(dn, h, g, dres, *, name):
    T, D = h.shape
    tr = _divisor(T, ROW_BLOCK, SUBLANES)

    def body(dn_ref, h_ref, g_ref, dres_ref, dh_ref, dg_ref):
        i = pl.program_id(0)

        @pl.when(i == 0)
        def _():
            dg_ref[...] = jnp.zeros_like(dg_ref)

        x = h_ref[...]
        dn_v = dn_ref[...].astype(F32)
        r = lax.rsqrt(jnp.mean(x * x, axis=-1, keepdims=True) + EPS)
        xh = x * r
        dg_ref[...] += jnp.sum(dn_v * xh, axis=0, keepdims=True)
        dxh = dn_v * g_ref[...]
        dh_ref[...] = dres_ref[...] + r * (dxh - xh * jnp.mean(dxh * xh, axis=-1, keepdims=True))

    return pl.pallas_call(
        body, name=name, grid=(T // tr,),
        in_specs=[pl.BlockSpec((tr, D), lambda i: (i, 0)), pl.BlockSpec((tr, D), lambda i: (i, 0)),
                  pl.BlockSpec((1, D), lambda i: (0, 0)), pl.BlockSpec((tr, D), lambda i: (i, 0))],
        out_specs=[pl.BlockSpec((tr, D), lambda i: (i, 0)), pl.BlockSpec((1, D), lambda i: (0, 0))],
        out_shape=[jax.ShapeDtypeStruct((T, D), F32), jax.ShapeDtypeStruct((1, D), F32)],
        compiler_params=_params(("arbitrary",)),
    )(dn, h, g, dres)


def _final_loss(h, g, target, n_valid, *, name):
    T, D = h.shape
    tr = _divisor(T, ROW_BLOCK, SUBLANES)

    def body(h_ref, g_ref, t_ref, dh_ref, sq_ref, dg_ref):
        i = pl.program_id(0)

        @pl.when(i == 0)
        def _():
            sq_ref[...] = jnp.zeros_like(sq_ref)
            dg_ref[...] = jnp.zeros_like(dg_ref)

        x = h_ref[...]
        r = lax.rsqrt(jnp.mean(x * x, axis=-1, keepdims=True) + EPS)
        xh = x * r
        gv = g_ref[...]
        row = i * tr + lax.broadcasted_iota(jnp.int32, (tr, 1), 0)
        valid = (row >= N_META) & (row < N_META + n_valid)
        err = jnp.where(valid, xh * gv - t_ref[...], 0.0)
        sq_ref[...] += jnp.sum(err * err)
        dy = err * (1.0 / D)
        dg_ref[...] += jnp.sum(dy * xh, axis=0, keepdims=True)
        dxh = dy * gv
        dh_ref[...] = r * (dxh - xh * jnp.mean(dxh * xh, axis=-1, keepdims=True))

    return pl.pallas_call(
        body, name=name, grid=(T // tr,),
        in_specs=[pl.BlockSpec((tr, D), lambda i: (i, 0)), pl.BlockSpec((1, D), lambda i: (0, 0)),
                  pl.BlockSpec((tr, D), lambda i: (i, 0))],
        out_specs=[pl.BlockSpec((tr, D), lambda i: (i, 0)), pl.BlockSpec((SUBLANES, LANES), lambda i: (0, 0)),
                   pl.BlockSpec((1, D), lambda i: (0, 0))],
        out_shape=[jax.ShapeDtypeStruct((T, D), F32), jax.ShapeDtypeStruct((SUBLANES, LANES), F32),
                   jax.ShapeDtypeStruct((1, D), F32)],
        compiler_params=_params(("arbitrary",)),
    )(h, g, target)


def _prefix_sum_lanes(x):
    lane = lax.broadcasted_iota(jnp.int32, x.shape, 1)
    d = 1
    while d < LANES:
        x = x + jnp.where(lane >= d, pltpu.roll(x, d, axis=1), 0.0)
        d *= 2
    return x


def _forget_cumsum(ft, bf, *, name):
    H, T = ft.shape
    nb = T // LANES

    def body(f_ref, b_ref, o_ref):
        carry = jnp.zeros((H, 1), F32)
        for j in range(nb):
            sl = pl.ds(j * LANES, LANES)
            lf = jax.nn.log_sigmoid(f_ref[:, sl] + b_ref[...])
            c = _prefix_sum_lanes(lf) + carry
            o_ref[:, sl] = c
            carry = c[:, LANES - 1:LANES]

    return pl.pallas_call(
        body, name=name,
        in_specs=[pl.BlockSpec(memory_space=pltpu.VMEM), pl.BlockSpec(memory_space=pltpu.VMEM)],
        out_specs=pl.BlockSpec(memory_space=pltpu.VMEM),
        out_shape=jax.ShapeDtypeStruct((H, T), F32),
        compiler_params=pltpu.CompilerParams(vmem_limit_bytes=VMEM_LIMIT),
    )(ft, bf)


def _forget_bwd(dF, ft, bf, *, name):
    H, T = ft.shape
    nb = T // LANES

    def body(d_ref, f_ref, b_ref, o_ref, s_ref):
        carry = jnp.zeros((H, 1), F32)
        acc = jnp.zeros((H, LANES), F32)
        for j in reversed(range(nb)):
            sl = pl.ds(j * LANES, LANES)
            d = d_ref[:, sl]
            pre = _prefix_sum_lanes(d)
            tot = pre[:, LANES - 1:LANES]
            dlf = tot - pre + d + carry
            carry = carry + tot
            z = f_ref[:, sl] + b_ref[...]
            df = dlf * jax.nn.sigmoid(-z)
            o_ref[:, sl] = df
            acc = acc + df
        s_ref[...] = jnp.broadcast_to(jnp.sum(acc, axis=1, keepdims=True), (H, LANES))

    return pl.pallas_call(
        body, name=name,
        in_specs=[pl.BlockSpec(memory_space=pltpu.VMEM)] * 3,
        out_specs=[pl.BlockSpec(memory_space=pltpu.VMEM)] * 2,
        out_shape=[jax.ShapeDtypeStruct((H, T), F32), jax.ShapeDtypeStruct((H, LANES), F32)],
        compiler_params=pltpu.CompilerParams(vmem_limit_bytes=VMEM_LIMIT),
    )(dF, ft, bf)


def _scores(q_ref, k_ref, f_ref, fq_ref, h, blk, scale, diagonal):
    hs = pl.ds(h * HEAD_DIM, HEAD_DIM)
    s = lax.dot_general(q_ref[:, hs], k_ref[:, hs], (((1,), (1,)), ((), ())), preferred_element_type=F32)
    s = s * scale + (fq_ref[:, h:h + 1] - f_ref[h:h + 1, :])
    if diagonal:
        rows = lax.broadcasted_iota(jnp.int32, (blk, blk), 0)
        cols = lax.broadcasted_iota(jnp.int32, (blk, blk), 1)
        s = jnp.where(cols <= rows, s, MASK_VALUE)
    return s


def _causal_blocks(i, j, compute):
    @pl.when(j < i)
    def _():
        compute(False)

    @pl.when(j == i)
    def _():
        compute(True)


def _attn_fwd(zqkv, fcum, fcol, *, name, riders=()):
    T = zqkv.shape[0]
    DA = zqkv.shape[1] // 3
    H = DA // HEAD_DIM
    blk = _divisor(T, ATT_BLOCK, LANES)
    nb = T // blk
    scale = HEAD_DIM ** -0.5

    def body(q_ref, k_ref, v_ref, f_ref, fq_ref, o_ref, lse_ref, m_ref, l_ref, acc_ref):
        i = pl.program_id(0)
        j = pl.program_id(1)

        @pl.when(j == 0)
        def _():
            m_ref[...] = jnp.full_like(m_ref, MASK_VALUE)
            l_ref[...] = jnp.zeros_like(l_ref)
            acc_ref[...] = jnp.zeros_like(acc_ref)

        def compute(diagonal):
            for h in range(H):
                hs = pl.ds(h * HEAD_DIM, HEAD_DIM)
                s = _scores(q_ref, k_ref, f_ref, fq_ref, h, blk, scale, diagonal)
                m_prev = m_ref[h]
                m_new = jnp.maximum(m_prev, jnp.max(s, axis=1, keepdims=True))
                alpha = jnp.exp(m_prev - m_new)
                p = jnp.exp(s - m_new)
                l_ref[h] = alpha * l_ref[h] + jnp.sum(p, axis=1, keepdims=True)
                acc_ref[:, hs] = alpha * acc_ref[:, hs] + jnp.dot(p.astype(BF16), v_ref[:, hs],
                                                                  preferred_element_type=F32)
                m_ref[h] = m_new

        _causal_blocks(i, j, compute)

        @pl.when(j == nb - 1)
        def _():
            for h in range(H):
                hs = pl.ds(h * HEAD_DIM, HEAD_DIM)
                l = l_ref[h]
                o_ref[:, hs] = (acc_ref[:, hs] / l).astype(o_ref.dtype)
                lse_ref[:, h:h + 1] = m_ref[h] + jnp.log(l)

    kv = lambda c: (lambda i, j: (jnp.minimum(j, i), c))
    (o, lse), carried = _gridded_call(
        name, (nb, nb), ("parallel", "arbitrary"), body,
        [pl.BlockSpec((blk, DA), lambda i, j: (i, 0)),
         pl.BlockSpec((blk, DA), kv(1)), pl.BlockSpec((blk, DA), kv(2)),
         pl.BlockSpec((H, blk), lambda i, j: (0, jnp.minimum(j, i))),
         pl.BlockSpec((blk, H), lambda i, j: (i, 0))],
        [pl.BlockSpec((blk, DA), lambda i, j: (i, 0)), pl.BlockSpec((blk, H), lambda i, j: (i, 0))],
        [jax.ShapeDtypeStruct((T, DA), BF16), jax.ShapeDtypeStruct((T, H), F32)],
        [pltpu.VMEM((H, blk, 1), F32), pltpu.VMEM((H, blk, 1), F32), pltpu.VMEM((blk, DA), F32)],
        [zqkv, zqkv, zqkv, fcum, fcol], riders)
    return o, lse, carried


def _attn_delta(zqkv, fcum, fcol, do, lse, *, name):
    T = zqkv.shape[0]
    DA = zqkv.shape[1] // 3
    H = DA // HEAD_DIM
    blk = _divisor(T, ATT_BLOCK, LANES)
    nb = T // blk
    scale = HEAD_DIM ** -0.5

    def body(q_ref, k_ref, v_ref, f_ref, fq_ref, do_ref, lse_ref, d_ref):
        i = pl.program_id(0)
        j = pl.program_id(1)

        @pl.when(j == 0)
        def _():
            d_ref[...] = jnp.zeros_like(d_ref)

        def compute(diagonal):
            for h in range(H):
                hs = pl.ds(h * HEAD_DIM, HEAD_DIM)
                s = _scores(q_ref, k_ref, f_ref, fq_ref, h, blk, scale, diagonal)
                p = jnp.exp(s - lse_ref[:, h:h + 1])
                dp = lax.dot_general(do_ref[:, hs], v_ref[:, hs], (((1,), (1,)), ((), ())),
                                     preferred_element_type=F32)
                d_ref[:, h:h + 1] += jnp.sum(p * dp, axis=1, keepdims=True)

        _causal_blocks(i, j, compute)

    kv = lambda c: (lambda i, j: (jnp.minimum(j, i), c))
    row = lambda i, j: (i, 0)
    return pl.pallas_call(
        body, name=name, grid=(nb, nb),
        in_specs=[pl.BlockSpec((blk, DA), row), pl.BlockSpec((blk, DA), kv(1)), pl.BlockSpec((blk, DA), kv(2)),
                  pl.BlockSpec((H, blk), lambda i, j: (0, jnp.minimum(j, i))), pl.BlockSpec((blk, H), row),
                  pl.BlockSpec((blk, DA), row), pl.BlockSpec((blk, H), row)],
        out_specs=pl.BlockSpec((blk, H), row),
        out_shape=jax.ShapeDtypeStruct((T, H), F32),
        compiler_params=_params(("parallel", "arbitrary")),
    )(zqkv, zqkv, zqkv, fcum, fcol, do, lse)


def _attn_bwd_dq(zqkv, fcum, fcol, do, lse, delta, *, name):
    T = zqkv.shape[0]
    DA = zqkv.shape[1] // 3
    H = DA // HEAD_DIM
    blk = _divisor(T, ATT_BLOCK, LANES)
    nb = T // blk
    scale = HEAD_DIM ** -0.5

    def body(q_ref, k_ref, v_ref, f_ref, fq_ref, do_ref, lse_ref, dl_ref, dq_ref, acc_ref):
        i = pl.program_id(0)
        j = pl.program_id(1)

        @pl.when(j == 0)
        def _():
            acc_ref[...] = jnp.zeros_like(acc_ref)

        def compute(diagonal):
            for h in range(H):
                hs = pl.ds(h * HEAD_DIM, HEAD_DIM)
                s = _scores(q_ref, k_ref, f_ref, fq_ref, h, blk, scale, diagonal)
                p = jnp.exp(s - lse_ref[:, h:h + 1])
                dp = lax.dot_general(do_ref[:, hs], v_ref[:, hs], (((1,), (1,)), ((), ())),
                                     preferred_element_type=F32)
                ds = p * (dp - dl_ref[:, h:h + 1])
                acc_ref[:, hs] += scale * jnp.dot(ds.astype(BF16), k_ref[:, hs], preferred_element_type=F32)

        _causal_blocks(i, j, compute)

        @pl.when(j == nb - 1)
        def _():
            dq_ref[...] = acc_ref[...].astype(dq_ref.dtype)

    kv = lambda c: (lambda i, j: (jnp.minimum(j, i), c))
    row = lambda i, j: (i, 0)
    return pl.pallas_call(
        body, name=name, grid=(nb, nb),
        in_specs=[pl.BlockSpec((blk, DA), row), pl.BlockSpec((blk, DA), kv(1)), pl.BlockSpec((blk, DA), kv(2)),
                  pl.BlockSpec((H, blk), lambda i, j: (0, jnp.minimum(j, i))), pl.BlockSpec((blk, H), row),
                  pl.BlockSpec((blk, DA), row), pl.BlockSpec((blk, H), row), pl.BlockSpec((blk, H), row)],
        out_specs=pl.BlockSpec((blk, DA), row),
        out_shape=jax.ShapeDtypeStruct((T, DA), BF16),
        scratch_shapes=[pltpu.VMEM((blk, DA), F32)],
        compiler_params=_params(("parallel", "arbitrary")),
    )(zqkv, zqkv, zqkv, fcum, fcol, do, lse, delta)


def _attn_bwd_dkv(zqkv, fcum, fcol, do, lse, delta, *, name, riders=()):
    T = zqkv.shape[0]
    DA = zqkv.shape[1] // 3
    H = DA // HEAD_DIM
    blk = _divisor(T, ATT_BLOCK, LANES)
    nb = T // blk
    scale = HEAD_DIM ** -0.5

    def body(q_ref, k_ref, v_ref, f_ref, fq_ref, do_ref, lse_ref, dl_ref, dk_ref, dv_ref, df_ref,
             dk_acc, dv_acc, df_acc):
        j = pl.program_id(0)
        i = pl.program_id(1)

        @pl.when(i == 0)
        def _():
            dk_acc[...] = jnp.zeros_like(dk_acc)
            dv_acc[...] = jnp.zeros_like(dv_acc)
            df_acc[...] = jnp.zeros_like(df_acc)

        def compute(diagonal):
            for h in range(H):
                hs = pl.ds(h * HEAD_DIM, HEAD_DIM)
                s = _scores(q_ref, k_ref, f_ref, fq_ref, h, blk, scale, diagonal)
                p = jnp.exp(s - lse_ref[:, h:h + 1])
                dov = do_ref[:, hs]
                dv_acc[:, hs] += lax.dot_general(p.astype(BF16), dov, (((0,), (0,)), ((), ())),
                                                 preferred_element_type=F32)
                dp = lax.dot_general(dov, v_ref[:, hs], (((1,), (1,)), ((), ())), preferred_element_type=F32)
                ds = p * (dp - dl_ref[:, h:h + 1])
                dk_acc[:, hs] += scale * lax.dot_general(ds.astype(BF16), q_ref[:, hs], (((0,), (0,)), ((), ())),
                                                         preferred_element_type=F32)
                df_acc[h:h + 1, :] -= jnp.sum(ds, axis=0, keepdims=True)

        _causal_blocks(i, j, compute)

        @pl.when(i == nb - 1)
        def _():
            dk_ref[...] = dk_acc[...].astype(dk_ref.dtype)
            dv_ref[...] = dv_acc[...].astype(dv_ref.dtype)
            df_ref[...] = df_acc[...]

    qrow = lambda j, i: (jnp.maximum(i, j), 0)
    kcol = lambda c: (lambda j, i: (j, c))
    (dk, dv, df), carried = _gridded_call(
        name, (nb, nb), ("parallel", "arbitrary"), body,
        [pl.BlockSpec((blk, DA), qrow), pl.BlockSpec((blk, DA), kcol(1)), pl.BlockSpec((blk, DA), kcol(2)),
         pl.BlockSpec((H, blk), lambda j, i: (0, j)), pl.BlockSpec((blk, H), qrow),
         pl.BlockSpec((blk, DA), qrow), pl.BlockSpec((blk, H), qrow), pl.BlockSpec((blk, H), qrow)],
        [pl.BlockSpec((blk, DA), kcol(0)), pl.BlockSpec((blk, DA), kcol(0)),
         pl.BlockSpec((H, blk), lambda j, i: (0, j))],
        [jax.ShapeDtypeStruct((T, DA), BF16), jax.ShapeDtypeStruct((T, DA), BF16),
         jax.ShapeDtypeStruct((H, T), F32)],
        [pltpu.VMEM((blk, DA), F32), pltpu.VMEM((blk, DA), F32), pltpu.VMEM((H, blk), F32)],
        [zqkv, zqkv, zqkv, fcum, fcol, do, lse, delta], riders)
    return dk, dv, df, carried


def _gelu(y):
    c = math.sqrt(2.0 / math.pi)
    return 0.5 * y * (1.0 + jnp.tanh(c * (y + 0.044715 * (y * y * y))))


def _gelu_grad(y):
    c = math.sqrt(2.0 / math.pi)
    th = jnp.tanh(c * (y + 0.044715 * (y * y * y)))
    return 0.5 * (1.0 + th) + 0.5 * y * (1.0 - th * th) * c * (1.0 + 3.0 * 0.044715 * y * y)


STATE_BLOCKS = SLAB_STATE // LANES


def _lane_blocks(ref, lead=()):
    return [ref[lead + (slice(None), pl.ds(b * LANES, LANES))] for b in range(2 * STATE_BLOCKS)]


def _put_lane_blocks(ref, blocks):
    for b, v in enumerate(blocks):
        ref[:, pl.ds(b * LANES, LANES)] = v


def _put_slab(x_ref, first, q, n_slab, chunk, value):
    for b in range(2 * STATE_BLOCKS):
        x_ref[b, pl.ds(first * n_slab + q, chunk, stride=n_slab), :] = value[:, b * LANES:(b + 1) * LANES]


def _get_slab(x_ref, first, q, n_slab, chunk):
    return jnp.concatenate([x_ref[b, pl.ds(first * n_slab + q, chunk, stride=n_slab), :]
                            for b in range(2 * STATE_BLOCKS)], axis=1)


def _ssm_scan_fwd(x_ref, a, h, chunk, n_slab, first=0):
    nb = STATE_BLOCKS

    def step(t, h):
        rows = pl.ds(pl.multiple_of((t + first) * n_slab, n_slab), n_slab)
        out = [None] * (2 * nb)
        for b in range(nb):
            n_re = a[b] * h[b] - a[nb + b] * h[nb + b] + x_ref[b, rows, :]
            n_im = a[b] * h[nb + b] + a[nb + b] * h[b] + x_ref[nb + b, rows, :]
            x_ref[b, rows, :] = n_re
            x_ref[nb + b, rows, :] = n_im
            out[b], out[nb + b] = n_re, n_im
        return tuple(out)

    return lax.fori_loop(0, chunk, step, tuple(h), unroll=4)


def _ssm_fwd(zu, w_b, w_c, a, d_skip, *, name):
    T, DS = zu.shape
    n_slab = DS // LANES
    chunk = _divisor(T, SSM_CHUNK, SUBLANES)
    n_chunk = T // chunk

    def body(u_ref, wb_ref, wc_ref, a_ref, ds_ref, y_ref, gy_ref, hin_ref, x_ref, h_ref):
        k = pl.program_id(0)

        @pl.when(k == 0)
        def _():
            h_ref[...] = jnp.zeros_like(h_ref)

        hin_ref[0] = h_ref[...]
        for q in range(n_slab):
            qs = pl.ds(q * LANES, LANES)
            _put_slab(x_ref, 0, q, n_slab, chunk, jnp.dot(u_ref[:, qs], wb_ref[q], preferred_element_type=F32))
        h = _ssm_scan_fwd(x_ref, _lane_blocks(a_ref), _lane_blocks(h_ref), chunk, n_slab)
        _put_lane_blocks(h_ref, h)
        for q in range(n_slab):
            qs = pl.ds(q * LANES, LANES)
            hq = _get_slab(x_ref, 0, q, n_slab, chunk).astype(BF16)
            y = jnp.dot(hq, wc_ref[q], preferred_element_type=F32) + ds_ref[:, qs] * u_ref[:, qs].astype(F32)
            y_ref[:, qs] = y
            gy_ref[:, qs] = _gelu(y).astype(gy_ref.dtype)

    whole = lambda shape: pl.BlockSpec(shape, lambda k: (0,) * len(shape))
    return pl.pallas_call(
        body, name=name, grid=(n_chunk,),
        in_specs=[pl.BlockSpec((chunk, DS), lambda k: (k, 0)), whole(w_b.shape), whole(w_c.shape),
                  whole(a.shape), whole(d_skip.shape)],
        out_specs=[pl.BlockSpec((chunk, DS), lambda k: (k, 0)), pl.BlockSpec((chunk, DS), lambda k: (k, 0)),
                   pl.BlockSpec((1, n_slab, 2 * SLAB_STATE), lambda k: (k, 0, 0))],
        out_shape=[jax.ShapeDtypeStruct((T, DS), F32), jax.ShapeDtypeStruct((T, DS), BF16),
                   jax.ShapeDtypeStruct((n_chunk, n_slab, 2 * SLAB_STATE), F32)],
        scratch_shapes=[pltpu.VMEM((2 * STATE_BLOCKS, chunk * n_slab, LANES), F32),
                        pltpu.VMEM((n_slab, 2 * SLAB_STATE), F32)],
        compiler_params=_params(("arbitrary",)),
    )(zu, w_b, w_c, a, d_skip)


def _ssm_bwd(zu, dgy, y, hin, w_b, w_bt, w_ct, a, d_skip, *, name):
    T, DS = zu.shape
    n_slab = DS // LANES
    chunk = _divisor(T, SSM_CHUNK, SUBLANES)
    n_chunk = T // chunk
    S = SLAB_STATE

    def body(u_ref, dgy_ref, y_ref, hin_ref, wb_ref, wbt_ref, wct_ref, a_ref, ds_ref,
             du_ref, dwb_ref, dwc_ref, da_ref, dds_ref, hb_ref, gb_ref, dy_ref, g_ref):
        k = pl.program_id(0)

        @pl.when(k == 0)
        def _():
            g_ref[...] = jnp.zeros_like(g_ref)
            dwb_ref[...] = jnp.zeros_like(dwb_ref)
            dwc_ref[...] = jnp.zeros_like(dwc_ref)
            da_ref[...] = jnp.zeros_like(da_ref)
            dds_ref[...] = jnp.zeros_like(dds_ref)

        nb = STATE_BLOCKS
        a = _lane_blocks(a_ref)
        hin = _lane_blocks(hin_ref, lead=(0,))

        for b in range(2 * nb):
            hb_ref[b, pl.ds(0, n_slab), :] = hin[b]
        dy_ref[...] = dgy_ref[...].astype(F32) * _gelu_grad(y_ref[...])
        for q in range(n_slab):
            qs = pl.ds(q * LANES, LANES)
            _put_slab(hb_ref, 1, q, n_slab, chunk, jnp.dot(u_ref[:, qs], wb_ref[q], preferred_element_type=F32))
            _put_slab(gb_ref, 0, q, n_slab, chunk,
                      jnp.dot(dy_ref[:, qs].astype(BF16), wct_ref[q], preferred_element_type=F32))
        _ssm_scan_fwd(hb_ref, a, hin, chunk, n_slab, first=1)

        def step(s, carry):
            g, da = carry[:2 * nb], carry[2 * nb:]
            t = chunk - 1 - s
            rows = pl.ds(pl.multiple_of(t * n_slab, n_slab), n_slab)
            g_out, da_out = [None] * (2 * nb), [None] * (2 * nb)
            for b in range(nb):
                n_re = gb_ref[b, rows, :] + a[b] * g[b] + a[nb + b] * g[nb + b]
                n_im = gb_ref[nb + b, rows, :] + a[b] * g[nb + b] - a[nb + b] * g[b]
                gb_ref[b, rows, :] = n_re
                gb_ref[nb + b, rows, :] = n_im
                p_re = hb_ref[b, rows, :]
                p_im = hb_ref[nb + b, rows, :]
                g_out[b], g_out[nb + b] = n_re, n_im
                da_out[b] = da[b] + n_re * p_re + n_im * p_im
                da_out[nb + b] = da[nb + b] + n_im * p_re - n_re * p_im
            return tuple(g_out) + tuple(da_out)

        zero = jnp.zeros((n_slab, LANES), F32)
        carry = lax.fori_loop(0, chunk, step, tuple(_lane_blocks(g_ref)) + (zero,) * (2 * nb), unroll=4)
        _put_lane_blocks(g_ref, carry[:2 * nb])
        for b in range(2 * nb):
            da_ref[:, pl.ds(b * LANES, LANES)] += carry[2 * nb + b]

        for q in range(n_slab):
            qs = pl.ds(q * LANES, LANES)
            uq = u_ref[:, qs]
            dy = dy_ref[:, qs]
            hq = _get_slab(hb_ref, 1, q, n_slab, chunk).astype(BF16)
            gq = _get_slab(gb_ref, 0, q, n_slab, chunk).astype(BF16)
            dwc_ref[q] += lax.dot_general(hq, dy.astype(BF16), (((0,), (0,)), ((), ())), preferred_element_type=F32)
            dwb_ref[q] += lax.dot_general(uq, gq, (((0,), (0,)), ((), ())), preferred_element_type=F32)
            du_ref[:, qs] = (jnp.dot(gq, wbt_ref[q], preferred_element_type=F32) + ds_ref[:, qs] * dy).astype(du_ref.dtype)
            dds_ref[:, qs] += jnp.sum(dy * uq.astype(F32), axis=0, keepdims=True)

    whole = lambda shape: pl.BlockSpec(shape, lambda k: (0,) * len(shape))
    rev = lambda k: (n_chunk - 1 - k, 0)
    return pl.pallas_call(
        body, name=name, grid=(n_chunk,),
        in_specs=[pl.BlockSpec((chunk, DS), rev), pl.BlockSpec((chunk, DS), rev), pl.BlockSpec((chunk, DS), rev),
                  pl.BlockSpec((1, n_slab, 2 * S), lambda k: (n_chunk - 1 - k, 0, 0)),
                  whole(w_b.shape), whole(w_bt.shape), whole(w_ct.shape), whole(a.shape), whole(d_skip.shape)],
        out_specs=[pl.BlockSpec((chunk, DS), rev), whole(w_b.shape), whole(w_bt.shape), whole(a.shape),
                   whole(d_skip.shape)],
        out_shape=[jax.ShapeDtypeStruct((T, DS), BF16), jax.ShapeDtypeStruct(w_b.shape, F32),
                   jax.ShapeDtypeStruct(w_bt.shape, F32), jax.ShapeDtypeStruct(a.shape, F32),
                   jax.ShapeDtypeStruct(d_skip.shape, F32)],
        scratch_shapes=[pltpu.VMEM((2 * STATE_BLOCKS, (chunk + 1) * n_slab, LANES), F32),
                        pltpu.VMEM((2 * STATE_BLOCKS, chunk * n_slab, LANES), F32),
                        pltpu.VMEM((chunk, DS), F32), pltpu.VMEM((n_slab, 2 * S), F32)],
        compiler_params=_params(("arbitrary",)),
    )(zu, dgy, y, hin, w_b, w_bt, w_ct, a, d_skip)


def _ssm_discretise(lam_re, lam_im, log_dt, b_re, b_im):
    dt = jnp.exp(log_dt)[:, None]
    mag = jnp.exp(lam_re * dt)
    a_re = mag * jnp.cos(lam_im * dt)
    a_im = mag * jnp.sin(lam_im * dt)
    den = lam_re * lam_re + lam_im * lam_im
    nr = a_re - 1.0
    z_re = (nr * lam_re + a_im * lam_im) / den
    z_im = (a_im * lam_re - nr * lam_im) / den
    bb_re = z_re[..., None] * b_re - z_im[..., None] * b_im
    bb_im = z_re[..., None] * b_im + z_im[..., None] * b_re
    return a_re, a_im, bb_re, bb_im


def _slab_in(m_re, m_im):
    G, P, C = m_re.shape
    n_slab = G // GROUPS_PER_SLAB
    eye = jnp.eye(GROUPS_PER_SLAB, dtype=m_re.dtype)

    def one(m):
        m = m.reshape(n_slab, GROUPS_PER_SLAB, P, C)
        w = jnp.einsum('sgpc,gh->sgchp', m, eye)
        return w.reshape(n_slab, GROUPS_PER_SLAB * C, GROUPS_PER_SLAB * P)

    return jnp.concatenate([one(m_re), one(m_im)], axis=2)


def _slab_in_grad(dw, G, P, C):
    n_slab = G // GROUPS_PER_SLAB
    eye = jnp.eye(GROUPS_PER_SLAB, dtype=dw.dtype)

    def one(w):
        w = w.reshape(n_slab, GROUPS_PER_SLAB, C, GROUPS_PER_SLAB, P)
        return jnp.einsum('sgchp,gh->sgpc', w, eye).reshape(G, P, C)

    return one(dw[:, :, :SLAB_STATE]), one(dw[:, :, SLAB_STATE:])


def _slab_out(c_re, c_im):
    G, C, P = c_re.shape
    n_slab = G // GROUPS_PER_SLAB
    eye = jnp.eye(GROUPS_PER_SLAB, dtype=c_re.dtype)

    def one(m):
        m = m.reshape(n_slab, GROUPS_PER_SLAB, C, P)
        w = jnp.einsum('sgcp,gh->shpgc', m, eye)
        return w.reshape(n_slab, GROUPS_PER_SLAB * P, GROUPS_PER_SLAB * C)

    return jnp.concatenate([one(c_re), one(-c_im)], axis=1)


def _slab_out_grad(dw, G, C, P):
    n_slab = G // GROUPS_PER_SLAB
    eye = jnp.eye(GROUPS_PER_SLAB, dtype=dw.dtype)

    def one(w):
        w = w.reshape(n_slab, GROUPS_PER_SLAB, P, GROUPS_PER_SLAB, C)
        return jnp.einsum('shpgc,gh->sgcp', w, eye).reshape(G, C, P)

    return one(dw[:, :SLAB_STATE, :]), -one(dw[:, SLAB_STATE:, :])


def _slab_diag(a_re, a_im):
    G, P = a_re.shape
    n_slab = G // GROUPS_PER_SLAB
    return jnp.concatenate([a_re.reshape(n_slab, SLAB_STATE), a_im.reshape(n_slab, SLAB_STATE)], axis=1)


def _merge_fwd(yab, zg, attn, *, name):
    T, D = attn.shape
    tr = _divisor(T, ROW_BLOCK, SUBLANES)

    def body(ya_ref, yb_ref, ga_ref, gb_ref, at_ref, o_ref):
        f = lambda r: r[...].astype(F32)
        ssm = f(ya_ref) * jax.nn.sigmoid(f(yb_ref))
        o_ref[...] = (jax.nn.sigmoid(f(ga_ref)) * ssm + jax.nn.sigmoid(f(gb_ref)) * f(at_ref)).astype(o_ref.dtype)

    lo = pl.BlockSpec((tr, D), lambda i: (i, 0))
    hi = pl.BlockSpec((tr, D), lambda i: (i, 1))
    return pl.pallas_call(
        body, name=name, grid=(T // tr,),
        in_specs=[lo, hi, lo, hi, lo],
        out_specs=lo,
        out_shape=jax.ShapeDtypeStruct((T, D), BF16),
        compiler_params=_params(("parallel",)),
    )(yab, yab, zg, zg, attn)


def _merge_bwd(dm, yab, zg, attn, *, name):
    T, D = attn.shape
    tr = _divisor(T, ROW_BLOCK, SUBLANES)

    def body(dm_ref, ya_ref, yb_ref, ga_ref, gb_ref, at_ref, dg_ref, dat_ref, dy_ref):
        f = lambda r: r[...].astype(F32)
        dmv, ya, at = f(dm_ref), f(ya_ref), f(at_ref)
        sa, sb, syb = jax.nn.sigmoid(f(ga_ref)), jax.nn.sigmoid(f(gb_ref)), jax.nn.sigmoid(f(yb_ref))
        ssm = ya * syb
        dssm = dmv * sa
        dg_ref[:, pl.ds(0, D)] = (dmv * ssm * sa * (1.0 - sa)).astype(dg_ref.dtype)
        dg_ref[:, pl.ds(D, D)] = (dmv * at * sb * (1.0 - sb)).astype(dg_ref.dtype)
        dat_ref[...] = (dmv * sb).astype(dat_ref.dtype)
        dy_ref[:, pl.ds(0, D)] = (dssm * syb).astype(dy_ref.dtype)
        dy_ref[:, pl.ds(D, D)] = (dssm * ya * syb * (1.0 - syb)).astype(dy_ref.dtype)

    lo = pl.BlockSpec((tr, D), lambda i: (i, 0))
    hi = pl.BlockSpec((tr, D), lambda i: (i, 1))
    both = pl.BlockSpec((tr, 2 * D), lambda i: (i, 0))
    return pl.pallas_call(
        body, name=name, grid=(T // tr,),
        in_specs=[lo, lo, hi, lo, hi, lo],
        out_specs=[both, lo, both],
        out_shape=[jax.ShapeDtypeStruct((T, 2 * D), BF16), jax.ShapeDtypeStruct((T, D), BF16),
                   jax.ShapeDtypeStruct((T, 2 * D), BF16)],
        compiler_params=_params(("parallel",)),
    )(dm, yab, yab, zg, zg, attn)


def _conv_taps(g_ref, halo_ref, i, tr):
    g0 = g_ref[...].astype(F32)
    halo = jnp.where(i > 0, halo_ref[...].astype(F32), 0.0)
    row = lax.broadcasted_iota(jnp.int32, g0.shape, 0)
    g1 = jnp.where(row == 0, halo[SUBLANES - 1:SUBLANES, :], pltpu.roll(g0, 1, axis=0))
    g2 = pltpu.roll(g0, 2, axis=0)
    g2 = jnp.where(row == 0, halo[SUBLANES - 2:SUBLANES - 1, :], g2)
    g2 = jnp.where(row == 1, halo[SUBLANES - 1:SUBLANES, :], g2)
    return g0, g1, g2


def _conv_blocks(T, FF):
    tr = _divisor(T, ROW_BLOCK, SUBLANES)
    tc = _divisor(FF, 1024, LANES)
    return tr, tc


def _conv_fwd(gu, conv_w, conv_b, *, name):
    T = gu.shape[0]
    FF = gu.shape[1] // 2
    tr, tc = _conv_blocks(T, FF)
    ncol = FF // tc

    def body(g_ref, halo_ref, u_ref, w_ref, b_ref, o_ref):
        i = pl.program_id(0)
        g0, g1, g2 = _conv_taps(g_ref, halo_ref, i, tr)
        gc = b_ref[...] + w_ref[0:1, :] * g2 + w_ref[1:2, :] * g1 + w_ref[2:3, :] * g0
        o_ref[...] = (gc * jax.nn.sigmoid(gc) * u_ref[...].astype(F32)).astype(o_ref.dtype)

    hb = tr // SUBLANES
    return pl.pallas_call(
        body, name=name, grid=(T // tr, ncol),
        in_specs=[pl.BlockSpec((tr, tc), lambda i, j: (i, j)),
                  pl.BlockSpec((SUBLANES, tc), lambda i, j: (jnp.maximum(i * hb - 1, 0), j)),
                  pl.BlockSpec((tr, tc), lambda i, j: (i, j + ncol)),
                  pl.BlockSpec((SUBLANES, tc), lambda i, j: (0, j)), pl.BlockSpec((1, tc), lambda i, j: (0, j))],
        out_specs=pl.BlockSpec((tr, tc), lambda i, j: (i, j)),
        out_shape=jax.ShapeDtypeStruct((T, FF), BF16),
        compiler_params=_params(("parallel", "parallel")),
    )(gu, gu, gu, conv_w, conv_b)


def _conv_bwd_gate(da, gu, conv_w, conv_b, *, name, riders=()):
    T = gu.shape[0]
    FF = gu.shape[1] // 2
    tr, tc = _conv_blocks(T, FF)
    ncol = FF // tc

    def body(da_ref, g_ref, halo_ref, u_ref, w_ref, b_ref, dgc_ref, du_ref, s_ref):
        i = pl.program_id(1)

        @pl.when(i == 0)
        def _():
            s_ref[...] = jnp.zeros_like(s_ref)

        g0, g1, g2 = _conv_taps(g_ref, halo_ref, i, tr)
        gc = b_ref[...] + w_ref[0:1, :] * g2 + w_ref[1:2, :] * g1 + w_ref[2:3, :] * g0
        sg = jax.nn.sigmoid(gc)
        dav = da_ref[...].astype(F32)
        du_ref[...] = (dav * gc * sg).astype(du_ref.dtype)
        dgc = dav * u_ref[...].astype(F32) * (sg * (1.0 + gc * (1.0 - sg)))
        dgc_ref[...] = dgc.astype(dgc_ref.dtype)
        s_ref[0:1, :] += jnp.sum(dgc * g2, axis=0, keepdims=True)
        s_ref[1:2, :] += jnp.sum(dgc * g1, axis=0, keepdims=True)
        s_ref[2:3, :] += jnp.sum(dgc * g0, axis=0, keepdims=True)
        s_ref[3:4, :] += jnp.sum(dgc, axis=0, keepdims=True)

    hb = tr // SUBLANES
    blk = pl.BlockSpec((tr, tc), lambda j, i: (i, j))
    (dgc, du, sums), carried = _gridded_call(
        name, (ncol, T // tr), ("parallel", "arbitrary"), body,
        [blk, blk,
         pl.BlockSpec((SUBLANES, tc), lambda j, i: (jnp.maximum(i * hb - 1, 0), j)),
         pl.BlockSpec((tr, tc), lambda j, i: (i, j + ncol)),
         pl.BlockSpec((SUBLANES, tc), lambda j, i: (0, j)), pl.BlockSpec((1, tc), lambda j, i: (0, j))],
        [blk, blk, pl.BlockSpec((SUBLANES, tc), lambda j, i: (0, j))],
        [jax.ShapeDtypeStruct((T, FF), BF16), jax.ShapeDtypeStruct((T, FF), BF16),
         jax.ShapeDtypeStruct((SUBLANES, FF), F32)],
        [], [da, gu, gu, gu, conv_w, conv_b], riders)
    return dgc, du, sums, carried


def _conv_bwd_taps(dgc, conv_w, *, name):
    T, FF = dgc.shape
    tr, tc = _conv_blocks(T, FF)
    ncol = FF // tc
    nrow = T // tr

    def body(d_ref, next_ref, w_ref, o_ref):
        i = pl.program_id(0)
        d0 = d_ref[...].astype(F32)
        nxt = jnp.where(i < nrow - 1, next_ref[...].astype(F32), 0.0)
        row = lax.broadcasted_iota(jnp.int32, d0.shape, 0)
        d1 = jnp.where(row == tr - 1, nxt[0:1, :], pltpu.roll(d0, tr - 1, axis=0))
        d2 = pltpu.roll(d0, tr - 2, axis=0)
        d2 = jnp.where(row == tr - 2, nxt[0:1, :], d2)
        d2 = jnp.where(row == tr - 1, nxt[1:2, :], d2)
        dg = w_ref[2:3, :] * d0 + w_ref[1:2, :] * d1 + w_ref[0:1, :] * d2
        o_ref[...] = dg.astype(o_ref.dtype)

    hb = tr // SUBLANES
    last = T // SUBLANES - 1
    return pl.pallas_call(
        body, name=name, grid=(nrow, ncol),
        in_specs=[pl.BlockSpec((tr, tc), lambda i, j: (i, j)),
                  pl.BlockSpec((SUBLANES, tc), lambda i, j: (jnp.minimum((i + 1) * hb, last), j)),
                  pl.BlockSpec((SUBLANES, tc), lambda i, j: (0, j))],
        out_specs=pl.BlockSpec((tr, tc), lambda i, j: (i, j)),
        out_shape=jax.ShapeDtypeStruct((T, FF), BF16),
        compiler_params=_params(("parallel", "parallel")),
    )(dgc, dgc, conv_w)


def _mesh_pos():
    return lax.axis_index("x"), lax.axis_index("y"), lax.axis_index("c")


def _flip(pos, mask):
    x, y, c = pos
    return (x ^ ((mask >> 2) & 1), y ^ ((mask >> 1) & 1), c ^ (mask & 1))


def _index_of(pos):
    x, y, c = pos
    return 4 * x + 2 * y + c


class _Rider(collections.namedtuple("_Rider", "src gather cols R c")):
    def out_shape(self):
        if not self.gather:
            shape = (N_DEV, self.R, self.c)
        elif self.cols:
            shape = (self.R, N_DEV * self.c)
        else:
            shape = (N_DEV * self.R, self.c)
        return jax.ShapeDtypeStruct(shape, self.src.dtype)

    def slab(self, ref, idx):
        if self.cols:
            return ref.at[:, pl.ds(pl.multiple_of(idx * self.c, LANES), self.c)]
        return ref.at[pl.ds(pl.multiple_of(idx * self.R, 2 * SUBLANES), self.R), :]

    def copy(self, src_ref, dst_ref, send_sems, recv_sems, me, k, arriving):
        peer = _flip(me, k)
        owner = _index_of(peer if arriving else me)
        if self.gather:
            src, dst = src_ref, self.slab(dst_ref, owner)
        else:
            src, dst = self.slab(src_ref, _index_of(peer)), dst_ref.at[owner]
        return pltpu.make_async_remote_copy(src_ref=src, dst_ref=dst, send_sem=send_sems.at[k - 1],
                                            recv_sem=recv_sems.at[k - 1], device_id=peer,
                                            device_id_type=pl.DeviceIdType.MESH)

    def own(self, src_ref, dst_ref, local_sem, me):
        my = _index_of(me)
        if self.gather:
            return pltpu.make_async_copy(src_ref, self.slab(dst_ref, my), local_sem)
        return pltpu.make_async_copy(self.slab(src_ref, my), dst_ref.at[my], local_sem)

    def start(self, src_ref, dst_ref, send_sems, recv_sems, local_sem):
        me = _mesh_pos()
        self.own(src_ref, dst_ref, local_sem, me).start()
        for k in range(1, N_DEV):
            self.copy(src_ref, dst_ref, send_sems, recv_sems, me, k, False).start()

    def wait(self, src_ref, dst_ref, send_sems, recv_sems, local_sem):
        me = _mesh_pos()
        for k in range(1, N_DEV):
            self.copy(src_ref, dst_ref, send_sems, recv_sems, me, k, True).wait_recv()
        for k in range(1, N_DEV):
            self.copy(src_ref, dst_ref, send_sems, recv_sems, me, k, False).wait_send()
        self.own(src_ref, dst_ref, local_sem, me).wait()


_RIDER_SEMS = [pltpu.SemaphoreType.DMA((N_DEV - 1,)), pltpu.SemaphoreType.DMA((N_DEV - 1,)), pltpu.SemaphoreType.DMA]
_ANY = pl.BlockSpec(memory_space=pl.ANY)


def _comm_call(riders, *, name):
    n = len(riders)

    def body(*refs):
        srcs, dsts, sems = refs[:n], refs[n:2 * n], refs[2 * n:]
        for r, rider in enumerate(riders):
            rider.start(srcs[r], dsts[r], *sems[3 * r:3 * r + 3])
        for r, rider in enumerate(riders):
            rider.wait(srcs[r], dsts[r], *sems[3 * r:3 * r + 3])

    return pl.pallas_call(
        body, name=name,
        in_specs=[_ANY] * n, out_specs=[_ANY] * n,
        out_shape=[rider.out_shape() for rider in riders],
        scratch_shapes=_RIDER_SEMS * n,
        compiler_params=pltpu.CompilerParams(has_side_effects=True),
    )(*[rider.src for rider in riders])


def _carry(riders, grid, body, in_specs, out_specs, out_shape, scratch_shapes, args):
    n, n_in, n_out, n_scratch = len(riders), len(in_specs), len(out_specs), len(scratch_shapes)

    def carrying(*refs):
        ins, refs = refs[:n_in], refs[n_in:]
        srcs, refs = refs[:n], refs[n:]
        outs, refs = refs[:n_out], refs[n_out:]
        dsts, refs = refs[:n], refs[n:]
        scratch, sems = refs[:n_scratch], refs[n_scratch:]
        ids = [pl.program_id(a) for a in range(len(grid))]
        first = functools.reduce(jnp.logical_and, [i == 0 for i in ids])
        last = functools.reduce(jnp.logical_and, [i == g - 1 for i, g in zip(ids, grid)])

        @pl.when(first)
        def _():
            for r, rider in enumerate(riders):
                rider.start(srcs[r], dsts[r], *sems[3 * r:3 * r + 3])

        body(*ins, *outs, *scratch)

        @pl.when(last)
        def _():
            for r, rider in enumerate(riders):
                rider.wait(srcs[r], dsts[r], *sems[3 * r:3 * r + 3])

    return dict(
        body=carrying,
        in_specs=list(in_specs) + [_ANY] * n,
        out_specs=list(out_specs) + [_ANY] * n,
        out_shape=list(out_shape) + [rider.out_shape() for rider in riders],
        scratch_shapes=list(scratch_shapes) + _RIDER_SEMS * n,
        args=list(args) + [rider.src for rider in riders])


def _gridded_call(name, grid, semantics, body, in_specs, out_specs, out_shape, scratch_shapes, args, riders=()):
    call = dict(body=body, in_specs=in_specs, out_specs=out_specs, out_shape=out_shape,
                scratch_shapes=scratch_shapes, args=args)
    if riders:
        call = _carry(list(riders), grid, **call)
        semantics = ("arbitrary",) * len(grid)
    outs = pl.pallas_call(
        call["body"], name=name, grid=grid, in_specs=call["in_specs"], out_specs=call["out_specs"],
        out_shape=call["out_shape"], scratch_shapes=call["scratch_shapes"],
        compiler_params=_params(semantics),
    )(*call["args"])
    n_out = len(out_shape)
    return list(outs[:n_out]), list(outs[n_out:])


def _all_gather(x, *, name):
    R, C = x.shape
    return _comm_call([_Rider(x, True, False, R, C)], name=name)[0].reshape(N_DEV, R, C)


def _exchange(parts, *, name):
    _, R, C = parts.shape
    return _comm_call([_Rider(parts.reshape(N_DEV * R, C), False, False, R, C)], name=name)[0]


def _sum_adamw(parts, w, m, v, *, name):
    R, C = w.shape
    tr = _divisor(R, max(2 * SUBLANES, ADAMW_BLOCK_ELEMS // C // SUBLANES * SUBLANES), 2 * SUBLANES)
    c1 = 1.0 - ADAM_B1 ** ADAM_STEP
    c2 = 1.0 - ADAM_B2 ** ADAM_STEP

    def body(p_ref, w_ref, m_ref, v_ref, g_ref, d_ref, nm_ref, nv_ref):
        g = p_ref[0].astype(F32)
        for s in range(1, N_DEV):
            g = g + p_ref[s].astype(F32)
        nm = ADAM_B1 * m_ref[...] + (1.0 - ADAM_B1) * g
        nv = ADAM_B2 * v_ref[...] + (1.0 - ADAM_B2) * (g * g)
        g_ref[...] = g
        nm_ref[...] = nm
        nv_ref[...] = nv
        d_ref[...] = -ADAM_LR * ((nm / c1) / (jnp.sqrt(nv / c2) + ADAM_EPS) + ADAM_WD * w_ref[...])

    blk = pl.BlockSpec((tr, C), lambda i: (i, 0))
    return pl.pallas_call(
        body, name=name, grid=(R // tr,),
        in_specs=[pl.BlockSpec((N_DEV, tr, C), lambda i: (0, i, 0)), blk, blk, blk],
        out_specs=[blk] * 4,
        out_shape=[jax.ShapeDtypeStruct((R, C), F32)] * 4,
        compiler_params=_params(("parallel",)),
    )(parts, w, m, v)


def _pad2(a, rows, cols):
    return jnp.pad(a, ((0, rows - a.shape[0]), (0, cols - a.shape[1])))


def _gather_cols(w, dtype, *, name):
    R, c = w.shape
    rp, cp = _round_up(R, 2 * SUBLANES), _round_up(c, LANES)
    g = _all_gather(_pad2(w.astype(dtype), rp, cp), name=name)
    return jnp.transpose(g[:, :R, :c], (1, 0, 2)).reshape(R, N_DEV * c)


def _column_parts(dw, c):
    R = dw.shape[0]
    parts = jnp.transpose(dw.reshape(R, N_DEV, c), (1, 0, 2))
    return jnp.pad(parts, ((0, 0), (0, _round_up(R, 2 * SUBLANES) - R), (0, _round_up(c, LANES) - c)))


def _padded_adamw(got, w, m, v, *, name):
    R, c = w.shape
    rp, cp = got.shape[1:]
    outs = _sum_adamw(got, _pad2(w, rp, cp), _pad2(m, rp, cp), _pad2(v, rp, cp), name=name)
    return [o[:R, :c] for o in outs]


def _update_cols(dw, w, m, v, *, name):
    got = _exchange(_column_parts(dw, w.shape[1]), name=name + "_exchange")
    return _padded_adamw(got, w, m, v, name=name + "_adamw")


def _update_replicated(grads, ws, ms, vs, *, name):
    sizes = [int(g.size) for g in grads]
    total = sum(sizes)
    rows = _round_up(-(-total // LANES), 2 * SUBLANES)

    def pack(arrs):
        flat = jnp.concatenate([a.reshape(-1).astype(F32) for a in arrs])
        return jnp.pad(flat, (0, rows * LANES - total)).reshape(rows, LANES)

    got = _all_gather(pack(grads), name=name + "_gather")
    outs = _sum_adamw(got, pack(ws), pack(ms), pack(vs), name=name + "_adamw")
    result = []
    for o in outs:
        flat = o.reshape(-1)
        arrs, off = [], 0
        for w, n in zip(ws, sizes):
            arrs.append(flat[off:off + n].reshape(w.shape))
            off += n
        result.append(arrs)
    return result


def kernel(x, meta, g_mix, w_in, b_f, lam_re, lam_im, log_dt, b_re, b_im, c_re, c_im, d_skip, w_glu, w_attn_o, w_out, g_ffn, w_up, conv_w, conv_b, w_down, g_final, loss_target, m_meta, m_g_mix, m_w_in, m_b_f, m_lam_re, m_lam_im, m_log_dt, m_b_re, m_b_im, m_c_re, m_c_im, m_d_skip, m_w_glu, m_w_attn_o, m_w_out, m_g_ffn, m_w_up, m_conv_w, m_conv_b, m_w_down, m_g_final, v_meta, v_g_mix, v_w_in, v_b_f, v_lam_re, v_lam_im, v_log_dt, v_b_re, v_b_im, v_c_re, v_c_im, v_d_skip, v_w_glu, v_w_attn_o, v_w_out, v_g_ffn, v_w_up, v_conv_w, v_conv_b, v_w_down, v_g_final):
    seq, D = x.shape[1], x.shape[2]
    L = N_META + seq
    T = _round_up(L, SEQ_BLOCK) if L > SEQ_BLOCK else _round_up(L, LANES)
    DS = d_skip.shape[1]
    H = b_f.shape[1]
    DA = H * HEAD_DIM
    FF = conv_b.shape[1]
    G, P, C = b_re.shape[1:]

    meta_full = _gather_cols(meta, F32, name="gather_meta")
    conv_w_full = _gather_cols(conv_w[0], F32, name="gather_conv_w")
    w_in_full = _gather_cols(w_in[0], BF16, name="gather_w_in")
    gathers = [_Rider(w[0].astype(BF16), True, cols, *w.shape[1:])
               for w, cols in ((w_attn_o, True), (w_glu, True), (w_out, False), (w_up, True), (w_down, False))]
    conv_w8 = jnp.pad(conv_w_full, ((0, SUBLANES - CONV_WIDTH), (0, 0)))

    a_re, a_im, bb_re, bb_im = _ssm_discretise(lam_re[0], lam_im[0], log_dt[0], b_re[0], b_im[0])
    w_b = _slab_in(bb_re, bb_im)
    w_c = _slab_out(c_re[0], c_im[0])
    a_slab = _slab_diag(a_re, a_im)
    w_b16, w_c16 = w_b.astype(BF16), w_c.astype(BF16)
    w_bt16, w_ct16 = jnp.swapaxes(w_b16, 1, 2), jnp.swapaxes(w_c16, 1, 2)

    h0 = jnp.concatenate([meta_full, x[0], jnp.zeros((T - L, D), F32)], axis=0)
    target = jnp.pad(loss_target[0], ((N_META, T - L), (0, 0)))
    n1 = _rms_fwd(h0, g_mix, name="rms_mix")
    o_f, o_u, o_g = 3 * DA, 3 * DA + H, 3 * DA + H + DS
    w_qkv = w_in_full[:, :o_f]
    w_f = jnp.pad(w_in_full[:, o_f:o_u], ((0, 0), (0, LANES - H)))
    w_u = w_in_full[:, o_u:o_g]
    w_g = w_in_full[:, o_g:]
    w_main = jnp.concatenate([w_qkv, w_u, w_g], axis=1)
    zqkv = _matmul(n1, w_qkv, name="mm_qkv")
    zu = _matmul(n1, w_u, name="mm_u")
    zg = _matmul(n1, w_g, name="mm_gates")
    zf = _matmul(n1, w_f, name="mm_forget", out_dtype=F32)
    f_t = zf[:, :H].T
    b_col = b_f.reshape(H, 1)
    fcum = _forget_cumsum(f_t, b_col, name="forget_cumsum")
    fcol = fcum.T
    o, lse, carried = _attn_fwd(zqkv, fcum, fcol, name="attn_fwd", riders=gathers)
    w_ao_full, w_glu_full, w_out_full, w_up_full, w_down_full = carried
    attn = _matmul(o, w_ao_full, name="mm_attn_o")
    y, gy, hin = _ssm_fwd(zu, w_b16, w_c16, a_slab, d_skip, name="ssm_fwd")
    yab = _matmul(gy, w_glu_full, name="mm_glu")
    merged = _merge_fwd(yab, zg, attn, name="merge_fwd")
    h1 = _matmul(merged, w_out_full, name="mm_out", out_dtype=F32, residual=h0)
    n2 = _rms_fwd(h1, g_ffn, name="rms_ffn")
    gu = _matmul(n2, w_up_full, name="mm_up")
    act = _conv_fwd(gu, conv_w8, conv_b, name="conv_fwd")
    h2 = _matmul(act, w_down_full, name="mm_down", out_dtype=F32, residual=h1)

    dh2, sq, dg_final = _final_loss(h2, g_final.reshape(1, D), target, seq, name="final_loss")
    loss = lax.psum(0.5 * sq[0, 0] / D, ("x", "y", "c"))
    dh2_16 = dh2.astype(BF16)
    d_act = _matmul(dh2_16, w_down_full, name="mm_down_dx", trans_b=True)
    dw_down = _matmul(act, dh2_16, name="mm_down_dw", trans_a=True)
    dgc, du2, conv_sums, (x_w_down,) = _conv_bwd_gate(
        d_act, gu, conv_w8, conv_b, name="conv_bwd_gate",
        riders=[_Rider(dw_down, False, False, *w_down.shape[1:])])
    dgu = jnp.concatenate([_conv_bwd_taps(dgc, conv_w8, name="conv_bwd_taps"), du2], axis=1)
    dn2 = _matmul(dgu, w_up_full, name="mm_up_dx", trans_b=True)
    dw_up = _matmul(n2, dgu, name="mm_up_dw", trans_a=True)
    dh1, dg_ffn = _rms_bwd(dn2, h1, g_ffn, dh2, name="rms_ffn_bwd")
    dh1_16 = dh1.astype(BF16)
    dmerged = _matmul(dh1_16, w_out_full, name="mm_out_dx", trans_b=True)
    dw_out = _matmul(merged, dh1_16, name="mm_out_dw", trans_a=True)
    dzg, dattn, dyab = _merge_bwd(dmerged, yab, zg, attn, name="merge_bwd")
    do = _matmul(dattn, w_ao_full, name="mm_attn_o_dx", trans_b=True)
    dw_ao = _matmul(o, dattn, name="mm_attn_o_dw", trans_a=True)
    dgy = _matmul(dyab, w_glu_full, name="mm_glu_dx", trans_b=True)
    dw_glu = _matmul(gy, dyab, name="mm_glu_dw", trans_a=True)
    dzu, dw_b, dw_c, da_slab, dd_skip = _ssm_bwd(zu, dgy, y, hin, w_b16, w_bt16, w_ct16, a_slab, d_skip,
                                                 name="ssm_bwd")
    delta = _attn_delta(zqkv, fcum, fcol, do, lse, name="attn_delta")
    dq = _attn_bwd_dq(zqkv, fcum, fcol, do, lse, delta, name="attn_bwd_dq")
    dk, dv, dfcum, (x_w_up, x_w_out, x_w_ao, x_w_glu) = _attn_bwd_dkv(
        zqkv, fcum, fcol, do, lse, delta, name="attn_bwd_dkv",
        riders=[_Rider(dw_up, False, True, *w_up.shape[1:]), _Rider(dw_out, False, False, *w_out.shape[1:]),
                _Rider(dw_ao, False, True, *w_attn_o.shape[1:]), _Rider(dw_glu, False, True, *w_glu.shape[1:])])
    df_t, db_f = _forget_bwd(dfcum, f_t, b_col, name="forget_bwd")
    dzf = jnp.pad(df_t.T, ((0, 0), (0, LANES - H))).astype(BF16)
    dz_main = jnp.concatenate([dq, dk, dv, dzu, dzg], axis=1)
    dw_main = _matmul(n1, dz_main, name="mm_in_dw", trans_a=True)
    dw_f = _matmul(n1, dzf, name="mm_forget_dw", trans_a=True)
    dw_in = jnp.concatenate([dw_main[:, :o_f], dw_f[:, :H], dw_main[:, o_f:]], axis=1)
    parts_in = _column_parts(dw_in, w_in.shape[2])
    dn1 = _matmul(dzf, w_f, name="mm_forget_dx", out_dtype=F32, trans_b=True)
    dn1, (x_w_in,) = _matmul(
        dz_main, w_main, name="mm_in_dx", out_dtype=F32, residual=dn1, trans_b=True,
        riders=[_Rider(parts_in.reshape(-1, parts_in.shape[2]), False, False, *parts_in.shape[1:])])
    dh0, dg_mix = _rms_bwd(dn1, h0, g_mix, dh1, name="rms_mix_bwd")
    grad_x = dh0[N_META:L][None]

    dbb_re, dbb_im = _slab_in_grad(dw_b, G, P, C)
    dc_re, dc_im = _slab_out_grad(dw_c, G, C, P)
    da_re = da_slab[:, :SLAB_STATE].reshape(G, P)
    da_im = da_slab[:, SLAB_STATE:].reshape(G, P)
    _, disc_vjp = jax.vjp(_ssm_discretise, lam_re[0], lam_im[0], log_dt[0], b_re[0], b_im[0])
    dlam_re, dlam_im, dlog_dt, db_re, db_im = disc_vjp((da_re, da_im, dbb_re, dbb_im))

    big = {}
    big["meta"] = _update_cols(dh0[:N_META], meta, m_meta, v_meta, name="meta")
    big["conv_w"] = _update_cols(conv_sums[:CONV_WIDTH], conv_w[0], m_conv_w[0], v_conv_w[0], name="conv_w")
    big["w_down"] = _sum_adamw(x_w_down, w_down[0], m_w_down[0], v_w_down[0], name="w_down_adamw")
    big["w_up"] = _sum_adamw(x_w_up, w_up[0], m_w_up[0], v_w_up[0], name="w_up_adamw")
    big["w_out"] = _sum_adamw(x_w_out, w_out[0], m_w_out[0], v_w_out[0], name="w_out_adamw")
    big["w_attn_o"] = _sum_adamw(x_w_ao, w_attn_o[0], m_w_attn_o[0], v_w_attn_o[0], name="w_attn_o_adamw")
    big["w_glu"] = _sum_adamw(x_w_glu, w_glu[0], m_w_glu[0], v_w_glu[0], name="w_glu_adamw")
    big["w_in"] = _padded_adamw(x_w_in, w_in[0], m_w_in[0], v_w_in[0], name="w_in_adamw")

    rep_names = ["g_mix", "b_f", "lam_re", "lam_im", "log_dt", "b_re", "b_im", "c_re", "c_im", "d_skip", "g_ffn",
                 "conv_b", "g_final"]
    rep_w = [g_mix, b_f, lam_re, lam_im, log_dt, b_re, b_im, c_re, c_im, d_skip, g_ffn, conv_b, g_final]
    rep_m = [m_g_mix, m_b_f, m_lam_re, m_lam_im, m_log_dt, m_b_re, m_b_im, m_c_re, m_c_im, m_d_skip, m_g_ffn,
             m_conv_b, m_g_final]
    rep_v = [v_g_mix, v_b_f, v_lam_re, v_lam_im, v_log_dt, v_b_re, v_b_im, v_c_re, v_c_im, v_d_skip, v_g_ffn,
             v_conv_b, v_g_final]
    rep_g = [dg_mix, db_f[:, 0], dlam_re, dlam_im, dlog_dt, db_re, db_im, dc_re, dc_im, dd_skip, dg_ffn,
             conv_sums[CONV_WIDTH], dg_final]
    rep = _update_replicated(rep_g, rep_w, rep_m, rep_v, name="replicated")
    rep_out = {n: [rep[k][i] for k in range(4)] for i, n in enumerate(rep_names)}

    order = ["meta", "g_mix", "w_in", "b_f", "lam_re", "lam_im", "log_dt", "b_re", "b_im", "c_re", "c_im", "d_skip",
             "w_glu", "w_attn_o", "w_out", "g_ffn", "w_up", "conv_w", "conv_b", "w_down", "g_final"]
    outs = [loss, grad_x]
    for kind in range(4):
        for n in order:
            if n in big:
                outs.append(big[n][kind] if n == "meta" else big[n][kind][None])
            else:
                outs.append(rep_out[n][kind])
    return tuple(outs)
```

```python
import collections
import functools
import math

import jax
import jax.numpy as jnp
from jax import lax
from jax.experimental import pallas as pl
from jax.experimental.pallas import tpu as pltpu

F32 = jnp.float32
BF16 = jnp.bfloat16

N_META = 16
EPS = 1e-6
HEAD_DIM = 128
SSM_GROUP = 16
SSM_STATE = 64
GROUPS_PER_SLAB = 8
SLAB_STATE = GROUPS_PER_SLAB * SSM_STATE
CONV_WIDTH = 3
N_DEV = 8

ADAM_LR = 0.001
ADAM_B1 = 0.9
ADAM_B2 = 0.999
ADAM_EPS = 1e-08
ADAM_WD = 0.01
ADAM_STEP = 10

LANES = 128
SUBLANES = 8
VMEM_LIMIT = 52 * 1024 * 1024

SEQ_BLOCK = 768
ATT_BLOCK = 384
ROW_BLOCK = 256
SSM_CHUNK = 128
ADAMW_BLOCK_ELEMS = 1 << 17
MASK_VALUE = -1e30


def _round_up(n, m):
    return (n + m - 1) // m * m


def _divisor(n, target, mult):
    if n <= target:
        return n
    best = None
    for d in range(mult, target + 1, mult):
        if n % d == 0:
            best = d
    assert best is not None, (n, target, mult)
    return best


def _params(sem):
    return pltpu.CompilerParams(dimension_semantics=sem, vmem_limit_bytes=VMEM_LIMIT)


def _matmul(a, b, *, name, trans_a=False, trans_b=False, out_dtype=None, residual=None, riders=(),
            tm=768, tn=1024, tk=2048):
    out_dtype = BF16 if out_dtype is None else out_dtype
    assert not (trans_a and trans_b)
    if trans_a:
        K, M = a.shape
    else:
        M, K = a.shape
    if trans_b:
        N, K2 = b.shape
    else:
        K2, N = b.shape
    assert K == K2, (a.shape, b.shape)
    tm = _divisor(M, tm, LANES if trans_a else SUBLANES)
    tn = _divisor(N, tn, LANES)
    tk = _divisor(K, tk, LANES if not trans_a else SUBLANES)
    nk = K // tk

    def body(*refs):
        if residual is None:
            a_ref, b_ref, o_ref, acc_ref = refs
        else:
            a_ref, b_ref, r_ref, o_ref, acc_ref = refs
        k = pl.program_id(2)

        @pl.when(k == 0)
        def _():
            acc_ref[...] = jnp.zeros_like(acc_ref)

        contract = (0, 0) if trans_a else (1, 1) if trans_b else (1, 0)
        acc_ref[...] += lax.dot_general(a_ref[...], b_ref[...], (((contract[0],), (contract[1],)), ((), ())),
                                        preferred_element_type=F32)

        @pl.when(k == nk - 1)
        def _():
            r = acc_ref[...]
            if residual is not None:
                r = r + r_ref[...]
            o_ref[...] = r.astype(o_ref.dtype)

    if trans_a:
        a_spec = pl.BlockSpec((tk, tm), lambda i, j, k: (k, i))
    else:
        a_spec = pl.BlockSpec((tm, tk), lambda i, j, k: (i, k))
    if trans_b:
        b_spec = pl.BlockSpec((tn, tk), lambda i, j, k: (j, k))
    else:
        b_spec = pl.BlockSpec((tk, tn), lambda i, j, k: (k, j))
    in_specs = [a_spec, b_spec]
    args = [a, b]
    if residual is not None:
        in_specs.append(pl.BlockSpec((tm, tn), lambda i, j, k: (i, j)))
        args.append(residual)
    (out,), carried = _gridded_call(
        name, (M // tm, N // tn, nk), ("parallel", "parallel", "arbitrary"), body, in_specs,
        [pl.BlockSpec((tm, tn), lambda i, j, k: (i, j))], [jax.ShapeDtypeStruct((M, N), out_dtype)],
        [pltpu.VMEM((tm, tn), F32)], args, riders)
    return (out, carried) if riders else out


def _rms_fwd(h, g, *, name):
    T, D = h.shape
    tr = _divisor(T, ROW_BLOCK, SUBLANES)

    def body(h_ref, g_ref, o_ref):
        x = h_ref[...]
        r = lax.rsqrt(jnp.mean(x * x, axis=-1, keepdims=True) + EPS)
        o_ref[...] = (x * r * g_ref[...]).astype(o_ref.dtype)

    return pl.pallas_call(
        body, name=name, grid=(T // tr,),
        in_specs=[pl.BlockSpec((tr, D), lambda i: (i, 0)), pl.BlockSpec((1, D), lambda i: (0, 0))],
        out_specs=pl.BlockSpec((tr, D), lambda i: (i, 0)),
        out_shape=jax.ShapeDtypeStruct((T, D), BF16),
        compiler_params=_params(("parallel",)),
    )(h, g)


def _rms_bwd(dn, h, g, dres, *, name):
    T, D = h.shape
    tr = _divisor(T, ROW_BLOCK, SUBLANES)

    def body(dn_ref, h_ref, g_ref, dres_ref, dh_ref, dg_ref):
        i = pl.program_id(0)

        @pl.when(i == 0)
        def _():
            dg_ref[...] = jnp.zeros_like(dg_ref)

        x = h_ref[...]
        dn_v = dn_ref[...].astype(F32)
        r = lax.rsqrt(jnp.mean(x * x, axis=-1, keepdims=True) + EPS)
        xh = x * r
        dg_ref[...] += jnp.sum(dn_v * xh, axis=0, keepdims=True)
        dxh = dn_v * g_ref[...]
        dh_ref[...] = dres_ref[...] + r * (dxh - xh * jnp.mean(dxh * xh, axis=-1, keepdims=True))

    return pl.pallas_call(
        body, name=name, grid=(T // tr,),
        in_specs=[pl.BlockSpec((tr, D), lambda i: (i, 0)), pl.BlockSpec((tr, D), lambda i: (i, 0)),
                  pl.BlockSpec((1, D), lambda i: (0, 0)), pl.BlockSpec((tr, D), lambda i: (i, 0))],
        out_specs=[pl.BlockSpec((tr, D), lambda i: (i, 0)), pl.BlockSpec((1, D), lambda i: (0, 0))],
        out_shape=[jax.ShapeDtypeStruct((T, D), F32), jax.ShapeDtypeStruct((1, D), F32)],
        compiler_params=_params(("arbitrary",)),
    )(dn, h, g, dres)


def _final_loss(h, g, target, n_valid, *, name):
    T, D = h.shape
    tr = _divisor(T, ROW_BLOCK, SUBLANES)

    def body(h_ref, g_ref, t_ref, dh_ref, sq_ref, dg_ref):
        i = pl.program_id(0)

        @pl.when(i == 0)
        def _():
            sq_ref[...] = jnp.zeros_like(sq_ref)
            dg_ref[...] = jnp.zeros_like(dg_ref)

        x = h_ref[...]
        r = lax.rsqrt(jnp.mean(x * x, axis=-1, keepdims=True) + EPS)
        xh = x * r
        gv = g_ref[...]
        row = i * tr + lax.broadcasted_iota(jnp.int32, (tr, 1), 0)
        valid = (row >= N_META) & (row < N_META + n_valid)
        err = jnp.where(valid, xh * gv - t_ref[...], 0.0)
        sq_ref[...] += jnp.sum(err * err)
        dy = err * (1.0 / D)
        dg_ref[...] += jnp.sum(dy * xh, axis=0, keepdims=True)
        dxh = dy * gv
        dh_ref[...] = r * (dxh - xh * jnp.mean(dxh * xh, axis=-1, keepdims=True))

    return pl.pallas_call(
        body, name=name, grid=(T // tr,),
        in_specs=[pl.BlockSpec((tr, D), lambda i: (i, 0)), pl.BlockSpec((1, D), lambda i: (0, 0)),
                  pl.BlockSpec((tr, D), lambda i: (i, 0))],
        out_specs=[pl.BlockSpec((tr, D), lambda i: (i, 0)), pl.BlockSpec((SUBLANES, LANES), lambda i: (0, 0)),
                   pl.BlockSpec((1, D), lambda i: (0, 0))],
        out_shape=[jax.ShapeDtypeStruct((T, D), F32), jax.ShapeDtypeStruct((SUBLANES, LANES), F32),
                   jax.ShapeDtypeStruct((1, D), F32)],
        compiler_params=_params(("arbitrary",)),
    )(h, g, target)


def _prefix_sum_lanes(x):
    lane = lax.broadcasted_iota(jnp.int32, x.shape, 1)
    d = 1
    while d < LANES:
        x = x + jnp.where(lane >= d, pltpu.roll(x, d, axis=1), 0.0)
        d *= 2
    return x


def _forget_cumsum(ft, bf, *, name):
    H, T = ft.shape
    nb = T // LANES

    def body(f_ref, b_ref, o_ref):
        carry = jnp.zeros((H, 1), F32)
        for j in range(nb):
            sl = pl.ds(j * LANES, LANES)
            lf = jax.nn.log_sigmoid(f_ref[:, sl] + b_ref[...])
            c = _prefix_sum_lanes(lf) + carry
            o_ref[:, sl] = c
            carry = c[:, LANES - 1:LANES]

    return pl.pallas_call(
        body, name=name,
        in_specs=[pl.BlockSpec(memory_space=pltpu.VMEM), pl.BlockSpec(memory_space=pltpu.VMEM)],
        out_specs=pl.BlockSpec(memory_space=pltpu.VMEM),
        out_shape=jax.ShapeDtypeStruct((H, T), F32),
        compiler_params=pltpu.CompilerParams(vmem_limit_bytes=VMEM_LIMIT),
    )(ft, bf)


def _forget_bwd(dF, ft, bf, *, name):
    H, T = ft.shape
    nb = T // LANES

    def body(d_ref, f_ref, b_ref, o_ref, s_ref):
        carry = jnp.zeros((H, 1), F32)
        acc = jnp.zeros((H, LANES), F32)
        for j in reversed(range(nb)):
            sl = pl.ds(j * LANES, LANES)
            d = d_ref[:, sl]
            pre = _prefix_sum_lanes(d)
            tot = pre[:, LANES - 1:LANES]
            dlf = tot - pre + d + carry
            carry = carry + tot
            z = f_ref[:, sl] + b_ref[...]
            df = dlf * jax.nn.sigmoid(-z)
            o_ref[:, sl] = df
            acc = acc + df
        s_ref[...] = jnp.broadcast_to(jnp.sum(acc, axis=1, keepdims=True), (H, LANES))

    return pl.pallas_call(
        body, name=name,
        in_specs=[pl.BlockSpec(memory_space=pltpu.VMEM)] * 3,
        out_specs=[pl.BlockSpec(memory_space=pltpu.VMEM)] * 2,
        out_shape=[jax.ShapeDtypeStruct((H, T), F32), jax.ShapeDtypeStruct((H, LANES), F32)],
        compiler_params=pltpu.CompilerParams(vmem_limit_bytes=VMEM_LIMIT),
    )(dF, ft, bf)


def _scores(q_ref, k_ref, fq_ref, fk_ref, h, blk, scale, diagonal):
    hs = pl.ds(h * HEAD_DIM, HEAD_DIM)
    s = lax.dot_general(k_ref[:, hs], q_ref[:, hs], (((1,), (1,)), ((), ())), preferred_element_type=F32)
    s = s * scale + (fq_ref[h:h + 1, :] - fk_ref[:, h:h + 1])
    if diagonal:
        key = lax.broadcasted_iota(jnp.int32, (blk, blk), 0)
        query = lax.broadcasted_iota(jnp.int32, (blk, blk), 1)
        s = jnp.where(key <= query, s, MASK_VALUE)
    return s


def _causal_blocks(i, j, compute):
    @pl.when(j < i)
    def _():
        compute(False)

    @pl.when(j == i)
    def _():
        compute(True)


def _attn_fwd(zqkv, fcum, fcol, *, name, riders=()):
    T = zqkv.shape[0]
    DA = zqkv.shape[1] // 3
    H = DA // HEAD_DIM
    blk = _divisor(T, ATT_BLOCK, LANES)
    nb = T // blk
    scale = HEAD_DIM ** -0.5

    def body(q_ref, k_ref, v_ref, fq_ref, fk_ref, o_ref, lse_ref, m_ref, l_ref, acc_ref):
        i = pl.program_id(0)
        j = pl.program_id(1)

        @pl.when(j == 0)
        def _():
            m_ref[...] = jnp.full_like(m_ref, MASK_VALUE)
            l_ref[...] = jnp.zeros_like(l_ref)
            acc_ref[...] = jnp.zeros_like(acc_ref)

        def compute(diagonal):
            for h in range(H):
                hs = pl.ds(h * HEAD_DIM, HEAD_DIM)
                s = _scores(q_ref, k_ref, fq_ref, fk_ref, h, blk, scale, diagonal)
                m_prev = m_ref[h:h + 1, :]
                m_new = jnp.maximum(m_prev, jnp.max(s, axis=0, keepdims=True))
                alpha = jnp.exp(m_prev - m_new)
                p = jnp.exp(s - m_new)
                l_ref[h:h + 1, :] = alpha * l_ref[h:h + 1, :] + jnp.sum(p, axis=0, keepdims=True)
                acc_ref[hs, :] = alpha * acc_ref[hs, :] + lax.dot_general(
                    v_ref[:, hs], p.astype(BF16), (((0,), (0,)), ((), ())), preferred_element_type=F32)
                m_ref[h:h + 1, :] = m_new

        _causal_blocks(i, j, compute)

        @pl.when(j == nb - 1)
        def _():
            for h in range(H):
                hs = pl.ds(h * HEAD_DIM, HEAD_DIM)
                l = l_ref[h:h + 1, :]
                o_ref[:, hs] = (acc_ref[hs, :] / l).T.astype(o_ref.dtype)
                lse_ref[h:h + 1, :] = m_ref[h:h + 1, :] + jnp.log(l)

    kv = lambda c: (lambda i, j: (jnp.minimum(j, i), c))
    (o, lse), carried = _gridded_call(
        name, (nb, nb), ("parallel", "arbitrary"), body,
        [pl.BlockSpec((blk, DA), lambda i, j: (i, 0)),
         pl.BlockSpec((blk, DA), kv(1)), pl.BlockSpec((blk, DA), kv(2)),
         pl.BlockSpec((H, blk), lambda i, j: (0, i)),
         pl.BlockSpec((blk, H), lambda i, j: (jnp.minimum(j, i), 0))],
        [pl.BlockSpec((blk, DA), lambda i, j: (i, 0)), pl.BlockSpec((H, blk), lambda i, j: (0, i))],
        [jax.ShapeDtypeStruct((T, DA), BF16), jax.ShapeDtypeStruct((H, T), F32)],
        [pltpu.VMEM((H, blk), F32), pltpu.VMEM((H, blk), F32), pltpu.VMEM((DA, blk), F32)],
        [zqkv, zqkv, zqkv, fcum, fcol], riders)
    return o, lse, carried


def _attn_delta(zqkv, fcum, fcol, do, lse, *, name):
    T = zqkv.shape[0]
    DA = zqkv.shape[1] // 3
    H = DA // HEAD_DIM
    blk = _divisor(T, ATT_BLOCK, LANES)
    nb = T // blk
    scale = HEAD_DIM ** -0.5

    def body(q_ref, k_ref, v_ref, fq_ref, fk_ref, do_ref, lse_ref, d_ref):
        i = pl.program_id(0)
        j = pl.program_id(1)

        @pl.when(j == 0)
        def _():
            d_ref[...] = jnp.zeros_like(d_ref)

        def compute(diagonal):
            for h in range(H):
                hs = pl.ds(h * HEAD_DIM, HEAD_DIM)
                s = _scores(q_ref, k_ref, fq_ref, fk_ref, h, blk, scale, diagonal)
                p = jnp.exp(s - lse_ref[h:h + 1, :])
                dp = lax.dot_general(v_ref[:, hs], do_ref[:, hs], (((1,), (1,)), ((), ())),
                                     preferred_element_type=F32)
                d_ref[h:h + 1, :] += jnp.sum(p * dp, axis=0, keepdims=True)

        _causal_blocks(i, j, compute)

    kv = lambda c: (lambda i, j: (jnp.minimum(j, i), c))
    row = lambda i, j: (i, 0)
    lane = lambda i, j: (0, i)
    return pl.pallas_call(
        body, name=name, grid=(nb, nb),
        in_specs=[pl.BlockSpec((blk, DA), row), pl.BlockSpec((blk, DA), kv(1)), pl.BlockSpec((blk, DA), kv(2)),
                  pl.BlockSpec((H, blk), lane), pl.BlockSpec((blk, H), lambda i, j: (jnp.minimum(j, i), 0)),
                  pl.BlockSpec((blk, DA), row), pl.BlockSpec((H, blk), lane)],
        out_specs=pl.BlockSpec((H, blk), lane),
        out_shape=jax.ShapeDtypeStruct((H, T), F32),
        compiler_params=_params(("parallel", "arbitrary")),
    )(zqkv, zqkv, zqkv, fcum, fcol, do, lse)


def _attn_bwd_dq(zqkv, fcum, fcol, do, lse, delta, *, name):
    T = zqkv.shape[0]
    DA = zqkv.shape[1] // 3
    H = DA // HEAD_DIM
    blk = _divisor(T, ATT_BLOCK, LANES)
    nb = T // blk
    scale = HEAD_DIM ** -0.5

    def body(q_ref, k_ref, v_ref, fq_ref, fk_ref, do_ref, lse_ref, dl_ref, dq_ref, acc_ref):
        i = pl.program_id(0)
        j = pl.program_id(1)

        @pl.when(j == 0)
        def _():
            acc_ref[...] = jnp.zeros_like(acc_ref)

        def compute(diagonal):
            for h in range(H):
                hs = pl.ds(h * HEAD_DIM, HEAD_DIM)
                s = _scores(q_ref, k_ref, fq_ref, fk_ref, h, blk, scale, diagonal)
                p = jnp.exp(s - lse_ref[h:h + 1, :])
                dp = lax.dot_general(v_ref[:, hs], do_ref[:, hs], (((1,), (1,)), ((), ())),
                                     preferred_element_type=F32)
                ds = p * (dp - dl_ref[h:h + 1, :])
                acc_ref[:, hs] += scale * lax.dot_general(ds.astype(BF16), k_ref[:, hs], (((0,), (0,)), ((), ())),
                                                          preferred_element_type=F32)

        _causal_blocks(i, j, compute)

        @pl.when(j == nb - 1)
        def _():
            dq_ref[...] = acc_ref[...].astype(dq_ref.dtype)

    kv = lambda c: (lambda i, j: (jnp.minimum(j, i), c))
    row = lambda i, j: (i, 0)
    lane = lambda i, j: (0, i)
    return pl.pallas_call(
        body, name=name, grid=(nb, nb),
        in_specs=[pl.BlockSpec((blk, DA), row), pl.BlockSpec((blk, DA), kv(1)), pl.BlockSpec((blk, DA), kv(2)),
                  pl.BlockSpec((H, blk), lane), pl.BlockSpec((blk, H), lambda i, j: (jnp.minimum(j, i), 0)),
                  pl.BlockSpec((blk, DA), row), pl.BlockSpec((H, blk), lane), pl.BlockSpec((H, blk), lane)],
        out_specs=pl.BlockSpec((blk, DA), row),
        out_shape=jax.ShapeDtypeStruct((T, DA), BF16),
        scratch_shapes=[pltpu.VMEM((blk, DA), F32)],
        compiler_params=_params(("parallel", "arbitrary")),
    )(zqkv, zqkv, zqkv, fcum, fcol, do, lse, delta)


def _attn_bwd_dkv(zqkv, fcum, fcol, do, lse, delta, *, name, riders=()):
    T = zqkv.shape[0]
    DA = zqkv.shape[1] // 3
    H = DA // HEAD_DIM
    blk = _divisor(T, ATT_BLOCK, LANES)
    nb = T // blk
    scale = HEAD_DIM ** -0.5

    def body(q_ref, k_ref, v_ref, fq_ref, fk_ref, do_ref, lse_ref, dl_ref, dk_ref, dv_ref, df_ref,
             dk_acc, dv_acc, df_acc):
        j = pl.program_id(0)
        i = pl.program_id(1)

        @pl.when(i == 0)
        def _():
            dk_acc[...] = jnp.zeros_like(dk_acc)
            dv_acc[...] = jnp.zeros_like(dv_acc)
            df_acc[...] = jnp.zeros_like(df_acc)

        def compute(diagonal):
            for h in range(H):
                hs = pl.ds(h * HEAD_DIM, HEAD_DIM)
                s = _scores(q_ref, k_ref, fq_ref, fk_ref, h, blk, scale, diagonal)
                p = jnp.exp(s - lse_ref[h:h + 1, :])
                dov = do_ref[:, hs]
                dv_acc[:, hs] += jnp.dot(p.astype(BF16), dov, preferred_element_type=F32)
                dp = lax.dot_general(v_ref[:, hs], dov, (((1,), (1,)), ((), ())), preferred_element_type=F32)
                ds = p * (dp - dl_ref[h:h + 1, :])
                dk_acc[:, hs] += scale * jnp.dot(ds.astype(BF16), q_ref[:, hs], preferred_element_type=F32)
                df_acc[:, h:h + 1] -= jnp.sum(ds, axis=1, keepdims=True)

        _causal_blocks(i, j, compute)

        @pl.when(i == nb - 1)
        def _():
            dk_ref[...] = dk_acc[...].astype(dk_ref.dtype)
            dv_ref[...] = dv_acc[...].astype(dv_ref.dtype)
            df_ref[...] = df_acc[...]

    qrow = lambda j, i: (jnp.maximum(i, j), 0)
    qlane = lambda j, i: (0, jnp.maximum(i, j))
    kcol = lambda c: (lambda j, i: (j, c))
    (dk, dv, df), carried = _gridded_call(
        name, (nb, nb), ("parallel", "arbitrary"), body,
        [pl.BlockSpec((blk, DA), qrow), pl.BlockSpec((blk, DA), kcol(1)), pl.BlockSpec((blk, DA), kcol(2)),
         pl.BlockSpec((H, blk), qlane), pl.BlockSpec((blk, H), kcol(0)),
         pl.BlockSpec((blk, DA), qrow), pl.BlockSpec((H, blk), qlane), pl.BlockSpec((H, blk), qlane)],
        [pl.BlockSpec((blk, DA), kcol(0)), pl.BlockSpec((blk, DA), kcol(0)), pl.BlockSpec((blk, H), kcol(0))],
        [jax.ShapeDtypeStruct((T, DA), BF16), jax.ShapeDtypeStruct((T, DA), BF16),
         jax.ShapeDtypeStruct((T, H), F32)],
        [pltpu.VMEM((blk, DA), F32), pltpu.VMEM((blk, DA), F32), pltpu.VMEM((blk, H), F32)],
        [zqkv, zqkv, zqkv, fcum, fcol, do, lse, delta], riders)
    return dk, dv, df, carried


def _gelu(y):
    c = math.sqrt(2.0 / math.pi)
    return 0.5 * y * (1.0 + jnp.tanh(c * (y + 0.044715 * (y * y * y))))


def _gelu_grad(y):
    c = math.sqrt(2.0 / math.pi)
    th = jnp.tanh(c * (y + 0.044715 * (y * y * y)))
    return 0.5 * (1.0 + th) + 0.5 * y * (1.0 - th * th) * c * (1.0 + 3.0 * 0.044715 * y * y)


STATE_BLOCKS = SLAB_STATE // LANES


def _lane_blocks(ref, lead=()):
    return [ref[lead + (slice(None), pl.ds(b * LANES, LANES))] for b in range(2 * STATE_BLOCKS)]


def _put_lane_blocks(ref, blocks):
    for b, v in enumerate(blocks):
        ref[:, pl.ds(b * LANES, LANES)] = v


def _put_slab(x_ref, first, q, n_slab, chunk, value):
    for b in range(2 * STATE_BLOCKS):
        x_ref[b, pl.ds(first * n_slab + q, chunk, stride=n_slab), :] = value[:, b * LANES:(b + 1) * LANES]


def _get_slab(x_ref, first, q, n_slab, chunk):
    return jnp.concatenate([x_ref[b, pl.ds(first * n_slab + q, chunk, stride=n_slab), :]
                            for b in range(2 * STATE_BLOCKS)], axis=1)


def _ssm_scan_fwd(x_ref, a, h, chunk, n_slab, first=0):
    nb = STATE_BLOCKS

    def step(t, h):
        rows = pl.ds(pl.multiple_of((t + first) * n_slab, n_slab), n_slab)
        out = [None] * (2 * nb)
        for b in range(nb):
            n_re = a[b] * h[b] - a[nb + b] * h[nb + b] + x_ref[b, rows, :]
            n_im = a[b] * h[nb + b] + a[nb + b] * h[b] + x_ref[nb + b, rows, :]
            x_ref[b, rows, :] = n_re
            x_ref[nb + b, rows, :] = n_im
            out[b], out[nb + b] = n_re, n_im
        return tuple(out)

    return lax.fori_loop(0, chunk, step, tuple(h), unroll=4)


def _ssm_fwd(zu, w_b, w_c, a, d_skip, *, name):
    T, DS = zu.shape
    n_slab = DS // LANES
    chunk = _divisor(T, SSM_CHUNK, SUBLANES)
    n_chunk = T // chunk

    def body(u_ref, wb_ref, wc_ref, a_ref, ds_ref, y_ref, gy_ref, hin_ref, x_ref, h_ref):
        k = pl.program_id(0)

        @pl.when(k == 0)
        def _():
            h_ref[...] = jnp.zeros_like(h_ref)

        hin_ref[0] = h_ref[...]
        for q in range(n_slab):
            qs = pl.ds(q * LANES, LANES)
            _put_slab(x_ref, 0, q, n_slab, chunk, jnp.dot(u_ref[:, qs], wb_ref[q], preferred_element_type=F32))
        h = _ssm_scan_fwd(x_ref, _lane_blocks(a_ref), _lane_blocks(h_ref), chunk, n_slab)
        _put_lane_blocks(h_ref, h)
        for q in range(n_slab):
            qs = pl.ds(q * LANES, LANES)
            hq = _get_slab(x_ref, 0, q, n_slab, chunk).astype(BF16)
            y = jnp.dot(hq, wc_ref[q], preferred_element_type=F32) + ds_ref[:, qs] * u_ref[:, qs].astype(F32)
            y_ref[:, qs] = y
            gy_ref[:, qs] = _gelu(y).astype(gy_ref.dtype)

    whole = lambda shape: pl.BlockSpec(shape, lambda k: (0,) * len(shape))
    return pl.pallas_call(
        body, name=name, grid=(n_chunk,),
        in_specs=[pl.BlockSpec((chunk, DS), lambda k: (k, 0)), whole(w_b.shape), whole(w_c.shape),
                  whole(a.shape), whole(d_skip.shape)],
        out_specs=[pl.BlockSpec((chunk, DS), lambda k: (k, 0)), pl.BlockSpec((chunk, DS), lambda k: (k, 0)),
                   pl.BlockSpec((1, n_slab, 2 * SLAB_STATE), lambda k: (k, 0, 0))],
        out_shape=[jax.ShapeDtypeStruct((T, DS), F32), jax.ShapeDtypeStruct((T, DS), BF16),
                   jax.ShapeDtypeStruct((n_chunk, n_slab, 2 * SLAB_STATE), F32)],
        scratch_shapes=[pltpu.VMEM((2 * STATE_BLOCKS, chunk * n_slab, LANES), F32),
                        pltpu.VMEM((n_slab, 2 * SLAB_STATE), F32)],
        compiler_params=_params(("arbitrary",)),
    )(zu, w_b, w_c, a, d_skip)


def _ssm_bwd(zu, dgy, y, hin, w_b, w_bt, w_ct, a, d_skip, *, name):
    T, DS = zu.shape
    n_slab = DS // LANES
    chunk = _divisor(T, SSM_CHUNK, SUBLANES)
    n_chunk = T // chunk
    S = SLAB_STATE

    def body(u_ref, dgy_ref, y_ref, hin_ref, wb_ref, wbt_ref, wct_ref, a_ref, ds_ref,
             du_ref, dwb_ref, dwc_ref, da_ref, dds_ref, hb_ref, gb_ref, dy_ref, g_ref):
        k = pl.program_id(0)

        @pl.when(k == 0)
        def _():
            g_ref[...] = jnp.zeros_like(g_ref)
            dwb_ref[...] = jnp.zeros_like(dwb_ref)
            dwc_ref[...] = jnp.zeros_like(dwc_ref)
            da_ref[...] = jnp.zeros_like(da_ref)
            dds_ref[...] = jnp.zeros_like(dds_ref)

        nb = STATE_BLOCKS
        a = _lane_blocks(a_ref)
        hin = _lane_blocks(hin_ref, lead=(0,))

        for b in range(2 * nb):
            hb_ref[b, pl.ds(0, n_slab), :] = hin[b]
        dy_ref[...] = dgy_ref[...].astype(F32) * _gelu_grad(y_ref[...])
        for q in range(n_slab):
            qs = pl.ds(q * LANES, LANES)
            _put_slab(hb_ref, 1, q, n_slab, chunk, jnp.dot(u_ref[:, qs], wb_ref[q], preferred_element_type=F32))
            _put_slab(gb_ref, 0, q, n_slab, chunk,
                      jnp.dot(dy_ref[:, qs].astype(BF16), wct_ref[q], preferred_element_type=F32))
        _ssm_scan_fwd(hb_ref, a, hin, chunk, n_slab, first=1)

        def step(s, carry):
            g, da = carry[:2 * nb], carry[2 * nb:]
            t = chunk - 1 - s
            rows = pl.ds(pl.multiple_of(t * n_slab, n_slab), n_slab)
            g_out, da_out = [None] * (2 * nb), [None] * (2 * nb)
            for b in range(nb):
                n_re = gb_ref[b, rows, :] + a[b] * g[b] + a[nb + b] * g[nb + b]
                n_im = gb_ref[nb + b, rows, :] + a[b] * g[nb + b] - a[nb + b] * g[b]
                gb_ref[b, rows, :] = n_re
                gb_ref[nb + b, rows, :] = n_im
                p_re = hb_ref[b, rows, :]
                p_im = hb_ref[nb + b, rows, :]
                g_out[b], g_out[nb + b] = n_re, n_im
                da_out[b] = da[b] + n_re * p_re + n_im * p_im
                da_out[nb + b] = da[nb + b] + n_im * p_re - n_re * p_im
            return tuple(g_out) + tuple(da_out)

        zero = jnp.zeros((n_slab, LANES), F32)
        carry = lax.fori_loop(0, chunk, step, tuple(_lane_blocks(g_ref)) + (zero,) * (2 * nb), unroll=4)
        _put_lane_blocks(g_ref, carry[:2 * nb])
        for b in range(2 * nb):
            da_ref[:, pl.ds(b * LANES, LANES)] += carry[2 * nb + b]

        for q in range(n_slab):
            qs = pl.ds(q * LANES, LANES)
            uq = u_ref[:, qs]
            dy = dy_ref[:, qs]
            hq = _get_slab(hb_ref, 1, q, n_slab, chunk).astype(BF16)
            gq = _get_slab(gb_ref, 0, q, n_slab, chunk).astype(BF16)
            dwc_ref[q] += lax.dot_general(hq, dy.astype(BF16), (((0,), (0,)), ((), ())), preferred_element_type=F32)
            dwb_ref[q] += lax.dot_general(uq, gq, (((0,), (0,)), ((), ())), preferred_element_type=F32)
            du_ref[:, qs] = (jnp.dot(gq, wbt_ref[q], preferred_element_type=F32) + ds_ref[:, qs] * dy).astype(du_ref.dtype)
            dds_ref[:, qs] += jnp.sum(dy * uq.astype(F32), axis=0, keepdims=True)

    whole = lambda shape: pl.BlockSpec(shape, lambda k: (0,) * len(shape))
    rev = lambda k: (n_chunk - 1 - k, 0)
    return pl.pallas_call(
        body, name=name, grid=(n_chunk,),
        in_specs=[pl.BlockSpec((chunk, DS), rev), pl.BlockSpec((chunk, DS), rev), pl.BlockSpec((chunk, DS), rev),
                  pl.BlockSpec((1, n_slab, 2 * S), lambda k: (n_chunk - 1 - k, 0, 0)),
                  whole(w_b.shape), whole(w_bt.shape), whole(w_ct.shape), whole(a.shape), whole(d_skip.shape)],
        out_specs=[pl.BlockSpec((chunk, DS), rev), whole(w_b.shape), whole(w_bt.shape), whole(a.shape),
                   whole(d_skip.shape)],
        out_shape=[jax.ShapeDtypeStruct((T, DS), BF16), jax.ShapeDtypeStruct(w_b.shape, F32),
                   jax.ShapeDtypeStruct(w_bt.shape, F32), jax.ShapeDtypeStruct(a.shape, F32),
                   jax.ShapeDtypeStruct(d_skip.shape, F32)],
        scratch_shapes=[pltpu.VMEM((2 * STATE_BLOCKS, (chunk + 1) * n_slab, LANES), F32),
                        pltpu.VMEM((2 * STATE_BLOCKS, chunk * n_slab, LANES), F32),
                        pltpu.VMEM((chunk, DS), F32), pltpu.VMEM((n_slab, 2 * S), F32)],
        compiler_params=_params(("arbitrary",)),
    )(zu, dgy, y, hin, w_b, w_bt, w_ct, a, d_skip)


def _ssm_discretise(lam_re, lam_im, log_dt, b_re, b_im):
    dt = jnp.exp(log_dt)[:, None]
    mag = jnp.exp(lam_re * dt)
    a_re = mag * jnp.cos(lam_im * dt)
    a_im = mag * jnp.sin(lam_im * dt)
    den = lam_re * lam_re + lam_im * lam_im
    nr = a_re - 1.0
    z_re = (nr * lam_re + a_im * lam_im) / den
    z_im = (a_im * lam_re - nr * lam_im) / den
    bb_re = z_re[..., None] * b_re - z_im[..., None] * b_im
    bb_im = z_re[..., None] * b_im + z_im[..., None] * b_re
    return a_re, a_im, bb_re, bb_im


def _slab_in(m_re, m_im):
    G, P, C = m_re.shape
    n_slab = G // GROUPS_PER_SLAB
    eye = jnp.eye(GROUPS_PER_SLAB, dtype=m_re.dtype)

    def one(m):
        m = m.reshape(n_slab, GROUPS_PER_SLAB, P, C)
        w = jnp.einsum('sgpc,gh->sgchp', m, eye)
        return w.reshape(n_slab, GROUPS_PER_SLAB * C, GROUPS_PER_SLAB * P)

    return jnp.concatenate([one(m_re), one(m_im)], axis=2)


def _slab_in_grad(dw, G, P, C):
    n_slab = G // GROUPS_PER_SLAB
    eye = jnp.eye(GROUPS_PER_SLAB, dtype=dw.dtype)

    def one(w):
        w = w.reshape(n_slab, GROUPS_PER_SLAB, C, GROUPS_PER_SLAB, P)
        return jnp.einsum('sgchp,gh->sgpc', w, eye).reshape(G, P, C)

    return one(dw[:, :, :SLAB_STATE]), one(dw[:, :, SLAB_STATE:])


def _slab_out(c_re, c_im):
    G, C, P = c_re.shape
    n_slab = G // GROUPS_PER_SLAB
    eye = jnp.eye(GROUPS_PER_SLAB, dtype=c_re.dtype)

    def one(m):
        m = m.reshape(n_slab, GROUPS_PER_SLAB, C, P)
        w = jnp.einsum('sgcp,gh->shpgc', m, eye)
        return w.reshape(n_slab, GROUPS_PER_SLAB * P, GROUPS_PER_SLAB * C)

    return jnp.concatenate([one(c_re), one(-c_im)], axis=1)


def _slab_out_grad(dw, G, C, P):
    n_slab = G // GROUPS_PER_SLAB
    eye = jnp.eye(GROUPS_PER_SLAB, dtype=dw.dtype)

    def one(w):
        w = w.reshape(n_slab, GROUPS_PER_SLAB, P, GROUPS_PER_SLAB, C)
        return jnp.einsum('shpgc,gh->sgcp', w, eye).reshape(G, C, P)

    return one(dw[:, :SLAB_STATE, :]), -one(dw[:, SLAB_STATE:, :])


def _slab_diag(a_re, a_im):
    G, P = a_re.shape
    n_slab = G // GROUPS_PER_SLAB
    return jnp.concatenate([a_re.reshape(n_slab, SLAB_STATE), a_im.reshape(n_slab, SLAB_STATE)], axis=1)


def _merge_fwd(yab, zg, attn, *, name):
    T, D = attn.shape
    tr = _divisor(T, ROW_BLOCK, SUBLANES)

    def body(ya_ref, yb_ref, ga_ref, gb_ref, at_ref, o_ref):
        f = lambda r: r[...].astype(F32)
        ssm = f(ya_ref) * jax.nn.sigmoid(f(yb_ref))
        o_ref[...] = (jax.nn.sigmoid(f(ga_ref)) * ssm + jax.nn.sigmoid(f(gb_ref)) * f(at_ref)).astype(o_ref.dtype)

    lo = pl.BlockSpec((tr, D), lambda i: (i, 0))
    hi = pl.BlockSpec((tr, D), lambda i: (i, 1))
    return pl.pallas_call(
        body, name=name, grid=(T // tr,),
        in_specs=[lo, hi, lo, hi, lo],
        out_specs=lo,
        out_shape=jax.ShapeDtypeStruct((T, D), BF16),
        compiler_params=_params(("parallel",)),
    )(yab, yab, zg, zg, attn)


def _merge_bwd(dm, yab, zg, attn, *, name):
    T, D = attn.shape
    tr = _divisor(T, ROW_BLOCK, SUBLANES)

    def body(dm_ref, ya_ref, yb_ref, ga_ref, gb_ref, at_ref, dg_ref, dat_ref, dy_ref):
        f = lambda r: r[...].astype(F32)
        dmv, ya, at = f(dm_ref), f(ya_ref), f(at_ref)
        sa, sb, syb = jax.nn.sigmoid(f(ga_ref)), jax.nn.sigmoid(f(gb_ref)), jax.nn.sigmoid(f(yb_ref))
        ssm = ya * syb
        dssm = dmv * sa
        dg_ref[:, pl.ds(0, D)] = (dmv * ssm * sa * (1.0 - sa)).astype(dg_ref.dtype)
        dg_ref[:, pl.ds(D, D)] = (dmv * at * sb * (1.0 - sb)).astype(dg_ref.dtype)
        dat_ref[...] = (dmv * sb).astype(dat_ref.dtype)
        dy_ref[:, pl.ds(0, D)] = (dssm * syb).astype(dy_ref.dtype)
        dy_ref[:, pl.ds(D, D)] = (dssm * ya * syb * (1.0 - syb)).astype(dy_ref.dtype)

    lo = pl.BlockSpec((tr, D), lambda i: (i, 0))
    hi = pl.BlockSpec((tr, D), lambda i: (i, 1))
    both = pl.BlockSpec((tr, 2 * D), lambda i: (i, 0))
    return pl.pallas_call(
        body, name=name, grid=(T // tr,),
        in_specs=[lo, lo, hi, lo, hi, lo],
        out_specs=[both, lo, both],
        out_shape=[jax.ShapeDtypeStruct((T, 2 * D), BF16), jax.ShapeDtypeStruct((T, D), BF16),
                   jax.ShapeDtypeStruct((T, 2 * D), BF16)],
        compiler_params=_params(("parallel",)),
    )(dm, yab, yab, zg, zg, attn)


def _conv_taps(g_ref, halo_ref, i, tr):
    g0 = g_ref[...].astype(F32)
    halo = jnp.where(i > 0, halo_ref[...].astype(F32), 0.0)
    row = lax.broadcasted_iota(jnp.int32, g0.shape, 0)
    g1 = jnp.where(row == 0, halo[SUBLANES - 1:SUBLANES, :], pltpu.roll(g0, 1, axis=0))
    g2 = pltpu.roll(g0, 2, axis=0)
    g2 = jnp.where(row == 0, halo[SUBLANES - 2:SUBLANES - 1, :], g2)
    g2 = jnp.where(row == 1, halo[SUBLANES - 1:SUBLANES, :], g2)
    return g0, g1, g2


def _conv_blocks(T, FF):
    tr = _divisor(T, ROW_BLOCK, SUBLANES)
    tc = _divisor(FF, 1024, LANES)
    return tr, tc


def _conv_fwd(gu, conv_w, conv_b, *, name):
    T = gu.shape[0]
    FF = gu.shape[1] // 2
    tr, tc = _conv_blocks(T, FF)
    ncol = FF // tc

    def body(g_ref, halo_ref, u_ref, w_ref, b_ref, o_ref):
        i = pl.program_id(0)
        g0, g1, g2 = _conv_taps(g_ref, halo_ref, i, tr)
        gc = b_ref[...] + w_ref[0:1, :] * g2 + w_ref[1:2, :] * g1 + w_ref[2:3, :] * g0
        o_ref[...] = (gc * jax.nn.sigmoid(gc) * u_ref[...].astype(F32)).astype(o_ref.dtype)

    hb = tr // SUBLANES
    return pl.pallas_call(
        body, name=name, grid=(T // tr, ncol),
        in_specs=[pl.BlockSpec((tr, tc), lambda i, j: (i, j)),
                  pl.BlockSpec((SUBLANES, tc), lambda i, j: (jnp.maximum(i * hb - 1, 0), j)),
                  pl.BlockSpec((tr, tc), lambda i, j: (i, j + ncol)),
                  pl.BlockSpec((SUBLANES, tc), lambda i, j: (0, j)), pl.BlockSpec((1, tc), lambda i, j: (0, j))],
        out_specs=pl.BlockSpec((tr, tc), lambda i, j: (i, j)),
        out_shape=jax.ShapeDtypeStruct((T, FF), BF16),
        compiler_params=_params(("parallel", "parallel")),
    )(gu, gu, gu, conv_w, conv_b)


def _conv_bwd_gate(da, gu, conv_w, conv_b, *, name, riders=()):
    T = gu.shape[0]
    FF = gu.shape[1] // 2
    tr, tc = _conv_blocks(T, FF)
    ncol = FF // tc

    def body(da_ref, g_ref, halo_ref, u_ref, w_ref, b_ref, dgc_ref, du_ref, s_ref):
        i = pl.program_id(1)

        @pl.when(i == 0)
        def _():
            s_ref[...] = jnp.zeros_like(s_ref)

        g0, g1, g2 = _conv_taps(g_ref, halo_ref, i, tr)
        gc = b_ref[...] + w_ref[0:1, :] * g2 + w_ref[1:2, :] * g1 + w_ref[2:3, :] * g0
        sg = jax.nn.sigmoid(gc)
        dav = da_ref[...].astype(F32)
        du_ref[...] = (dav * gc * sg).astype(du_ref.dtype)
        dgc = dav * u_ref[...].astype(F32) * (sg * (1.0 + gc * (1.0 - sg)))
        dgc_ref[...] = dgc.astype(dgc_ref.dtype)
        s_ref[0:1, :] += jnp.sum(dgc * g2, axis=0, keepdims=True)
        s_ref[1:2, :] += jnp.sum(dgc * g1, axis=0, keepdims=True)
        s_ref[2:3, :] += jnp.sum(dgc * g0, axis=0, keepdims=True)
        s_ref[3:4, :] += jnp.sum(dgc, axis=0, keepdims=True)

    hb = tr // SUBLANES
    blk = pl.BlockSpec((tr, tc), lambda j, i: (i, j))
    (dgc, du, sums), carried = _gridded_call(
        name, (ncol, T // tr), ("parallel", "arbitrary"), body,
        [blk, blk,
         pl.BlockSpec((SUBLANES, tc), lambda j, i: (jnp.maximum(i * hb - 1, 0), j)),
         pl.BlockSpec((tr, tc), lambda j, i: (i, j + ncol)),
         pl.BlockSpec((SUBLANES, tc), lambda j, i: (0, j)), pl.BlockSpec((1, tc), lambda j, i: (0, j))],
        [blk, blk, pl.BlockSpec((SUBLANES, tc), lambda j, i: (0, j))],
        [jax.ShapeDtypeStruct((T, FF), BF16), jax.ShapeDtypeStruct((T, FF), BF16),
         jax.ShapeDtypeStruct((SUBLANES, FF), F32)],
        [], [da, gu, gu, gu, conv_w, conv_b], riders)
    return dgc, du, sums, carried


def _conv_bwd_taps(dgc, conv_w, *, name):
    T, FF = dgc.shape
    tr, tc = _conv_blocks(T, FF)
    ncol = FF // tc
    nrow = T // tr

    def body(d_ref, next_ref, w_ref, o_ref):
        i = pl.program_id(0)
        d0 = d_ref[...].astype(F32)
        nxt = jnp.where(i < nrow - 1, next_ref[...].astype(F32), 0.0)
        row = lax.broadcasted_iota(jnp.int32, d0.shape, 0)
        d1 = jnp.where(row == tr - 1, nxt[0:1, :], pltpu.roll(d0, tr - 1, axis=0))
        d2 = pltpu.roll(d0, tr - 2, axis=0)
        d2 = jnp.where(row == tr - 2, nxt[0:1, :], d2)
        d2 = jnp.where(row == tr - 1, nxt[1:2, :], d2)
        dg = w_ref[2:3, :] * d0 + w_ref[1:2, :] * d1 + w_ref[0:1, :] * d2
        o_ref[...] = dg.astype(o_ref.dtype)

    hb = tr // SUBLANES
    last = T // SUBLANES - 1
    return pl.pallas_call(
        body, name=name, grid=(nrow, ncol),
        in_specs=[pl.BlockSpec((tr, tc), lambda i, j: (i, j)),
                  pl.BlockSpec((SUBLANES, tc), lambda i, j: (jnp.minimum((i + 1) * hb, last), j)),
                  pl.BlockSpec((SUBLANES, tc), lambda i, j: (0, j))],
        out_specs=pl.BlockSpec((tr, tc), lambda i, j: (i, j)),
        out_shape=jax.ShapeDtypeStruct((T, FF), BF16),
        compiler_params=_params(("parallel", "parallel")),
    )(dgc, dgc, conv_w)


def _mesh_pos():
    return lax.axis_index("x"), lax.axis_index("y"), lax.axis_index("c")


def _flip(pos, mask):
    x, y, c = pos
    return (x ^ ((mask >> 2) & 1), y ^ ((mask >> 1) & 1), c ^ (mask & 1))


def _index_of(pos):
    x, y, c = pos
    return 4 * x + 2 * y + c


class _Rider(collections.namedtuple("_Rider", "src gather cols R c")):
    def out_shape(self):
        if not self.gather:
            shape = (N_DEV, self.R, self.c)
        elif self.cols:
            shape = (self.R, N_DEV * self.c)
        else:
            shape = (N_DEV * self.R, self.c)
        return jax.ShapeDtypeStruct(shape, self.src.dtype)

    def slab(self, ref, idx):
        if self.cols:
            return ref.at[:, pl.ds(pl.multiple_of(idx * self.c, LANES), self.c)]
        return ref.at[pl.ds(pl.multiple_of(idx * self.R, 2 * SUBLANES), self.R), :]

    def copy(self, src_ref, dst_ref, send_sems, recv_sems, me, k, arriving):
        peer = _flip(me, k)
        owner = _index_of(peer if arriving else me)
        if self.gather:
            src, dst = src_ref, self.slab(dst_ref, owner)
        else:
            src, dst = self.slab(src_ref, _index_of(peer)), dst_ref.at[owner]
        return pltpu.make_async_remote_copy(src_ref=src, dst_ref=dst, send_sem=send_sems.at[k - 1],
                                            recv_sem=recv_sems.at[k - 1], device_id=peer,
                                            device_id_type=pl.DeviceIdType.MESH)

    def own(self, src_ref, dst_ref, local_sem, me):
        my = _index_of(me)
        if self.gather:
            return pltpu.make_async_copy(src_ref, self.slab(dst_ref, my), local_sem)
        return pltpu.make_async_copy(self.slab(src_ref, my), dst_ref.at[my], local_sem)

    def start(self, src_ref, dst_ref, send_sems, recv_sems, local_sem):
        me = _mesh_pos()
        self.own(src_ref, dst_ref, local_sem, me).start()
        for k in range(1, N_DEV):
            self.copy(src_ref, dst_ref, send_sems, recv_sems, me, k, False).start()

    def wait(self, src_ref, dst_ref, send_sems, recv_sems, local_sem):
        me = _mesh_pos()
        for k in range(1, N_DEV):
            self.copy(src_ref, dst_ref, send_sems, recv_sems, me, k, True).wait_recv()
        for k in range(1, N_DEV):
            self.copy(src_ref, dst_ref, send_sems, recv_sems, me, k, False).wait_send()
        self.own(src_ref, dst_ref, local_sem, me).wait()


_RIDER_SEMS = [pltpu.SemaphoreType.DMA((N_DEV - 1,)), pltpu.SemaphoreType.DMA((N_DEV - 1,)), pltpu.SemaphoreType.DMA]
_ANY = pl.BlockSpec(memory_space=pl.ANY)


def _comm_call(riders, *, name):
    n = len(riders)

    def body(*refs):
        srcs, dsts, sems = refs[:n], refs[n:2 * n], refs[2 * n:]
        for r, rider in enumerate(riders):
            rider.start(srcs[r], dsts[r], *sems[3 * r:3 * r + 3])
        for r, rider in enumerate(riders):
            rider.wait(srcs[r], dsts[r], *sems[3 * r:3 * r + 3])

    return pl.pallas_call(
        body, name=name,
        in_specs=[_ANY] * n, out_specs=[_ANY] * n,
        out_shape=[rider.out_shape() for rider in riders],
        scratch_shapes=_RIDER_SEMS * n,
        compiler_params=pltpu.CompilerParams(has_side_effects=True),
    )(*[rider.src for rider in riders])


def _carry(riders, grid, body, in_specs, out_specs, out_shape, scratch_shapes, args):
    n, n_in, n_out, n_scratch = len(riders), len(in_specs), len(out_specs), len(scratch_shapes)

    def carrying(*refs):
        ins, refs = refs[:n_in], refs[n_in:]
        srcs, refs = refs[:n], refs[n:]
        outs, refs = refs[:n_out], refs[n_out:]
        dsts, refs = refs[:n], refs[n:]
        scratch, sems = refs[:n_scratch], refs[n_scratch:]
        ids = [pl.program_id(a) for a in range(len(grid))]
        first = functools.reduce(jnp.logical_and, [i == 0 for i in ids])
        last = functools.reduce(jnp.logical_and, [i == g - 1 for i, g in zip(ids, grid)])

        @pl.when(first)
        def _():
            for r, rider in enumerate(riders):
                rider.start(srcs[r], dsts[r], *sems[3 * r:3 * r + 3])

        body(*ins, *outs, *scratch)

        @pl.when(last)
        def _():
            for r, rider in enumerate(riders):
                rider.wait(srcs[r], dsts[r], *sems[3 * r:3 * r + 3])

    return dict(
        body=carrying,
        in_specs=list(in_specs) + [_ANY] * n,
        out_specs=list(out_specs) + [_ANY] * n,
        out_shape=list(out_shape) + [rider.out_shape() for rider in riders],
        scratch_shapes=list(scratch_shapes) + _RIDER_SEMS * n,
        args=list(args) + [rider.src for rider in riders])


def _gridded_call(name, grid, semantics, body, in_specs, out_specs, out_shape, scratch_shapes, args, riders=()):
    call = dict(body=body, in_specs=in_specs, out_specs=out_specs, out_shape=out_shape,
                scratch_shapes=scratch_shapes, args=args)
    if riders:
        call = _carry(list(riders), grid, **call)
        semantics = ("arbitrary",) * len(grid)
    outs = pl.pallas_call(
        call["body"], name=name, grid=grid, in_specs=call["in_specs"], out_specs=call["out_specs"],
        out_shape=call["out_shape"], scratch_shapes=call["scratch_shapes"],
        compiler_params=_params(semantics),
    )(*call["args"])
    n_out = len(out_shape)
    return list(outs[:n_out]), list(outs[n_out:])


def _all_gather(x, *, name):
    R, C = x.shape
    return _comm_call([_Rider(x, True, False, R, C)], name=name)[0].reshape(N_DEV, R, C)


def _exchange(parts, *, name):
    _, R, C = parts.shape
    return _comm_call([_Rider(parts.reshape(N_DEV * R, C), False, False, R, C)], name=name)[0]


def _sum_adamw(parts, w, m, v, *, name):
    R, C = w.shape
    tr = _divisor(R, max(2 * SUBLANES, ADAMW_BLOCK_ELEMS // C // SUBLANES * SUBLANES), 2 * SUBLANES)
    c1 = 1.0 - ADAM_B1 ** ADAM_STEP
    c2 = 1.0 - ADAM_B2 ** ADAM_STEP

    def body(p_ref, w_ref, m_ref, v_ref, g_ref, d_ref, nm_ref, nv_ref):
        g = p_ref[0].astype(F32)
        for s in range(1, N_DEV):
            g = g + p_ref[s].astype(F32)
        nm = ADAM_B1 * m_ref[...] + (1.0 - ADAM_B1) * g
        nv = ADAM_B2 * v_ref[...] + (1.0 - ADAM_B2) * (g * g)
        g_ref[...] = g
        nm_ref[...] = nm
        nv_ref[...] = nv
        d_ref[...] = -ADAM_LR * ((nm / c1) / (jnp.sqrt(nv / c2) + ADAM_EPS) + ADAM_WD * w_ref[...])

    blk = pl.BlockSpec((tr, C), lambda i: (i, 0))
    return pl.pallas_call(
        body, name=name, grid=(R // tr,),
        in_specs=[pl.BlockSpec((N_DEV, tr, C), lambda i: (0, i, 0)), blk, blk, blk],
        out_specs=[blk] * 4,
        out_shape=[jax.ShapeDtypeStruct((R, C), F32)] * 4,
        compiler_params=_params(("parallel",)),
    )(parts, w, m, v)


def _pad2(a, rows, cols):
    return jnp.pad(a, ((0, rows - a.shape[0]), (0, cols - a.shape[1])))


def _gather_cols(w, dtype, *, name):
    R, c = w.shape
    rp, cp = _round_up(R, 2 * SUBLANES), _round_up(c, LANES)
    g = _all_gather(_pad2(w.astype(dtype), rp, cp), name=name)
    return jnp.transpose(g[:, :R, :c], (1, 0, 2)).reshape(R, N_DEV * c)


def _column_parts(dw, c):
    R = dw.shape[0]
    parts = jnp.transpose(dw.reshape(R, N_DEV, c), (1, 0, 2))
    return jnp.pad(parts, ((0, 0), (0, _round_up(R, 2 * SUBLANES) - R), (0, _round_up(c, LANES) - c)))


def _padded_adamw(got, w, m, v, *, name):
    R, c = w.shape
    rp, cp = got.shape[1:]
    outs = _sum_adamw(got, _pad2(w, rp, cp), _pad2(m, rp, cp), _pad2(v, rp, cp), name=name)
    return [o[:R, :c] for o in outs]


def _update_cols(dw, w, m, v, *, name):
    got = _exchange(_column_parts(dw, w.shape[1]), name=name + "_exchange")
    return _padded_adamw(got, w, m, v, name=name + "_adamw")


def _update_replicated(grads, ws, ms, vs, *, name):
    sizes = [int(g.size) for g in grads]
    total = sum(sizes)
    rows = _round_up(-(-total // LANES), 2 * SUBLANES)

    def pack(arrs):
        flat = jnp.concatenate([a.reshape(-1).astype(F32) for a in arrs])
        return jnp.pad(flat, (0, rows * LANES - total)).reshape(rows, LANES)

    got = _all_gather(pack(grads), name=name + "_gather")
    outs = _sum_adamw(got, pack(ws), pack(ms), pack(vs), name=name + "_adamw")
    result = []
    for o in outs:
        flat = o.reshape(-1)
        arrs, off = [], 0
        for w, n in zip(ws, sizes):
            arrs.append(flat[off:off + n].reshape(w.shape))
            off += n
        result.append(arrs)
    return result


def kernel(x, meta, g_mix, w_in, b_f, lam_re, lam_im, log_dt, b_re, b_im, c_re, c_im, d_skip, w_glu, w_attn_o, w_out, g_ffn, w_up, conv_w, conv_b, w_down, g_final, loss_target, m_meta, m_g_mix, m_w_in, m_b_f, m_lam_re, m_lam_im, m_log_dt, m_b_re, m_b_im, m_c_re, m_c_im, m_d_skip, m_w_glu, m_w_attn_o, m_w_out, m_g_ffn, m_w_up, m_conv_w, m_conv_b, m_w_down, m_g_final, v_meta, v_g_mix, v_w_in, v_b_f, v_lam_re, v_lam_im, v_log_dt, v_b_re, v_b_im, v_c_re, v_c_im, v_d_skip, v_w_glu, v_w_attn_o, v_w_out, v_g_ffn, v_w_up, v_conv_w, v_conv_b, v_w_down, v_g_final):
    seq, D = x.shape[1], x.shape[2]
    L = N_META + seq
    T = _round_up(L, SEQ_BLOCK) if L > SEQ_BLOCK else _round_up(L, LANES)
    DS = d_skip.shape[1]
    H = b_f.shape[1]
    DA = H * HEAD_DIM
    FF = conv_b.shape[1]
    G, P, C = b_re.shape[1:]

    meta_full = _gather_cols(meta, F32, name="gather_meta")
    conv_w_full = _gather_cols(conv_w[0], F32, name="gather_conv_w")
    w_in_full = _gather_cols(w_in[0], BF16, name="gather_w_in")
    gathers = [_Rider(w[0].astype(BF16), True, cols, *w.shape[1:])
               for w, cols in ((w_attn_o, True), (w_glu, True), (w_out, False), (w_up, True), (w_down, False))]
    conv_w8 = jnp.pad(conv_w_full, ((0, SUBLANES - CONV_WIDTH), (0, 0)))

    a_re, a_im, bb_re, bb_im = _ssm_discretise(lam_re[0], lam_im[0], log_dt[0], b_re[0], b_im[0])
    w_b = _slab_in(bb_re, bb_im)
    w_c = _slab_out(c_re[0], c_im[0])
    a_slab = _slab_diag(a_re, a_im)
    w_b16, w_c16 = w_b.astype(BF16), w_c.astype(BF16)
    w_bt16, w_ct16 = jnp.swapaxes(w_b16, 1, 2), jnp.swapaxes(w_c16, 1, 2)

    h0 = jnp.concatenate([meta_full, x[0], jnp.zeros((T - L, D), F32)], axis=0)
    target = jnp.pad(loss_target[0], ((N_META, T - L), (0, 0)))
    n1 = _rms_fwd(h0, g_mix, name="rms_mix")
    o_f, o_u, o_g = 3 * DA, 3 * DA + H, 3 * DA + H + DS
    w_qkv = w_in_full[:, :o_f]
    w_f = jnp.pad(w_in_full[:, o_f:o_u], ((0, 0), (0, LANES - H)))
    w_u = w_in_full[:, o_u:o_g]
    w_g = w_in_full[:, o_g:]
    w_main = jnp.concatenate([w_qkv, w_u, w_g], axis=1)
    zqkv = _matmul(n1, w_qkv, name="mm_qkv")
    zu = _matmul(n1, w_u, name="mm_u")
    zg = _matmul(n1, w_g, name="mm_gates")
    zf = _matmul(n1, w_f, name="mm_forget", out_dtype=F32)
    f_t = zf[:, :H].T
    b_col = b_f.reshape(H, 1)
    fcum = _forget_cumsum(f_t, b_col, name="forget_cumsum")
    fcol = fcum.T
    o, lse, carried = _attn_fwd(zqkv, fcum, fcol, name="attn_fwd", riders=gathers)
    w_ao_full, w_glu_full, w_out_full, w_up_full, w_down_full = carried
    attn = _matmul(o, w_ao_full, name="mm_attn_o")
    y, gy, hin = _ssm_fwd(zu, w_b16, w_c16, a_slab, d_skip, name="ssm_fwd")
    yab = _matmul(gy, w_glu_full, name="mm_glu")
    merged = _merge_fwd(yab, zg, attn, name="merge_fwd")
    h1 = _matmul(merged, w_out_full, name="mm_out", out_dtype=F32, residual=h0)
    n2 = _rms_fwd(h1, g_ffn, name="rms_ffn")
    gu = _matmul(n2, w_up_full, name="mm_up")
    act = _conv_fwd(gu, conv_w8, conv_b, name="conv_fwd")
    h2 = _matmul(act, w_down_full, name="mm_down", out_dtype=F32, residual=h1)

    dh2, sq, dg_final = _final_loss(h2, g_final.reshape(1, D), target, seq, name="final_loss")
    loss = lax.psum(0.5 * sq[0, 0] / D, ("x", "y", "c"))
    dh2_16 = dh2.astype(BF16)
    d_act = _matmul(dh2_16, w_down_full, name="mm_down_dx", trans_b=True)
    dw_down = _matmul(act, dh2_16, name="mm_down_dw", trans_a=True)
    dgc, du2, conv_sums, (x_w_down,) = _conv_bwd_gate(
        d_act, gu, conv_w8, conv_b, name="conv_bwd_gate",
        riders=[_Rider(dw_down, False, False, *w_down.shape[1:])])
    dgu = jnp.concatenate([_conv_bwd_taps(dgc, conv_w8, name="conv_bwd_taps"), du2], axis=1)
    dn2 = _matmul(dgu, w_up_full, name="mm_up_dx", trans_b=True)
    dw_up = _matmul(n2, dgu, name="mm_up_dw", trans_a=True)
    dh1, dg_ffn = _rms_bwd(dn2, h1, g_ffn, dh2, name="rms_ffn_bwd")
    dh1_16 = dh1.astype(BF16)
    dmerged = _matmul(dh1_16, w_out_full, name="mm_out_dx", trans_b=True)
    dw_out = _matmul(merged, dh1_16, name="mm_out_dw", trans_a=True)
    dzg, dattn, dyab = _merge_bwd(dmerged, yab, zg, attn, name="merge_bwd")
    do = _matmul(dattn, w_ao_full, name="mm_attn_o_dx", trans_b=True)
    dw_ao = _matmul(o, dattn, name="mm_attn_o_dw", trans_a=True)
    dgy = _matmul(dyab, w_glu_full, name="mm_glu_dx", trans_b=True)
    dw_glu = _matmul(gy, dyab, name="mm_glu_dw", trans_a=True)
    dzu, dw_b, dw_c, da_slab, dd_skip = _ssm_bwd(zu, dgy, y, hin, w_b16, w_bt16, w_ct16, a_slab, d_skip,
                                                 name="ssm_bwd")
    delta = _attn_delta(zqkv, fcum, fcol, do, lse, name="attn_delta")
    dq = _attn_bwd_dq(zqkv, fcum, fcol, do, lse, delta, name="attn_bwd_dq")
    dk, dv, dfcum, (x_w_up, x_w_out, x_w_ao, x_w_glu) = _attn_bwd_dkv(
        zqkv, fcum, fcol, do, lse, delta, name="attn_bwd_dkv",
        riders=[_Rider(dw_up, False, True, *w_up.shape[1:]), _Rider(dw_out, False, False, *w_out.shape[1:]),
                _Rider(dw_ao, False, True, *w_attn_o.shape[1:]), _Rider(dw_glu, False, True, *w_glu.shape[1:])])
    df_t, db_f = _forget_bwd(dfcum.T, f_t, b_col, name="forget_bwd")
    dzf = jnp.pad(df_t.T, ((0, 0), (0, LANES - H))).astype(BF16)
    dz_main = jnp.concatenate([dq, dk, dv, dzu, dzg], axis=1)
    dw_main = _matmul(n1, dz_main, name="mm_in_dw", trans_a=True)
    dw_f = _matmul(n1, dzf, name="mm_forget_dw", trans_a=True)
    dw_in = jnp.concatenate([dw_main[:, :o_f], dw_f[:, :H], dw_main[:, o_f:]], axis=1)
    parts_in = _column_parts(dw_in, w_in.shape[2])
    dn1 = _matmul(dzf, w_f, name="mm_forget_dx", out_dtype=F32, trans_b=True)
    dn1, (x_w_in,) = _matmul(
        dz_main, w_main, name="mm_in_dx", out_dtype=F32, residual=dn1, trans_b=True,
        riders=[_Rider(parts_in.reshape(-1, parts_in.shape[2]), False, False, *parts_in.shape[1:])])
    dh0, dg_mix = _rms_bwd(dn1, h0, g_mix, dh1, name="rms_mix_bwd")
    grad_x = dh0[N_META:L][None]

    dbb_re, dbb_im = _slab_in_grad(dw_b, G, P, C)
    dc_re, dc_im = _slab_out_grad(dw_c, G, C, P)
    da_re = da_slab[:, :SLAB_STATE].reshape(G, P)
    da_im = da_slab[:, SLAB_STATE:].reshape(G, P)
    _, disc_vjp = jax.vjp(_ssm_discretise, lam_re[0], lam_im[0], log_dt[0], b_re[0], b_im[0])
    dlam_re, dlam_im, dlog_dt, db_re, db_im = disc_vjp((da_re, da_im, dbb_re, dbb_im))

    big = {}
    big["meta"] = _update_cols(dh0[:N_META], meta, m_meta, v_meta, name="meta")
    big["conv_w"] = _update_cols(conv_sums[:CONV_WIDTH], conv_w[0], m_conv_w[0], v_conv_w[0], name="conv_w")
    big["w_down"] = _sum_adamw(x_w_down, w_down[0], m_w_down[0], v_w_down[0], name="w_down_adamw")
    big["w_up"] = _sum_adamw(x_w_up, w_up[0], m_w_up[0], v_w_up[0], name="w_up_adamw")
    big["w_out"] = _sum_adamw(x_w_out, w_out[0], m_w_out[0], v_w_out[0], name="w_out_adamw")
    big["w_attn_o"] = _sum_adamw(x_w_ao, w_attn_o[0], m_w_attn_o[0], v_w_attn_o[0], name="w_attn_o_adamw")
    big["w_glu"] = _sum_adamw(x_w_glu, w_glu[0], m_w_glu[0], v_w_glu[0], name="w_glu_adamw")
    big["w_in"] = _padded_adamw(x_w_in, w_in[0], m_w_in[0], v_w_in[0], name="w_in_adamw")

    rep_names = ["g_mix", "b_f", "lam_re", "lam_im", "log_dt", "b_re", "b_im", "c_re", "c_im", "d_skip", "g_ffn",
                 "conv_b", "g_final"]
    rep_w = [g_mix, b_f, lam_re, lam_im, log_dt, b_re, b_im, c_re, c_im, d_skip, g_ffn, conv_b, g_final]
    rep_m = [m_g_mix, m_b_f, m_lam_re, m_lam_im, m_log_dt, m_b_re, m_b_im, m_c_re, m_c_im, m_d_skip, m_g_ffn,
             m_conv_b, m_g_final]
    rep_v = [v_g_mix, v_b_f, v_lam_re, v_lam_im, v_log_dt, v_b_re, v_b_im, v_c_re, v_c_im, v_d_skip, v_g_ffn,
             v_conv_b, v_g_final]
    rep_g = [dg_mix, db_f[:, 0], dlam_re, dlam_im, dlog_dt, db_re, db_im, dc_re, dc_im, dd_skip, dg_ffn,
             conv_sums[CONV_WIDTH], dg_final]
    rep = _update_replicated(rep_g, rep_w, rep_m, rep_v, name="replicated")
    rep_out = {n: [rep[k][i] for k in range(4)] for i, n in enumerate(rep_names)}

    order = ["meta", "g_mix", "w_in", "b_f", "lam_re", "lam_im", "log_dt", "b_re", "b_im", "c_re", "c_im", "d_skip",
             "w_glu", "w_attn_o", "w_out", "g_ffn", "w_up", "conv_w", "conv_b", "w_down", "g_final"]
    outs = [loss, grad_x]
    for kind in range(4):
        for n in order:
            if n in big:
                outs.append(big[n][kind] if n == "meta" else big[n][kind][None])
            else:
                outs.append(rep_out[n][kind])
    return tuple(outs)
```

```python
import collections
import functools
import math

import jax
import jax.numpy as jnp
from jax import lax
from jax.experimental import pallas as pl
from jax.experimental.pallas import tpu as pltpu

F32 = jnp.float32
BF16 = jnp.bfloat16

N_META = 16
EPS = 1e-6
HEAD_DIM = 128
SSM_GROUP = 16
SSM_STATE = 64
GROUPS_PER_SLAB = 8
SLAB_STATE = GROUPS_PER_SLAB * SSM_STATE
CONV_WIDTH = 3
N_DEV = 8

ADAM_LR = 0.001
ADAM_B1 = 0.9
ADAM_B2 = 0.999
ADAM_EPS = 1e-08
ADAM_WD = 0.01
ADAM_STEP = 10

LANES = 128
SUBLANES = 8
VMEM_LIMIT = 52 * 1024 * 1024

SEQ_BLOCK = 768
ATT_BLOCK = 384
ROW_BLOCK = 256
SSM_CHUNK = 128
ADAMW_BLOCK_ELEMS = 1 << 17
WGRAD_BLOCK_M = 1408
MASK_VALUE = -1e30


def _round_up(n, m):
    return (n + m - 1) // m * m


def _divisor(n, target, mult):
    if n <= target:
        return n
    best = None
    for d in range(mult, target + 1, mult):
        if n % d == 0:
            best = d
    assert best is not None, (n, target, mult)
    return best


def _params(sem):
    return pltpu.CompilerParams(dimension_semantics=sem, vmem_limit_bytes=VMEM_LIMIT)


def _matmul(a, b, *, name, trans_a=False, trans_b=False, out_dtype=None, residual=None, riders=(),
            tm=768, tn=1024, tk=2048):
    out_dtype = BF16 if out_dtype is None else out_dtype
    assert not (trans_a and trans_b)
    if trans_a:
        K, M = a.shape
    else:
        M, K = a.shape
    if trans_b:
        N, K2 = b.shape
    else:
        K2, N = b.shape
    assert K == K2, (a.shape, b.shape)
    if trans_a:
        tm = max(tm, WGRAD_BLOCK_M)
    tm = _divisor(M, tm, LANES if trans_a else SUBLANES)
    tn = _divisor(N, tn, LANES)
    tk = _divisor(K, tk, LANES if not trans_a else SUBLANES)
    nk = K // tk

    def body(*refs):
        if residual is None:
            a_ref, b_ref, o_ref, acc_ref = refs
        else:
            a_ref, b_ref, r_ref, o_ref, acc_ref = refs
        k = pl.program_id(2)

        @pl.when(k == 0)
        def _():
            acc_ref[...] = jnp.zeros_like(acc_ref)

        contract = (0, 0) if trans_a else (1, 1) if trans_b else (1, 0)
        acc_ref[...] += lax.dot_general(a_ref[...], b_ref[...], (((contract[0],), (contract[1],)), ((), ())),
                                        preferred_element_type=F32)

        @pl.when(k == nk - 1)
        def _():
            r = acc_ref[...]
            if residual is not None:
                r = r + r_ref[...]
            o_ref[...] = r.astype(o_ref.dtype)

    if trans_a:
        a_spec = pl.BlockSpec((tk, tm), lambda i, j, k: (k, i))
    else:
        a_spec = pl.BlockSpec((tm, tk), lambda i, j, k: (i, k))
    if trans_b:
        b_spec = pl.BlockSpec((tn, tk), lambda i, j, k: (j, k))
    else:
        b_spec = pl.BlockSpec((tk, tn), lambda i, j, k: (k, j))
    in_specs = [a_spec, b_spec]
    args = [a, b]
    if residual is not None:
        in_specs.append(pl.BlockSpec((tm, tn), lambda i, j, k: (i, j)))
        args.append(residual)
    (out,), carried = _gridded_call(
        name, (M // tm, N // tn, nk), ("parallel", "parallel", "arbitrary"), body, in_specs,
        [pl.BlockSpec((tm, tn), lambda i, j, k: (i, j))], [jax.ShapeDtypeStruct((M, N), out_dtype)],
        [pltpu.VMEM((tm, tn), F32)], args, riders)
    return (out, carried) if riders else out


def _rms_fwd(h, g, *, name):
    T, D = h.shape
    tr = _divisor(T, ROW_BLOCK, SUBLANES)

    def body(h_ref, g_ref, o_ref):
        x = h_ref[...]
        r = lax.rsqrt(jnp.mean(x * x, axis=-1, keepdims=True) + EPS)
        o_ref[...] = (x * r * g_ref[...]).astype(o_ref.dtype)

    return pl.pallas_call(
        body, name=name, grid=(T // tr,),
        in_specs=[pl.BlockSpec((tr, D), lambda i: (i, 0)), pl.BlockSpec((1, D), lambda i: (0, 0))],
        out_specs=pl.BlockSpec((tr, D), lambda i: (i, 0)),
        out_shape=jax.ShapeDtypeStruct((T, D), BF16),
        compiler_params=_params(("parallel",)),
    )(h, g)


def _rms_bwd(dn, h, g, dres, *, name):
    T, D = h.shape
    tr = _divisor(T, ROW_BLOCK, SUBLANES)

    def body(dn_ref, h_ref, g_ref, dres_ref, dh_ref, dg_ref):
        i = pl.program_id(0)

        @pl.when(i == 0)
        def _():
            dg_ref[...] = jnp.zeros_like(dg_ref)

        x = h_ref[...]
        dn_v = dn_ref[...].astype(F32)
        r = lax.rsqrt(jnp.mean(x * x, axis=-1, keepdims=True) + EPS)
        xh = x * r
        dg_ref[...] += jnp.sum(dn_v * xh, axis=0, keepdims=True)
        dxh = dn_v * g_ref[...]
        dh_ref[...] = dres_ref[...] + r * (dxh - xh * jnp.mean(dxh * xh, axis=-1, keepdims=True))

    return pl.pallas_call(
        body, name=name, grid=(T // tr,),
        in_specs=[pl.BlockSpec((tr, D), lambda i: (i, 0)), pl.BlockSpec((tr, D), lambda i: (i, 0)),
                  pl.BlockSpec((1, D), lambda i: (0, 0)), pl.BlockSpec((tr, D), lambda i: (i, 0))],
        out_specs=[pl.BlockSpec((tr, D), lambda i: (i, 0)), pl.BlockSpec((1, D), lambda i: (0, 0))],
        out_shape=[jax.ShapeDtypeStruct((T, D), F32), jax.ShapeDtypeStruct((1, D), F32)],
        compiler_params=_params(("arbitrary",)),
    )(dn, h, g, dres)


def _final_loss(h, g, target, n_valid, *, name):
    T, D = h.shape
    tr = _divisor(T, ROW_BLOCK, SUBLANES)

    def body(h_ref, g_ref, t_ref, dh_ref, sq_ref, dg_ref):
        i = pl.program_id(0)

        @pl.when(i == 0)
        def _():
            sq_ref[...] = jnp.zeros_like(sq_ref)
            dg_ref[...] = jnp.zeros_like(dg_ref)

        x = h_ref[...]
        r = lax.rsqrt(jnp.mean(x * x, axis=-1, keepdims=True) + EPS)
        xh = x * r
        gv = g_ref[...]
        row = i * tr + lax.broadcasted_iota(jnp.int32, (tr, 1), 0)
        valid = (row >= N_META) & (row < N_META + n_valid)
        err = jnp.where(valid, xh * gv - t_ref[...], 0.0)
        sq_ref[...] += jnp.sum(err * err)
        dy = err * (1.0 / D)
        dg_ref[...] += jnp.sum(dy * xh, axis=0, keepdims=True)
        dxh = dy * gv
        dh_ref[...] = r * (dxh - xh * jnp.mean(dxh * xh, axis=-1, keepdims=True))

    return pl.pallas_call(
        body, name=name, grid=(T // tr,),
        in_specs=[pl.BlockSpec((tr, D), lambda i: (i, 0)), pl.BlockSpec((1, D), lambda i: (0, 0)),
                  pl.BlockSpec((tr, D), lambda i: (i, 0))],
        out_specs=[pl.BlockSpec((tr, D), lambda i: (i, 0)), pl.BlockSpec((SUBLANES, LANES), lambda i: (0, 0)),
                   pl.BlockSpec((1, D), lambda i: (0, 0))],
        out_shape=[jax.ShapeDtypeStruct((T, D), F32), jax.ShapeDtypeStruct((SUBLANES, LANES), F32),
                   jax.ShapeDtypeStruct((1, D), F32)],
        compiler_params=_params(("arbitrary",)),
    )(h, g, target)


def _prefix_sum_lanes(x):
    lane = lax.broadcasted_iota(jnp.int32, x.shape, 1)
    d = 1
    while d < LANES:
        x = x + jnp.where(lane >= d, pltpu.roll(x, d, axis=1), 0.0)
        d *= 2
    return x


def _forget_cumsum(ft, bf, *, name):
    H, T = ft.shape
    nb = T // LANES

    def body(f_ref, b_ref, o_ref):
        carry = jnp.zeros((H, 1), F32)
        for j in range(nb):
            sl = pl.ds(j * LANES, LANES)
            lf = jax.nn.log_sigmoid(f_ref[:, sl] + b_ref[...])
            c = _prefix_sum_lanes(lf) + carry
            o_ref[:, sl] = c
            carry = c[:, LANES - 1:LANES]

    return pl.pallas_call(
        body, name=name,
        in_specs=[pl.BlockSpec(memory_space=pltpu.VMEM), pl.BlockSpec(memory_space=pltpu.VMEM)],
        out_specs=pl.BlockSpec(memory_space=pltpu.VMEM),
        out_shape=jax.ShapeDtypeStruct((H, T), F32),
        compiler_params=pltpu.CompilerParams(vmem_limit_bytes=VMEM_LIMIT),
    )(ft, bf)


def _forget_bwd(dF, ft, bf, *, name):
    H, T = ft.shape
    nb = T // LANES

    def body(d_ref, f_ref, b_ref, o_ref, s_ref):
        carry = jnp.zeros((H, 1), F32)
        acc = jnp.zeros((H, LANES), F32)
        for j in reversed(range(nb)):
            sl = pl.ds(j * LANES, LANES)
            d = d_ref[:, sl]
            pre = _prefix_sum_lanes(d)
            tot = pre[:, LANES - 1:LANES]
            dlf = tot - pre + d + carry
            carry = carry + tot
            z = f_ref[:, sl] + b_ref[...]
            df = dlf * jax.nn.sigmoid(-z)
            o_ref[:, sl] = df
            acc = acc + df
        s_ref[...] = jnp.broadcast_to(jnp.sum(acc, axis=1, keepdims=True), (H, LANES))

    return pl.pallas_call(
        body, name=name,
        in_specs=[pl.BlockSpec(memory_space=pltpu.VMEM)] * 3,
        out_specs=[pl.BlockSpec(memory_space=pltpu.VMEM)] * 2,
        out_shape=[jax.ShapeDtypeStruct((H, T), F32), jax.ShapeDtypeStruct((H, LANES), F32)],
        compiler_params=pltpu.CompilerParams(vmem_limit_bytes=VMEM_LIMIT),
    )(dF, ft, bf)


def _scores(q_ref, k_ref, fq_ref, fk_ref, h, blk, scale, diagonal):
    hs = pl.ds(h * HEAD_DIM, HEAD_DIM)
    s = lax.dot_general(k_ref[:, hs], q_ref[:, hs], (((1,), (1,)), ((), ())), preferred_element_type=F32)
    s = s * scale + (fq_ref[h:h + 1, :] - fk_ref[:, h:h + 1])
    if diagonal:
        key = lax.broadcasted_iota(jnp.int32, (blk, blk), 0)
        query = lax.broadcasted_iota(jnp.int32, (blk, blk), 1)
        s = jnp.where(key <= query, s, MASK_VALUE)
    return s


def _causal_blocks(i, j, compute):
    @pl.when(j < i)
    def _():
        compute(False)

    @pl.when(j == i)
    def _():
        compute(True)


def _attn_fwd(zqkv, fcum, fcol, *, name, riders=()):
    T = zqkv.shape[0]
    DA = zqkv.shape[1] // 3
    H = DA // HEAD_DIM
    blk = _divisor(T, ATT_BLOCK, LANES)
    nb = T // blk
    scale = HEAD_DIM ** -0.5

    def body(q_ref, k_ref, v_ref, fq_ref, fk_ref, o_ref, lse_ref, m_ref, l_ref, acc_ref):
        i = pl.program_id(0)
        j = pl.program_id(1)

        @pl.when(j == 0)
        def _():
            m_ref[...] = jnp.full_like(m_ref, MASK_VALUE)
            l_ref[...] = jnp.zeros_like(l_ref)
            acc_ref[...] = jnp.zeros_like(acc_ref)

        def compute(diagonal):
            for h in range(H):
                hs = pl.ds(h * HEAD_DIM, HEAD_DIM)
                s = _scores(q_ref, k_ref, fq_ref, fk_ref, h, blk, scale, diagonal)
                m_prev = m_ref[h:h + 1, :]
                m_new = jnp.maximum(m_prev, jnp.max(s, axis=0, keepdims=True))
                alpha = jnp.exp(m_prev - m_new)
                p = jnp.exp(s - m_new)
                l_ref[h:h + 1, :] = alpha * l_ref[h:h + 1, :] + jnp.sum(p, axis=0, keepdims=True)
                acc_ref[hs, :] = alpha * acc_ref[hs, :] + lax.dot_general(
                    v_ref[:, hs], p.astype(BF16), (((0,), (0,)), ((), ())), preferred_element_type=F32)
                m_ref[h:h + 1, :] = m_new

        _causal_blocks(i, j, compute)

        @pl.when(j == nb - 1)
        def _():
            for h in range(H):
                hs = pl.ds(h * HEAD_DIM, HEAD_DIM)
                l = l_ref[h:h + 1, :]
                o_ref[:, hs] = (acc_ref[hs, :] / l).T.astype(o_ref.dtype)
                lse_ref[h:h + 1, :] = m_ref[h:h + 1, :] + jnp.log(l)

    kv = lambda c: (lambda i, j: (jnp.minimum(j, i), c))
    (o, lse), carried = _gridded_call(
        name, (nb, nb), ("parallel", "arbitrary"), body,
        [pl.BlockSpec((blk, DA), lambda i, j: (i, 0)),
         pl.BlockSpec((blk, DA), kv(1)), pl.BlockSpec((blk, DA), kv(2)),
         pl.BlockSpec((H, blk), lambda i, j: (0, i)),
         pl.BlockSpec((blk, H), lambda i, j: (jnp.minimum(j, i), 0))],
        [pl.BlockSpec((blk, DA), lambda i, j: (i, 0)), pl.BlockSpec((H, blk), lambda i, j: (0, i))],
        [jax.ShapeDtypeStruct((T, DA), BF16), jax.ShapeDtypeStruct((H, T), F32)],
        [pltpu.VMEM((H, blk), F32), pltpu.VMEM((H, blk), F32), pltpu.VMEM((DA, blk), F32)],
        [zqkv, zqkv, zqkv, fcum, fcol], riders)
    return o, lse, carried


def _attn_delta(zqkv, fcum, fcol, do, lse, *, name):
    T = zqkv.shape[0]
    DA = zqkv.shape[1] // 3
    H = DA // HEAD_DIM
    blk = _divisor(T, ATT_BLOCK, LANES)
    nb = T // blk
    scale = HEAD_DIM ** -0.5

    def body(q_ref, k_ref, v_ref, fq_ref, fk_ref, do_ref, lse_ref, d_ref):
        i = pl.program_id(0)
        j = pl.program_id(1)

        @pl.when(j == 0)
        def _():
            d_ref[...] = jnp.zeros_like(d_ref)

        def compute(diagonal):
            for h in range(H):
                hs = pl.ds(h * HEAD_DIM, HEAD_DIM)
                s = _scores(q_ref, k_ref, fq_ref, fk_ref, h, blk, scale, diagonal)
                p = jnp.exp(s - lse_ref[h:h + 1, :])
                dp = lax.dot_general(v_ref[:, hs], do_ref[:, hs], (((1,), (1,)), ((), ())),
                                     preferred_element_type=F32)
                d_ref[h:h + 1, :] += jnp.sum(p * dp, axis=0, keepdims=True)

        _causal_blocks(i, j, compute)

    kv = lambda c: (lambda i, j: (jnp.minimum(j, i), c))
    row = lambda i, j: (i, 0)
    lane = lambda i, j: (0, i)
    return pl.pallas_call(
        body, name=name, grid=(nb, nb),
        in_specs=[pl.BlockSpec((blk, DA), row), pl.BlockSpec((blk, DA), kv(1)), pl.BlockSpec((blk, DA), kv(2)),
                  pl.BlockSpec((H, blk), lane), pl.BlockSpec((blk, H), lambda i, j: (jnp.minimum(j, i), 0)),
                  pl.BlockSpec((blk, DA), row), pl.BlockSpec((H, blk), lane)],
        out_specs=pl.BlockSpec((H, blk), lane),
        out_shape=jax.ShapeDtypeStruct((H, T), F32),
        compiler_params=_params(("parallel", "arbitrary")),
    )(zqkv, zqkv, zqkv, fcum, fcol, do, lse)


def _attn_bwd_dq(zqkv, fcum, fcol, do, lse, delta, *, name):
    T = zqkv.shape[0]
    DA = zqkv.shape[1] // 3
    H = DA // HEAD_DIM
    blk = _divisor(T, ATT_BLOCK, LANES)
    nb = T // blk
    scale = HEAD_DIM ** -0.5

    def body(q_ref, k_ref, v_ref, fq_ref, fk_ref, do_ref, lse_ref, dl_ref, dq_ref, acc_ref):
        i = pl.program_id(0)
        j = pl.program_id(1)

        @pl.when(j == 0)
        def _():
            acc_ref[...] = jnp.zeros_like(acc_ref)

        def compute(diagonal):
            for h in range(H):
                hs = pl.ds(h * HEAD_DIM, HEAD_DIM)
                s = _scores(q_ref, k_ref, fq_ref, fk_ref, h, blk, scale, diagonal)
                p = jnp.exp(s - lse_ref[h:h + 1, :])
                dp = lax.dot_general(v_ref[:, hs], do_ref[:, hs], (((1,), (1,)), ((), ())),
                                     preferred_element_type=F32)
                ds = p * (dp - dl_ref[h:h + 1, :])
                acc_ref[hs, :] += scale * lax.dot_general(k_ref[:, hs], ds.astype(BF16), (((0,), (0,)), ((), ())),
                                                          preferred_element_type=F32)

        _causal_blocks(i, j, compute)

        @pl.when(j == nb - 1)
        def _():
            for h in range(H):
                hs = pl.ds(h * HEAD_DIM, HEAD_DIM)
                dq_ref[:, hs] = acc_ref[hs, :].T.astype(dq_ref.dtype)

    kv = lambda c: (lambda i, j: (jnp.minimum(j, i), c))
    row = lambda i, j: (i, 0)
    lane = lambda i, j: (0, i)
    return pl.pallas_call(
        body, name=name, grid=(nb, nb),
        in_specs=[pl.BlockSpec((blk, DA), row), pl.BlockSpec((blk, DA), kv(1)), pl.BlockSpec((blk, DA), kv(2)),
                  pl.BlockSpec((H, blk), lane), pl.BlockSpec((blk, H), lambda i, j: (jnp.minimum(j, i), 0)),
                  pl.BlockSpec((blk, DA), row), pl.BlockSpec((H, blk), lane), pl.BlockSpec((H, blk), lane)],
        out_specs=pl.BlockSpec((blk, DA), row),
        out_shape=jax.ShapeDtypeStruct((T, DA), BF16),
        scratch_shapes=[pltpu.VMEM((DA, blk), F32)],
        compiler_params=_params(("parallel", "arbitrary")),
    )(zqkv, zqkv, zqkv, fcum, fcol, do, lse, delta)


def _attn_bwd_dkv(zqkv, fcum, fcol, do, lse, delta, *, name, riders=()):
    T = zqkv.shape[0]
    DA = zqkv.shape[1] // 3
    H = DA // HEAD_DIM
    blk = _divisor(T, ATT_BLOCK, LANES)
    nb = T // blk
    scale = HEAD_DIM ** -0.5

    def body(q_ref, k_ref, v_ref, fq_ref, fk_ref, do_ref, lse_ref, dl_ref, dk_ref, dv_ref, df_ref,
             dk_acc, dv_acc, df_acc):
        j = pl.program_id(0)
        i = pl.program_id(1)

        @pl.when(i == 0)
        def _():
            dk_acc[...] = jnp.zeros_like(dk_acc)
            dv_acc[...] = jnp.zeros_like(dv_acc)
            df_acc[...] = jnp.zeros_like(df_acc)

        def compute(diagonal):
            for h in range(H):
                hs = pl.ds(h * HEAD_DIM, HEAD_DIM)
                s = _scores(q_ref, k_ref, fq_ref, fk_ref, h, blk, scale, diagonal)
                p = jnp.exp(s - lse_ref[h:h + 1, :])
                dov = do_ref[:, hs]
                dv_acc[:, hs] += jnp.dot(p.astype(BF16), dov, preferred_element_type=F32)
                dp = lax.dot_general(v_ref[:, hs], dov, (((1,), (1,)), ((), ())), preferred_element_type=F32)
                ds = p * (dp - dl_ref[h:h + 1, :])
                dk_acc[:, hs] += scale * jnp.dot(ds.astype(BF16), q_ref[:, hs], preferred_element_type=F32)
                df_acc[:, h:h + 1] -= jnp.sum(ds, axis=1, keepdims=True)

        _causal_blocks(i, j, compute)

        @pl.when(i == nb - 1)
        def _():
            dk_ref[...] = dk_acc[...].astype(dk_ref.dtype)
            dv_ref[...] = dv_acc[...].astype(dv_ref.dtype)
            df_ref[...] = df_acc[...]

    qrow = lambda j, i: (jnp.maximum(i, j), 0)
    qlane = lambda j, i: (0, jnp.maximum(i, j))
    kcol = lambda c: (lambda j, i: (j, c))
    (dk, dv, df), carried = _gridded_call(
        name, (nb, nb), ("parallel", "arbitrary"), body,
        [pl.BlockSpec((blk, DA), qrow), pl.BlockSpec((blk, DA), kcol(1)), pl.BlockSpec((blk, DA), kcol(2)),
         pl.BlockSpec((H, blk), qlane), pl.BlockSpec((blk, H), kcol(0)),
         pl.BlockSpec((blk, DA), qrow), pl.BlockSpec((H, blk), qlane), pl.BlockSpec((H, blk), qlane)],
        [pl.BlockSpec((blk, DA), kcol(0)), pl.BlockSpec((blk, DA), kcol(0)), pl.BlockSpec((blk, H), kcol(0))],
        [jax.ShapeDtypeStruct((T, DA), BF16), jax.ShapeDtypeStruct((T, DA), BF16),
         jax.ShapeDtypeStruct((T, H), F32)],
        [pltpu.VMEM((blk, DA), F32), pltpu.VMEM((blk, DA), F32), pltpu.VMEM((blk, H), F32)],
        [zqkv, zqkv, zqkv, fcum, fcol, do, lse, delta], riders)
    return dk, dv, df, carried


def _gelu(y):
    c = math.sqrt(2.0 / math.pi)
    return 0.5 * y * (1.0 + jnp.tanh(c * (y + 0.044715 * (y * y * y))))


def _gelu_grad(y):
    c = math.sqrt(2.0 / math.pi)
    th = jnp.tanh(c * (y + 0.044715 * (y * y * y)))
    return 0.5 * (1.0 + th) + 0.5 * y * (1.0 - th * th) * c * (1.0 + 3.0 * 0.044715 * y * y)


STATE_BLOCKS = SLAB_STATE // LANES


def _lane_blocks(ref, lead=()):
    return [ref[lead + (slice(None), pl.ds(b * LANES, LANES))] for b in range(2 * STATE_BLOCKS)]


def _put_lane_blocks(ref, blocks):
    for b, v in enumerate(blocks):
        ref[:, pl.ds(b * LANES, LANES)] = v


def _put_slab(x_ref, first, q, n_slab, chunk, value):
    for b in range(2 * STATE_BLOCKS):
        x_ref[b, pl.ds(first * n_slab + q, chunk, stride=n_slab), :] = value[:, b * LANES:(b + 1) * LANES]


def _get_slab(x_ref, first, q, n_slab, chunk):
    return jnp.concatenate([x_ref[b, pl.ds(first * n_slab + q, chunk, stride=n_slab), :]
                            for b in range(2 * STATE_BLOCKS)], axis=1)


def _ssm_scan_fwd(x_ref, a, h, chunk, n_slab, first=0):
    nb = STATE_BLOCKS

    def step(t, h):
        rows = pl.ds(pl.multiple_of((t + first) * n_slab, n_slab), n_slab)
        out = [None] * (2 * nb)
        for b in range(nb):
            n_re = a[b] * h[b] - a[nb + b] * h[nb + b] + x_ref[b, rows, :]
            n_im = a[b] * h[nb + b] + a[nb + b] * h[b] + x_ref[nb + b, rows, :]
            x_ref[b, rows, :] = n_re
            x_ref[nb + b, rows, :] = n_im
            out[b], out[nb + b] = n_re, n_im
        return tuple(out)

    return lax.fori_loop(0, chunk, step, tuple(h), unroll=4)


def _ssm_fwd(zu, w_b, w_c, a, d_skip, *, name):
    T, DS = zu.shape
    n_slab = DS // LANES
    chunk = _divisor(T, SSM_CHUNK, SUBLANES)
    n_chunk = T // chunk

    def body(u_ref, wb_ref, wc_ref, a_ref, ds_ref, y_ref, gy_ref, hin_ref, x_ref, h_ref):
        k = pl.program_id(0)

        @pl.when(k == 0)
        def _():
            h_ref[...] = jnp.zeros_like(h_ref)

        hin_ref[0] = h_ref[...]
        for q in range(n_slab):
            qs = pl.ds(q * LANES, LANES)
            _put_slab(x_ref, 0, q, n_slab, chunk, jnp.dot(u_ref[:, qs], wb_ref[q], preferred_element_type=F32))
        h = _ssm_scan_fwd(x_ref, _lane_blocks(a_ref), _lane_blocks(h_ref), chunk, n_slab)
        _put_lane_blocks(h_ref, h)
        for q in range(n_slab):
            qs = pl.ds(q * LANES, LANES)
            hq = _get_slab(x_ref, 0, q, n_slab, chunk).astype(BF16)
            y = jnp.dot(hq, wc_ref[q], preferred_element_type=F32) + ds_ref[:, qs] * u_ref[:, qs].astype(F32)
            y_ref[:, qs] = y
            gy_ref[:, qs] = _gelu(y).astype(gy_ref.dtype)

    whole = lambda shape: pl.BlockSpec(shape, lambda k: (0,) * len(shape))
    return pl.pallas_call(
        body, name=name, grid=(n_chunk,),
        in_specs=[pl.BlockSpec((chunk, DS), lambda k: (k, 0)), whole(w_b.shape), whole(w_c.shape),
                  whole(a.shape), whole(d_skip.shape)],
        out_specs=[pl.BlockSpec((chunk, DS), lambda k: (k, 0)), pl.BlockSpec((chunk, DS), lambda k: (k, 0)),
                   pl.BlockSpec((1, n_slab, 2 * SLAB_STATE), lambda k: (k, 0, 0))],
        out_shape=[jax.ShapeDtypeStruct((T, DS), F32), jax.ShapeDtypeStruct((T, DS), BF16),
                   jax.ShapeDtypeStruct((n_chunk, n_slab, 2 * SLAB_STATE), F32)],
        scratch_shapes=[pltpu.VMEM((2 * STATE_BLOCKS, chunk * n_slab, LANES), F32),
                        pltpu.VMEM((n_slab, 2 * SLAB_STATE), F32)],
        compiler_params=_params(("arbitrary",)),
    )(zu, w_b, w_c, a, d_skip)


def _ssm_bwd(zu, dgy, y, hin, w_b, w_bt, w_ct, a, d_skip, *, name):
    T, DS = zu.shape
    n_slab = DS // LANES
    chunk = _divisor(T, SSM_CHUNK, SUBLANES)
    n_chunk = T // chunk
    S = SLAB_STATE

    def body(u_ref, dgy_ref, y_ref, hin_ref, wb_ref, wbt_ref, wct_ref, a_ref, ds_ref,
             du_ref, dwb_ref, dwc_ref, da_ref, dds_ref, hb_ref, gb_ref, dy_ref, g_ref):
        k = pl.program_id(0)

        @pl.when(k == 0)
        def _():
            g_ref[...] = jnp.zeros_like(g_ref)
            dwb_ref[...] = jnp.zeros_like(dwb_ref)
            dwc_ref[...] = jnp.zeros_like(dwc_ref)
            da_ref[...] = jnp.zeros_like(da_ref)
            dds_ref[...] = jnp.zeros_like(dds_ref)

        nb = STATE_BLOCKS
        a = _lane_blocks(a_ref)
        hin = _lane_blocks(hin_ref, lead=(0,))

        for b in range(2 * nb):
            hb_ref[b, pl.ds(0, n_slab), :] = hin[b]
        dy_ref[...] = dgy_ref[...].astype(F32) * _gelu_grad(y_ref[...])
        for q in range(n_slab):
            qs = pl.ds(q * LANES, LANES)
            _put_slab(hb_ref, 1, q, n_slab, chunk, jnp.dot(u_ref[:, qs], wb_ref[q], preferred_element_type=F32))
            _put_slab(gb_ref, 0, q, n_slab, chunk,
                      jnp.dot(dy_ref[:, qs].astype(BF16), wct_ref[q], preferred_element_type=F32))
        _ssm_scan_fwd(hb_ref, a, hin, chunk, n_slab, first=1)

        def step(s, carry):
            g, da = carry[:2 * nb], carry[2 * nb:]
            t = chunk - 1 - s
            rows = pl.ds(pl.multiple_of(t * n_slab, n_slab), n_slab)
            g_out, da_out = [None] * (2 * nb), [None] * (2 * nb)
            for b in range(nb):
                n_re = gb_ref[b, rows, :] + a[b] * g[b] + a[nb + b] * g[nb + b]
                n_im = gb_ref[nb + b, rows, :] + a[b] * g[nb + b] - a[nb + b] * g[b]
                gb_ref[b, rows, :] = n_re
                gb_ref[nb + b, rows, :] = n_im
                p_re = hb_ref[b, rows, :]
                p_im = hb_ref[nb + b, rows, :]
                g_out[b], g_out[nb + b] = n_re, n_im
                da_out[b] = da[b] + n_re * p_re + n_im * p_im
                da_out[nb + b] = da[nb + b] + n_im * p_re - n_re * p_im
            return tuple(g_out) + tuple(da_out)

        zero = jnp.zeros((n_slab, LANES), F32)
        carry = lax.fori_loop(0, chunk, step, tuple(_lane_blocks(g_ref)) + (zero,) * (2 * nb), unroll=4)
        _put_lane_blocks(g_ref, carry[:2 * nb])
        for b in range(2 * nb):
            da_ref[:, pl.ds(b * LANES, LANES)] += carry[2 * nb + b]

        for q in range(n_slab):
            qs = pl.ds(q * LANES, LANES)
            uq = u_ref[:, qs]
            dy = dy_ref[:, qs]
            hq = _get_slab(hb_ref, 1, q, n_slab, chunk).astype(BF16)
            gq = _get_slab(gb_ref, 0, q, n_slab, chunk).astype(BF16)
            dwc_ref[q] += lax.dot_general(hq, dy.astype(BF16), (((0,), (0,)), ((), ())), preferred_element_type=F32)
            dwb_ref[q] += lax.dot_general(uq, gq, (((0,), (0,)), ((), ())), preferred_element_type=F32)
            du_ref[:, qs] = (jnp.dot(gq, wbt_ref[q], preferred_element_type=F32) + ds_ref[:, qs] * dy).astype(du_ref.dtype)
            dds_ref[:, qs] += jnp.sum(dy * uq.astype(F32), axis=0, keepdims=True)

    whole = lambda shape: pl.BlockSpec(shape, lambda k: (0,) * len(shape))
    rev = lambda k: (n_chunk - 1 - k, 0)
    return pl.pallas_call(
        body, name=name, grid=(n_chunk,),
        in_specs=[pl.BlockSpec((chunk, DS), rev), pl.BlockSpec((chunk, DS), rev), pl.BlockSpec((chunk, DS), rev),
                  pl.BlockSpec((1, n_slab, 2 * S), lambda k: (n_chunk - 1 - k, 0, 0)),
                  whole(w_b.shape), whole(w_bt.shape), whole(w_ct.shape), whole(a.shape), whole(d_skip.shape)],
        out_specs=[pl.BlockSpec((chunk, DS), rev), whole(w_b.shape), whole(w_bt.shape), whole(a.shape),
                   whole(d_skip.shape)],
        out_shape=[jax.ShapeDtypeStruct((T, DS), BF16), jax.ShapeDtypeStruct(w_b.shape, F32),
                   jax.ShapeDtypeStruct(w_bt.shape, F32), jax.ShapeDtypeStruct(a.shape, F32),
                   jax.ShapeDtypeStruct(d_skip.shape, F32)],
        scratch_shapes=[pltpu.VMEM((2 * STATE_BLOCKS, (chunk + 1) * n_slab, LANES), F32),
                        pltpu.VMEM((2 * STATE_BLOCKS, chunk * n_slab, LANES), F32),
                        pltpu.VMEM((chunk, DS), F32), pltpu.VMEM((n_slab, 2 * S), F32)],
        compiler_params=_params(("arbitrary",)),
    )(zu, dgy, y, hin, w_b, w_bt, w_ct, a, d_skip)


def _ssm_discretise(lam_re, lam_im, log_dt, b_re, b_im):
    dt = jnp.exp(log_dt)[:, None]
    mag = jnp.exp(lam_re * dt)
    a_re = mag * jnp.cos(lam_im * dt)
    a_im = mag * jnp.sin(lam_im * dt)
    den = lam_re * lam_re + lam_im * lam_im
    nr = a_re - 1.0
    z_re = (nr * lam_re + a_im * lam_im) / den
    z_im = (a_im * lam_re - nr * lam_im) / den
    bb_re = z_re[..., None] * b_re - z_im[..., None] * b_im
    bb_im = z_re[..., None] * b_im + z_im[..., None] * b_re
    return a_re, a_im, bb_re, bb_im


def _slab_in(m_re, m_im):
    G, P, C = m_re.shape
    n_slab = G // GROUPS_PER_SLAB
    eye = jnp.eye(GROUPS_PER_SLAB, dtype=m_re.dtype)

    def one(m):
        m = m.reshape(n_slab, GROUPS_PER_SLAB, P, C)
        w = jnp.einsum('sgpc,gh->sgchp', m, eye)
        return w.reshape(n_slab, GROUPS_PER_SLAB * C, GROUPS_PER_SLAB * P)

    return jnp.concatenate([one(m_re), one(m_im)], axis=2)


def _slab_in_grad(dw, G, P, C):
    n_slab = G // GROUPS_PER_SLAB
    eye = jnp.eye(GROUPS_PER_SLAB, dtype=dw.dtype)

    def one(w):
        w = w.reshape(n_slab, GROUPS_PER_SLAB, C, GROUPS_PER_SLAB, P)
        return jnp.einsum('sgchp,gh->sgpc', w, eye).reshape(G, P, C)

    return one(dw[:, :, :SLAB_STATE]), one(dw[:, :, SLAB_STATE:])


def _slab_out(c_re, c_im):
    G, C, P = c_re.shape
    n_slab = G // GROUPS_PER_SLAB
    eye = jnp.eye(GROUPS_PER_SLAB, dtype=c_re.dtype)

    def one(m):
        m = m.reshape(n_slab, GROUPS_PER_SLAB, C, P)
        w = jnp.einsum('sgcp,gh->shpgc', m, eye)
        return w.reshape(n_slab, GROUPS_PER_SLAB * P, GROUPS_PER_SLAB * C)

    return jnp.concatenate([one(c_re), one(-c_im)], axis=1)


def _slab_out_grad(dw, G, C, P):
    n_slab = G // GROUPS_PER_SLAB
    eye = jnp.eye(GROUPS_PER_SLAB, dtype=dw.dtype)

    def one(w):
        w = w.reshape(n_slab, GROUPS_PER_SLAB, P, GROUPS_PER_SLAB, C)
        return jnp.einsum('shpgc,gh->sgcp', w, eye).reshape(G, C, P)

    return one(dw[:, :SLAB_STATE, :]), -one(dw[:, SLAB_STATE:, :])


def _slab_diag(a_re, a_im):
    G, P = a_re.shape
    n_slab = G // GROUPS_PER_SLAB
    return jnp.concatenate([a_re.reshape(n_slab, SLAB_STATE), a_im.reshape(n_slab, SLAB_STATE)], axis=1)


def _merge_fwd(yab, zg, attn, *, name):
    T, D = attn.shape
    tr = _divisor(T, ROW_BLOCK, SUBLANES)

    def body(ya_ref, yb_ref, ga_ref, gb_ref, at_ref, o_ref):
        f = lambda r: r[...].astype(F32)
        ssm = f(ya_ref) * jax.nn.sigmoid(f(yb_ref))
        o_ref[...] = (jax.nn.sigmoid(f(ga_ref)) * ssm + jax.nn.sigmoid(f(gb_ref)) * f(at_ref)).astype(o_ref.dtype)

    lo = pl.BlockSpec((tr, D), lambda i: (i, 0))
    hi = pl.BlockSpec((tr, D), lambda i: (i, 1))
    return pl.pallas_call(
        body, name=name, grid=(T // tr,),
        in_specs=[lo, hi, lo, hi, lo],
        out_specs=lo,
        out_shape=jax.ShapeDtypeStruct((T, D), BF16),
        compiler_params=_params(("parallel",)),
    )(yab, yab, zg, zg, attn)


def _merge_bwd(dm, yab, zg, attn, *, name):
    T, D = attn.shape
    tr = _divisor(T, ROW_BLOCK, SUBLANES)

    def body(dm_ref, ya_ref, yb_ref, ga_ref, gb_ref, at_ref, dg_ref, dat_ref, dy_ref):
        f = lambda r: r[...].astype(F32)
        dmv, ya, at = f(dm_ref), f(ya_ref), f(at_ref)
        sa, sb, syb = jax.nn.sigmoid(f(ga_ref)), jax.nn.sigmoid(f(gb_ref)), jax.nn.sigmoid(f(yb_ref))
        ssm = ya * syb
        dssm = dmv * sa
        dg_ref[:, pl.ds(0, D)] = (dmv * ssm * sa * (1.0 - sa)).astype(dg_ref.dtype)
        dg_ref[:, pl.ds(D, D)] = (dmv * at * sb * (1.0 - sb)).astype(dg_ref.dtype)
        dat_ref[...] = (dmv * sb).astype(dat_ref.dtype)
        dy_ref[:, pl.ds(0, D)] = (dssm * syb).astype(dy_ref.dtype)
        dy_ref[:, pl.ds(D, D)] = (dssm * ya * syb * (1.0 - syb)).astype(dy_ref.dtype)

    lo = pl.BlockSpec((tr, D), lambda i: (i, 0))
    hi = pl.BlockSpec((tr, D), lambda i: (i, 1))
    both = pl.BlockSpec((tr, 2 * D), lambda i: (i, 0))
    return pl.pallas_call(
        body, name=name, grid=(T // tr,),
        in_specs=[lo, lo, hi, lo, hi, lo],
        out_specs=[both, lo, both],
        out_shape=[jax.ShapeDtypeStruct((T, 2 * D), BF16), jax.ShapeDtypeStruct((T, D), BF16),
                   jax.ShapeDtypeStruct((T, 2 * D), BF16)],
        compiler_params=_params(("parallel",)),
    )(dm, yab, yab, zg, zg, attn)


def _conv_taps(g_ref, halo_ref, i, tr):
    g0 = g_ref[...].astype(F32)
    halo = jnp.where(i > 0, halo_ref[...].astype(F32), 0.0)
    row = lax.broadcasted_iota(jnp.int32, g0.shape, 0)
    g1 = jnp.where(row == 0, halo[SUBLANES - 1:SUBLANES, :], pltpu.roll(g0, 1, axis=0))
    g2 = pltpu.roll(g0, 2, axis=0)
    g2 = jnp.where(row == 0, halo[SUBLANES - 2:SUBLANES - 1, :], g2)
    g2 = jnp.where(row == 1, halo[SUBLANES - 1:SUBLANES, :], g2)
    return g0, g1, g2


def _conv_blocks(T, FF):
    tr = _divisor(T, ROW_BLOCK, SUBLANES)
    tc = _divisor(FF, 1024, LANES)
    return tr, tc


def _conv_fwd(gu, conv_w, conv_b, *, name):
    T = gu.shape[0]
    FF = gu.shape[1] // 2
    tr, tc = _conv_blocks(T, FF)
    ncol = FF // tc

    def body(g_ref, halo_ref, u_ref, w_ref, b_ref, o_ref):
        i = pl.program_id(0)
        g0, g1, g2 = _conv_taps(g_ref, halo_ref, i, tr)
        gc = b_ref[...] + w_ref[0:1, :] * g2 + w_ref[1:2, :] * g1 + w_ref[2:3, :] * g0
        o_ref[...] = (gc * jax.nn.sigmoid(gc) * u_ref[...].astype(F32)).astype(o_ref.dtype)

    hb = tr // SUBLANES
    return pl.pallas_call(
        body, name=name, grid=(T // tr, ncol),
        in_specs=[pl.BlockSpec((tr, tc), lambda i, j: (i, j)),
                  pl.BlockSpec((SUBLANES, tc), lambda i, j: (jnp.maximum(i * hb - 1, 0), j)),
                  pl.BlockSpec((tr, tc), lambda i, j: (i, j + ncol)),
                  pl.BlockSpec((SUBLANES, tc), lambda i, j: (0, j)), pl.BlockSpec((1, tc), lambda i, j: (0, j))],
        out_specs=pl.BlockSpec((tr, tc), lambda i, j: (i, j)),
        out_shape=jax.ShapeDtypeStruct((T, FF), BF16),
        compiler_params=_params(("parallel", "parallel")),
    )(gu, gu, gu, conv_w, conv_b)


def _conv_bwd_gate(da, gu, conv_w, conv_b, *, name, riders=()):
    T = gu.shape[0]
    FF = gu.shape[1] // 2
    tr, tc = _conv_blocks(T, FF)
    ncol = FF // tc

    def body(da_ref, g_ref, halo_ref, u_ref, w_ref, b_ref, dgc_ref, du_ref, s_ref):
        i = pl.program_id(1)

        @pl.when(i == 0)
        def _():
            s_ref[...] = jnp.zeros_like(s_ref)

        g0, g1, g2 = _conv_taps(g_ref, halo_ref, i, tr)
        gc = b_ref[...] + w_ref[0:1, :] * g2 + w_ref[1:2, :] * g1 + w_ref[2:3, :] * g0
        sg = jax.nn.sigmoid(gc)
        dav = da_ref[...].astype(F32)
        du_ref[...] = (dav * gc * sg).astype(du_ref.dtype)
        dgc = dav * u_ref[...].astype(F32) * (sg * (1.0 + gc * (1.0 - sg)))
        dgc_ref[...] = dgc.astype(dgc_ref.dtype)
        s_ref[0:1, :] += jnp.sum(dgc * g2, axis=0, keepdims=True)
        s_ref[1:2, :] += jnp.sum(dgc * g1, axis=0, keepdims=True)
        s_ref[2:3, :] += jnp.sum(dgc * g0, axis=0, keepdims=True)
        s_ref[3:4, :] += jnp.sum(dgc, axis=0, keepdims=True)

    hb = tr // SUBLANES
    blk = pl.BlockSpec((tr, tc), lambda j, i: (i, j))
    (dgc, du, sums), carried = _gridded_call(
        name, (ncol, T // tr), ("parallel", "arbitrary"), body,
        [blk, blk,
         pl.BlockSpec((SUBLANES, tc), lambda j, i: (jnp.maximum(i * hb - 1, 0), j)),
         pl.BlockSpec((tr, tc), lambda j, i: (i, j + ncol)),
         pl.BlockSpec((SUBLANES, tc), lambda j, i: (0, j)), pl.BlockSpec((1, tc), lambda j, i: (0, j))],
        [blk, blk, pl.BlockSpec((SUBLANES, tc), lambda j, i: (0, j))],
        [jax.ShapeDtypeStruct((T, FF), BF16), jax.ShapeDtypeStruct((T, FF), BF16),
         jax.ShapeDtypeStruct((SUBLANES, FF), F32)],
        [], [da, gu, gu, gu, conv_w, conv_b], riders)
    return dgc, du, sums, carried


def _conv_bwd_taps(dgc, conv_w, *, name):
    T, FF = dgc.shape
    tr, tc = _conv_blocks(T, FF)
    ncol = FF // tc
    nrow = T // tr

    def body(d_ref, next_ref, w_ref, o_ref):
        i = pl.program_id(0)
        d0 = d_ref[...].astype(F32)
        nxt = jnp.where(i < nrow - 1, next_ref[...].astype(F32), 0.0)
        row = lax.broadcasted_iota(jnp.int32, d0.shape, 0)
        d1 = jnp.where(row == tr - 1, nxt[0:1, :], pltpu.roll(d0, tr - 1, axis=0))
        d2 = pltpu.roll(d0, tr - 2, axis=0)
        d2 = jnp.where(row == tr - 2, nxt[0:1, :], d2)
        d2 = jnp.where(row == tr - 1, nxt[1:2, :], d2)
        dg = w_ref[2:3, :] * d0 + w_ref[1:2, :] * d1 + w_ref[0:1, :] * d2
        o_ref[...] = dg.astype(o_ref.dtype)

    hb = tr // SUBLANES
    last = T // SUBLANES - 1
    return pl.pallas_call(
        body, name=name, grid=(nrow, ncol),
        in_specs=[pl.BlockSpec((tr, tc), lambda i, j: (i, j)),
                  pl.BlockSpec((SUBLANES, tc), lambda i, j: (jnp.minimum((i + 1) * hb, last), j)),
                  pl.BlockSpec((SUBLANES, tc), lambda i, j: (0, j))],
        out_specs=pl.BlockSpec((tr, tc), lambda i, j: (i, j)),
        out_shape=jax.ShapeDtypeStruct((T, FF), BF16),
        compiler_params=_params(("parallel", "parallel")),
    )(dgc, dgc, conv_w)


def _mesh_pos():
    return lax.axis_index("x"), lax.axis_index("y"), lax.axis_index("c")


def _flip(pos, mask):
    x, y, c = pos
    return (x ^ ((mask >> 2) & 1), y ^ ((mask >> 1) & 1), c ^ (mask & 1))


def _index_of(pos):
    x, y, c = pos
    return 4 * x + 2 * y + c


class _Rider(collections.namedtuple("_Rider", "src gather cols R c")):
    def out_shape(self):
        if not self.gather:
            shape = (N_DEV, self.R, self.c)
        elif self.cols:
            shape = (self.R, N_DEV * self.c)
        else:
            shape = (N_DEV * self.R, self.c)
        return jax.ShapeDtypeStruct(shape, self.src.dtype)

    def slab(self, ref, idx):
        if self.cols:
            return ref.at[:, pl.ds(pl.multiple_of(idx * self.c, LANES), self.c)]
        return ref.at[pl.ds(pl.multiple_of(idx * self.R, 2 * SUBLANES), self.R), :]

    def copy(self, src_ref, dst_ref, send_sems, recv_sems, me, k, arriving):
        peer = _flip(me, k)
        owner = _index_of(peer if arriving else me)
        if self.gather:
            src, dst = src_ref, self.slab(dst_ref, owner)
        else:
            src, dst = self.slab(src_ref, _index_of(peer)), dst_ref.at[owner]
        return pltpu.make_async_remote_copy(src_ref=src, dst_ref=dst, send_sem=send_sems.at[k - 1],
                                            recv_sem=recv_sems.at[k - 1], device_id=peer,
                                            device_id_type=pl.DeviceIdType.MESH)

    def own(self, src_ref, dst_ref, local_sem, me):
        my = _index_of(me)
        if self.gather:
            return pltpu.make_async_copy(src_ref, self.slab(dst_ref, my), local_sem)
        return pltpu.make_async_copy(self.slab(src_ref, my), dst_ref.at[my], local_sem)

    def start(self, src_ref, dst_ref, send_sems, recv_sems, local_sem):
        me = _mesh_pos()
        self.own(src_ref, dst_ref, local_sem, me).start()
        for k in range(1, N_DEV):
            self.copy(src_ref, dst_ref, send_sems, recv_sems, me, k, False).start()

    def wait(self, src_ref, dst_ref, send_sems, recv_sems, local_sem):
        me = _mesh_pos()
        for k in range(1, N_DEV):
            self.copy(src_ref, dst_ref, send_sems, recv_sems, me, k, True).wait_recv()
        for k in range(1, N_DEV):
            self.copy(src_ref, dst_ref, send_sems, recv_sems, me, k, False).wait_send()
        self.own(src_ref, dst_ref, local_sem, me).wait()


_RIDER_SEMS = [pltpu.SemaphoreType.DMA((N_DEV - 1,)), pltpu.SemaphoreType.DMA((N_DEV - 1,)), pltpu.SemaphoreType.DMA]
_ANY = pl.BlockSpec(memory_space=pl.ANY)


def _comm_call(riders, *, name):
    n = len(riders)

    def body(*refs):
        srcs, dsts, sems = refs[:n], refs[n:2 * n], refs[2 * n:]
        for r, rider in enumerate(riders):
            rider.start(srcs[r], dsts[r], *sems[3 * r:3 * r + 3])
        for r, rider in enumerate(riders):
            rider.wait(srcs[r], dsts[r], *sems[3 * r:3 * r + 3])

    return pl.pallas_call(
        body, name=name,
        in_specs=[_ANY] * n, out_specs=[_ANY] * n,
        out_shape=[rider.out_shape() for rider in riders],
        scratch_shapes=_RIDER_SEMS * n,
        compiler_params=pltpu.CompilerParams(has_side_effects=True),
    )(*[rider.src for rider in riders])


def _carry(riders, grid, body, in_specs, out_specs, out_shape, scratch_shapes, args):
    n, n_in, n_out, n_scratch = len(riders), len(in_specs), len(out_specs), len(scratch_shapes)

    def carrying(*refs):
        ins, refs = refs[:n_in], refs[n_in:]
        srcs, refs = refs[:n], refs[n:]
        outs, refs = refs[:n_out], refs[n_out:]
        dsts, refs = refs[:n], refs[n:]
        scratch, sems = refs[:n_scratch], refs[n_scratch:]
        ids = [pl.program_id(a) for a in range(len(grid))]
        first = functools.reduce(jnp.logical_and, [i == 0 for i in ids])
        last = functools.reduce(jnp.logical_and, [i == g - 1 for i, g in zip(ids, grid)])

        @pl.when(first)
        def _():
            for r, rider in enumerate(riders):
                rider.start(srcs[r], dsts[r], *sems[3 * r:3 * r + 3])

        body(*ins, *outs, *scratch)

        @pl.when(last)
        def _():
            for r, rider in enumerate(riders):
                rider.wait(srcs[r], dsts[r], *sems[3 * r:3 * r + 3])

    return dict(
        body=carrying,
        in_specs=list(in_specs) + [_ANY] * n,
        out_specs=list(out_specs) + [_ANY] * n,
        out_shape=list(out_shape) + [rider.out_shape() for rider in riders],
        scratch_shapes=list(scratch_shapes) + _RIDER_SEMS * n,
        args=list(args) + [rider.src for rider in riders])


def _gridded_call(name, grid, semantics, body, in_specs, out_specs, out_shape, scratch_shapes, args, riders=()):
    call = dict(body=body, in_specs=in_specs, out_specs=out_specs, out_shape=out_shape,
                scratch_shapes=scratch_shapes, args=args)
    if riders:
        call = _carry(list(riders), grid, **call)
        semantics = ("arbitrary",) * len(grid)
    outs = pl.pallas_call(
        call["body"], name=name, grid=grid, in_specs=call["in_specs"], out_specs=call["out_specs"],
        out_shape=call["out_shape"], scratch_shapes=call["scratch_shapes"],
        compiler_params=_params(semantics),
    )(*call["args"])
    n_out = len(out_shape)
    return list(outs[:n_out]), list(outs[n_out:])


def _all_gather(x, *, name):
    R, C = x.shape
    return _comm_call([_Rider(x, True, False, R, C)], name=name)[0].reshape(N_DEV, R, C)


def _exchange(parts, *, name):
    _, R, C = parts.shape
    return _comm_call([_Rider(parts.reshape(N_DEV * R, C), False, False, R, C)], name=name)[0]


def _sum_adamw(parts, w, m, v, *, name):
    R, C = w.shape
    tr = _divisor(R, max(2 * SUBLANES, ADAMW_BLOCK_ELEMS // C // SUBLANES * SUBLANES), 2 * SUBLANES)
    c1 = 1.0 - ADAM_B1 ** ADAM_STEP
    c2 = 1.0 - ADAM_B2 ** ADAM_STEP

    def body(p_ref, w_ref, m_ref, v_ref, g_ref, d_ref, nm_ref, nv_ref):
        g = p_ref[0].astype(F32)
        for s in range(1, N_DEV):
            g = g + p_ref[s].astype(F32)
        nm = ADAM_B1 * m_ref[...] + (1.0 - ADAM_B1) * g
        nv = ADAM_B2 * v_ref[...] + (1.0 - ADAM_B2) * (g * g)
        g_ref[...] = g
        nm_ref[...] = nm
        nv_ref[...] = nv
        d_ref[...] = -ADAM_LR * ((nm / c1) / (jnp.sqrt(nv / c2) + ADAM_EPS) + ADAM_WD * w_ref[...])

    blk = pl.BlockSpec((tr, C), lambda i: (i, 0))
    return pl.pallas_call(
        body, name=name, grid=(R // tr,),
        in_specs=[pl.BlockSpec((N_DEV, tr, C), lambda i: (0, i, 0)), blk, blk, blk],
        out_specs=[blk] * 4,
        out_shape=[jax.ShapeDtypeStruct((R, C), F32)] * 4,
        compiler_params=_params(("parallel",)),
    )(parts, w, m, v)


def _pad2(a, rows, cols):
    return jnp.pad(a, ((0, rows - a.shape[0]), (0, cols - a.shape[1])))


def _gather_cols(w, dtype, *, name):
    R, c = w.shape
    rp, cp = _round_up(R, 2 * SUBLANES), _round_up(c, LANES)
    g = _all_gather(_pad2(w.astype(dtype), rp, cp), name=name)
    return jnp.transpose(g[:, :R, :c], (1, 0, 2)).reshape(R, N_DEV * c)


def _column_parts(dw, c):
    R = dw.shape[0]
    parts = jnp.transpose(dw.reshape(R, N_DEV, c), (1, 0, 2))
    return jnp.pad(parts, ((0, 0), (0, _round_up(R, 2 * SUBLANES) - R), (0, _round_up(c, LANES) - c)))


def _padded_adamw(got, w, m, v, *, name):
    R, c = w.shape
    rp, cp = got.shape[1:]
    outs = _sum_adamw(got, _pad2(w, rp, cp), _pad2(m, rp, cp), _pad2(v, rp, cp), name=name)
    return [o[:R, :c] for o in outs]


def _update_cols(dw, w, m, v, *, name):
    got = _exchange(_column_parts(dw, w.shape[1]), name=name + "_exchange")
    return _padded_adamw(got, w, m, v, name=name + "_adamw")


def _update_replicated(grads, ws, ms, vs, *, name):
    sizes = [int(g.size) for g in grads]
    total = sum(sizes)
    rows = _round_up(-(-total // LANES), 2 * SUBLANES)

    def pack(arrs):
        flat = jnp.concatenate([a.reshape(-1).astype(F32) for a in arrs])
        return jnp.pad(flat, (0, rows * LANES - total)).reshape(rows, LANES)

    got = _all_gather(pack(grads), name=name + "_gather")
    outs = _sum_adamw(got, pack(ws), pack(ms), pack(vs), name=name + "_adamw")
    result = []
    for o in outs:
        flat = o.reshape(-1)
        arrs, off = [], 0
        for w, n in zip(ws, sizes):
            arrs.append(flat[off:off + n].reshape(w.shape))
            off += n
        result.append(arrs)
    return result


def kernel(x, meta, g_mix, w_in, b_f, lam_re, lam_im, log_dt, b_re, b_im, c_re, c_im, d_skip, w_glu, w_attn_o, w_out, g_ffn, w_up, conv_w, conv_b, w_down, g_final, loss_target, m_meta, m_g_mix, m_w_in, m_b_f, m_lam_re, m_lam_im, m_log_dt, m_b_re, m_b_im, m_c_re, m_c_im, m_d_skip, m_w_glu, m_w_attn_o, m_w_out, m_g_ffn, m_w_up, m_conv_w, m_conv_b, m_w_down, m_g_final, v_meta, v_g_mix, v_w_in, v_b_f, v_lam_re, v_lam_im, v_log_dt, v_b_re, v_b_im, v_c_re, v_c_im, v_d_skip, v_w_glu, v_w_attn_o, v_w_out, v_g_ffn, v_w_up, v_conv_w, v_conv_b, v_w_down, v_g_final):
    seq, D = x.shape[1], x.shape[2]
    L = N_META + seq
    T = _round_up(L, SEQ_BLOCK) if L > SEQ_BLOCK else _round_up(L, LANES)
    DS = d_skip.shape[1]
    H = b_f.shape[1]
    DA = H * HEAD_DIM
    FF = conv_b.shape[1]
    G, P, C = b_re.shape[1:]

    meta_full = _gather_cols(meta, F32, name="gather_meta")
    conv_w_full = _gather_cols(conv_w[0], F32, name="gather_conv_w")
    w_in_full = _gather_cols(w_in[0], BF16, name="gather_w_in")
    gathers = [_Rider(w[0].astype(BF16), True, cols, *w.shape[1:])
               for w, cols in ((w_attn_o, True), (w_glu, True), (w_out, False), (w_up, True), (w_down, False))]
    conv_w8 = jnp.pad(conv_w_full, ((0, SUBLANES - CONV_WIDTH), (0, 0)))

    a_re, a_im, bb_re, bb_im = _ssm_discretise(lam_re[0], lam_im[0], log_dt[0], b_re[0], b_im[0])
    w_b = _slab_in(bb_re, bb_im)
    w_c = _slab_out(c_re[0], c_im[0])
    a_slab = _slab_diag(a_re, a_im)
    w_b16, w_c16 = w_b.astype(BF16), w_c.astype(BF16)
    w_bt16, w_ct16 = jnp.swapaxes(w_b16, 1, 2), jnp.swapaxes(w_c16, 1, 2)

    h0 = jnp.concatenate([meta_full, x[0], jnp.zeros((T - L, D), F32)], axis=0)
    target = jnp.pad(loss_target[0], ((N_META, T - L), (0, 0)))
    n1 = _rms_fwd(h0, g_mix, name="rms_mix")
    o_f, o_u, o_g = 3 * DA, 3 * DA + H, 3 * DA + H + DS
    w_qkv = w_in_full[:, :o_f]
    w_f = jnp.pad(w_in_full[:, o_f:o_u], ((0, 0), (0, LANES - H)))
    w_u = w_in_full[:, o_u:o_g]
    w_g = w_in_full[:, o_g:]
    w_main = jnp.concatenate([w_qkv, w_u, w_g], axis=1)
    zqkv = _matmul(n1, w_qkv, name="mm_qkv")
    zu = _matmul(n1, w_u, name="mm_u")
    zg = _matmul(n1, w_g, name="mm_gates")
    zf = _matmul(n1, w_f, name="mm_forget", out_dtype=F32)
    f_t = zf[:, :H].T
    b_col = b_f.reshape(H, 1)
    fcum = _forget_cumsum(f_t, b_col, name="forget_cumsum")
    fcol = fcum.T
    o, lse, carried = _attn_fwd(zqkv, fcum, fcol, name="attn_fwd", riders=gathers)
    w_ao_full, w_glu_full, w_out_full, w_up_full, w_down_full = carried
    attn = _matmul(o, w_ao_full, name="mm_attn_o")
    y, gy, hin = _ssm_fwd(zu, w_b16, w_c16, a_slab, d_skip, name="ssm_fwd")
    yab = _matmul(gy, w_glu_full, name="mm_glu")
    merged = _merge_fwd(yab, zg, attn, name="merge_fwd")
    h1 = _matmul(merged, w_out_full, name="mm_out", out_dtype=F32, residual=h0)
    n2 = _rms_fwd(h1, g_ffn, name="rms_ffn")
    gu = _matmul(n2, w_up_full, name="mm_up")
    act = _conv_fwd(gu, conv_w8, conv_b, name="conv_fwd")
    h2 = _matmul(act, w_down_full, name="mm_down", out_dtype=F32, residual=h1)

    dh2, sq, dg_final = _final_loss(h2, g_final.reshape(1, D), target, seq, name="final_loss")
    loss = lax.psum(0.5 * sq[0, 0] / D, ("x", "y", "c"))
    dh2_16 = dh2.astype(BF16)
    d_act = _matmul(dh2_16, w_down_full, name="mm_down_dx", trans_b=True)
    dw_down = _matmul(act, dh2_16, name="mm_down_dw", trans_a=True)
    dgc, du2, conv_sums, (x_w_down,) = _conv_bwd_gate(
        d_act, gu, conv_w8, conv_b, name="conv_bwd_gate",
        riders=[_Rider(dw_down, False, False, *w_down.shape[1:])])
    dgu = jnp.concatenate([_conv_bwd_taps(dgc, conv_w8, name="conv_bwd_taps"), du2], axis=1)
    dn2 = _matmul(dgu, w_up_full, name="mm_up_dx", trans_b=True)
    dw_up = _matmul(n2, dgu, name="mm_up_dw", trans_a=True)
    dh1, dg_ffn = _rms_bwd(dn2, h1, g_ffn, dh2, name="rms_ffn_bwd")
    dh1_16 = dh1.astype(BF16)
    dmerged = _matmul(dh1_16, w_out_full, name="mm_out_dx", trans_b=True)
    dw_out = _matmul(merged, dh1_16, name="mm_out_dw", trans_a=True)
    dzg, dattn, dyab = _merge_bwd(dmerged, yab, zg, attn, name="merge_bwd")
    do = _matmul(dattn, w_ao_full, name="mm_attn_o_dx", trans_b=True)
    dw_ao = _matmul(o, dattn, name="mm_attn_o_dw", trans_a=True)
    dgy = _matmul(dyab, w_glu_full, name="mm_glu_dx", trans_b=True)
    dw_glu = _matmul(gy, dyab, name="mm_glu_dw", trans_a=True)
    dzu, dw_b, dw_c, da_slab, dd_skip = _ssm_bwd(zu, dgy, y, hin, w_b16, w_bt16, w_ct16, a_slab, d_skip,
                                                 name="ssm_bwd")
    delta = _attn_delta(zqkv, fcum, fcol, do, lse, name="attn_delta")
    dq = _attn_bwd_dq(zqkv, fcum, fcol, do, lse, delta, name="attn_bwd_dq")
    dk, dv, dfcum, (x_w_up, x_w_out, x_w_ao, x_w_glu) = _attn_bwd_dkv(
        zqkv, fcum, fcol, do, lse, delta, name="attn_bwd_dkv",
        riders=[_Rider(dw_up, False, True, *w_up.shape[1:]), _Rider(dw_out, False, False, *w_out.shape[1:]),
                _Rider(dw_ao, False, True, *w_attn_o.shape[1:]), _Rider(dw_glu, False, True, *w_glu.shape[1:])])
    df_t, db_f = _forget_bwd(dfcum.T, f_t, b_col, name="forget_bwd")
    dzf = jnp.pad(df_t.T, ((0, 0), (0, LANES - H))).astype(BF16)
    dz_main = jnp.concatenate([dq, dk, dv, dzu, dzg], axis=1)
    dw_main = _matmul(n1, dz_main, name="mm_in_dw", trans_a=True)
    dw_f = _matmul(n1, dzf, name="mm_forget_dw", trans_a=True)
    dw_in = jnp.concatenate([dw_main[:, :o_f], dw_f[:, :H], dw_main[:, o_f:]], axis=1)
    parts_in = _column_parts(dw_in, w_in.shape[2])
    dn1 = _matmul(dzf, w_f, name="mm_forget_dx", out_dtype=F32, trans_b=True)
    dn1, (x_w_in,) = _matmul(
        dz_main, w_main, name="mm_in_dx", out_dtype=F32, residual=dn1, trans_b=True,
        riders=[_Rider(parts_in.reshape(-1, parts_in.shape[2]), False, False, *parts_in.shape[1:])])
    dh0, dg_mix = _rms_bwd(dn1, h0, g_mix, dh1, name="rms_mix_bwd")
    grad_x = dh0[N_META:L][None]

    dbb_re, dbb_im = _slab_in_grad(dw_b, G, P, C)
    dc_re, dc_im = _slab_out_grad(dw_c, G, C, P)
    da_re = da_slab[:, :SLAB_STATE].reshape(G, P)
    da_im = da_slab[:, SLAB_STATE:].reshape(G, P)
    _, disc_vjp = jax.vjp(_ssm_discretise, lam_re[0], lam_im[0], log_dt[0], b_re[0], b_im[0])
    dlam_re, dlam_im, dlog_dt, db_re, db_im = disc_vjp((da_re, da_im, dbb_re, dbb_im))

    big = {}
    big["meta"] = _update_cols(dh0[:N_META], meta, m_meta, v_meta, name="meta")
    big["conv_w"] = _update_cols(conv_sums[:CONV_WIDTH], conv_w[0], m_conv_w[0], v_conv_w[0], name="conv_w")
    big["w_down"] = _sum_adamw(x_w_down, w_down[0], m_w_down[0], v_w_down[0], name="w_down_adamw")
    big["w_up"] = _sum_adamw(x_w_up, w_up[0], m_w_up[0], v_w_up[0], name="w_up_adamw")
    big["w_out"] = _sum_adamw(x_w_out, w_out[0], m_w_out[0], v_w_out[0], name="w_out_adamw")
    big["w_attn_o"] = _sum_adamw(x_w_ao, w_attn_o[0], m_w_attn_o[0], v_w_attn_o[0], name="w_attn_o_adamw")
    big["w_glu"] = _sum_adamw(x_w_glu, w_glu[0], m_w_glu[0], v_w_glu[0], name="w_glu_adamw")
    big["w_in"] = _padded_adamw(x_w_in, w_in[0], m_w_in[0], v_w_in[0], name="w_in_adamw")

    rep_names = ["g_mix", "b_f", "lam_re", "lam_im", "log_dt", "b_re", "b_im", "c_re", "c_im", "d_skip", "g_ffn",
                 "conv_b", "g_final"]
    rep_w = [g_mix, b_f, lam_re, lam_im, log_dt, b_re, b_im, c_re, c_im, d_skip, g_ffn, conv_b, g_final]
    rep_m = [m_g_mix, m_b_f, m_lam_re, m_lam_im, m_log_dt, m_b_re, m_b_im, m_c_re, m_c_im, m_d_skip, m_g_ffn,
             m_conv_b, m_g_final]
    rep_v = [v_g_mix, v_b_f, v_lam_re, v_lam_im, v_log_dt, v_b_re, v_b_im, v_c_re, v_c_im, v_d_skip, v_g_ffn,
             v_conv_b, v_g_final]
    rep_g = [dg_mix, db_f[:, 0], dlam_re, dlam_im, dlog_dt, db_re, db_im, dc_re, dc_im, dd_skip, dg_ffn,
             conv_sums[CONV_WIDTH], dg_final]
    rep = _update_replicated(rep_g, rep_w, rep_m, rep_v, name="replicated")
    rep_out = {n: [rep[k][i] for k in range(4)] for i, n in enumerate(rep_names)}

    order = ["meta", "g_mix", "w_in", "b_f", "lam_re", "lam_im", "log_dt", "b_re", "b_im", "c_re", "c_im", "d_skip",
             "w_glu", "w_attn_o", "w_out", "g_ffn", "w_up", "conv_w", "conv_b", "w_down", "g_final"]
    outs = [loss, grad_x]
    for kind in range(4):
        for n in order:
            if n in big:
                outs.append(big[n][kind] if n == "meta" else big[n][kind][None])
            else:
                outs.append(rep_out[n][kind])
    return tuple(outs)
```

```python
import collections
import functools
import math

import jax
import jax.numpy as jnp
from jax import lax
from jax.experimental import pallas as pl
from jax.experimental.pallas import tpu as pltpu

F32 = jnp.float32
BF16 = jnp.bfloat16

N_META = 16
EPS = 1e-6
HEAD_DIM = 128
SSM_GROUP = 16
SSM_STATE = 64
GROUPS_PER_SLAB = 8
SLAB_STATE = GROUPS_PER_SLAB * SSM_STATE
CONV_WIDTH = 3
N_DEV = 8

ADAM_LR = 0.001
ADAM_B1 = 0.9
ADAM_B2 = 0.999
ADAM_EPS = 1e-08
ADAM_WD = 0.01
ADAM_STEP = 10

LANES = 128
SUBLANES = 8
VMEM_LIMIT = 52 * 1024 * 1024

SEQ_BLOCK = 768
ATT_BLOCK = 384
ROW_BLOCK = 256
SSM_CHUNK = 128
ADAMW_BLOCK_ELEMS = 1 << 17
WGRAD_BLOCK_M = 1408
MASK_VALUE = -1e30
LOG2_E = math.log2(math.e)


def _round_up(n, m):
    return (n + m - 1) // m * m


def _divisor(n, target, mult):
    if n <= target:
        return n
    best = None
    for d in range(mult, target + 1, mult):
        if n % d == 0:
            best = d
    assert best is not None, (n, target, mult)
    return best


def _params(sem):
    return pltpu.CompilerParams(dimension_semantics=sem, vmem_limit_bytes=VMEM_LIMIT)


def _matmul(a, b, *, name, trans_a=False, trans_b=False, out_dtype=None, residual=None, riders=(),
            tm=768, tn=1024, tk=2816):
    out_dtype = BF16 if out_dtype is None else out_dtype
    assert not (trans_a and trans_b)
    if trans_a:
        K, M = a.shape
    else:
        M, K = a.shape
    if trans_b:
        N, K2 = b.shape
    else:
        K2, N = b.shape
    assert K == K2, (a.shape, b.shape)
    if trans_a:
        tm = max(tm, WGRAD_BLOCK_M)
    tm = _divisor(M, tm, LANES if trans_a else SUBLANES)
    tn = _divisor(N, tn, LANES)
    tk = _divisor(K, tk, LANES if not trans_a else SUBLANES)
    nk = K // tk

    def body(*refs):
        if residual is None:
            a_ref, b_ref, o_ref, acc_ref = refs
        else:
            a_ref, b_ref, r_ref, o_ref, acc_ref = refs
        k = pl.program_id(2)

        @pl.when(k == 0)
        def _():
            acc_ref[...] = jnp.zeros_like(acc_ref)

        contract = (0, 0) if trans_a else (1, 1) if trans_b else (1, 0)
        acc_ref[...] += lax.dot_general(a_ref[...], b_ref[...], (((contract[0],), (contract[1],)), ((), ())),
                                        preferred_element_type=F32)

        @pl.when(k == nk - 1)
        def _():
            r = acc_ref[...]
            if residual is not None:
                r = r + r_ref[...]
            o_ref[...] = r.astype(o_ref.dtype)

    if trans_a:
        a_spec = pl.BlockSpec((tk, tm), lambda i, j, k: (k, i))
    else:
        a_spec = pl.BlockSpec((tm, tk), lambda i, j, k: (i, k))
    if trans_b:
        b_spec = pl.BlockSpec((tn, tk), lambda i, j, k: (j, k))
    else:
        b_spec = pl.BlockSpec((tk, tn), lambda i, j, k: (k, j))
    in_specs = [a_spec, b_spec]
    args = [a, b]
    if residual is not None:
        in_specs.append(pl.BlockSpec((tm, tn), lambda i, j, k: (i, j)))
        args.append(residual)
    (out,), carried = _gridded_call(
        name, (M // tm, N // tn, nk), ("parallel", "parallel", "arbitrary"), body, in_specs,
        [pl.BlockSpec((tm, tn), lambda i, j, k: (i, j))], [jax.ShapeDtypeStruct((M, N), out_dtype)],
        [pltpu.VMEM((tm, tn), F32)], args, riders)
    return (out, carried) if riders else out


def _rms_fwd(h, g, *, name):
    T, D = h.shape
    tr = _divisor(T, ROW_BLOCK, SUBLANES)

    def body(h_ref, g_ref, o_ref):
        x = h_ref[...]
        r = lax.rsqrt(jnp.mean(x * x, axis=-1, keepdims=True) + EPS)
        o_ref[...] = (x * r * g_ref[...]).astype(o_ref.dtype)

    return pl.pallas_call(
        body, name=name, grid=(T // tr,),
        in_specs=[pl.BlockSpec((tr, D), lambda i: (i, 0)), pl.BlockSpec((1, D), lambda i: (0, 0))],
        out_specs=pl.BlockSpec((tr, D), lambda i: (i, 0)),
        out_shape=jax.ShapeDtypeStruct((T, D), BF16),
        compiler_params=_params(("parallel",)),
    )(h, g)


def _rms_bwd(dn, h, g, dres, *, name):
    T, D = h.shape
    tr = _divisor(T, ROW_BLOCK, SUBLANES)

    def body(dn_ref, h_ref, g_ref, dres_ref, dh_ref, dg_ref):
        i = pl.program_id(0)

        @pl.when(i == 0)
        def _():
            dg_ref[...] = jnp.zeros_like(dg_ref)

        x = h_ref[...]
        dn_v = dn_ref[...].astype(F32)
        r = lax.rsqrt(jnp.mean(x * x, axis=-1, keepdims=True) + EPS)
        xh = x * r
        dg_ref[...] += jnp.sum(dn_v * xh, axis=0, keepdims=True)
        dxh = dn_v * g_ref[...]
        dh_ref[...] = dres_ref[...] + r * (dxh - xh * jnp.mean(dxh * xh, axis=-1, keepdims=True))

    return pl.pallas_call(
        body, name=name, grid=(T // tr,),
        in_specs=[pl.BlockSpec((tr, D), lambda i: (i, 0)), pl.BlockSpec((tr, D), lambda i: (i, 0)),
                  pl.BlockSpec((1, D), lambda i: (0, 0)), pl.BlockSpec((tr, D), lambda i: (i, 0))],
        out_specs=[pl.BlockSpec((tr, D), lambda i: (i, 0)), pl.BlockSpec((1, D), lambda i: (0, 0))],
        out_shape=[jax.ShapeDtypeStruct((T, D), F32), jax.ShapeDtypeStruct((1, D), F32)],
        compiler_params=_params(("arbitrary",)),
    )(dn, h, g, dres)


def _final_loss(h, g, target, n_valid, *, name):
    T, D = h.shape
    tr = _divisor(T, ROW_BLOCK, SUBLANES)

    def body(h_ref, g_ref, t_ref, dh_ref, sq_ref, dg_ref):
        i = pl.program_id(0)

        @pl.when(i == 0)
        def _():
            sq_ref[...] = jnp.zeros_like(sq_ref)
            dg_ref[...] = jnp.zeros_like(dg_ref)

        x = h_ref[...]
        r = lax.rsqrt(jnp.mean(x * x, axis=-1, keepdims=True) + EPS)
        xh = x * r
        gv = g_ref[...]
        row = i * tr + lax.broadcasted_iota(jnp.int32, (tr, 1), 0)
        valid = (row >= N_META) & (row < N_META + n_valid)
        err = jnp.where(valid, xh * gv - t_ref[...], 0.0)
        sq_ref[...] += jnp.sum(err * err)
        dy = err * (1.0 / D)
        dg_ref[...] += jnp.sum(dy * xh, axis=0, keepdims=True)
        dxh = dy * gv
        dh_ref[...] = r * (dxh - xh * jnp.mean(dxh * xh, axis=-1, keepdims=True))

    return pl.pallas_call(
        body, name=name, grid=(T // tr,),
        in_specs=[pl.BlockSpec((tr, D), lambda i: (i, 0)), pl.BlockSpec((1, D), lambda i: (0, 0)),
                  pl.BlockSpec((tr, D), lambda i: (i, 0))],
        out_specs=[pl.BlockSpec((tr, D), lambda i: (i, 0)), pl.BlockSpec((SUBLANES, LANES), lambda i: (0, 0)),
                   pl.BlockSpec((1, D), lambda i: (0, 0))],
        out_shape=[jax.ShapeDtypeStruct((T, D), F32), jax.ShapeDtypeStruct((SUBLANES, LANES), F32),
                   jax.ShapeDtypeStruct((1, D), F32)],
        compiler_params=_params(("arbitrary",)),
    )(h, g, target)


def _prefix_sum_lanes(x):
    lane = lax.broadcasted_iota(jnp.int32, x.shape, 1)
    d = 1
    while d < LANES:
        x = x + jnp.where(lane >= d, pltpu.roll(x, d, axis=1), 0.0)
        d *= 2
    return x


def _forget_cumsum(ft, bf, *, name):
    H, T = ft.shape
    nb = T // LANES

    def body(f_ref, b_ref, o_ref):
        carry = jnp.zeros((H, 1), F32)
        for j in range(nb):
            sl = pl.ds(j * LANES, LANES)
            lf = jax.nn.log_sigmoid(f_ref[:, sl] + b_ref[...])
            c = _prefix_sum_lanes(lf) + carry
            o_ref[:, sl] = c
            carry = c[:, LANES - 1:LANES]

    return pl.pallas_call(
        body, name=name,
        in_specs=[pl.BlockSpec(memory_space=pltpu.VMEM), pl.BlockSpec(memory_space=pltpu.VMEM)],
        out_specs=pl.BlockSpec(memory_space=pltpu.VMEM),
        out_shape=jax.ShapeDtypeStruct((H, T), F32),
        compiler_params=pltpu.CompilerParams(vmem_limit_bytes=VMEM_LIMIT),
    )(ft, bf)


def _forget_bwd(dF_key, dF_query, ft, bf, *, name):
    H, T = ft.shape
    nb = T // LANES

    def body(d_ref, dq_ref, f_ref, b_ref, o_ref, s_ref):
        carry = jnp.zeros((H, 1), F32)
        acc = jnp.zeros((H, LANES), F32)
        for j in reversed(range(nb)):
            sl = pl.ds(j * LANES, LANES)
            d = d_ref[:, sl] + dq_ref[:, sl]
            pre = _prefix_sum_lanes(d)
            tot = pre[:, LANES - 1:LANES]
            dlf = tot - pre + d + carry
            carry = carry + tot
            z = f_ref[:, sl] + b_ref[...]
            df = dlf * jax.nn.sigmoid(-z)
            o_ref[:, sl] = df
            acc = acc + df
        s_ref[...] = jnp.broadcast_to(jnp.sum(acc, axis=1, keepdims=True), (H, LANES))

    return pl.pallas_call(
        body, name=name,
        in_specs=[pl.BlockSpec(memory_space=pltpu.VMEM)] * 4,
        out_specs=[pl.BlockSpec(memory_space=pltpu.VMEM)] * 2,
        out_shape=[jax.ShapeDtypeStruct((H, T), F32), jax.ShapeDtypeStruct((H, LANES), F32)],
        compiler_params=pltpu.CompilerParams(vmem_limit_bytes=VMEM_LIMIT),
    )(dF_key, dF_query, ft, bf)


def _scores(q_ref, k_ref, fq_ref, fk_ref, h, blk, scale, diagonal):
    hs = pl.ds(h * HEAD_DIM, HEAD_DIM)
    s = lax.dot_general(k_ref[:, hs], q_ref[:, hs], (((1,), (1,)), ((), ())), preferred_element_type=F32)
    s = s * (scale * LOG2_E) + (fq_ref[h:h + 1, :] - fk_ref[:, h:h + 1])
    if diagonal:
        key = lax.broadcasted_iota(jnp.int32, (blk, blk), 0)
        query = lax.broadcasted_iota(jnp.int32, (blk, blk), 1)
        s = jnp.where(key <= query, s, MASK_VALUE)
    return s


def _causal_blocks(i, j, compute):
    @pl.when(j < i)
    def _():
        compute(False)

    @pl.when(j == i)
    def _():
        compute(True)


def _attn_fwd(zqkv, fcum, fcol, *, name, riders=()):
    T = zqkv.shape[0]
    DA = zqkv.shape[1] // 3
    H = DA // HEAD_DIM
    blk = _divisor(T, ATT_BLOCK, LANES)
    nb = T // blk
    scale = HEAD_DIM ** -0.5

    def body(q_ref, k_ref, v_ref, fq_ref, fk_ref, o_ref, lse_ref, m_ref, l_ref, acc_ref):
        i = pl.program_id(0)
        j = pl.program_id(1)

        @pl.when(j == 0)
        def _():
            m_ref[...] = jnp.full_like(m_ref, MASK_VALUE)
            l_ref[...] = jnp.zeros_like(l_ref)
            acc_ref[...] = jnp.zeros_like(acc_ref)

        def compute(diagonal):
            for h in range(H):
                hs = pl.ds(h * HEAD_DIM, HEAD_DIM)
                s = _scores(q_ref, k_ref, fq_ref, fk_ref, h, blk, scale, diagonal)
                m_prev = m_ref[h:h + 1, :]
                m_new = jnp.maximum(m_prev, jnp.max(s, axis=0, keepdims=True))
                alpha = jnp.exp2(m_prev - m_new)
                p = jnp.exp2(s - m_new)
                l_ref[h:h + 1, :] = alpha * l_ref[h:h + 1, :] + jnp.sum(p, axis=0, keepdims=True)
                acc_ref[hs, :] = alpha * acc_ref[hs, :] + lax.dot_general(
                    v_ref[:, hs], p.astype(BF16), (((0,), (0,)), ((), ())), preferred_element_type=F32)
                m_ref[h:h + 1, :] = m_new

        _causal_blocks(i, j, compute)

        @pl.when(j == nb - 1)
        def _():
            for h in range(H):
                hs = pl.ds(h * HEAD_DIM, HEAD_DIM)
                l = l_ref[h:h + 1, :]
                o_ref[:, hs] = (acc_ref[hs, :] / l).T.astype(o_ref.dtype)
                lse_ref[h:h + 1, :] = m_ref[h:h + 1, :] + jnp.log2(l)

    kv = lambda c: (lambda i, j: (jnp.minimum(j, i), c))
    (o, lse), carried = _gridded_call(
        name, (nb, nb), ("parallel", "arbitrary"), body,
        [pl.BlockSpec((blk, DA), lambda i, j: (i, 0)),
         pl.BlockSpec((blk, DA), kv(1)), pl.BlockSpec((blk, DA), kv(2)),
         pl.BlockSpec((H, blk), lambda i, j: (0, i)),
         pl.BlockSpec((blk, H), lambda i, j: (jnp.minimum(j, i), 0))],
        [pl.BlockSpec((blk, DA), lambda i, j: (i, 0)), pl.BlockSpec((H, blk), lambda i, j: (0, i))],
        [jax.ShapeDtypeStruct((T, DA), BF16), jax.ShapeDtypeStruct((H, T), F32)],
        [pltpu.VMEM((H, blk), F32), pltpu.VMEM((H, blk), F32), pltpu.VMEM((DA, blk), F32)],
        [zqkv, zqkv, zqkv, fcum, fcol], riders)
    return o, lse, carried


def _attn_bwd_dq(zqkv, fcum, fcol, do, o, lse, *, name):
    T = zqkv.shape[0]
    DA = zqkv.shape[1] // 3
    H = DA // HEAD_DIM
    blk = _divisor(T, ATT_BLOCK, LANES)
    nb = T // blk
    scale = HEAD_DIM ** -0.5

    def body(q_ref, k_ref, v_ref, fq_ref, fk_ref, do_ref, o_ref, lse_ref, dq_ref, dl_ref, dfq_ref, acc_ref):
        i = pl.program_id(0)
        j = pl.program_id(1)

        @pl.when(j == 0)
        def _():
            acc_ref[...] = jnp.zeros_like(acc_ref)
            dfq_ref[...] = jnp.zeros_like(dfq_ref)
            for h in range(H):
                hs = pl.ds(h * HEAD_DIM, HEAD_DIM)
                prod = do_ref[:, hs].astype(F32) * o_ref[:, hs].astype(F32)
                dl_ref[h:h + 1, :] = jnp.sum(prod.T, axis=0, keepdims=True)

        def compute(diagonal):
            for h in range(H):
                hs = pl.ds(h * HEAD_DIM, HEAD_DIM)
                s = _scores(q_ref, k_ref, fq_ref, fk_ref, h, blk, scale, diagonal)
                p = jnp.exp2(s - lse_ref[h:h + 1, :])
                dp = lax.dot_general(v_ref[:, hs], do_ref[:, hs], (((1,), (1,)), ((), ())),
                                     preferred_element_type=F32)
                ds = p * (dp - dl_ref[h:h + 1, :])
                dfq_ref[h:h + 1, :] += jnp.sum(ds, axis=0, keepdims=True)
                acc_ref[hs, :] += scale * lax.dot_general(k_ref[:, hs], ds.astype(BF16), (((0,), (0,)), ((), ())),
                                                          preferred_element_type=F32)

        _causal_blocks(i, j, compute)

        @pl.when(j == nb - 1)
        def _():
            for h in range(H):
                hs = pl.ds(h * HEAD_DIM, HEAD_DIM)
                dq_ref[:, hs] = acc_ref[hs, :].T.astype(dq_ref.dtype)

    kv = lambda c: (lambda i, j: (jnp.minimum(j, i), c))
    row = lambda i, j: (i, 0)
    lane = lambda i, j: (0, i)
    return pl.pallas_call(
        body, name=name, grid=(nb, nb),
        in_specs=[pl.BlockSpec((blk, DA), row), pl.BlockSpec((blk, DA), kv(1)), pl.BlockSpec((blk, DA), kv(2)),
                  pl.BlockSpec((H, blk), lane), pl.BlockSpec((blk, H), lambda i, j: (jnp.minimum(j, i), 0)),
                  pl.BlockSpec((blk, DA), row), pl.BlockSpec((blk, DA), row), pl.BlockSpec((H, blk), lane)],
        out_specs=[pl.BlockSpec((blk, DA), row), pl.BlockSpec((H, blk), lane), pl.BlockSpec((H, blk), lane)],
        out_shape=[jax.ShapeDtypeStruct((T, DA), BF16), jax.ShapeDtypeStruct((H, T), F32),
                   jax.ShapeDtypeStruct((H, T), F32)],
        scratch_shapes=[pltpu.VMEM((DA, blk), F32)],
        compiler_params=_params(("parallel", "arbitrary")),
    )(zqkv, zqkv, zqkv, fcum, fcol, do, o, lse)


def _attn_bwd_dkv(zqkv, fcum, fcol, do, lse, delta, *, name, riders=()):
    T = zqkv.shape[0]
    DA = zqkv.shape[1] // 3
    H = DA // HEAD_DIM
    blk = _divisor(T, ATT_BLOCK, LANES)
    nb = T // blk
    scale = HEAD_DIM ** -0.5

    def body(q_ref, k_ref, v_ref, fq_ref, fk_ref, do_ref, lse_ref, dl_ref, dk_ref, dv_ref, df_ref,
             dk_acc, dv_acc, df_acc):
        j = pl.program_id(0)
        i = pl.program_id(1)

        @pl.when(i == 0)
        def _():
            dk_acc[...] = jnp.zeros_like(dk_acc)
            dv_acc[...] = jnp.zeros_like(dv_acc)
            df_acc[...] = jnp.zeros_like(df_acc)

        def compute(diagonal):
            for h in range(H):
                hs = pl.ds(h * HEAD_DIM, HEAD_DIM)
                s = _scores(q_ref, k_ref, fq_ref, fk_ref, h, blk, scale, diagonal)
                p = jnp.exp2(s - lse_ref[h:h + 1, :])
                dov = do_ref[:, hs]
                dv_acc[:, hs] += jnp.dot(p.astype(BF16), dov, preferred_element_type=F32)
                dp = lax.dot_general(v_ref[:, hs], dov, (((1,), (1,)), ((), ())), preferred_element_type=F32)
                ds = p * (dp - dl_ref[h:h + 1, :])
                dk_acc[:, hs] += scale * jnp.dot(ds.astype(BF16), q_ref[:, hs], preferred_element_type=F32)
                df_acc[:, h:h + 1] -= jnp.sum(ds, axis=1, keepdims=True)

        _causal_blocks(i, j, compute)

        @pl.when(i == nb - 1)
        def _():
            dk_ref[...] = dk_acc[...].astype(dk_ref.dtype)
            dv_ref[...] = dv_acc[...].astype(dv_ref.dtype)
            df_ref[...] = df_acc[...]

    qrow = lambda j, i: (jnp.maximum(i, j), 0)
    qlane = lambda j, i: (0, jnp.maximum(i, j))
    kcol = lambda c: (lambda j, i: (j, c))
    (dk, dv, df), carried = _gridded_call(
        name, (nb, nb), ("parallel", "arbitrary"), body,
        [pl.BlockSpec((blk, DA), qrow), pl.BlockSpec((blk, DA), kcol(1)), pl.BlockSpec((blk, DA), kcol(2)),
         pl.BlockSpec((H, blk), qlane), pl.BlockSpec((blk, H), kcol(0)),
         pl.BlockSpec((blk, DA), qrow), pl.BlockSpec((H, blk), qlane), pl.BlockSpec((H, blk), qlane)],
        [pl.BlockSpec((blk, DA), kcol(0)), pl.BlockSpec((blk, DA), kcol(0)), pl.BlockSpec((blk, H), kcol(0))],
        [jax.ShapeDtypeStruct((T, DA), BF16), jax.ShapeDtypeStruct((T, DA), BF16),
         jax.ShapeDtypeStruct((T, H), F32)],
        [pltpu.VMEM((blk, DA), F32), pltpu.VMEM((blk, DA), F32), pltpu.VMEM((blk, H), F32)],
        [zqkv, zqkv, zqkv, fcum, fcol, do, lse, delta], riders)
    return dk, dv, df, carried


def _gelu(y):
    c = math.sqrt(2.0 / math.pi)
    return 0.5 * y * (1.0 + jnp.tanh(c * (y + 0.044715 * (y * y * y))))


def _gelu_grad(y):
    c = math.sqrt(2.0 / math.pi)
    th = jnp.tanh(c * (y + 0.044715 * (y * y * y)))
    return 0.5 * (1.0 + th) + 0.5 * y * (1.0 - th * th) * c * (1.0 + 3.0 * 0.044715 * y * y)


STATE_BLOCKS = SLAB_STATE // LANES


def _lane_blocks(ref, lead=()):
    return [ref[lead + (slice(None), pl.ds(b * LANES, LANES))] for b in range(2 * STATE_BLOCKS)]


def _put_lane_blocks(ref, blocks):
    for b, v in enumerate(blocks):
        ref[:, pl.ds(b * LANES, LANES)] = v


def _put_slab(x_ref, first, q, n_slab, chunk, value):
    for b in range(2 * STATE_BLOCKS):
        x_ref[b, pl.ds(first * n_slab + q, chunk, stride=n_slab), :] = value[:, b * LANES:(b + 1) * LANES]


def _get_slab(x_ref, first, q, n_slab, chunk):
    return jnp.concatenate([x_ref[b, pl.ds(first * n_slab + q, chunk, stride=n_slab), :]
                            for b in range(2 * STATE_BLOCKS)], axis=1)


def _ssm_scan_fwd(x_ref, a, h, chunk, n_slab, first=0):
    nb = STATE_BLOCKS

    def step(t, h):
        rows = pl.ds(pl.multiple_of((t + first) * n_slab, n_slab), n_slab)
        out = [None] * (2 * nb)
        for b in range(nb):
            n_re = a[b] * h[b] - a[nb + b] * h[nb + b] + x_ref[b, rows, :]
            n_im = a[b] * h[nb + b] + a[nb + b] * h[b] + x_ref[nb + b, rows, :]
            x_ref[b, rows, :] = n_re
            x_ref[nb + b, rows, :] = n_im
            out[b], out[nb + b] = n_re, n_im
        return tuple(out)

    return lax.fori_loop(0, chunk, step, tuple(h), unroll=4)


def _ssm_fwd(zu, w_b, w_c, a, d_skip, *, name):
    T, DS = zu.shape
    n_slab = DS // LANES
    chunk = _divisor(T, SSM_CHUNK, SUBLANES)
    n_chunk = T // chunk

    def body(u_ref, wb_ref, wc_ref, a_ref, ds_ref, y_ref, gy_ref, hin_ref, x_ref, h_ref):
        k = pl.program_id(0)

        @pl.when(k == 0)
        def _():
            h_ref[...] = jnp.zeros_like(h_ref)

        hin_ref[0] = h_ref[...]
        for q in range(n_slab):
            qs = pl.ds(q * LANES, LANES)
            _put_slab(x_ref, 0, q, n_slab, chunk, jnp.dot(u_ref[:, qs], wb_ref[q], preferred_element_type=F32))
        h = _ssm_scan_fwd(x_ref, _lane_blocks(a_ref), _lane_blocks(h_ref), chunk, n_slab)
        _put_lane_blocks(h_ref, h)
        for q in range(n_slab):
            qs = pl.ds(q * LANES, LANES)
            hq = _get_slab(x_ref, 0, q, n_slab, chunk).astype(BF16)
            y = jnp.dot(hq, wc_ref[q], preferred_element_type=F32) + ds_ref[:, qs] * u_ref[:, qs].astype(F32)
            y_ref[:, qs] = y
            gy_ref[:, qs] = _gelu(y).astype(gy_ref.dtype)

    whole = lambda shape: pl.BlockSpec(shape, lambda k: (0,) * len(shape))
    return pl.pallas_call(
        body, name=name, grid=(n_chunk,),
        in_specs=[pl.BlockSpec((chunk, DS), lambda k: (k, 0)), whole(w_b.shape), whole(w_c.shape),
                  whole(a.shape), whole(d_skip.shape)],
        out_specs=[pl.BlockSpec((chunk, DS), lambda k: (k, 0)), pl.BlockSpec((chunk, DS), lambda k: (k, 0)),
                   pl.BlockSpec((1, n_slab, 2 * SLAB_STATE), lambda k: (k, 0, 0))],
        out_shape=[jax.ShapeDtypeStruct((T, DS), F32), jax.ShapeDtypeStruct((T, DS), BF16),
                   jax.ShapeDtypeStruct((n_chunk, n_slab, 2 * SLAB_STATE), F32)],
        scratch_shapes=[pltpu.VMEM((2 * STATE_BLOCKS, chunk * n_slab, LANES), F32),
                        pltpu.VMEM((n_slab, 2 * SLAB_STATE), F32)],
        compiler_params=_params(("arbitrary",)),
    )(zu, w_b, w_c, a, d_skip)


def _ssm_bwd(zu, dgy, y, hin, w_b, w_bt, w_ct, a, d_skip, *, name):
    T, DS = zu.shape
    n_slab = DS // LANES
    chunk = _divisor(T, SSM_CHUNK, SUBLANES)
    n_chunk = T // chunk
    S = SLAB_STATE

    def body(u_ref, dgy_ref, y_ref, hin_ref, wb_ref, wbt_ref, wct_ref, a_ref, ds_ref,
             du_ref, dwb_ref, dwc_ref, da_ref, dds_ref, hb_ref, gb_ref, dy_ref, g_ref):
        k = pl.program_id(0)

        @pl.when(k == 0)
        def _():
            g_ref[...] = jnp.zeros_like(g_ref)
            dwb_ref[...] = jnp.zeros_like(dwb_ref)
            dwc_ref[...] = jnp.zeros_like(dwc_ref)
            da_ref[...] = jnp.zeros_like(da_ref)
            dds_ref[...] = jnp.zeros_like(dds_ref)

        nb = STATE_BLOCKS
        a = _lane_blocks(a_ref)
        hin = _lane_blocks(hin_ref, lead=(0,))

        for b in range(2 * nb):
            hb_ref[b, pl.ds(0, n_slab), :] = hin[b]
        dy_ref[...] = dgy_ref[...].astype(F32) * _gelu_grad(y_ref[...])
        for q in range(n_slab):
            qs = pl.ds(q * LANES, LANES)
            _put_slab(hb_ref, 1, q, n_slab, chunk, jnp.dot(u_ref[:, qs], wb_ref[q], preferred_element_type=F32))
            _put_slab(gb_ref, 0, q, n_slab, chunk,
                      jnp.dot(dy_ref[:, qs].astype(BF16), wct_ref[q], preferred_element_type=F32))
        _ssm_scan_fwd(hb_ref, a, hin, chunk, n_slab, first=1)

        def step(s, carry):
            g, da = carry[:2 * nb], carry[2 * nb:]
            t = chunk - 1 - s
            rows = pl.ds(pl.multiple_of(t * n_slab, n_slab), n_slab)
            g_out, da_out = [None] * (2 * nb), [None] * (2 * nb)
            for b in range(nb):
                n_re = gb_ref[b, rows, :] + a[b] * g[b] + a[nb + b] * g[nb + b]
                n_im = gb_ref[nb + b, rows, :] + a[b] * g[nb + b] - a[nb + b] * g[b]
                gb_ref[b, rows, :] = n_re
                gb_ref[nb + b, rows, :] = n_im
                p_re = hb_ref[b, rows, :]
                p_im = hb_ref[nb + b, rows, :]
                g_out[b], g_out[nb + b] = n_re, n_im
                da_out[b] = da[b] + n_re * p_re + n_im * p_im
                da_out[nb + b] = da[nb + b] + n_im * p_re - n_re * p_im
            return tuple(g_out) + tuple(da_out)

        zero = jnp.zeros((n_slab, LANES), F32)
        carry = lax.fori_loop(0, chunk, step, tuple(_lane_blocks(g_ref)) + (zero,) * (2 * nb), unroll=4)
        _put_lane_blocks(g_ref, carry[:2 * nb])
        for b in range(2 * nb):
            da_ref[:, pl.ds(b * LANES, LANES)] += carry[2 * nb + b]

        for q in range(n_slab):
            qs = pl.ds(q * LANES, LANES)
            uq = u_ref[:, qs]
            dy = dy_ref[:, qs]
            hq = _get_slab(hb_ref, 1, q, n_slab, chunk).astype(BF16)
            gq = _get_slab(gb_ref, 0, q, n_slab, chunk).astype(BF16)
            dwc_ref[q] += lax.dot_general(hq, dy.astype(BF16), (((0,), (0,)), ((), ())), preferred_element_type=F32)
            dwb_ref[q] += lax.dot_general(uq, gq, (((0,), (0,)), ((), ())), preferred_element_type=F32)
            du_ref[:, qs] = (jnp.dot(gq, wbt_ref[q], preferred_element_type=F32) + ds_ref[:, qs] * dy).astype(du_ref.dtype)
            dds_ref[:, qs] += jnp.sum(dy * uq.astype(F32), axis=0, keepdims=True)

    whole = lambda shape: pl.BlockSpec(shape, lambda k: (0,) * len(shape))
    rev = lambda k: (n_chunk - 1 - k, 0)
    return pl.pallas_call(
        body, name=name, grid=(n_chunk,),
        in_specs=[pl.BlockSpec((chunk, DS), rev), pl.BlockSpec((chunk, DS), rev), pl.BlockSpec((chunk, DS), rev),
                  pl.BlockSpec((1, n_slab, 2 * S), lambda k: (n_chunk - 1 - k, 0, 0)),
                  whole(w_b.shape), whole(w_bt.shape), whole(w_ct.shape), whole(a.shape), whole(d_skip.shape)],
        out_specs=[pl.BlockSpec((chunk, DS), rev), whole(w_b.shape), whole(w_bt.shape), whole(a.shape),
                   whole(d_skip.shape)],
        out_shape=[jax.ShapeDtypeStruct((T, DS), BF16), jax.ShapeDtypeStruct(w_b.shape, F32),
                   jax.ShapeDtypeStruct(w_bt.shape, F32), jax.ShapeDtypeStruct(a.shape, F32),
                   jax.ShapeDtypeStruct(d_skip.shape, F32)],
        scratch_shapes=[pltpu.VMEM((2 * STATE_BLOCKS, (chunk + 1) * n_slab, LANES), F32),
                        pltpu.VMEM((2 * STATE_BLOCKS, chunk * n_slab, LANES), F32),
                        pltpu.VMEM((chunk, DS), F32), pltpu.VMEM((n_slab, 2 * S), F32)],
        compiler_params=_params(("arbitrary",)),
    )(zu, dgy, y, hin, w_b, w_bt, w_ct, a, d_skip)


def _ssm_discretise(lam_re, lam_im, log_dt, b_re, b_im):
    dt = jnp.exp(log_dt)[:, None]
    mag = jnp.exp(lam_re * dt)
    a_re = mag * jnp.cos(lam_im * dt)
    a_im = mag * jnp.sin(lam_im * dt)
    den = lam_re * lam_re + lam_im * lam_im
    nr = a_re - 1.0
    z_re = (nr * lam_re + a_im * lam_im) / den
    z_im = (a_im * lam_re - nr * lam_im) / den
    bb_re = z_re[..., None] * b_re - z_im[..., None] * b_im
    bb_im = z_re[..., None] * b_im + z_im[..., None] * b_re
    return a_re, a_im, bb_re, bb_im


def _slab_in(m_re, m_im):
    G, P, C = m_re.shape
    n_slab = G // GROUPS_PER_SLAB
    eye = jnp.eye(GROUPS_PER_SLAB, dtype=m_re.dtype)

    def one(m):
        m = m.reshape(n_slab, GROUPS_PER_SLAB, P, C)
        w = jnp.einsum('sgpc,gh->sgchp', m, eye)
        return w.reshape(n_slab, GROUPS_PER_SLAB * C, GROUPS_PER_SLAB * P)

    return jnp.concatenate([one(m_re), one(m_im)], axis=2)


def _slab_in_grad(dw, G, P, C):
    n_slab = G // GROUPS_PER_SLAB
    eye = jnp.eye(GROUPS_PER_SLAB, dtype=dw.dtype)

    def one(w):
        w = w.reshape(n_slab, GROUPS_PER_SLAB, C, GROUPS_PER_SLAB, P)
        return jnp.einsum('sgchp,gh->sgpc', w, eye).reshape(G, P, C)

    return one(dw[:, :, :SLAB_STATE]), one(dw[:, :, SLAB_STATE:])


def _slab_out(c_re, c_im):
    G, C, P = c_re.shape
    n_slab = G // GROUPS_PER_SLAB
    eye = jnp.eye(GROUPS_PER_SLAB, dtype=c_re.dtype)

    def one(m):
        m = m.reshape(n_slab, GROUPS_PER_SLAB, C, P)
        w = jnp.einsum('sgcp,gh->shpgc', m, eye)
        return w.reshape(n_slab, GROUPS_PER_SLAB * P, GROUPS_PER_SLAB * C)

    return jnp.concatenate([one(c_re), one(-c_im)], axis=1)


def _slab_out_grad(dw, G, C, P):
    n_slab = G // GROUPS_PER_SLAB
    eye = jnp.eye(GROUPS_PER_SLAB, dtype=dw.dtype)

    def one(w):
        w = w.reshape(n_slab, GROUPS_PER_SLAB, P, GROUPS_PER_SLAB, C)
        return jnp.einsum('shpgc,gh->sgcp', w, eye).reshape(G, C, P)

    return one(dw[:, :SLAB_STATE, :]), -one(dw[:, SLAB_STATE:, :])


def _slab_diag(a_re, a_im):
    G, P = a_re.shape
    n_slab = G // GROUPS_PER_SLAB
    return jnp.concatenate([a_re.reshape(n_slab, SLAB_STATE), a_im.reshape(n_slab, SLAB_STATE)], axis=1)


def _merge_fwd(yab, zg, attn, *, name):
    T, D = attn.shape
    tr = _divisor(T, ROW_BLOCK, SUBLANES)

    def body(ya_ref, yb_ref, ga_ref, gb_ref, at_ref, o_ref):
        f = lambda r: r[...].astype(F32)
        ssm = f(ya_ref) * jax.nn.sigmoid(f(yb_ref))
        o_ref[...] = (jax.nn.sigmoid(f(ga_ref)) * ssm + jax.nn.sigmoid(f(gb_ref)) * f(at_ref)).astype(o_ref.dtype)

    lo = pl.BlockSpec((tr, D), lambda i: (i, 0))
    hi = pl.BlockSpec((tr, D), lambda i: (i, 1))
    return pl.pallas_call(
        body, name=name, grid=(T // tr,),
        in_specs=[lo, hi, lo, hi, lo],
        out_specs=lo,
        out_shape=jax.ShapeDtypeStruct((T, D), BF16),
        compiler_params=_params(("parallel",)),
    )(yab, yab, zg, zg, attn)


def _merge_bwd(dm, yab, zg, attn, *, name):
    T, D = attn.shape
    tr = _divisor(T, ROW_BLOCK, SUBLANES)

    def body(dm_ref, ya_ref, yb_ref, ga_ref, gb_ref, at_ref, dg_ref, dat_ref, dy_ref):
        f = lambda r: r[...].astype(F32)
        dmv, ya, at = f(dm_ref), f(ya_ref), f(at_ref)
        sa, sb, syb = jax.nn.sigmoid(f(ga_ref)), jax.nn.sigmoid(f(gb_ref)), jax.nn.sigmoid(f(yb_ref))
        ssm = ya * syb
        dssm = dmv * sa
        dg_ref[:, pl.ds(0, D)] = (dmv * ssm * sa * (1.0 - sa)).astype(dg_ref.dtype)
        dg_ref[:, pl.ds(D, D)] = (dmv * at * sb * (1.0 - sb)).astype(dg_ref.dtype)
        dat_ref[...] = (dmv * sb).astype(dat_ref.dtype)
        dy_ref[:, pl.ds(0, D)] = (dssm * syb).astype(dy_ref.dtype)
        dy_ref[:, pl.ds(D, D)] = (dssm * ya * syb * (1.0 - syb)).astype(dy_ref.dtype)

    lo = pl.BlockSpec((tr, D), lambda i: (i, 0))
    hi = pl.BlockSpec((tr, D), lambda i: (i, 1))
    both = pl.BlockSpec((tr, 2 * D), lambda i: (i, 0))
    return pl.pallas_call(
        body, name=name, grid=(T // tr,),
        in_specs=[lo, lo, hi, lo, hi, lo],
        out_specs=[both, lo, both],
        out_shape=[jax.ShapeDtypeStruct((T, 2 * D), BF16), jax.ShapeDtypeStruct((T, D), BF16),
                   jax.ShapeDtypeStruct((T, 2 * D), BF16)],
        compiler_params=_params(("parallel",)),
    )(dm, yab, yab, zg, zg, attn)


def _conv_taps(g_ref, halo_ref, i, tr):
    g0 = g_ref[...].astype(F32)
    halo = jnp.where(i > 0, halo_ref[...].astype(F32), 0.0)
    row = lax.broadcasted_iota(jnp.int32, g0.shape, 0)
    g1 = jnp.where(row == 0, halo[SUBLANES - 1:SUBLANES, :], pltpu.roll(g0, 1, axis=0))
    g2 = pltpu.roll(g0, 2, axis=0)
    g2 = jnp.where(row == 0, halo[SUBLANES - 2:SUBLANES - 1, :], g2)
    g2 = jnp.where(row == 1, halo[SUBLANES - 1:SUBLANES, :], g2)
    return g0, g1, g2


def _conv_blocks(T, FF):
    tr = _divisor(T, ROW_BLOCK, SUBLANES)
    tc = _divisor(FF, 1024, LANES)
    return tr, tc


def _conv_fwd(gu, conv_w, conv_b, *, name):
    T = gu.shape[0]
    FF = gu.shape[1] // 2
    tr, tc = _conv_blocks(T, FF)
    ncol = FF // tc

    def body(g_ref, halo_ref, u_ref, w_ref, b_ref, o_ref):
        i = pl.program_id(0)
        g0, g1, g2 = _conv_taps(g_ref, halo_ref, i, tr)
        gc = b_ref[...] + w_ref[0:1, :] * g2 + w_ref[1:2, :] * g1 + w_ref[2:3, :] * g0
        o_ref[...] = (gc * jax.nn.sigmoid(gc) * u_ref[...].astype(F32)).astype(o_ref.dtype)

    hb = tr // SUBLANES
    return pl.pallas_call(
        body, name=name, grid=(T // tr, ncol),
        in_specs=[pl.BlockSpec((tr, tc), lambda i, j: (i, j)),
                  pl.BlockSpec((SUBLANES, tc), lambda i, j: (jnp.maximum(i * hb - 1, 0), j)),
                  pl.BlockSpec((tr, tc), lambda i, j: (i, j + ncol)),
                  pl.BlockSpec((SUBLANES, tc), lambda i, j: (0, j)), pl.BlockSpec((1, tc), lambda i, j: (0, j))],
        out_specs=pl.BlockSpec((tr, tc), lambda i, j: (i, j)),
        out_shape=jax.ShapeDtypeStruct((T, FF), BF16),
        compiler_params=_params(("parallel", "parallel")),
    )(gu, gu, gu, conv_w, conv_b)


def _conv_bwd_gate(da, gu, conv_w, conv_b, *, name, riders=()):
    T = gu.shape[0]
    FF = gu.shape[1] // 2
    tr, tc = _conv_blocks(T, FF)
    ncol = FF // tc

    def body(da_ref, g_ref, halo_ref, u_ref, w_ref, b_ref, dgc_ref, du_ref, s_ref):
        i = pl.program_id(1)

        @pl.when(i == 0)
        def _():
            s_ref[...] = jnp.zeros_like(s_ref)

        g0, g1, g2 = _conv_taps(g_ref, halo_ref, i, tr)
        gc = b_ref[...] + w_ref[0:1, :] * g2 + w_ref[1:2, :] * g1 + w_ref[2:3, :] * g0
        sg = jax.nn.sigmoid(gc)
        dav = da_ref[...].astype(F32)
        du_ref[...] = (dav * gc * sg).astype(du_ref.dtype)
        dgc = dav * u_ref[...].astype(F32) * (sg * (1.0 + gc * (1.0 - sg)))
        dgc_ref[...] = dgc.astype(dgc_ref.dtype)
        s_ref[0:1, :] += jnp.sum(dgc * g2, axis=0, keepdims=True)
        s_ref[1:2, :] += jnp.sum(dgc * g1, axis=0, keepdims=True)
        s_ref[2:3, :] += jnp.sum(dgc * g0, axis=0, keepdims=True)
        s_ref[3:4, :] += jnp.sum(dgc, axis=0, keepdims=True)

    hb = tr // SUBLANES
    blk = pl.BlockSpec((tr, tc), lambda j, i: (i, j))
    (dgc, du, sums), carried = _gridded_call(
        name, (ncol, T // tr), ("parallel", "arbitrary"), body,
        [blk, blk,
         pl.BlockSpec((SUBLANES, tc), lambda j, i: (jnp.maximum(i * hb - 1, 0), j)),
         pl.BlockSpec((tr, tc), lambda j, i: (i, j + ncol)),
         pl.BlockSpec((SUBLANES, tc), lambda j, i: (0, j)), pl.BlockSpec((1, tc), lambda j, i: (0, j))],
        [blk, blk, pl.BlockSpec((SUBLANES, tc), lambda j, i: (0, j))],
        [jax.ShapeDtypeStruct((T, FF), BF16), jax.ShapeDtypeStruct((T, FF), BF16),
         jax.ShapeDtypeStruct((SUBLANES, FF), F32)],
        [], [da, gu, gu, gu, conv_w, conv_b], riders)
    return dgc, du, sums, carried


def _conv_bwd_taps(dgc, conv_w, *, name):
    T, FF = dgc.shape
    tr, tc = _conv_blocks(T, FF)
    ncol = FF // tc
    nrow = T // tr

    def body(d_ref, next_ref, w_ref, o_ref):
        i = pl.program_id(0)
        d0 = d_ref[...].astype(F32)
        nxt = jnp.where(i < nrow - 1, next_ref[...].astype(F32), 0.0)
        row = lax.broadcasted_iota(jnp.int32, d0.shape, 0)
        d1 = jnp.where(row == tr - 1, nxt[0:1, :], pltpu.roll(d0, tr - 1, axis=0))
        d2 = pltpu.roll(d0, tr - 2, axis=0)
        d2 = jnp.where(row == tr - 2, nxt[0:1, :], d2)
        d2 = jnp.where(row == tr - 1, nxt[1:2, :], d2)
        dg = w_ref[2:3, :] * d0 + w_ref[1:2, :] * d1 + w_ref[0:1, :] * d2
        o_ref[...] = dg.astype(o_ref.dtype)

    hb = tr // SUBLANES
    last = T // SUBLANES - 1
    return pl.pallas_call(
        body, name=name, grid=(nrow, ncol),
        in_specs=[pl.BlockSpec((tr, tc), lambda i, j: (i, j)),
                  pl.BlockSpec((SUBLANES, tc), lambda i, j: (jnp.minimum((i + 1) * hb, last), j)),
                  pl.BlockSpec((SUBLANES, tc), lambda i, j: (0, j))],
        out_specs=pl.BlockSpec((tr, tc), lambda i, j: (i, j)),
        out_shape=jax.ShapeDtypeStruct((T, FF), BF16),
        compiler_params=_params(("parallel", "parallel")),
    )(dgc, dgc, conv_w)


def _mesh_pos():
    return lax.axis_index("x"), lax.axis_index("y"), lax.axis_index("c")


def _flip(pos, mask):
    x, y, c = pos
    return (x ^ ((mask >> 2) & 1), y ^ ((mask >> 1) & 1), c ^ (mask & 1))


def _index_of(pos):
    x, y, c = pos
    return 4 * x + 2 * y + c


class _Rider(collections.namedtuple("_Rider", "src gather cols R c")):
    def out_shape(self):
        if not self.gather:
            shape = (N_DEV, self.R, self.c)
        elif self.cols:
            shape = (self.R, N_DEV * self.c)
        else:
            shape = (N_DEV * self.R, self.c)
        return jax.ShapeDtypeStruct(shape, self.src.dtype)

    def slab(self, ref, idx):
        if self.cols:
            return ref.at[:, pl.ds(pl.multiple_of(idx * self.c, LANES), self.c)]
        return ref.at[pl.ds(pl.multiple_of(idx * self.R, 2 * SUBLANES), self.R), :]

    def copy(self, src_ref, dst_ref, send_sems, recv_sems, me, k, arriving):
        peer = _flip(me, k)
        owner = _index_of(peer if arriving else me)
        if self.gather:
            src, dst = src_ref, self.slab(dst_ref, owner)
        else:
            src, dst = self.slab(src_ref, _index_of(peer)), dst_ref.at[owner]
        return pltpu.make_async_remote_copy(src_ref=src, dst_ref=dst, send_sem=send_sems.at[k - 1],
                                            recv_sem=recv_sems.at[k - 1], device_id=peer,
                                            device_id_type=pl.DeviceIdType.MESH)

    def own(self, src_ref, dst_ref, local_sem, me):
        my = _index_of(me)
        if self.gather:
            return pltpu.make_async_copy(src_ref, self.slab(dst_ref, my), local_sem)
        return pltpu.make_async_copy(self.slab(src_ref, my), dst_ref.at[my], local_sem)

    def start(self, src_ref, dst_ref, send_sems, recv_sems, local_sem):
        me = _mesh_pos()
        self.own(src_ref, dst_ref, local_sem, me).start()
        for k in range(1, N_DEV):
            self.copy(src_ref, dst_ref, send_sems, recv_sems, me, k, False).start()

    def wait(self, src_ref, dst_ref, send_sems, recv_sems, local_sem):
        me = _mesh_pos()
        for k in range(1, N_DEV):
            self.copy(src_ref, dst_ref, send_sems, recv_sems, me, k, True).wait_recv()
        for k in range(1, N_DEV):
            self.copy(src_ref, dst_ref, send_sems, recv_sems, me, k, False).wait_send()
        self.own(src_ref, dst_ref, local_sem, me).wait()


_RIDER_SEMS = [pltpu.SemaphoreType.DMA((N_DEV - 1,)), pltpu.SemaphoreType.DMA((N_DEV - 1,)), pltpu.SemaphoreType.DMA]
_ANY = pl.BlockSpec(memory_space=pl.ANY)


def _comm_call(riders, *, name):
    n = len(riders)

    def body(*refs):
        srcs, dsts, sems = refs[:n], refs[n:2 * n], refs[2 * n:]
        for r, rider in enumerate(riders):
            rider.start(srcs[r], dsts[r], *sems[3 * r:3 * r + 3])
        for r, rider in enumerate(riders):
            rider.wait(srcs[r], dsts[r], *sems[3 * r:3 * r + 3])

    return pl.pallas_call(
        body, name=name,
        in_specs=[_ANY] * n, out_specs=[_ANY] * n,
        out_shape=[rider.out_shape() for rider in riders],
        scratch_shapes=_RIDER_SEMS * n,
        compiler_params=pltpu.CompilerParams(has_side_effects=True),
    )(*[rider.src for rider in riders])


def _carry(riders, grid, body, in_specs, out_specs, out_shape, scratch_shapes, args):
    n, n_in, n_out, n_scratch = len(riders), len(in_specs), len(out_specs), len(scratch_shapes)

    def carrying(*refs):
        ins, refs = refs[:n_in], refs[n_in:]
        srcs, refs = refs[:n], refs[n:]
        outs, refs = refs[:n_out], refs[n_out:]
        dsts, refs = refs[:n], refs[n:]
        scratch, sems = refs[:n_scratch], refs[n_scratch:]
        ids = [pl.program_id(a) for a in range(len(grid))]
        first = functools.reduce(jnp.logical_and, [i == 0 for i in ids])
        last = functools.reduce(jnp.logical_and, [i == g - 1 for i, g in zip(ids, grid)])

        @pl.when(first)
        def _():
            for r, rider in enumerate(riders):
                rider.start(srcs[r], dsts[r], *sems[3 * r:3 * r + 3])

        body(*ins, *outs, *scratch)

        @pl.when(last)
        def _():
            for r, rider in enumerate(riders):
                rider.wait(srcs[r], dsts[r], *sems[3 * r:3 * r + 3])

    return dict(
        body=carrying,
        in_specs=list(in_specs) + [_ANY] * n,
        out_specs=list(out_specs) + [_ANY] * n,
        out_shape=list(out_shape) + [rider.out_shape() for rider in riders],
        scratch_shapes=list(scratch_shapes) + _RIDER_SEMS * n,
        args=list(args) + [rider.src for rider in riders])


def _gridded_call(name, grid, semantics, body, in_specs, out_specs, out_shape, scratch_shapes, args, riders=()):
    call = dict(body=body, in_specs=in_specs, out_specs=out_specs, out_shape=out_shape,
                scratch_shapes=scratch_shapes, args=args)
    if riders:
        call = _carry(list(riders), grid, **call)
        semantics = ("arbitrary",) * len(grid)
    outs = pl.pallas_call(
        call["body"], name=name, grid=grid, in_specs=call["in_specs"], out_specs=call["out_specs"],
        out_shape=call["out_shape"], scratch_shapes=call["scratch_shapes"],
        compiler_params=_params(semantics),
    )(*call["args"])
    n_out = len(out_shape)
    return list(outs[:n_out]), list(outs[n_out:])


def _all_gather(x, *, name):
    R, C = x.shape
    return _comm_call([_Rider(x, True, False, R, C)], name=name)[0].reshape(N_DEV, R, C)


def _exchange(parts, *, name):
    _, R, C = parts.shape
    return _comm_call([_Rider(parts.reshape(N_DEV * R, C), False, False, R, C)], name=name)[0]


def _sum_adamw(parts, w, m, v, *, name):
    R, C = w.shape
    tr = _divisor(R, max(2 * SUBLANES, ADAMW_BLOCK_ELEMS // C // SUBLANES * SUBLANES), 2 * SUBLANES)
    c1 = 1.0 - ADAM_B1 ** ADAM_STEP
    c2 = 1.0 - ADAM_B2 ** ADAM_STEP

    def body(p_ref, w_ref, m_ref, v_ref, g_ref, d_ref, nm_ref, nv_ref):
        g = p_ref[0].astype(F32)
        for s in range(1, N_DEV):
            g = g + p_ref[s].astype(F32)
        nm = ADAM_B1 * m_ref[...] + (1.0 - ADAM_B1) * g
        nv = ADAM_B2 * v_ref[...] + (1.0 - ADAM_B2) * (g * g)
        g_ref[...] = g
        nm_ref[...] = nm
        nv_ref[...] = nv
        d_ref[...] = -ADAM_LR * ((nm / c1) / (jnp.sqrt(nv / c2) + ADAM_EPS) + ADAM_WD * w_ref[...])

    blk = pl.BlockSpec((tr, C), lambda i: (i, 0))
    return pl.pallas_call(
        body, name=name, grid=(R // tr,),
        in_specs=[pl.BlockSpec((N_DEV, tr, C), lambda i: (0, i, 0)), blk, blk, blk],
        out_specs=[blk] * 4,
        out_shape=[jax.ShapeDtypeStruct((R, C), F32)] * 4,
        compiler_params=_params(("parallel",)),
    )(parts, w, m, v)


def _pad2(a, rows, cols):
    return jnp.pad(a, ((0, rows - a.shape[0]), (0, cols - a.shape[1])))


def _gather_cols(w, dtype, *, name):
    R, c = w.shape
    rp, cp = _round_up(R, 2 * SUBLANES), _round_up(c, LANES)
    g = _all_gather(_pad2(w.astype(dtype), rp, cp), name=name)
    return jnp.transpose(g[:, :R, :c], (1, 0, 2)).reshape(R, N_DEV * c)


def _column_parts(dw, c):
    R = dw.shape[0]
    parts = jnp.transpose(dw.reshape(R, N_DEV, c), (1, 0, 2))
    return jnp.pad(parts, ((0, 0), (0, _round_up(R, 2 * SUBLANES) - R), (0, _round_up(c, LANES) - c)))


def _padded_adamw(got, w, m, v, *, name):
    R, c = w.shape
    rp, cp = got.shape[1:]
    outs = _sum_adamw(got, _pad2(w, rp, cp), _pad2(m, rp, cp), _pad2(v, rp, cp), name=name)
    return [o[:R, :c] for o in outs]


def _update_cols(dw, w, m, v, *, name):
    got = _exchange(_column_parts(dw, w.shape[1]), name=name + "_exchange")
    return _padded_adamw(got, w, m, v, name=name + "_adamw")


def _update_replicated(grads, ws, ms, vs, *, name):
    sizes = [int(g.size) for g in grads]
    total = sum(sizes)
    rows = _round_up(-(-total // LANES), 2 * SUBLANES)

    def pack(arrs):
        flat = jnp.concatenate([a.reshape(-1).astype(F32) for a in arrs])
        return jnp.pad(flat, (0, rows * LANES - total)).reshape(rows, LANES)

    got = _all_gather(pack(grads), name=name + "_gather")
    outs = _sum_adamw(got, pack(ws), pack(ms), pack(vs), name=name + "_adamw")
    result = []
    for o in outs:
        flat = o.reshape(-1)
        arrs, off = [], 0
        for w, n in zip(ws, sizes):
            arrs.append(flat[off:off + n].reshape(w.shape))
            off += n
        result.append(arrs)
    return result


def kernel(x, meta, g_mix, w_in, b_f, lam_re, lam_im, log_dt, b_re, b_im, c_re, c_im, d_skip, w_glu, w_attn_o, w_out, g_ffn, w_up, conv_w, conv_b, w_down, g_final, loss_target, m_meta, m_g_mix, m_w_in, m_b_f, m_lam_re, m_lam_im, m_log_dt, m_b_re, m_b_im, m_c_re, m_c_im, m_d_skip, m_w_glu, m_w_attn_o, m_w_out, m_g_ffn, m_w_up, m_conv_w, m_conv_b, m_w_down, m_g_final, v_meta, v_g_mix, v_w_in, v_b_f, v_lam_re, v_lam_im, v_log_dt, v_b_re, v_b_im, v_c_re, v_c_im, v_d_skip, v_w_glu, v_w_attn_o, v_w_out, v_g_ffn, v_w_up, v_conv_w, v_conv_b, v_w_down, v_g_final):
    seq, D = x.shape[1], x.shape[2]
    L = N_META + seq
    T = _round_up(L, SEQ_BLOCK) if L > SEQ_BLOCK else _round_up(L, LANES)
    DS = d_skip.shape[1]
    H = b_f.shape[1]
    DA = H * HEAD_DIM
    FF = conv_b.shape[1]
    G, P, C = b_re.shape[1:]

    meta_full = _gather_cols(meta, F32, name="gather_meta")
    conv_w_full = _gather_cols(conv_w[0], F32, name="gather_conv_w")
    w_in_full = _gather_cols(w_in[0], BF16, name="gather_w_in")
    gathers = [_Rider(w[0].astype(BF16), True, cols, *w.shape[1:])
               for w, cols in ((w_attn_o, True), (w_glu, True), (w_out, False), (w_up, True), (w_down, False))]
    conv_w8 = jnp.pad(conv_w_full, ((0, SUBLANES - CONV_WIDTH), (0, 0)))

    a_re, a_im, bb_re, bb_im = _ssm_discretise(lam_re[0], lam_im[0], log_dt[0], b_re[0], b_im[0])
    w_b = _slab_in(bb_re, bb_im)
    w_c = _slab_out(c_re[0], c_im[0])
    a_slab = _slab_diag(a_re, a_im)
    w_b16, w_c16 = w_b.astype(BF16), w_c.astype(BF16)
    w_bt16, w_ct16 = jnp.swapaxes(w_b16, 1, 2), jnp.swapaxes(w_c16, 1, 2)

    h0 = jnp.concatenate([meta_full, x[0], jnp.zeros((T - L, D), F32)], axis=0)
    target = jnp.pad(loss_target[0], ((N_META, T - L), (0, 0)))
    n1 = _rms_fwd(h0, g_mix, name="rms_mix")
    o_f, o_u, o_g = 3 * DA, 3 * DA + H, 3 * DA + H + DS
    w_qkv = w_in_full[:, :o_f]
    w_f = jnp.pad(w_in_full[:, o_f:o_u], ((0, 0), (0, LANES - H)))
    w_u = w_in_full[:, o_u:o_g]
    w_g = w_in_full[:, o_g:]
    w_main = jnp.concatenate([w_qkv, w_u, w_g], axis=1)
    zqkv = _matmul(n1, w_qkv, name="mm_qkv")
    zu = _matmul(n1, w_u, name="mm_u")
    zg = _matmul(n1, w_g, name="mm_gates")
    zf = _matmul(n1, w_f, name="mm_forget", out_dtype=F32)
    f_t = zf[:, :H].T
    b_col = b_f.reshape(H, 1)
    fcum = _forget_cumsum(f_t, b_col, name="forget_cumsum") * LOG2_E
    fcol = fcum.T
    o, lse, carried = _attn_fwd(zqkv, fcum, fcol, name="attn_fwd", riders=gathers)
    w_ao_full, w_glu_full, w_out_full, w_up_full, w_down_full = carried
    attn = _matmul(o, w_ao_full, name="mm_attn_o")
    y, gy, hin = _ssm_fwd(zu, w_b16, w_c16, a_slab, d_skip, name="ssm_fwd")
    yab = _matmul(gy, w_glu_full, name="mm_glu")
    merged = _merge_fwd(yab, zg, attn, name="merge_fwd")
    h1 = _matmul(merged, w_out_full, name="mm_out", out_dtype=F32, residual=h0)
    n2 = _rms_fwd(h1, g_ffn, name="rms_ffn")
    gu = _matmul(n2, w_up_full, name="mm_up")
    act = _conv_fwd(gu, conv_w8, conv_b, name="conv_fwd")
    h2 = _matmul(act, w_down_full, name="mm_down", out_dtype=F32, residual=h1)

    dh2, sq, dg_final = _final_loss(h2, g_final.reshape(1, D), target, seq, name="final_loss")
    loss = lax.psum(0.5 * sq[0, 0] / D, ("x", "y", "c"))
    dh2_16 = dh2.astype(BF16)
    d_act = _matmul(dh2_16, w_down_full, name="mm_down_dx", trans_b=True)
    dw_down = _matmul(act, dh2_16, name="mm_down_dw", trans_a=True)
    dgc, du2, conv_sums, (x_w_down,) = _conv_bwd_gate(
        d_act, gu, conv_w8, conv_b, name="conv_bwd_gate",
        riders=[_Rider(dw_down, False, False, *w_down.shape[1:])])
    dgu = jnp.concatenate([_conv_bwd_taps(dgc, conv_w8, name="conv_bwd_taps"), du2], axis=1)
    dn2 = _matmul(dgu, w_up_full, name="mm_up_dx", trans_b=True)
    dw_up = _matmul(n2, dgu, name="mm_up_dw", trans_a=True)
    dh1, dg_ffn = _rms_bwd(dn2, h1, g_ffn, dh2, name="rms_ffn_bwd")
    dh1_16 = dh1.astype(BF16)
    dmerged = _matmul(dh1_16, w_out_full, name="mm_out_dx", trans_b=True)
    dw_out = _matmul(merged, dh1_16, name="mm_out_dw", trans_a=True)
    dzg, dattn, dyab = _merge_bwd(dmerged, yab, zg, attn, name="merge_bwd")
    do = _matmul(dattn, w_ao_full, name="mm_attn_o_dx", trans_b=True)
    dw_ao = _matmul(o, dattn, name="mm_attn_o_dw", trans_a=True)
    dgy = _matmul(dyab, w_glu_full, name="mm_glu_dx", trans_b=True)
    dw_glu = _matmul(gy, dyab, name="mm_glu_dw", trans_a=True)
    dzu, dw_b, dw_c, da_slab, dd_skip = _ssm_bwd(zu, dgy, y, hin, w_b16, w_bt16, w_ct16, a_slab, d_skip,
                                                 name="ssm_bwd")
    dq, delta, dfq = _attn_bwd_dq(zqkv, fcum, fcol, do, o, lse, name="attn_bwd_dq")
    dk, dv, dfcum, (x_w_up, x_w_out, x_w_ao, x_w_glu) = _attn_bwd_dkv(
        zqkv, fcum, fcol, do, lse, delta, name="attn_bwd_dkv",
        riders=[_Rider(dw_up, False, True, *w_up.shape[1:]), _Rider(dw_out, False, False, *w_out.shape[1:]),
                _Rider(dw_ao, False, True, *w_attn_o.shape[1:]), _Rider(dw_glu, False, True, *w_glu.shape[1:])])
    df_t, db_f = _forget_bwd(dfcum.T, dfq, f_t, b_col, name="forget_bwd")
    dzf = jnp.pad(df_t.T, ((0, 0), (0, LANES - H))).astype(BF16)
    dz_main = jnp.concatenate([dq, dk, dv, dzu, dzg], axis=1)
    dw_main = _matmul(n1, dz_main, name="mm_in_dw", trans_a=True)
    dw_f = _matmul(n1, dzf, name="mm_forget_dw", trans_a=True)
    dw_in = jnp.concatenate([dw_main[:, :o_f], dw_f[:, :H], dw_main[:, o_f:]], axis=1)
    parts_in = _column_parts(dw_in, w_in.shape[2])
    dn1 = _matmul(dzf, w_f, name="mm_forget_dx", out_dtype=F32, trans_b=True)
    dn1, (x_w_in,) = _matmul(
        dz_main, w_main, name="mm_in_dx", out_dtype=F32, residual=dn1, trans_b=True,
        riders=[_Rider(parts_in.reshape(-1, parts_in.shape[2]), False, False, *parts_in.shape[1:])])
    dh0, dg_mix = _rms_bwd(dn1, h0, g_mix, dh1, name="rms_mix_bwd")
    grad_x = dh0[N_META:L][None]

    dbb_re, dbb_im = _slab_in_grad(dw_b, G, P, C)
    dc_re, dc_im = _slab_out_grad(dw_c, G, C, P)
    da_re = da_slab[:, :SLAB_STATE].reshape(G, P)
    da_im = da_slab[:, SLAB_STATE:].reshape(G, P)
    _, disc_vjp = jax.vjp(_ssm_discretise, lam_re[0], lam_im[0], log_dt[0], b_re[0], b_im[0])
    dlam_re, dlam_im, dlog_dt, db_re, db_im = disc_vjp((da_re, da_im, dbb_re, dbb_im))

    big = {}
    big["meta"] = _update_cols(dh0[:N_META], meta, m_meta, v_meta, name="meta")
    big["conv_w"] = _update_cols(conv_sums[:CONV_WIDTH], conv_w[0], m_conv_w[0], v_conv_w[0], name="conv_w")
    big["w_down"] = _sum_adamw(x_w_down, w_down[0], m_w_down[0], v_w_down[0], name="w_down_adamw")
    big["w_up"] = _sum_adamw(x_w_up, w_up[0], m_w_up[0], v_w_up[0], name="w_up_adamw")
    big["w_out"] = _sum_adamw(x_w_out, w_out[0], m_w_out[0], v_w_out[0], name="w_out_adamw")
    big["w_attn_o"] = _sum_adamw(x_w_ao, w_attn_o[0], m_w_attn_o[0], v_w_attn_o[0], name="w_attn_o_adamw")
    big["w_glu"] = _sum_adamw(x_w_glu, w_glu[0], m_w_glu[0], v_w_glu[0], name="w_glu_adamw")
    big["w_in"] = _padded_adamw(x_w_in, w_in[0], m_w_in[0], v_w_in[0], name="w_in_adamw")

    rep_names = ["g_mix", "b_f", "lam_re", "lam_im", "log_dt", "b_re", "b_im", "c_re", "c_im", "d_skip", "g_ffn",
                 "conv_b", "g_final"]
    rep_w = [g_mix, b_f, lam_re, lam_im, log_dt, b_re, b_im, c_re, c_im, d_skip, g_ffn, conv_b, g_final]
    rep_m = [m_g_mix, m_b_f, m_lam_re, m_lam_im, m_log_dt, m_b_re, m_b_im, m_c_re, m_c_im, m_d_skip, m_g_ffn,
             m_conv_b, m_g_final]
    rep_v = [v_g_mix, v_b_f, v_lam_re, v_lam_im, v_log_dt, v_b_re, v_b_im, v_c_re, v_c_im, v_d_skip, v_g_ffn,
             v_conv_b, v_g_final]
    rep_g = [dg_mix, db_f[:, 0], dlam_re, dlam_im, dlog_dt, db_re, db_im, dc_re, dc_im, dd_skip, dg_ffn,
             conv_sums[CONV_WIDTH], dg_final]
    rep = _update_replicated(rep_g, rep_w, rep_m, rep_v, name="replicated")
    rep_out = {n: [rep[k][i] for k in range(4)] for i, n in enumerate(rep_names)}

    order = ["meta", "g_mix", "w_in", "b_f", "lam_re", "lam_im", "log_dt", "b_re", "b_im", "c_re", "c_im", "d_skip",
             "w_glu", "w_attn_o", "w_out", "g_ffn", "w_up", "conv_w", "conv_b", "w_down", "g_final"]
    outs = [loss, grad_x]
    for kind in range(4):
        for n in order:
            if n in big:
                outs.append(big[n][kind] if n == "meta" else big[n][kind][None])
            else:
                outs.append(rep_out[n][kind])
    return tuple(outs)
```

```python
import collections
import functools
import math

import jax
import jax.numpy as jnp
from jax import lax
from jax.experimental import pallas as pl
from jax.experimental.pallas import tpu as pltpu

F32 = jnp.float32
BF16 = jnp.bfloat16

N_META = 16
EPS = 1e-6
HEAD_DIM = 128
SSM_GROUP = 16
SSM_STATE = 64
GROUPS_PER_SLAB = 8
SLAB_STATE = GROUPS_PER_SLAB * SSM_STATE
CONV_WIDTH = 3
N_DEV = 8

ADAM_LR = 0.001
ADAM_B1 = 0.9
ADAM_B2 = 0.999
ADAM_EPS = 1e-08
ADAM_WD = 0.01
ADAM_STEP = 10

LANES = 128
SUBLANES = 8
VMEM_LIMIT = 52 * 1024 * 1024

SEQ_BLOCK = 768
ATT_BLOCK = 384
ROW_BLOCK = 256
SSM_CHUNK = 256
CONV_ROW_BLOCK = 768
ADAMW_BLOCK_ELEMS = 1 << 17
WGRAD_BLOCK_M = 1408
MASK_VALUE = -1e30
LOG2_E = math.log2(math.e)


def _round_up(n, m):
    return (n + m - 1) // m * m


def _divisor(n, target, mult):
    if n <= target:
        return n
    best = None
    for d in range(mult, target + 1, mult):
        if n % d == 0:
            best = d
    assert best is not None, (n, target, mult)
    return best


def _params(sem):
    return pltpu.CompilerParams(dimension_semantics=sem, vmem_limit_bytes=VMEM_LIMIT)


def _matmul(a, b, *, name, trans_a=False, trans_b=False, out_dtype=None, residual=None, riders=(),
            tm=768, tn=1024, tk=2816):
    out_dtype = BF16 if out_dtype is None else out_dtype
    assert not (trans_a and trans_b)
    if trans_a:
        K, M = a.shape
    else:
        M, K = a.shape
    if trans_b:
        N, K2 = b.shape
    else:
        K2, N = b.shape
    assert K == K2, (a.shape, b.shape)
    if trans_a:
        tm = max(tm, WGRAD_BLOCK_M)
    tm = _divisor(M, tm, LANES if trans_a else SUBLANES)
    tn = _divisor(N, tn, LANES)
    tk = _divisor(K, tk, LANES if not trans_a else SUBLANES)
    nk = K // tk

    def body(*refs):
        if residual is None:
            a_ref, b_ref, o_ref, acc_ref = refs
        else:
            a_ref, b_ref, r_ref, o_ref, acc_ref = refs
        k = pl.program_id(2)

        @pl.when(k == 0)
        def _():
            acc_ref[...] = jnp.zeros_like(acc_ref)

        contract = (0, 0) if trans_a else (1, 1) if trans_b else (1, 0)
        acc_ref[...] += lax.dot_general(a_ref[...], b_ref[...], (((contract[0],), (contract[1],)), ((), ())),
                                        preferred_element_type=F32)

        @pl.when(k == nk - 1)
        def _():
            r = acc_ref[...]
            if residual is not None:
                r = r + r_ref[...]
            o_ref[...] = r.astype(o_ref.dtype)

    if trans_a:
        a_spec = pl.BlockSpec((tk, tm), lambda i, j, k: (k, i))
    else:
        a_spec = pl.BlockSpec((tm, tk), lambda i, j, k: (i, k))
    if trans_b:
        b_spec = pl.BlockSpec((tn, tk), lambda i, j, k: (j, k))
    else:
        b_spec = pl.BlockSpec((tk, tn), lambda i, j, k: (k, j))
    in_specs = [a_spec, b_spec]
    args = [a, b]
    if residual is not None:
        in_specs.append(pl.BlockSpec((tm, tn), lambda i, j, k: (i, j)))
        args.append(residual)
    (out,), carried = _gridded_call(
        name, (M // tm, N // tn, nk), ("parallel", "parallel", "arbitrary"), body, in_specs,
        [pl.BlockSpec((tm, tn), lambda i, j, k: (i, j))], [jax.ShapeDtypeStruct((M, N), out_dtype)],
        [pltpu.VMEM((tm, tn), F32)], args, riders)
    return (out, carried) if riders else out


def _rms_fwd(h, g, *, name):
    T, D = h.shape
    tr = _divisor(T, ROW_BLOCK, SUBLANES)

    def body(h_ref, g_ref, o_ref):
        x = h_ref[...]
        r = lax.rsqrt(jnp.mean(x * x, axis=-1, keepdims=True) + EPS)
        o_ref[...] = (x * r * g_ref[...]).astype(o_ref.dtype)

    return pl.pallas_call(
        body, name=name, grid=(T // tr,),
        in_specs=[pl.BlockSpec((tr, D), lambda i: (i, 0)), pl.BlockSpec((1, D), lambda i: (0, 0))],
        out_specs=pl.BlockSpec((tr, D), lambda i: (i, 0)),
        out_shape=jax.ShapeDtypeStruct((T, D), BF16),
        compiler_params=_params(("parallel",)),
    )(h, g)


def _rms_bwd(dn, h, g, dres, *, name):
    T, D = h.shape
    tr = _divisor(T, ROW_BLOCK, SUBLANES)

    def body(dn_ref, h_ref, g_ref, dres_ref, dh_ref, dg_ref):
        i = pl.program_id(0)

        @pl.when(i == 0)
        def _():
            dg_ref[...] = jnp.zeros_like(dg_ref)

        x = h_ref[...]
        dn_v = dn_ref[...].astype(F32)
        r = lax.rsqrt(jnp.mean(x * x, axis=-1, keepdims=True) + EPS)
        xh = x * r
        dg_ref[...] += jnp.sum(dn_v * xh, axis=0, keepdims=True)
        dxh = dn_v * g_ref[...]
        dh_ref[...] = dres_ref[...] + r * (dxh - xh * jnp.mean(dxh * xh, axis=-1, keepdims=True))

    return pl.pallas_call(
        body, name=name, grid=(T // tr,),
        in_specs=[pl.BlockSpec((tr, D), lambda i: (i, 0)), pl.BlockSpec((tr, D), lambda i: (i, 0)),
                  pl.BlockSpec((1, D), lambda i: (0, 0)), pl.BlockSpec((tr, D), lambda i: (i, 0))],
        out_specs=[pl.BlockSpec((tr, D), lambda i: (i, 0)), pl.BlockSpec((1, D), lambda i: (0, 0))],
        out_shape=[jax.ShapeDtypeStruct((T, D), F32), jax.ShapeDtypeStruct((1, D), F32)],
        compiler_params=_params(("arbitrary",)),
    )(dn, h, g, dres)


def _final_loss(h, g, target, n_valid, *, name):
    T, D = h.shape
    tr = _divisor(T, ROW_BLOCK, SUBLANES)

    def body(h_ref, g_ref, t_ref, dh_ref, sq_ref, dg_ref):
        i = pl.program_id(0)

        @pl.when(i == 0)
        def _():
            sq_ref[...] = jnp.zeros_like(sq_ref)
            dg_ref[...] = jnp.zeros_like(dg_ref)

        x = h_ref[...]
        r = lax.rsqrt(jnp.mean(x * x, axis=-1, keepdims=True) + EPS)
        xh = x * r
        gv = g_ref[...]
        row = i * tr + lax.broadcasted_iota(jnp.int32, (tr, 1), 0)
        valid = (row >= N_META) & (row < N_META + n_valid)
        err = jnp.where(valid, xh * gv - t_ref[...], 0.0)
        sq_ref[...] += jnp.sum(err * err)
        dy = err * (1.0 / D)
        dg_ref[...] += jnp.sum(dy * xh, axis=0, keepdims=True)
        dxh = dy * gv
        dh_ref[...] = r * (dxh - xh * jnp.mean(dxh * xh, axis=-1, keepdims=True))

    return pl.pallas_call(
        body, name=name, grid=(T // tr,),
        in_specs=[pl.BlockSpec((tr, D), lambda i: (i, 0)), pl.BlockSpec((1, D), lambda i: (0, 0)),
                  pl.BlockSpec((tr, D), lambda i: (i, 0))],
        out_specs=[pl.BlockSpec((tr, D), lambda i: (i, 0)), pl.BlockSpec((SUBLANES, LANES), lambda i: (0, 0)),
                   pl.BlockSpec((1, D), lambda i: (0, 0))],
        out_shape=[jax.ShapeDtypeStruct((T, D), F32), jax.ShapeDtypeStruct((SUBLANES, LANES), F32),
                   jax.ShapeDtypeStruct((1, D), F32)],
        compiler_params=_params(("arbitrary",)),
    )(h, g, target)


def _prefix_sum_lanes(x):
    lane = lax.broadcasted_iota(jnp.int32, x.shape, 1)
    d = 1
    while d < LANES:
        x = x + jnp.where(lane >= d, pltpu.roll(x, d, axis=1), 0.0)
        d *= 2
    return x


def _forget_cumsum(ft, bf, *, name):
    H, T = ft.shape
    nb = T // LANES

    def body(f_ref, b_ref, o_ref):
        carry = jnp.zeros((H, 1), F32)
        for j in range(nb):
            sl = pl.ds(j * LANES, LANES)
            lf = jax.nn.log_sigmoid(f_ref[:, sl] + b_ref[...])
            c = _prefix_sum_lanes(lf) + carry
            o_ref[:, sl] = c
            carry = c[:, LANES - 1:LANES]

    return pl.pallas_call(
        body, name=name,
        in_specs=[pl.BlockSpec(memory_space=pltpu.VMEM), pl.BlockSpec(memory_space=pltpu.VMEM)],
        out_specs=pl.BlockSpec(memory_space=pltpu.VMEM),
        out_shape=jax.ShapeDtypeStruct((H, T), F32),
        compiler_params=pltpu.CompilerParams(vmem_limit_bytes=VMEM_LIMIT),
    )(ft, bf)


def _forget_bwd(dF_key, dF_query, ft, bf, *, name):
    H, T = ft.shape
    nb = T // LANES

    def body(d_ref, dq_ref, f_ref, b_ref, o_ref, s_ref):
        carry = jnp.zeros((H, 1), F32)
        acc = jnp.zeros((H, LANES), F32)
        for j in reversed(range(nb)):
            sl = pl.ds(j * LANES, LANES)
            d = d_ref[:, sl] + dq_ref[:, sl]
            pre = _prefix_sum_lanes(d)
            tot = pre[:, LANES - 1:LANES]
            dlf = tot - pre + d + carry
            carry = carry + tot
            z = f_ref[:, sl] + b_ref[...]
            df = dlf * jax.nn.sigmoid(-z)
            o_ref[:, sl] = df
            acc = acc + df
        s_ref[...] = jnp.broadcast_to(jnp.sum(acc, axis=1, keepdims=True), (H, LANES))

    return pl.pallas_call(
        body, name=name,
        in_specs=[pl.BlockSpec(memory_space=pltpu.VMEM)] * 4,
        out_specs=[pl.BlockSpec(memory_space=pltpu.VMEM)] * 2,
        out_shape=[jax.ShapeDtypeStruct((H, T), F32), jax.ShapeDtypeStruct((H, LANES), F32)],
        compiler_params=pltpu.CompilerParams(vmem_limit_bytes=VMEM_LIMIT),
    )(dF_key, dF_query, ft, bf)


def _scores(q_ref, k_ref, fq_ref, fk_ref, h, blk, scale, diagonal):
    hs = pl.ds(h * HEAD_DIM, HEAD_DIM)
    s = lax.dot_general(k_ref[:, hs], q_ref[:, hs], (((1,), (1,)), ((), ())), preferred_element_type=F32)
    s = s * (scale * LOG2_E) + (fq_ref[h:h + 1, :] - fk_ref[:, h:h + 1])
    if diagonal:
        key = lax.broadcasted_iota(jnp.int32, (blk, blk), 0)
        query = lax.broadcasted_iota(jnp.int32, (blk, blk), 1)
        s = jnp.where(key <= query, s, MASK_VALUE)
    return s


def _causal_blocks(i, j, compute):
    @pl.when(j < i)
    def _():
        compute(False)

    @pl.when(j == i)
    def _():
        compute(True)


def _attn_fwd(zqkv, fcum, fcol, *, name, riders=()):
    T = zqkv.shape[0]
    DA = zqkv.shape[1] // 3
    H = DA // HEAD_DIM
    blk = _divisor(T, ATT_BLOCK, LANES)
    nb = T // blk
    scale = HEAD_DIM ** -0.5

    def body(q_ref, k_ref, v_ref, fq_ref, fk_ref, o_ref, lse_ref, m_ref, l_ref, acc_ref):
        i = pl.program_id(0)
        j = pl.program_id(1)

        @pl.when(j == 0)
        def _():
            m_ref[...] = jnp.full_like(m_ref, MASK_VALUE)
            l_ref[...] = jnp.zeros_like(l_ref)
            acc_ref[...] = jnp.zeros_like(acc_ref)

        def compute(diagonal):
            for h in range(H):
                hs = pl.ds(h * HEAD_DIM, HEAD_DIM)
                s = _scores(q_ref, k_ref, fq_ref, fk_ref, h, blk, scale, diagonal)
                m_prev = m_ref[h:h + 1, :]
                m_new = jnp.maximum(m_prev, jnp.max(s, axis=0, keepdims=True))
                alpha = jnp.exp2(m_prev - m_new)
                p = jnp.exp2(s - m_new)
                l_ref[h:h + 1, :] = alpha * l_ref[h:h + 1, :] + jnp.sum(p, axis=0, keepdims=True)
                acc_ref[hs, :] = alpha * acc_ref[hs, :] + lax.dot_general(
                    v_ref[:, hs], p.astype(BF16), (((0,), (0,)), ((), ())), preferred_element_type=F32)
                m_ref[h:h + 1, :] = m_new

        _causal_blocks(i, j, compute)

        @pl.when(j == nb - 1)
        def _():
            for h in range(H):
                hs = pl.ds(h * HEAD_DIM, HEAD_DIM)
                l = l_ref[h:h + 1, :]
                o_ref[:, hs] = (acc_ref[hs, :] / l).T.astype(o_ref.dtype)
                lse_ref[h:h + 1, :] = m_ref[h:h + 1, :] + jnp.log2(l)

    kv = lambda c: (lambda i, j: (jnp.minimum(j, i), c))
    (o, lse), carried = _gridded_call(
        name, (nb, nb), ("parallel", "arbitrary"), body,
        [pl.BlockSpec((blk, DA), lambda i, j: (i, 0)),
         pl.BlockSpec((blk, DA), kv(1)), pl.BlockSpec((blk, DA), kv(2)),
         pl.BlockSpec((H, blk), lambda i, j: (0, i)),
         pl.BlockSpec((blk, H), lambda i, j: (jnp.minimum(j, i), 0))],
        [pl.BlockSpec((blk, DA), lambda i, j: (i, 0)), pl.BlockSpec((H, blk), lambda i, j: (0, i))],
        [jax.ShapeDtypeStruct((T, DA), BF16), jax.ShapeDtypeStruct((H, T), F32)],
        [pltpu.VMEM((H, blk), F32), pltpu.VMEM((H, blk), F32), pltpu.VMEM((DA, blk), F32)],
        [zqkv, zqkv, zqkv, fcum, fcol], riders)
    return o, lse, carried


def _attn_bwd_dq(zqkv, fcum, fcol, do, o, lse, *, name):
    T = zqkv.shape[0]
    DA = zqkv.shape[1] // 3
    H = DA // HEAD_DIM
    blk = _divisor(T, ATT_BLOCK, LANES)
    nb = T // blk
    scale = HEAD_DIM ** -0.5

    def body(q_ref, k_ref, v_ref, fq_ref, fk_ref, do_ref, o_ref, lse_ref, dq_ref, dl_ref, dfq_ref, acc_ref):
        i = pl.program_id(0)
        j = pl.program_id(1)

        @pl.when(j == 0)
        def _():
            acc_ref[...] = jnp.zeros_like(acc_ref)
            dfq_ref[...] = jnp.zeros_like(dfq_ref)
            for h in range(H):
                hs = pl.ds(h * HEAD_DIM, HEAD_DIM)
                prod = do_ref[:, hs].astype(F32) * o_ref[:, hs].astype(F32)
                dl_ref[h:h + 1, :] = jnp.sum(prod.T, axis=0, keepdims=True)

        def compute(diagonal):
            for h in range(H):
                hs = pl.ds(h * HEAD_DIM, HEAD_DIM)
                s = _scores(q_ref, k_ref, fq_ref, fk_ref, h, blk, scale, diagonal)
                p = jnp.exp2(s - lse_ref[h:h + 1, :])
                dp = lax.dot_general(v_ref[:, hs], do_ref[:, hs], (((1,), (1,)), ((), ())),
                                     preferred_element_type=F32)
                ds = p * (dp - dl_ref[h:h + 1, :])
                dfq_ref[h:h + 1, :] += jnp.sum(ds, axis=0, keepdims=True)
                acc_ref[hs, :] += scale * lax.dot_general(k_ref[:, hs], ds.astype(BF16), (((0,), (0,)), ((), ())),
                                                          preferred_element_type=F32)

        _causal_blocks(i, j, compute)

        @pl.when(j == nb - 1)
        def _():
            for h in range(H):
                hs = pl.ds(h * HEAD_DIM, HEAD_DIM)
                dq_ref[:, hs] = acc_ref[hs, :].T.astype(dq_ref.dtype)

    kv = lambda c: (lambda i, j: (jnp.minimum(j, i), c))
    row = lambda i, j: (i, 0)
    lane = lambda i, j: (0, i)
    return pl.pallas_call(
        body, name=name, grid=(nb, nb),
        in_specs=[pl.BlockSpec((blk, DA), row), pl.BlockSpec((blk, DA), kv(1)), pl.BlockSpec((blk, DA), kv(2)),
                  pl.BlockSpec((H, blk), lane), pl.BlockSpec((blk, H), lambda i, j: (jnp.minimum(j, i), 0)),
                  pl.BlockSpec((blk, DA), row), pl.BlockSpec((blk, DA), row), pl.BlockSpec((H, blk), lane)],
        out_specs=[pl.BlockSpec((blk, DA), row), pl.BlockSpec((H, blk), lane), pl.BlockSpec((H, blk), lane)],
        out_shape=[jax.ShapeDtypeStruct((T, DA), BF16), jax.ShapeDtypeStruct((H, T), F32),
                   jax.ShapeDtypeStruct((H, T), F32)],
        scratch_shapes=[pltpu.VMEM((DA, blk), F32)],
        compiler_params=_params(("parallel", "arbitrary")),
    )(zqkv, zqkv, zqkv, fcum, fcol, do, o, lse)


def _attn_bwd_dkv(zqkv, fcum, fcol, do, lse, delta, *, name, riders=()):
    T = zqkv.shape[0]
    DA = zqkv.shape[1] // 3
    H = DA // HEAD_DIM
    blk = _divisor(T, ATT_BLOCK, LANES)
    nb = T // blk
    scale = HEAD_DIM ** -0.5

    def body(q_ref, k_ref, v_ref, fq_ref, fk_ref, do_ref, lse_ref, dl_ref, dk_ref, dv_ref, df_ref,
             dk_acc, dv_acc, df_acc):
        j = pl.program_id(0)
        i = pl.program_id(1)

        @pl.when(i == 0)
        def _():
            dk_acc[...] = jnp.zeros_like(dk_acc)
            dv_acc[...] = jnp.zeros_like(dv_acc)
            df_acc[...] = jnp.zeros_like(df_acc)

        def compute(diagonal):
            for h in range(H):
                hs = pl.ds(h * HEAD_DIM, HEAD_DIM)
                s = _scores(q_ref, k_ref, fq_ref, fk_ref, h, blk, scale, diagonal)
                p = jnp.exp2(s - lse_ref[h:h + 1, :])
                dov = do_ref[:, hs]
                dv_acc[:, hs] += jnp.dot(p.astype(BF16), dov, preferred_element_type=F32)
                dp = lax.dot_general(v_ref[:, hs], dov, (((1,), (1,)), ((), ())), preferred_element_type=F32)
                ds = p * (dp - dl_ref[h:h + 1, :])
                dk_acc[:, hs] += scale * jnp.dot(ds.astype(BF16), q_ref[:, hs], preferred_element_type=F32)
                df_acc[:, h:h + 1] -= jnp.sum(ds, axis=1, keepdims=True)

        _causal_blocks(i, j, compute)

        @pl.when(i == nb - 1)
        def _():
            dk_ref[...] = dk_acc[...].astype(dk_ref.dtype)
            dv_ref[...] = dv_acc[...].astype(dv_ref.dtype)
            df_ref[...] = df_acc[...]

    qrow = lambda j, i: (jnp.maximum(i, j), 0)
    qlane = lambda j, i: (0, jnp.maximum(i, j))
    kcol = lambda c: (lambda j, i: (j, c))
    (dk, dv, df), carried = _gridded_call(
        name, (nb, nb), ("parallel", "arbitrary"), body,
        [pl.BlockSpec((blk, DA), qrow), pl.BlockSpec((blk, DA), kcol(1)), pl.BlockSpec((blk, DA), kcol(2)),
         pl.BlockSpec((H, blk), qlane), pl.BlockSpec((blk, H), kcol(0)),
         pl.BlockSpec((blk, DA), qrow), pl.BlockSpec((H, blk), qlane), pl.BlockSpec((H, blk), qlane)],
        [pl.BlockSpec((blk, DA), kcol(0)), pl.BlockSpec((blk, DA), kcol(0)), pl.BlockSpec((blk, H), kcol(0))],
        [jax.ShapeDtypeStruct((T, DA), BF16), jax.ShapeDtypeStruct((T, DA), BF16),
         jax.ShapeDtypeStruct((T, H), F32)],
        [pltpu.VMEM((blk, DA), F32), pltpu.VMEM((blk, DA), F32), pltpu.VMEM((blk, H), F32)],
        [zqkv, zqkv, zqkv, fcum, fcol, do, lse, delta], riders)
    return dk, dv, df, carried


def _gelu(y):
    c = math.sqrt(2.0 / math.pi)
    return 0.5 * y * (1.0 + jnp.tanh(c * (y + 0.044715 * (y * y * y))))


def _gelu_grad(y):
    c = math.sqrt(2.0 / math.pi)
    th = jnp.tanh(c * (y + 0.044715 * (y * y * y)))
    return 0.5 * (1.0 + th) + 0.5 * y * (1.0 - th * th) * c * (1.0 + 3.0 * 0.044715 * y * y)


STATE_BLOCKS = SLAB_STATE // LANES


def _lane_blocks(ref, lead=()):
    return [ref[lead + (slice(None), pl.ds(b * LANES, LANES))] for b in range(2 * STATE_BLOCKS)]


def _put_lane_blocks(ref, blocks):
    for b, v in enumerate(blocks):
        ref[:, pl.ds(b * LANES, LANES)] = v


def _put_slab(x_ref, first, q, n_slab, chunk, value):
    for b in range(2 * STATE_BLOCKS):
        x_ref[b, pl.ds(first * n_slab + q, chunk, stride=n_slab), :] = value[:, b * LANES:(b + 1) * LANES]


def _get_slab(x_ref, first, q, n_slab, chunk):
    return jnp.concatenate([x_ref[b, pl.ds(first * n_slab + q, chunk, stride=n_slab), :]
                            for b in range(2 * STATE_BLOCKS)], axis=1)


def _ssm_scan_fwd(x_ref, a, h, chunk, n_slab, first=0):
    nb = STATE_BLOCKS

    def step(t, h):
        rows = pl.ds(pl.multiple_of((t + first) * n_slab, n_slab), n_slab)
        out = [None] * (2 * nb)
        for b in range(nb):
            n_re = a[b] * h[b] - a[nb + b] * h[nb + b] + x_ref[b, rows, :]
            n_im = a[b] * h[nb + b] + a[nb + b] * h[b] + x_ref[nb + b, rows, :]
            x_ref[b, rows, :] = n_re
            x_ref[nb + b, rows, :] = n_im
            out[b], out[nb + b] = n_re, n_im
        return tuple(out)

    return lax.fori_loop(0, chunk, step, tuple(h), unroll=4)


def _ssm_fwd(zu, w_b, w_c, a, d_skip, *, name):
    T, DS = zu.shape
    n_slab = DS // LANES
    chunk = _divisor(T, SSM_CHUNK, SUBLANES)
    n_chunk = T // chunk

    def body(u_ref, wb_ref, wc_ref, a_ref, ds_ref, y_ref, gy_ref, hin_ref, x_ref, h_ref):
        k = pl.program_id(0)

        @pl.when(k == 0)
        def _():
            h_ref[...] = jnp.zeros_like(h_ref)

        hin_ref[0] = h_ref[...]
        for q in range(n_slab):
            qs = pl.ds(q * LANES, LANES)
            _put_slab(x_ref, 0, q, n_slab, chunk, jnp.dot(u_ref[:, qs], wb_ref[q], preferred_element_type=F32))
        h = _ssm_scan_fwd(x_ref, _lane_blocks(a_ref), _lane_blocks(h_ref), chunk, n_slab)
        _put_lane_blocks(h_ref, h)
        for q in range(n_slab):
            qs = pl.ds(q * LANES, LANES)
            hq = _get_slab(x_ref, 0, q, n_slab, chunk).astype(BF16)
            y = jnp.dot(hq, wc_ref[q], preferred_element_type=F32) + ds_ref[:, qs] * u_ref[:, qs].astype(F32)
            y_ref[:, qs] = y
            gy_ref[:, qs] = _gelu(y).astype(gy_ref.dtype)

    whole = lambda shape: pl.BlockSpec(shape, lambda k: (0,) * len(shape))
    return pl.pallas_call(
        body, name=name, grid=(n_chunk,),
        in_specs=[pl.BlockSpec((chunk, DS), lambda k: (k, 0)), whole(w_b.shape), whole(w_c.shape),
                  whole(a.shape), whole(d_skip.shape)],
        out_specs=[pl.BlockSpec((chunk, DS), lambda k: (k, 0)), pl.BlockSpec((chunk, DS), lambda k: (k, 0)),
                   pl.BlockSpec((1, n_slab, 2 * SLAB_STATE), lambda k: (k, 0, 0))],
        out_shape=[jax.ShapeDtypeStruct((T, DS), F32), jax.ShapeDtypeStruct((T, DS), BF16),
                   jax.ShapeDtypeStruct((n_chunk, n_slab, 2 * SLAB_STATE), F32)],
        scratch_shapes=[pltpu.VMEM((2 * STATE_BLOCKS, chunk * n_slab, LANES), F32),
                        pltpu.VMEM((n_slab, 2 * SLAB_STATE), F32)],
        compiler_params=_params(("arbitrary",)),
    )(zu, w_b, w_c, a, d_skip)


def _ssm_bwd(zu, dgy, y, hin, w_b, w_bt, w_ct, a, d_skip, *, name):
    T, DS = zu.shape
    n_slab = DS // LANES
    chunk = _divisor(T, SSM_CHUNK, SUBLANES)
    n_chunk = T // chunk
    S = SLAB_STATE

    def body(u_ref, dgy_ref, y_ref, hin_ref, wb_ref, wbt_ref, wct_ref, a_ref, ds_ref,
             du_ref, dwb_ref, dwc_ref, da_ref, dds_ref, hb_ref, gb_ref, dy_ref, g_ref):
        k = pl.program_id(0)

        @pl.when(k == 0)
        def _():
            g_ref[...] = jnp.zeros_like(g_ref)
            dwb_ref[...] = jnp.zeros_like(dwb_ref)
            dwc_ref[...] = jnp.zeros_like(dwc_ref)
            da_ref[...] = jnp.zeros_like(da_ref)
            dds_ref[...] = jnp.zeros_like(dds_ref)

        nb = STATE_BLOCKS
        a = _lane_blocks(a_ref)
        hin = _lane_blocks(hin_ref, lead=(0,))

        for b in range(2 * nb):
            hb_ref[b, pl.ds(0, n_slab), :] = hin[b]
        dy_ref[...] = dgy_ref[...].astype(F32) * _gelu_grad(y_ref[...])
        for q in range(n_slab):
            qs = pl.ds(q * LANES, LANES)
            _put_slab(hb_ref, 1, q, n_slab, chunk, jnp.dot(u_ref[:, qs], wb_ref[q], preferred_element_type=F32))
            _put_slab(gb_ref, 0, q, n_slab, chunk,
                      jnp.dot(dy_ref[:, qs].astype(BF16), wct_ref[q], preferred_element_type=F32))
        _ssm_scan_fwd(hb_ref, a, hin, chunk, n_slab, first=1)

        def step(s, carry):
            g, da = carry[:2 * nb], carry[2 * nb:]
            t = chunk - 1 - s
            rows = pl.ds(pl.multiple_of(t * n_slab, n_slab), n_slab)
            g_out, da_out = [None] * (2 * nb), [None] * (2 * nb)
            for b in range(nb):
                n_re = gb_ref[b, rows, :] + a[b] * g[b] + a[nb + b] * g[nb + b]
                n_im = gb_ref[nb + b, rows, :] + a[b] * g[nb + b] - a[nb + b] * g[b]
                gb_ref[b, rows, :] = n_re
                gb_ref[nb + b, rows, :] = n_im
                p_re = hb_ref[b, rows, :]
                p_im = hb_ref[nb + b, rows, :]
                g_out[b], g_out[nb + b] = n_re, n_im
                da_out[b] = da[b] + n_re * p_re + n_im * p_im
                da_out[nb + b] = da[nb + b] + n_im * p_re - n_re * p_im
            return tuple(g_out) + tuple(da_out)

        zero = jnp.zeros((n_slab, LANES), F32)
        carry = lax.fori_loop(0, chunk, step, tuple(_lane_blocks(g_ref)) + (zero,) * (2 * nb), unroll=4)
        _put_lane_blocks(g_ref, carry[:2 * nb])
        for b in range(2 * nb):
            da_ref[:, pl.ds(b * LANES, LANES)] += carry[2 * nb + b]

        for q in range(n_slab):
            qs = pl.ds(q * LANES, LANES)
            uq = u_ref[:, qs]
            dy = dy_ref[:, qs]
            hq = _get_slab(hb_ref, 1, q, n_slab, chunk).astype(BF16)
            gq = _get_slab(gb_ref, 0, q, n_slab, chunk).astype(BF16)
            dwc_ref[q] += lax.dot_general(hq, dy.astype(BF16), (((0,), (0,)), ((), ())), preferred_element_type=F32)
            dwb_ref[q] += lax.dot_general(uq, gq, (((0,), (0,)), ((), ())), preferred_element_type=F32)
            du_ref[:, qs] = (jnp.dot(gq, wbt_ref[q], preferred_element_type=F32) + ds_ref[:, qs] * dy).astype(du_ref.dtype)
            dds_ref[:, qs] += jnp.sum(dy * uq.astype(F32), axis=0, keepdims=True)

    whole = lambda shape: pl.BlockSpec(shape, lambda k: (0,) * len(shape))
    rev = lambda k: (n_chunk - 1 - k, 0)
    return pl.pallas_call(
        body, name=name, grid=(n_chunk,),
        in_specs=[pl.BlockSpec((chunk, DS), rev), pl.BlockSpec((chunk, DS), rev), pl.BlockSpec((chunk, DS), rev),
                  pl.BlockSpec((1, n_slab, 2 * S), lambda k: (n_chunk - 1 - k, 0, 0)),
                  whole(w_b.shape), whole(w_bt.shape), whole(w_ct.shape), whole(a.shape), whole(d_skip.shape)],
        out_specs=[pl.BlockSpec((chunk, DS), rev), whole(w_b.shape), whole(w_bt.shape), whole(a.shape),
                   whole(d_skip.shape)],
        out_shape=[jax.ShapeDtypeStruct((T, DS), BF16), jax.ShapeDtypeStruct(w_b.shape, F32),
                   jax.ShapeDtypeStruct(w_bt.shape, F32), jax.ShapeDtypeStruct(a.shape, F32),
                   jax.ShapeDtypeStruct(d_skip.shape, F32)],
        scratch_shapes=[pltpu.VMEM((2 * STATE_BLOCKS, (chunk + 1) * n_slab, LANES), F32),
                        pltpu.VMEM((2 * STATE_BLOCKS, chunk * n_slab, LANES), F32),
                        pltpu.VMEM((chunk, DS), F32), pltpu.VMEM((n_slab, 2 * S), F32)],
        compiler_params=_params(("arbitrary",)),
    )(zu, dgy, y, hin, w_b, w_bt, w_ct, a, d_skip)


def _ssm_discretise(lam_re, lam_im, log_dt, b_re, b_im):
    dt = jnp.exp(log_dt)[:, None]
    mag = jnp.exp(lam_re * dt)
    a_re = mag * jnp.cos(lam_im * dt)
    a_im = mag * jnp.sin(lam_im * dt)
    den = lam_re * lam_re + lam_im * lam_im
    nr = a_re - 1.0
    z_re = (nr * lam_re + a_im * lam_im) / den
    z_im = (a_im * lam_re - nr * lam_im) / den
    bb_re = z_re[..., None] * b_re - z_im[..., None] * b_im
    bb_im = z_re[..., None] * b_im + z_im[..., None] * b_re
    return a_re, a_im, bb_re, bb_im


def _slab_in(m_re, m_im):
    G, P, C = m_re.shape
    n_slab = G // GROUPS_PER_SLAB
    eye = jnp.eye(GROUPS_PER_SLAB, dtype=m_re.dtype)

    def one(m):
        m = m.reshape(n_slab, GROUPS_PER_SLAB, P, C)
        w = jnp.einsum('sgpc,gh->sgchp', m, eye)
        return w.reshape(n_slab, GROUPS_PER_SLAB * C, GROUPS_PER_SLAB * P)

    return jnp.concatenate([one(m_re), one(m_im)], axis=2)


def _slab_in_grad(dw, G, P, C):
    n_slab = G // GROUPS_PER_SLAB
    eye = jnp.eye(GROUPS_PER_SLAB, dtype=dw.dtype)

    def one(w):
        w = w.reshape(n_slab, GROUPS_PER_SLAB, C, GROUPS_PER_SLAB, P)
        return jnp.einsum('sgchp,gh->sgpc', w, eye).reshape(G, P, C)

    return one(dw[:, :, :SLAB_STATE]), one(dw[:, :, SLAB_STATE:])


def _slab_out(c_re, c_im):
    G, C, P = c_re.shape
    n_slab = G // GROUPS_PER_SLAB
    eye = jnp.eye(GROUPS_PER_SLAB, dtype=c_re.dtype)

    def one(m):
        m = m.reshape(n_slab, GROUPS_PER_SLAB, C, P)
        w = jnp.einsum('sgcp,gh->shpgc', m, eye)
        return w.reshape(n_slab, GROUPS_PER_SLAB * P, GROUPS_PER_SLAB * C)

    return jnp.concatenate([one(c_re), one(-c_im)], axis=1)


def _slab_out_grad(dw, G, C, P):
    n_slab = G // GROUPS_PER_SLAB
    eye = jnp.eye(GROUPS_PER_SLAB, dtype=dw.dtype)

    def one(w):
        w = w.reshape(n_slab, GROUPS_PER_SLAB, P, GROUPS_PER_SLAB, C)
        return jnp.einsum('shpgc,gh->sgcp', w, eye).reshape(G, C, P)

    return one(dw[:, :SLAB_STATE, :]), -one(dw[:, SLAB_STATE:, :])


def _slab_diag(a_re, a_im):
    G, P = a_re.shape
    n_slab = G // GROUPS_PER_SLAB
    return jnp.concatenate([a_re.reshape(n_slab, SLAB_STATE), a_im.reshape(n_slab, SLAB_STATE)], axis=1)


def _merge_fwd(yab, zg, attn, *, name):
    T, D = attn.shape
    tr = _divisor(T, ROW_BLOCK, SUBLANES)

    def body(ya_ref, yb_ref, ga_ref, gb_ref, at_ref, o_ref):
        f = lambda r: r[...].astype(F32)
        ssm = f(ya_ref) * jax.nn.sigmoid(f(yb_ref))
        o_ref[...] = (jax.nn.sigmoid(f(ga_ref)) * ssm + jax.nn.sigmoid(f(gb_ref)) * f(at_ref)).astype(o_ref.dtype)

    lo = pl.BlockSpec((tr, D), lambda i: (i, 0))
    hi = pl.BlockSpec((tr, D), lambda i: (i, 1))
    return pl.pallas_call(
        body, name=name, grid=(T // tr,),
        in_specs=[lo, hi, lo, hi, lo],
        out_specs=lo,
        out_shape=jax.ShapeDtypeStruct((T, D), BF16),
        compiler_params=_params(("parallel",)),
    )(yab, yab, zg, zg, attn)


def _merge_bwd(dm, yab, zg, attn, *, name):
    T, D = attn.shape
    tr = _divisor(T, ROW_BLOCK, SUBLANES)

    def body(dm_ref, ya_ref, yb_ref, ga_ref, gb_ref, at_ref, dg_ref, dat_ref, dy_ref):
        f = lambda r: r[...].astype(F32)
        dmv, ya, at = f(dm_ref), f(ya_ref), f(at_ref)
        sa, sb, syb = jax.nn.sigmoid(f(ga_ref)), jax.nn.sigmoid(f(gb_ref)), jax.nn.sigmoid(f(yb_ref))
        ssm = ya * syb
        dssm = dmv * sa
        dg_ref[:, pl.ds(0, D)] = (dmv * ssm * sa * (1.0 - sa)).astype(dg_ref.dtype)
        dg_ref[:, pl.ds(D, D)] = (dmv * at * sb * (1.0 - sb)).astype(dg_ref.dtype)
        dat_ref[...] = (dmv * sb).astype(dat_ref.dtype)
        dy_ref[:, pl.ds(0, D)] = (dssm * syb).astype(dy_ref.dtype)
        dy_ref[:, pl.ds(D, D)] = (dssm * ya * syb * (1.0 - syb)).astype(dy_ref.dtype)

    lo = pl.BlockSpec((tr, D), lambda i: (i, 0))
    hi = pl.BlockSpec((tr, D), lambda i: (i, 1))
    both = pl.BlockSpec((tr, 2 * D), lambda i: (i, 0))
    return pl.pallas_call(
        body, name=name, grid=(T // tr,),
        in_specs=[lo, lo, hi, lo, hi, lo],
        out_specs=[both, lo, both],
        out_shape=[jax.ShapeDtypeStruct((T, 2 * D), BF16), jax.ShapeDtypeStruct((T, D), BF16),
                   jax.ShapeDtypeStruct((T, 2 * D), BF16)],
        compiler_params=_params(("parallel",)),
    )(dm, yab, yab, zg, zg, attn)


def _conv_taps(g_ref, halo_ref, i, tr):
    g0 = g_ref[...].astype(F32)
    halo = jnp.where(i > 0, halo_ref[...].astype(F32), 0.0)
    row = lax.broadcasted_iota(jnp.int32, g0.shape, 0)
    g1 = jnp.where(row == 0, halo[SUBLANES - 1:SUBLANES, :], pltpu.roll(g0, 1, axis=0))
    g2 = pltpu.roll(g0, 2, axis=0)
    g2 = jnp.where(row == 0, halo[SUBLANES - 2:SUBLANES - 1, :], g2)
    g2 = jnp.where(row == 1, halo[SUBLANES - 1:SUBLANES, :], g2)
    return g0, g1, g2


def _conv_blocks(T, FF):
    tr = _divisor(T, CONV_ROW_BLOCK, SUBLANES)
    tc = _divisor(FF, 1024, LANES)
    return tr, tc


def _conv_fwd(gu, conv_w, conv_b, *, name):
    T = gu.shape[0]
    FF = gu.shape[1] // 2
    tr, tc = _conv_blocks(T, FF)
    ncol = FF // tc

    def body(g_ref, halo_ref, u_ref, w_ref, b_ref, o_ref):
        i = pl.program_id(0)
        g0, g1, g2 = _conv_taps(g_ref, halo_ref, i, tr)
        gc = b_ref[...] + w_ref[0:1, :] * g2 + w_ref[1:2, :] * g1 + w_ref[2:3, :] * g0
        o_ref[...] = (gc * jax.nn.sigmoid(gc) * u_ref[...].astype(F32)).astype(o_ref.dtype)

    hb = tr // SUBLANES
    return pl.pallas_call(
        body, name=name, grid=(T // tr, ncol),
        in_specs=[pl.BlockSpec((tr, tc), lambda i, j: (i, j)),
                  pl.BlockSpec((SUBLANES, tc), lambda i, j: (jnp.maximum(i * hb - 1, 0), j)),
                  pl.BlockSpec((tr, tc), lambda i, j: (i, j + ncol)),
                  pl.BlockSpec((SUBLANES, tc), lambda i, j: (0, j)), pl.BlockSpec((1, tc), lambda i, j: (0, j))],
        out_specs=pl.BlockSpec((tr, tc), lambda i, j: (i, j)),
        out_shape=jax.ShapeDtypeStruct((T, FF), BF16),
        compiler_params=_params(("parallel", "parallel")),
    )(gu, gu, gu, conv_w, conv_b)


def _conv_bwd_gate(da, gu, conv_w, conv_b, *, name, riders=()):
    T = gu.shape[0]
    FF = gu.shape[1] // 2
    tr, tc = _conv_blocks(T, FF)
    ncol = FF // tc

    def body(da_ref, g_ref, halo_ref, u_ref, w_ref, b_ref, dgc_ref, du_ref, s_ref):
        i = pl.program_id(1)

        @pl.when(i == 0)
        def _():
            s_ref[...] = jnp.zeros_like(s_ref)

        g0, g1, g2 = _conv_taps(g_ref, halo_ref, i, tr)
        gc = b_ref[...] + w_ref[0:1, :] * g2 + w_ref[1:2, :] * g1 + w_ref[2:3, :] * g0
        sg = jax.nn.sigmoid(gc)
        dav = da_ref[...].astype(F32)
        du_ref[...] = (dav * gc * sg).astype(du_ref.dtype)
        dgc = dav * u_ref[...].astype(F32) * (sg * (1.0 + gc * (1.0 - sg)))
        dgc_ref[...] = dgc.astype(dgc_ref.dtype)
        s_ref[0:1, :] += jnp.sum(dgc * g2, axis=0, keepdims=True)
        s_ref[1:2, :] += jnp.sum(dgc * g1, axis=0, keepdims=True)
        s_ref[2:3, :] += jnp.sum(dgc * g0, axis=0, keepdims=True)
        s_ref[3:4, :] += jnp.sum(dgc, axis=0, keepdims=True)

    hb = tr // SUBLANES
    blk = pl.BlockSpec((tr, tc), lambda j, i: (i, j))
    (dgc, du, sums), carried = _gridded_call(
        name, (ncol, T // tr), ("parallel", "arbitrary"), body,
        [blk, blk,
         pl.BlockSpec((SUBLANES, tc), lambda j, i: (jnp.maximum(i * hb - 1, 0), j)),
         pl.BlockSpec((tr, tc), lambda j, i: (i, j + ncol)),
         pl.BlockSpec((SUBLANES, tc), lambda j, i: (0, j)), pl.BlockSpec((1, tc), lambda j, i: (0, j))],
        [blk, blk, pl.BlockSpec((SUBLANES, tc), lambda j, i: (0, j))],
        [jax.ShapeDtypeStruct((T, FF), BF16), jax.ShapeDtypeStruct((T, FF), BF16),
         jax.ShapeDtypeStruct((SUBLANES, FF), F32)],
        [], [da, gu, gu, gu, conv_w, conv_b], riders)
    return dgc, du, sums, carried


def _conv_bwd_taps(dgc, conv_w, *, name):
    T, FF = dgc.shape
    tr, tc = _conv_blocks(T, FF)
    ncol = FF // tc
    nrow = T // tr

    def body(d_ref, next_ref, w_ref, o_ref):
        i = pl.program_id(0)
        d0 = d_ref[...].astype(F32)
        nxt = jnp.where(i < nrow - 1, next_ref[...].astype(F32), 0.0)
        row = lax.broadcasted_iota(jnp.int32, d0.shape, 0)
        d1 = jnp.where(row == tr - 1, nxt[0:1, :], pltpu.roll(d0, tr - 1, axis=0))
        d2 = pltpu.roll(d0, tr - 2, axis=0)
        d2 = jnp.where(row == tr - 2, nxt[0:1, :], d2)
        d2 = jnp.where(row == tr - 1, nxt[1:2, :], d2)
        dg = w_ref[2:3, :] * d0 + w_ref[1:2, :] * d1 + w_ref[0:1, :] * d2
        o_ref[...] = dg.astype(o_ref.dtype)

    hb = tr // SUBLANES
    last = T // SUBLANES - 1
    return pl.pallas_call(
        body, name=name, grid=(nrow, ncol),
        in_specs=[pl.BlockSpec((tr, tc), lambda i, j: (i, j)),
                  pl.BlockSpec((SUBLANES, tc), lambda i, j: (jnp.minimum((i + 1) * hb, last), j)),
                  pl.BlockSpec((SUBLANES, tc), lambda i, j: (0, j))],
        out_specs=pl.BlockSpec((tr, tc), lambda i, j: (i, j)),
        out_shape=jax.ShapeDtypeStruct((T, FF), BF16),
        compiler_params=_params(("parallel", "parallel")),
    )(dgc, dgc, conv_w)


def _mesh_pos():
    return lax.axis_index("x"), lax.axis_index("y"), lax.axis_index("c")


def _flip(pos, mask):
    x, y, c = pos
    return (x ^ ((mask >> 2) & 1), y ^ ((mask >> 1) & 1), c ^ (mask & 1))


def _index_of(pos):
    x, y, c = pos
    return 4 * x + 2 * y + c


class _Rider(collections.namedtuple("_Rider", "src gather cols R c")):
    def out_shape(self):
        if not self.gather:
            shape = (N_DEV, self.R, self.c)
        elif self.cols:
            shape = (self.R, N_DEV * self.c)
        else:
            shape = (N_DEV * self.R, self.c)
        return jax.ShapeDtypeStruct(shape, self.src.dtype)

    def slab(self, ref, idx):
        if self.cols:
            return ref.at[:, pl.ds(pl.multiple_of(idx * self.c, LANES), self.c)]
        return ref.at[pl.ds(pl.multiple_of(idx * self.R, 2 * SUBLANES), self.R), :]

    def copy(self, src_ref, dst_ref, send_sems, recv_sems, me, k, arriving):
        peer = _flip(me, k)
        owner = _index_of(peer if arriving else me)
        if self.gather:
            src, dst = src_ref, self.slab(dst_ref, owner)
        else:
            src, dst = self.slab(src_ref, _index_of(peer)), dst_ref.at[owner]
        return pltpu.make_async_remote_copy(src_ref=src, dst_ref=dst, send_sem=send_sems.at[k - 1],
                                            recv_sem=recv_sems.at[k - 1], device_id=peer,
                                            device_id_type=pl.DeviceIdType.MESH)

    def own(self, src_ref, dst_ref, local_sem, me):
        my = _index_of(me)
        if self.gather:
            return pltpu.make_async_copy(src_ref, self.slab(dst_ref, my), local_sem)
        return pltpu.make_async_copy(self.slab(src_ref, my), dst_ref.at[my], local_sem)

    def start(self, src_ref, dst_ref, send_sems, recv_sems, local_sem):
        me = _mesh_pos()
        self.own(src_ref, dst_ref, local_sem, me).start()
        for k in range(1, N_DEV):
            self.copy(src_ref, dst_ref, send_sems, recv_sems, me, k, False).start()

    def wait(self, src_ref, dst_ref, send_sems, recv_sems, local_sem):
        me = _mesh_pos()
        for k in range(1, N_DEV):
            self.copy(src_ref, dst_ref, send_sems, recv_sems, me, k, True).wait_recv()
        for k in range(1, N_DEV):
            self.copy(src_ref, dst_ref, send_sems, recv_sems, me, k, False).wait_send()
        self.own(src_ref, dst_ref, local_sem, me).wait()


_RIDER_SEMS = [pltpu.SemaphoreType.DMA((N_DEV - 1,)), pltpu.SemaphoreType.DMA((N_DEV - 1,)), pltpu.SemaphoreType.DMA]
_ANY = pl.BlockSpec(memory_space=pl.ANY)


def _comm_call(riders, *, name):
    n = len(riders)

    def body(*refs):
        srcs, dsts, sems = refs[:n], refs[n:2 * n], refs[2 * n:]
        for r, rider in enumerate(riders):
            rider.start(srcs[r], dsts[r], *sems[3 * r:3 * r + 3])
        for r, rider in enumerate(riders):
            rider.wait(srcs[r], dsts[r], *sems[3 * r:3 * r + 3])

    return pl.pallas_call(
        body, name=name,
        in_specs=[_ANY] * n, out_specs=[_ANY] * n,
        out_shape=[rider.out_shape() for rider in riders],
        scratch_shapes=_RIDER_SEMS * n,
        compiler_params=pltpu.CompilerParams(has_side_effects=True),
    )(*[rider.src for rider in riders])


def _carry(riders, grid, body, in_specs, out_specs, out_shape, scratch_shapes, args):
    n, n_in, n_out, n_scratch = len(riders), len(in_specs), len(out_specs), len(scratch_shapes)

    def carrying(*refs):
        ins, refs = refs[:n_in], refs[n_in:]
        srcs, refs = refs[:n], refs[n:]
        outs, refs = refs[:n_out], refs[n_out:]
        dsts, refs = refs[:n], refs[n:]
        scratch, sems = refs[:n_scratch], refs[n_scratch:]
        ids = [pl.program_id(a) for a in range(len(grid))]
        first = functools.reduce(jnp.logical_and, [i == 0 for i in ids])
        last = functools.reduce(jnp.logical_and, [i == g - 1 for i, g in zip(ids, grid)])

        @pl.when(first)
        def _():
            for r, rider in enumerate(riders):
                rider.start(srcs[r], dsts[r], *sems[3 * r:3 * r + 3])

        body(*ins, *outs, *scratch)

        @pl.when(last)
        def _():
            for r, rider in enumerate(riders):
                rider.wait(srcs[r], dsts[r], *sems[3 * r:3 * r + 3])

    return dict(
        body=carrying,
        in_specs=list(in_specs) + [_ANY] * n,
        out_specs=list(out_specs) + [_ANY] * n,
        out_shape=list(out_shape) + [rider.out_shape() for rider in riders],
        scratch_shapes=list(scratch_shapes) + _RIDER_SEMS * n,
        args=list(args) + [rider.src for rider in riders])


def _gridded_call(name, grid, semantics, body, in_specs, out_specs, out_shape, scratch_shapes, args, riders=()):
    call = dict(body=body, in_specs=in_specs, out_specs=out_specs, out_shape=out_shape,
                scratch_shapes=scratch_shapes, args=args)
    if riders:
        call = _carry(list(riders), grid, **call)
        semantics = ("arbitrary",) * len(grid)
    outs = pl.pallas_call(
        call["body"], name=name, grid=grid, in_specs=call["in_specs"], out_specs=call["out_specs"],
        out_shape=call["out_shape"], scratch_shapes=call["scratch_shapes"],
        compiler_params=_params(semantics),
    )(*call["args"])
    n_out = len(out_shape)
    return list(outs[:n_out]), list(outs[n_out:])


def _all_gather(x, *, name):
    R, C = x.shape
    return _comm_call([_Rider(x, True, False, R, C)], name=name)[0].reshape(N_DEV, R, C)


def _exchange(parts, *, name):
    _, R, C = parts.shape
    return _comm_call([_Rider(parts.reshape(N_DEV * R, C), False, False, R, C)], name=name)[0]


def _sum_adamw(parts, w, m, v, *, name):
    R, C = w.shape
    tr = _divisor(R, max(2 * SUBLANES, ADAMW_BLOCK_ELEMS // C // SUBLANES * SUBLANES), 2 * SUBLANES)
    c1 = 1.0 - ADAM_B1 ** ADAM_STEP
    c2 = 1.0 - ADAM_B2 ** ADAM_STEP

    def body(p_ref, w_ref, m_ref, v_ref, g_ref, d_ref, nm_ref, nv_ref):
        g = p_ref[0].astype(F32)
        for s in range(1, N_DEV):
            g = g + p_ref[s].astype(F32)
        nm = ADAM_B1 * m_ref[...] + (1.0 - ADAM_B1) * g
        nv = ADAM_B2 * v_ref[...] + (1.0 - ADAM_B2) * (g * g)
        g_ref[...] = g
        nm_ref[...] = nm
        nv_ref[...] = nv
        d_ref[...] = -ADAM_LR * ((nm / c1) / (jnp.sqrt(nv / c2) + ADAM_EPS) + ADAM_WD * w_ref[...])

    blk = pl.BlockSpec((tr, C), lambda i: (i, 0))
    return pl.pallas_call(
        body, name=name, grid=(R // tr,),
        in_specs=[pl.BlockSpec((N_DEV, tr, C), lambda i: (0, i, 0)), blk, blk, blk],
        out_specs=[blk] * 4,
        out_shape=[jax.ShapeDtypeStruct((R, C), F32)] * 4,
        compiler_params=_params(("parallel",)),
    )(parts, w, m, v)


def _pad2(a, rows, cols):
    return jnp.pad(a, ((0, rows - a.shape[0]), (0, cols - a.shape[1])))


def _gather_cols(w, dtype, *, name):
    R, c = w.shape
    rp, cp = _round_up(R, 2 * SUBLANES), _round_up(c, LANES)
    g = _all_gather(_pad2(w.astype(dtype), rp, cp), name=name)
    return jnp.transpose(g[:, :R, :c], (1, 0, 2)).reshape(R, N_DEV * c)


def _column_parts(dw, c):
    R = dw.shape[0]
    parts = jnp.transpose(dw.reshape(R, N_DEV, c), (1, 0, 2))
    return jnp.pad(parts, ((0, 0), (0, _round_up(R, 2 * SUBLANES) - R), (0, _round_up(c, LANES) - c)))


def _padded_adamw(got, w, m, v, *, name):
    R, c = w.shape
    rp, cp = got.shape[1:]
    outs = _sum_adamw(got, _pad2(w, rp, cp), _pad2(m, rp, cp), _pad2(v, rp, cp), name=name)
    return [o[:R, :c] for o in outs]


def _update_cols(dw, w, m, v, *, name):
    got = _exchange(_column_parts(dw, w.shape[1]), name=name + "_exchange")
    return _padded_adamw(got, w, m, v, name=name + "_adamw")


def _update_replicated(grads, ws, ms, vs, *, name):
    sizes = [int(g.size) for g in grads]
    total = sum(sizes)
    rows = _round_up(-(-total // LANES), 2 * SUBLANES)

    def pack(arrs):
        flat = jnp.concatenate([a.reshape(-1).astype(F32) for a in arrs])
        return jnp.pad(flat, (0, rows * LANES - total)).reshape(rows, LANES)

    got = _all_gather(pack(grads), name=name + "_gather")
    outs = _sum_adamw(got, pack(ws), pack(ms), pack(vs), name=name + "_adamw")
    result = []
    for o in outs:
        flat = o.reshape(-1)
        arrs, off = [], 0
        for w, n in zip(ws, sizes):
            arrs.append(flat[off:off + n].reshape(w.shape))
            off += n
        result.append(arrs)
    return result


def kernel(x, meta, g_mix, w_in, b_f, lam_re, lam_im, log_dt, b_re, b_im, c_re, c_im, d_skip, w_glu, w_attn_o, w_out, g_ffn, w_up, conv_w, conv_b, w_down, g_final, loss_target, m_meta, m_g_mix, m_w_in, m_b_f, m_lam_re, m_lam_im, m_log_dt, m_b_re, m_b_im, m_c_re, m_c_im, m_d_skip, m_w_glu, m_w_attn_o, m_w_out, m_g_ffn, m_w_up, m_conv_w, m_conv_b, m_w_down, m_g_final, v_meta, v_g_mix, v_w_in, v_b_f, v_lam_re, v_lam_im, v_log_dt, v_b_re, v_b_im, v_c_re, v_c_im, v_d_skip, v_w_glu, v_w_attn_o, v_w_out, v_g_ffn, v_w_up, v_conv_w, v_conv_b, v_w_down, v_g_final):
    seq, D = x.shape[1], x.shape[2]
    L = N_META + seq
    T = _round_up(L, SEQ_BLOCK) if L > SEQ_BLOCK else _round_up(L, LANES)
    DS = d_skip.shape[1]
    H = b_f.shape[1]
    DA = H * HEAD_DIM
    FF = conv_b.shape[1]
    G, P, C = b_re.shape[1:]

    meta_full = _gather_cols(meta, F32, name="gather_meta")
    conv_w_full = _gather_cols(conv_w[0], F32, name="gather_conv_w")
    w_in_full = _gather_cols(w_in[0], BF16, name="gather_w_in")
    gathers = [_Rider(w[0].astype(BF16), True, cols, *w.shape[1:])
               for w, cols in ((w_attn_o, True), (w_glu, True), (w_out, False), (w_up, True), (w_down, False))]
    conv_w8 = jnp.pad(conv_w_full, ((0, SUBLANES - CONV_WIDTH), (0, 0)))

    a_re, a_im, bb_re, bb_im = _ssm_discretise(lam_re[0], lam_im[0], log_dt[0], b_re[0], b_im[0])
    w_b = _slab_in(bb_re, bb_im)
    w_c = _slab_out(c_re[0], c_im[0])
    a_slab = _slab_diag(a_re, a_im)
    w_b16, w_c16 = w_b.astype(BF16), w_c.astype(BF16)
    w_bt16, w_ct16 = jnp.swapaxes(w_b16, 1, 2), jnp.swapaxes(w_c16, 1, 2)

    h0 = jnp.concatenate([meta_full, x[0], jnp.zeros((T - L, D), F32)], axis=0)
    target = jnp.pad(loss_target[0], ((N_META, T - L), (0, 0)))
    n1 = _rms_fwd(h0, g_mix, name="rms_mix")
    o_f, o_u, o_g = 3 * DA, 3 * DA + H, 3 * DA + H + DS
    w_qkv = w_in_full[:, :o_f]
    w_f = jnp.pad(w_in_full[:, o_f:o_u], ((0, 0), (0, LANES - H)))
    w_u = w_in_full[:, o_u:o_g]
    w_g = w_in_full[:, o_g:]
    w_main = jnp.concatenate([w_qkv, w_u, w_g], axis=1)
    zqkv = _matmul(n1, w_qkv, name="mm_qkv")
    zu = _matmul(n1, w_u, name="mm_u")
    zg = _matmul(n1, w_g, name="mm_gates")
    zf = _matmul(n1, w_f, name="mm_forget", out_dtype=F32)
    f_t = zf[:, :H].T
    b_col = b_f.reshape(H, 1)
    fcum = _forget_cumsum(f_t, b_col, name="forget_cumsum") * LOG2_E
    fcol = fcum.T
    o, lse, carried = _attn_fwd(zqkv, fcum, fcol, name="attn_fwd", riders=gathers)
    w_ao_full, w_glu_full, w_out_full, w_up_full, w_down_full = carried
    attn = _matmul(o, w_ao_full, name="mm_attn_o")
    y, gy, hin = _ssm_fwd(zu, w_b16, w_c16, a_slab, d_skip, name="ssm_fwd")
    yab = _matmul(gy, w_glu_full, name="mm_glu")
    merged = _merge_fwd(yab, zg, attn, name="merge_fwd")
    h1 = _matmul(merged, w_out_full, name="mm_out", out_dtype=F32, residual=h0)
    n2 = _rms_fwd(h1, g_ffn, name="rms_ffn")
    gu = _matmul(n2, w_up_full, name="mm_up")
    act = _conv_fwd(gu, conv_w8, conv_b, name="conv_fwd")
    h2 = _matmul(act, w_down_full, name="mm_down", out_dtype=F32, residual=h1)

    dh2, sq, dg_final = _final_loss(h2, g_final.reshape(1, D), target, seq, name="final_loss")
    loss = lax.psum(0.5 * sq[0, 0] / D, ("x", "y", "c"))
    dh2_16 = dh2.astype(BF16)
    d_act = _matmul(dh2_16, w_down_full, name="mm_down_dx", trans_b=True)
    dw_down = _matmul(act, dh2_16, name="mm_down_dw", trans_a=True)
    dgc, du2, conv_sums, (x_w_down,) = _conv_bwd_gate(
        d_act, gu, conv_w8, conv_b, name="conv_bwd_gate",
        riders=[_Rider(dw_down, False, False, *w_down.shape[1:])])
    dgu = jnp.concatenate([_conv_bwd_taps(dgc, conv_w8, name="conv_bwd_taps"), du2], axis=1)
    dn2 = _matmul(dgu, w_up_full, name="mm_up_dx", trans_b=True)
    dw_up = _matmul(n2, dgu, name="mm_up_dw", trans_a=True)
    dh1, dg_ffn = _rms_bwd(dn2, h1, g_ffn, dh2, name="rms_ffn_bwd")
    dh1_16 = dh1.astype(BF16)
    dmerged = _matmul(dh1_16, w_out_full, name="mm_out_dx", trans_b=True)
    dw_out = _matmul(merged, dh1_16, name="mm_out_dw", trans_a=True)
    dzg, dattn, dyab = _merge_bwd(dmerged, yab, zg, attn, name="merge_bwd")
    do = _matmul(dattn, w_ao_full, name="mm_attn_o_dx", trans_b=True)
    dw_ao = _matmul(o, dattn, name="mm_attn_o_dw", trans_a=True)
    dgy = _matmul(dyab, w_glu_full, name="mm_glu_dx", trans_b=True)
    dw_glu = _matmul(gy, dyab, name="mm_glu_dw", trans_a=True)
    dzu, dw_b, dw_c, da_slab, dd_skip = _ssm_bwd(zu, dgy, y, hin, w_b16, w_bt16, w_ct16, a_slab, d_skip,
                                                 name="ssm_bwd")
    dq, delta, dfq = _attn_bwd_dq(zqkv, fcum, fcol, do, o, lse, name="attn_bwd_dq")
    dk, dv, dfcum, (x_w_up, x_w_out, x_w_ao, x_w_glu) = _attn_bwd_dkv(
        zqkv, fcum, fcol, do, lse, delta, name="attn_bwd_dkv",
        riders=[_Rider(dw_up, False, True, *w_up.shape[1:]), _Rider(dw_out, False, False, *w_out.shape[1:]),
                _Rider(dw_ao, False, True, *w_attn_o.shape[1:]), _Rider(dw_glu, False, True, *w_glu.shape[1:])])
    df_t, db_f = _forget_bwd(dfcum.T, dfq, f_t, b_col, name="forget_bwd")
    dzf = jnp.pad(df_t.T, ((0, 0), (0, LANES - H))).astype(BF16)
    dz_main = jnp.concatenate([dq, dk, dv, dzu, dzg], axis=1)
    dw_main = _matmul(n1, dz_main, name="mm_in_dw", trans_a=True)
    dw_f = _matmul(n1, dzf, name="mm_forget_dw", trans_a=True)
    dw_in = jnp.concatenate([dw_main[:, :o_f], dw_f[:, :H], dw_main[:, o_f:]], axis=1)
    parts_in = _column_parts(dw_in, w_in.shape[2])
    dn1 = _matmul(dzf, w_f, name="mm_forget_dx", out_dtype=F32, trans_b=True)
    dn1, (x_w_in,) = _matmul(
        dz_main, w_main, name="mm_in_dx", out_dtype=F32, residual=dn1, trans_b=True,
        riders=[_Rider(parts_in.reshape(-1, parts_in.shape[2]), False, False, *parts_in.shape[1:])])
    dh0, dg_mix = _rms_bwd(dn1, h0, g_mix, dh1, name="rms_mix_bwd")
    grad_x = dh0[N_META:L][None]

    dbb_re, dbb_im = _slab_in_grad(dw_b, G, P, C)
    dc_re, dc_im = _slab_out_grad(dw_c, G, C, P)
    da_re = da_slab[:, :SLAB_STATE].reshape(G, P)
    da_im = da_slab[:, SLAB_STATE:].reshape(G, P)
    _, disc_vjp = jax.vjp(_ssm_discretise, lam_re[0], lam_im[0], log_dt[0], b_re[0], b_im[0])
    dlam_re, dlam_im, dlog_dt, db_re, db_im = disc_vjp((da_re, da_im, dbb_re, dbb_im))

    big = {}
    big["meta"] = _update_cols(dh0[:N_META], meta, m_meta, v_meta, name="meta")
    big["conv_w"] = _update_cols(conv_sums[:CONV_WIDTH], conv_w[0], m_conv_w[0], v_conv_w[0], name="conv_w")
    big["w_down"] = _sum_adamw(x_w_down, w_down[0], m_w_down[0], v_w_down[0], name="w_down_adamw")
    big["w_up"] = _sum_adamw(x_w_up, w_up[0], m_w_up[0], v_w_up[0], name="w_up_adamw")
    big["w_out"] = _sum_adamw(x_w_out, w_out[0], m_w_out[0], v_w_out[0], name="w_out_adamw")
    big["w_attn_o"] = _sum_adamw(x_w_ao, w_attn_o[0], m_w_attn_o[0], v_w_attn_o[0], name="w_attn_o_adamw")
    big["w_glu"] = _sum_adamw(x_w_glu, w_glu[0], m_w_glu[0], v_w_glu[0], name="w_glu_adamw")
    big["w_in"] = _padded_adamw(x_w_in, w_in[0], m_w_in[0], v_w_in[0], name="w_in_adamw")

    rep_names = ["g_mix", "b_f", "lam_re", "lam_im", "log_dt", "b_re", "b_im", "c_re", "c_im", "d_skip", "g_ffn",
                 "conv_b", "g_final"]
    rep_w = [g_mix, b_f, lam_re, lam_im, log_dt, b_re, b_im, c_re, c_im, d_skip, g_ffn, conv_b, g_final]
    rep_m = [m_g_mix, m_b_f, m_lam_re, m_lam_im, m_log_dt, m_b_re, m_b_im, m_c_re, m_c_im, m_d_skip, m_g_ffn,
             m_conv_b, m_g_final]
    rep_v = [v_g_mix, v_b_f, v_lam_re, v_lam_im, v_log_dt, v_b_re, v_b_im, v_c_re, v_c_im, v_d_skip, v_g_ffn,
             v_conv_b, v_g_final]
    rep_g = [dg_mix, db_f[:, 0], dlam_re, dlam_im, dlog_dt, db_re, db_im, dc_re, dc_im, dd_skip, dg_ffn,
             conv_sums[CONV_WIDTH], dg_final]
    rep = _update_replicated(rep_g, rep_w, rep_m, rep_v, name="replicated")
    rep_out = {n: [rep[k][i] for k in range(4)] for i, n in enumerate(rep_names)}

    order = ["meta", "g_mix", "w_in", "b_f", "lam_re", "lam_im", "log_dt", "b_re", "b_im", "c_re", "c_im", "d_skip",
             "w_glu", "w_attn_o", "w_out", "g_ffn", "w_up", "conv_w", "conv_b", "w_down", "g_final"]
    outs = [loss, grad_x]
    for kind in range(4):
        for n in order:
            if n in big:
                outs.append(big[n][kind] if n == "meta" else big[n][kind][None])
            else:
                outs.append(rep_out[n][kind])
    return tuple(outs)
```

```python
import collections
import functools
import math

import jax
import jax.numpy as jnp
from jax import lax
from jax.experimental import pallas as pl
from jax.experimental.pallas import tpu as pltpu

F32 = jnp.float32
BF16 = jnp.bfloat16

N_META = 16
EPS = 1e-6
HEAD_DIM = 128
SSM_GROUP = 16
SSM_STATE = 64
GROUPS_PER_SLAB = 8
SLAB_STATE = GROUPS_PER_SLAB * SSM_STATE
CONV_WIDTH = 3
N_DEV = 8

ADAM_LR = 0.001
ADAM_B1 = 0.9
ADAM_B2 = 0.999
ADAM_EPS = 1e-08
ADAM_WD = 0.01
ADAM_STEP = 10

LANES = 128
SUBLANES = 8
VMEM_LIMIT = 52 * 1024 * 1024

SEQ_BLOCK = 768
ATT_BLOCK = 768
ROW_BLOCK = 256
SSM_CHUNK = 256
CONV_ROW_BLOCK = 768
ADAMW_BLOCK_ELEMS = 1 << 17
WGRAD_BLOCK_M = 1408
MASK_VALUE = -1e30
LOG2_E = math.log2(math.e)


def _round_up(n, m):
    return (n + m - 1) // m * m


def _divisor(n, target, mult):
    if n <= target:
        return n
    best = None
    for d in range(mult, target + 1, mult):
        if n % d == 0:
            best = d
    assert best is not None, (n, target, mult)
    return best


def _params(sem):
    return pltpu.CompilerParams(dimension_semantics=sem, vmem_limit_bytes=VMEM_LIMIT)


def _matmul(a, b, *, name, trans_a=False, trans_b=False, out_dtype=None, residual=None, riders=(),
            tm=768, tn=1024, tk=2816):
    out_dtype = BF16 if out_dtype is None else out_dtype
    assert not (trans_a and trans_b)
    if trans_a:
        K, M = a.shape
    else:
        M, K = a.shape
    if trans_b:
        N, K2 = b.shape
    else:
        K2, N = b.shape
    assert K == K2, (a.shape, b.shape)
    if trans_a:
        tm = max(tm, WGRAD_BLOCK_M)
    tm = _divisor(M, tm, LANES if trans_a else SUBLANES)
    tn = _divisor(N, tn, LANES)
    tk = _divisor(K, tk, LANES if not trans_a else SUBLANES)
    nk = K // tk

    def body(*refs):
        if residual is None:
            a_ref, b_ref, o_ref, acc_ref = refs
        else:
            a_ref, b_ref, r_ref, o_ref, acc_ref = refs
        k = pl.program_id(2)

        @pl.when(k == 0)
        def _():
            acc_ref[...] = jnp.zeros_like(acc_ref)

        contract = (0, 0) if trans_a else (1, 1) if trans_b else (1, 0)
        acc_ref[...] += lax.dot_general(a_ref[...], b_ref[...], (((contract[0],), (contract[1],)), ((), ())),
                                        preferred_element_type=F32)

        @pl.when(k == nk - 1)
        def _():
            r = acc_ref[...]
            if residual is not None:
                r = r + r_ref[...]
            o_ref[...] = r.astype(o_ref.dtype)

    if trans_a:
        a_spec = pl.BlockSpec((tk, tm), lambda i, j, k: (k, i))
    else:
        a_spec = pl.BlockSpec((tm, tk), lambda i, j, k: (i, k))
    if trans_b:
        b_spec = pl.BlockSpec((tn, tk), lambda i, j, k: (j, k))
    else:
        b_spec = pl.BlockSpec((tk, tn), lambda i, j, k: (k, j))
    in_specs = [a_spec, b_spec]
    args = [a, b]
    if residual is not None:
        in_specs.append(pl.BlockSpec((tm, tn), lambda i, j, k: (i, j)))
        args.append(residual)
    (out,), carried = _gridded_call(
        name, (M // tm, N // tn, nk), ("parallel", "parallel", "arbitrary"), body, in_specs,
        [pl.BlockSpec((tm, tn), lambda i, j, k: (i, j))], [jax.ShapeDtypeStruct((M, N), out_dtype)],
        [pltpu.VMEM((tm, tn), F32)], args, riders)
    return (out, carried) if riders else out


def _rms_fwd(h, g, *, name):
    T, D = h.shape
    tr = _divisor(T, ROW_BLOCK, SUBLANES)

    def body(h_ref, g_ref, o_ref):
        x = h_ref[...]
        r = lax.rsqrt(jnp.mean(x * x, axis=-1, keepdims=True) + EPS)
        o_ref[...] = (x * r * g_ref[...]).astype(o_ref.dtype)

    return pl.pallas_call(
        body, name=name, grid=(T // tr,),
        in_specs=[pl.BlockSpec((tr, D), lambda i: (i, 0)), pl.BlockSpec((1, D), lambda i: (0, 0))],
        out_specs=pl.BlockSpec((tr, D), lambda i: (i, 0)),
        out_shape=jax.ShapeDtypeStruct((T, D), BF16),
        compiler_params=_params(("parallel",)),
    )(h, g)


def _rms_bwd(dn, h, g, dres, *, name):
    T, D = h.shape
    tr = _divisor(T, ROW_BLOCK, SUBLANES)

    def body(dn_ref, h_ref, g_ref, dres_ref, dh_ref, dg_ref):
        i = pl.program_id(0)

        @pl.when(i == 0)
        def _():
            dg_ref[...] = jnp.zeros_like(dg_ref)

        x = h_ref[...]
        dn_v = dn_ref[...].astype(F32)
        r = lax.rsqrt(jnp.mean(x * x, axis=-1, keepdims=True) + EPS)
        xh = x * r
        dg_ref[...] += jnp.sum(dn_v * xh, axis=0, keepdims=True)
        dxh = dn_v * g_ref[...]
        dh_ref[...] = dres_ref[...] + r * (dxh - xh * jnp.mean(dxh * xh, axis=-1, keepdims=True))

    return pl.pallas_call(
        body, name=name, grid=(T // tr,),
        in_specs=[pl.BlockSpec((tr, D), lambda i: (i, 0)), pl.BlockSpec((tr, D), lambda i: (i, 0)),
                  pl.BlockSpec((1, D), lambda i: (0, 0)), pl.BlockSpec((tr, D), lambda i: (i, 0))],
        out_specs=[pl.BlockSpec((tr, D), lambda i: (i, 0)), pl.BlockSpec((1, D), lambda i: (0, 0))],
        out_shape=[jax.ShapeDtypeStruct((T, D), F32), jax.ShapeDtypeStruct((1, D), F32)],
        compiler_params=_params(("arbitrary",)),
    )(dn, h, g, dres)


def _final_loss(h, g, target, n_valid, *, name):
    T, D = h.shape
    tr = _divisor(T, ROW_BLOCK, SUBLANES)

    def body(h_ref, g_ref, t_ref, dh_ref, sq_ref, dg_ref):
        i = pl.program_id(0)

        @pl.when(i == 0)
        def _():
            sq_ref[...] = jnp.zeros_like(sq_ref)
            dg_ref[...] = jnp.zeros_like(dg_ref)

        x = h_ref[...]
        r = lax.rsqrt(jnp.mean(x * x, axis=-1, keepdims=True) + EPS)
        xh = x * r
        gv = g_ref[...]
        row = i * tr + lax.broadcasted_iota(jnp.int32, (tr, 1), 0)
        valid = (row >= N_META) & (row < N_META + n_valid)
        err = jnp.where(valid, xh * gv - t_ref[...], 0.0)
        sq_ref[...] += jnp.sum(err * err)
        dy = err * (1.0 / D)
        dg_ref[...] += jnp.sum(dy * xh, axis=0, keepdims=True)
        dxh = dy * gv
        dh_ref[...] = r * (dxh - xh * jnp.mean(dxh * xh, axis=-1, keepdims=True))

    return pl.pallas_call(
        body, name=name, grid=(T // tr,),
        in_specs=[pl.BlockSpec((tr, D), lambda i: (i, 0)), pl.BlockSpec((1, D), lambda i: (0, 0)),
                  pl.BlockSpec((tr, D), lambda i: (i, 0))],
        out_specs=[pl.BlockSpec((tr, D), lambda i: (i, 0)), pl.BlockSpec((SUBLANES, LANES), lambda i: (0, 0)),
                   pl.BlockSpec((1, D), lambda i: (0, 0))],
        out_shape=[jax.ShapeDtypeStruct((T, D), F32), jax.ShapeDtypeStruct((SUBLANES, LANES), F32),
                   jax.ShapeDtypeStruct((1, D), F32)],
        compiler_params=_params(("arbitrary",)),
    )(h, g, target)


def _prefix_sum_lanes(x):
    lane = lax.broadcasted_iota(jnp.int32, x.shape, 1)
    d = 1
    while d < LANES:
        x = x + jnp.where(lane >= d, pltpu.roll(x, d, axis=1), 0.0)
        d *= 2
    return x


def _forget_cumsum(ft, bf, *, name):
    H, T = ft.shape
    nb = T // LANES

    def body(f_ref, b_ref, o_ref):
        carry = jnp.zeros((H, 1), F32)
        for j in range(nb):
            sl = pl.ds(j * LANES, LANES)
            lf = jax.nn.log_sigmoid(f_ref[:, sl] + b_ref[...])
            c = _prefix_sum_lanes(lf) + carry
            o_ref[:, sl] = c
            carry = c[:, LANES - 1:LANES]

    return pl.pallas_call(
        body, name=name,
        in_specs=[pl.BlockSpec(memory_space=pltpu.VMEM), pl.BlockSpec(memory_space=pltpu.VMEM)],
        out_specs=pl.BlockSpec(memory_space=pltpu.VMEM),
        out_shape=jax.ShapeDtypeStruct((H, T), F32),
        compiler_params=pltpu.CompilerParams(vmem_limit_bytes=VMEM_LIMIT),
    )(ft, bf)


def _forget_bwd(dF_key, dF_query, ft, bf, *, name):
    H, T = ft.shape
    nb = T // LANES

    def body(d_ref, dq_ref, f_ref, b_ref, o_ref, s_ref):
        carry = jnp.zeros((H, 1), F32)
        acc = jnp.zeros((H, LANES), F32)
        for j in reversed(range(nb)):
            sl = pl.ds(j * LANES, LANES)
            d = d_ref[:, sl] + dq_ref[:, sl]
            pre = _prefix_sum_lanes(d)
            tot = pre[:, LANES - 1:LANES]
            dlf = tot - pre + d + carry
            carry = carry + tot
            z = f_ref[:, sl] + b_ref[...]
            df = dlf * jax.nn.sigmoid(-z)
            o_ref[:, sl] = df
            acc = acc + df
        s_ref[...] = jnp.broadcast_to(jnp.sum(acc, axis=1, keepdims=True), (H, LANES))

    return pl.pallas_call(
        body, name=name,
        in_specs=[pl.BlockSpec(memory_space=pltpu.VMEM)] * 4,
        out_specs=[pl.BlockSpec(memory_space=pltpu.VMEM)] * 2,
        out_shape=[jax.ShapeDtypeStruct((H, T), F32), jax.ShapeDtypeStruct((H, LANES), F32)],
        compiler_params=pltpu.CompilerParams(vmem_limit_bytes=VMEM_LIMIT),
    )(dF_key, dF_query, ft, bf)


def _scores(q_ref, k_ref, fq_ref, fk_ref, h, blk, scale, diagonal):
    hs = pl.ds(h * HEAD_DIM, HEAD_DIM)
    s = lax.dot_general(k_ref[:, hs], q_ref[:, hs], (((1,), (1,)), ((), ())), preferred_element_type=F32)
    s = s * (scale * LOG2_E) + (fq_ref[h:h + 1, :] - fk_ref[:, h:h + 1])
    if diagonal:
        key = lax.broadcasted_iota(jnp.int32, (blk, blk), 0)
        query = lax.broadcasted_iota(jnp.int32, (blk, blk), 1)
        s = jnp.where(key <= query, s, MASK_VALUE)
    return s


def _causal_blocks(i, j, compute):
    @pl.when(j < i)
    def _():
        compute(False)

    @pl.when(j == i)
    def _():
        compute(True)


def _attn_fwd(zqkv, fcum, fcol, *, name, riders=()):
    T = zqkv.shape[0]
    DA = zqkv.shape[1] // 3
    H = DA // HEAD_DIM
    blk = _divisor(T, ATT_BLOCK, LANES)
    nb = T // blk
    scale = HEAD_DIM ** -0.5

    def body(q_ref, k_ref, v_ref, fq_ref, fk_ref, o_ref, lse_ref, m_ref, l_ref, acc_ref):
        i = pl.program_id(0)
        j = pl.program_id(1)

        @pl.when(j == 0)
        def _():
            m_ref[...] = jnp.full_like(m_ref, MASK_VALUE)
            l_ref[...] = jnp.zeros_like(l_ref)
            acc_ref[...] = jnp.zeros_like(acc_ref)

        def compute(diagonal):
            for h in range(H):
                hs = pl.ds(h * HEAD_DIM, HEAD_DIM)
                s = _scores(q_ref, k_ref, fq_ref, fk_ref, h, blk, scale, diagonal)
                m_prev = m_ref[h:h + 1, :]
                m_new = jnp.maximum(m_prev, jnp.max(s, axis=0, keepdims=True))
                alpha = jnp.exp2(m_prev - m_new)
                p = jnp.exp2(s - m_new)
                l_ref[h:h + 1, :] = alpha * l_ref[h:h + 1, :] + jnp.sum(p, axis=0, keepdims=True)
                acc_ref[hs, :] = alpha * acc_ref[hs, :] + lax.dot_general(
                    v_ref[:, hs], p.astype(BF16), (((0,), (0,)), ((), ())), preferred_element_type=F32)
                m_ref[h:h + 1, :] = m_new

        _causal_blocks(i, j, compute)

        @pl.when(j == nb - 1)
        def _():
            for h in range(H):
                hs = pl.ds(h * HEAD_DIM, HEAD_DIM)
                l = l_ref[h:h + 1, :]
                o_ref[:, hs] = (acc_ref[hs, :] / l).T.astype(o_ref.dtype)
                lse_ref[h:h + 1, :] = m_ref[h:h + 1, :] + jnp.log2(l)

    kv = lambda c: (lambda i, j: (jnp.minimum(j, i), c))
    (o, lse), carried = _gridded_call(
        name, (nb, nb), ("parallel", "arbitrary"), body,
        [pl.BlockSpec((blk, DA), lambda i, j: (i, 0)),
         pl.BlockSpec((blk, DA), kv(1)), pl.BlockSpec((blk, DA), kv(2)),
         pl.BlockSpec((H, blk), lambda i, j: (0, i)),
         pl.BlockSpec((blk, H), lambda i, j: (jnp.minimum(j, i), 0))],
        [pl.BlockSpec((blk, DA), lambda i, j: (i, 0)), pl.BlockSpec((H, blk), lambda i, j: (0, i))],
        [jax.ShapeDtypeStruct((T, DA), BF16), jax.ShapeDtypeStruct((H, T), F32)],
        [pltpu.VMEM((H, blk), F32), pltpu.VMEM((H, blk), F32), pltpu.VMEM((DA, blk), F32)],
        [zqkv, zqkv, zqkv, fcum, fcol], riders)
    return o, lse, carried


def _attn_bwd_dq(zqkv, fcum, fcol, do, o, lse, *, name):
    T = zqkv.shape[0]
    DA = zqkv.shape[1] // 3
    H = DA // HEAD_DIM
    blk = _divisor(T, ATT_BLOCK, LANES)
    nb = T // blk
    scale = HEAD_DIM ** -0.5

    def body(q_ref, k_ref, v_ref, fq_ref, fk_ref, do_ref, o_ref, lse_ref, dq_ref, dl_ref, dfq_ref, acc_ref):
        i = pl.program_id(0)
        j = pl.program_id(1)

        @pl.when(j == 0)
        def _():
            acc_ref[...] = jnp.zeros_like(acc_ref)
            dfq_ref[...] = jnp.zeros_like(dfq_ref)
            for h in range(H):
                hs = pl.ds(h * HEAD_DIM, HEAD_DIM)
                prod = do_ref[:, hs].astype(F32) * o_ref[:, hs].astype(F32)
                dl_ref[h:h + 1, :] = jnp.sum(prod.T, axis=0, keepdims=True)

        def compute(diagonal):
            for h in range(H):
                hs = pl.ds(h * HEAD_DIM, HEAD_DIM)
                s = _scores(q_ref, k_ref, fq_ref, fk_ref, h, blk, scale, diagonal)
                p = jnp.exp2(s - lse_ref[h:h + 1, :])
                dp = lax.dot_general(v_ref[:, hs], do_ref[:, hs], (((1,), (1,)), ((), ())),
                                     preferred_element_type=F32)
                ds = p * (dp - dl_ref[h:h + 1, :])
                dfq_ref[h:h + 1, :] += jnp.sum(ds, axis=0, keepdims=True)
                acc_ref[hs, :] += scale * lax.dot_general(k_ref[:, hs], ds.astype(BF16), (((0,), (0,)), ((), ())),
                                                          preferred_element_type=F32)

        _causal_blocks(i, j, compute)

        @pl.when(j == nb - 1)
        def _():
            for h in range(H):
                hs = pl.ds(h * HEAD_DIM, HEAD_DIM)
                dq_ref[:, hs] = acc_ref[hs, :].T.astype(dq_ref.dtype)

    kv = lambda c: (lambda i, j: (jnp.minimum(j, i), c))
    row = lambda i, j: (i, 0)
    lane = lambda i, j: (0, i)
    return pl.pallas_call(
        body, name=name, grid=(nb, nb),
        in_specs=[pl.BlockSpec((blk, DA), row), pl.BlockSpec((blk, DA), kv(1)), pl.BlockSpec((blk, DA), kv(2)),
                  pl.BlockSpec((H, blk), lane), pl.BlockSpec((blk, H), lambda i, j: (jnp.minimum(j, i), 0)),
                  pl.BlockSpec((blk, DA), row), pl.BlockSpec((blk, DA), row), pl.BlockSpec((H, blk), lane)],
        out_specs=[pl.BlockSpec((blk, DA), row), pl.BlockSpec((H, blk), lane), pl.BlockSpec((H, blk), lane)],
        out_shape=[jax.ShapeDtypeStruct((T, DA), BF16), jax.ShapeDtypeStruct((H, T), F32),
                   jax.ShapeDtypeStruct((H, T), F32)],
        scratch_shapes=[pltpu.VMEM((DA, blk), F32)],
        compiler_params=_params(("parallel", "arbitrary")),
    )(zqkv, zqkv, zqkv, fcum, fcol, do, o, lse)


def _attn_bwd_dkv(zqkv, fcum, fcol, do, lse, delta, *, name, riders=()):
    T = zqkv.shape[0]
    DA = zqkv.shape[1] // 3
    H = DA // HEAD_DIM
    blk = _divisor(T, ATT_BLOCK, LANES)
    nb = T // blk
    scale = HEAD_DIM ** -0.5

    def body(q_ref, k_ref, v_ref, fq_ref, fk_ref, do_ref, lse_ref, dl_ref, dk_ref, dv_ref, df_ref,
             dk_acc, dv_acc, df_acc):
        j = pl.program_id(0)
        i = pl.program_id(1)

        @pl.when(i == 0)
        def _():
            dk_acc[...] = jnp.zeros_like(dk_acc)
            dv_acc[...] = jnp.zeros_like(dv_acc)
            df_acc[...] = jnp.zeros_like(df_acc)

        def compute(diagonal):
            for h in range(H):
                hs = pl.ds(h * HEAD_DIM, HEAD_DIM)
                s = _scores(q_ref, k_ref, fq_ref, fk_ref, h, blk, scale, diagonal)
                p = jnp.exp2(s - lse_ref[h:h + 1, :])
                dov = do_ref[:, hs]
                dv_acc[:, hs] += jnp.dot(p.astype(BF16), dov, preferred_element_type=F32)
                dp = lax.dot_general(v_ref[:, hs], dov, (((1,), (1,)), ((), ())), preferred_element_type=F32)
                ds = p * (dp - dl_ref[h:h + 1, :])
                dk_acc[:, hs] += scale * jnp.dot(ds.astype(BF16), q_ref[:, hs], preferred_element_type=F32)
                df_acc[:, h:h + 1] -= jnp.sum(ds, axis=1, keepdims=True)

        _causal_blocks(i, j, compute)

        @pl.when(i == nb - 1)
        def _():
            dk_ref[...] = dk_acc[...].astype(dk_ref.dtype)
            dv_ref[...] = dv_acc[...].astype(dv_ref.dtype)
            df_ref[...] = df_acc[...]

    qrow = lambda j, i: (jnp.maximum(i, j), 0)
    qlane = lambda j, i: (0, jnp.maximum(i, j))
    kcol = lambda c: (lambda j, i: (j, c))
    (dk, dv, df), carried = _gridded_call(
        name, (nb, nb), ("parallel", "arbitrary"), body,
        [pl.BlockSpec((blk, DA), qrow), pl.BlockSpec((blk, DA), kcol(1)), pl.BlockSpec((blk, DA), kcol(2)),
         pl.BlockSpec((H, blk), qlane), pl.BlockSpec((blk, H), kcol(0)),
         pl.BlockSpec((blk, DA), qrow), pl.BlockSpec((H, blk), qlane), pl.BlockSpec((H, blk), qlane)],
        [pl.BlockSpec((blk, DA), kcol(0)), pl.BlockSpec((blk, DA), kcol(0)), pl.BlockSpec((blk, H), kcol(0))],
        [jax.ShapeDtypeStruct((T, DA), BF16), jax.ShapeDtypeStruct((T, DA), BF16),
         jax.ShapeDtypeStruct((T, H), F32)],
        [pltpu.VMEM((blk, DA), F32), pltpu.VMEM((blk, DA), F32), pltpu.VMEM((blk, H), F32)],
        [zqkv, zqkv, zqkv, fcum, fcol, do, lse, delta], riders)
    return dk, dv, df, carried


def _gelu(y):
    c = math.sqrt(2.0 / math.pi)
    return 0.5 * y * (1.0 + jnp.tanh(c * (y + 0.044715 * (y * y * y))))


def _gelu_grad(y):
    c = math.sqrt(2.0 / math.pi)
    th = jnp.tanh(c * (y + 0.044715 * (y * y * y)))
    return 0.5 * (1.0 + th) + 0.5 * y * (1.0 - th * th) * c * (1.0 + 3.0 * 0.044715 * y * y)


STATE_BLOCKS = SLAB_STATE // LANES


def _lane_blocks(ref, lead=()):
    return [ref[lead + (slice(None), pl.ds(b * LANES, LANES))] for b in range(2 * STATE_BLOCKS)]


def _put_lane_blocks(ref, blocks):
    for b, v in enumerate(blocks):
        ref[:, pl.ds(b * LANES, LANES)] = v


def _put_slab(x_ref, first, q, n_slab, chunk, value):
    for b in range(2 * STATE_BLOCKS):
        x_ref[b, pl.ds(first * n_slab + q, chunk, stride=n_slab), :] = value[:, b * LANES:(b + 1) * LANES]


def _get_slab(x_ref, first, q, n_slab, chunk):
    return jnp.concatenate([x_ref[b, pl.ds(first * n_slab + q, chunk, stride=n_slab), :]
                            for b in range(2 * STATE_BLOCKS)], axis=1)


def _ssm_scan_fwd(x_ref, a, h, chunk, n_slab, first=0):
    nb = STATE_BLOCKS

    def step(t, h):
        rows = pl.ds(pl.multiple_of((t + first) * n_slab, n_slab), n_slab)
        out = [None] * (2 * nb)
        for b in range(nb):
            n_re = a[b] * h[b] - a[nb + b] * h[nb + b] + x_ref[b, rows, :]
            n_im = a[b] * h[nb + b] + a[nb + b] * h[b] + x_ref[nb + b, rows, :]
            x_ref[b, rows, :] = n_re
            x_ref[nb + b, rows, :] = n_im
            out[b], out[nb + b] = n_re, n_im
        return tuple(out)

    return lax.fori_loop(0, chunk, step, tuple(h), unroll=4)


def _ssm_fwd(zu, w_b, w_c, a, d_skip, *, name):
    T, DS = zu.shape
    n_slab = DS // LANES
    chunk = _divisor(T, SSM_CHUNK, SUBLANES)
    n_chunk = T // chunk

    def body(u_ref, wb_ref, wc_ref, a_ref, ds_ref, y_ref, gy_ref, hin_ref, x_ref, h_ref):
        k = pl.program_id(0)

        @pl.when(k == 0)
        def _():
            h_ref[...] = jnp.zeros_like(h_ref)

        hin_ref[0] = h_ref[...]
        for q in range(n_slab):
            qs = pl.ds(q * LANES, LANES)
            _put_slab(x_ref, 0, q, n_slab, chunk, jnp.dot(u_ref[:, qs], wb_ref[q], preferred_element_type=F32))
        h = _ssm_scan_fwd(x_ref, _lane_blocks(a_ref), _lane_blocks(h_ref), chunk, n_slab)
        _put_lane_blocks(h_ref, h)
        for q in range(n_slab):
            qs = pl.ds(q * LANES, LANES)
            hq = _get_slab(x_ref, 0, q, n_slab, chunk).astype(BF16)
            y = jnp.dot(hq, wc_ref[q], preferred_element_type=F32) + ds_ref[:, qs] * u_ref[:, qs].astype(F32)
            y_ref[:, qs] = y
            gy_ref[:, qs] = _gelu(y).astype(gy_ref.dtype)

    whole = lambda shape: pl.BlockSpec(shape, lambda k: (0,) * len(shape))
    return pl.pallas_call(
        body, name=name, grid=(n_chunk,),
        in_specs=[pl.BlockSpec((chunk, DS), lambda k: (k, 0)), whole(w_b.shape), whole(w_c.shape),
                  whole(a.shape), whole(d_skip.shape)],
        out_specs=[pl.BlockSpec((chunk, DS), lambda k: (k, 0)), pl.BlockSpec((chunk, DS), lambda k: (k, 0)),
                   pl.BlockSpec((1, n_slab, 2 * SLAB_STATE), lambda k: (k, 0, 0))],
        out_shape=[jax.ShapeDtypeStruct((T, DS), F32), jax.ShapeDtypeStruct((T, DS), BF16),
                   jax.ShapeDtypeStruct((n_chunk, n_slab, 2 * SLAB_STATE), F32)],
        scratch_shapes=[pltpu.VMEM((2 * STATE_BLOCKS, chunk * n_slab, LANES), F32),
                        pltpu.VMEM((n_slab, 2 * SLAB_STATE), F32)],
        compiler_params=_params(("arbitrary",)),
    )(zu, w_b, w_c, a, d_skip)


def _ssm_bwd(zu, dgy, y, hin, w_b, w_bt, w_ct, a, d_skip, *, name):
    T, DS = zu.shape
    n_slab = DS // LANES
    chunk = _divisor(T, SSM_CHUNK, SUBLANES)
    n_chunk = T // chunk
    S = SLAB_STATE

    def body(u_ref, dgy_ref, y_ref, hin_ref, wb_ref, wbt_ref, wct_ref, a_ref, ds_ref,
             du_ref, dwb_ref, dwc_ref, da_ref, dds_ref, hb_ref, gb_ref, dy_ref, g_ref):
        k = pl.program_id(0)

        @pl.when(k == 0)
        def _():
            g_ref[...] = jnp.zeros_like(g_ref)
            dwb_ref[...] = jnp.zeros_like(dwb_ref)
            dwc_ref[...] = jnp.zeros_like(dwc_ref)
            da_ref[...] = jnp.zeros_like(da_ref)
            dds_ref[...] = jnp.zeros_like(dds_ref)

        nb = STATE_BLOCKS
        a = _lane_blocks(a_ref)
        hin = _lane_blocks(hin_ref, lead=(0,))

        for b in range(2 * nb):
            hb_ref[b, pl.ds(0, n_slab), :] = hin[b]
        dy_ref[...] = dgy_ref[...].astype(F32) * _gelu_grad(y_ref[...])
        for q in range(n_slab):
            qs = pl.ds(q * LANES, LANES)
            _put_slab(hb_ref, 1, q, n_slab, chunk, jnp.dot(u_ref[:, qs], wb_ref[q], preferred_element_type=F32))
            _put_slab(gb_ref, 0, q, n_slab, chunk,
                      jnp.dot(dy_ref[:, qs].astype(BF16), wct_ref[q], preferred_element_type=F32))
        _ssm_scan_fwd(hb_ref, a, hin, chunk, n_slab, first=1)

        def step(s, carry):
            g, da = carry[:2 * nb], carry[2 * nb:]
            t = chunk - 1 - s
            rows = pl.ds(pl.multiple_of(t * n_slab, n_slab), n_slab)
            g_out, da_out = [None] * (2 * nb), [None] * (2 * nb)
            for b in range(nb):
                n_re = gb_ref[b, rows, :] + a[b] * g[b] + a[nb + b] * g[nb + b]
                n_im = gb_ref[nb + b, rows, :] + a[b] * g[nb + b] - a[nb + b] * g[b]
                gb_ref[b, rows, :] = n_re
                gb_ref[nb + b, rows, :] = n_im
                p_re = hb_ref[b, rows, :]
                p_im = hb_ref[nb + b, rows, :]
                g_out[b], g_out[nb + b] = n_re, n_im
                da_out[b] = da[b] + n_re * p_re + n_im * p_im
                da_out[nb + b] = da[nb + b] + n_im * p_re - n_re * p_im
            return tuple(g_out) + tuple(da_out)

        zero = jnp.zeros((n_slab, LANES), F32)
        carry = lax.fori_loop(0, chunk, step, tuple(_lane_blocks(g_ref)) + (zero,) * (2 * nb), unroll=4)
        _put_lane_blocks(g_ref, carry[:2 * nb])
        for b in range(2 * nb):
            da_ref[:, pl.ds(b * LANES, LANES)] += carry[2 * nb + b]

        for q in range(n_slab):
            qs = pl.ds(q * LANES, LANES)
            uq = u_ref[:, qs]
            dy = dy_ref[:, qs]
            hq = _get_slab(hb_ref, 1, q, n_slab, chunk).astype(BF16)
            gq = _get_slab(gb_ref, 0, q, n_slab, chunk).astype(BF16)
            dwc_ref[q] += lax.dot_general(hq, dy.astype(BF16), (((0,), (0,)), ((), ())), preferred_element_type=F32)
            dwb_ref[q] += lax.dot_general(uq, gq, (((0,), (0,)), ((), ())), preferred_element_type=F32)
            du_ref[:, qs] = (jnp.dot(gq, wbt_ref[q], preferred_element_type=F32) + ds_ref[:, qs] * dy).astype(du_ref.dtype)
            dds_ref[:, qs] += jnp.sum(dy * uq.astype(F32), axis=0, keepdims=True)

    whole = lambda shape: pl.BlockSpec(shape, lambda k: (0,) * len(shape))
    rev = lambda k: (n_chunk - 1 - k, 0)
    return pl.pallas_call(
        body, name=name, grid=(n_chunk,),
        in_specs=[pl.BlockSpec((chunk, DS), rev), pl.BlockSpec((chunk, DS), rev), pl.BlockSpec((chunk, DS), rev),
                  pl.BlockSpec((1, n_slab, 2 * S), lambda k: (n_chunk - 1 - k, 0, 0)),
                  whole(w_b.shape), whole(w_bt.shape), whole(w_ct.shape), whole(a.shape), whole(d_skip.shape)],
        out_specs=[pl.BlockSpec((chunk, DS), rev), whole(w_b.shape), whole(w_bt.shape), whole(a.shape),
                   whole(d_skip.shape)],
        out_shape=[jax.ShapeDtypeStruct((T, DS), BF16), jax.ShapeDtypeStruct(w_b.shape, F32),
                   jax.ShapeDtypeStruct(w_bt.shape, F32), jax.ShapeDtypeStruct(a.shape, F32),
                   jax.ShapeDtypeStruct(d_skip.shape, F32)],
        scratch_shapes=[pltpu.VMEM((2 * STATE_BLOCKS, (chunk + 1) * n_slab, LANES), F32),
                        pltpu.VMEM((2 * STATE_BLOCKS, chunk * n_slab, LANES), F32),
                        pltpu.VMEM((chunk, DS), F32), pltpu.VMEM((n_slab, 2 * S), F32)],
        compiler_params=_params(("arbitrary",)),
    )(zu, dgy, y, hin, w_b, w_bt, w_ct, a, d_skip)


def _ssm_discretise(lam_re, lam_im, log_dt, b_re, b_im):
    dt = jnp.exp(log_dt)[:, None]
    mag = jnp.exp(lam_re * dt)
    a_re = mag * jnp.cos(lam_im * dt)
    a_im = mag * jnp.sin(lam_im * dt)
    den = lam_re * lam_re + lam_im * lam_im
    nr = a_re - 1.0
    z_re = (nr * lam_re + a_im * lam_im) / den
    z_im = (a_im * lam_re - nr * lam_im) / den
    bb_re = z_re[..., None] * b_re - z_im[..., None] * b_im
    bb_im = z_re[..., None] * b_im + z_im[..., None] * b_re
    return a_re, a_im, bb_re, bb_im


def _slab_in(m_re, m_im):
    G, P, C = m_re.shape
    n_slab = G // GROUPS_PER_SLAB
    eye = jnp.eye(GROUPS_PER_SLAB, dtype=m_re.dtype)

    def one(m):
        m = m.reshape(n_slab, GROUPS_PER_SLAB, P, C)
        w = jnp.einsum('sgpc,gh->sgchp', m, eye)
        return w.reshape(n_slab, GROUPS_PER_SLAB * C, GROUPS_PER_SLAB * P)

    return jnp.concatenate([one(m_re), one(m_im)], axis=2)


def _slab_in_grad(dw, G, P, C):
    n_slab = G // GROUPS_PER_SLAB
    eye = jnp.eye(GROUPS_PER_SLAB, dtype=dw.dtype)

    def one(w):
        w = w.reshape(n_slab, GROUPS_PER_SLAB, C, GROUPS_PER_SLAB, P)
        return jnp.einsum('sgchp,gh->sgpc', w, eye).reshape(G, P, C)

    return one(dw[:, :, :SLAB_STATE]), one(dw[:, :, SLAB_STATE:])


def _slab_out(c_re, c_im):
    G, C, P = c_re.shape
    n_slab = G // GROUPS_PER_SLAB
    eye = jnp.eye(GROUPS_PER_SLAB, dtype=c_re.dtype)

    def one(m):
        m = m.reshape(n_slab, GROUPS_PER_SLAB, C, P)
        w = jnp.einsum('sgcp,gh->shpgc', m, eye)
        return w.reshape(n_slab, GROUPS_PER_SLAB * P, GROUPS_PER_SLAB * C)

    return jnp.concatenate([one(c_re), one(-c_im)], axis=1)


def _slab_out_grad(dw, G, C, P):
    n_slab = G // GROUPS_PER_SLAB
    eye = jnp.eye(GROUPS_PER_SLAB, dtype=dw.dtype)

    def one(w):
        w = w.reshape(n_slab, GROUPS_PER_SLAB, P, GROUPS_PER_SLAB, C)
        return jnp.einsum('shpgc,gh->sgcp', w, eye).reshape(G, C, P)

    return one(dw[:, :SLAB_STATE, :]), -one(dw[:, SLAB_STATE:, :])


def _slab_diag(a_re, a_im):
    G, P = a_re.shape
    n_slab = G // GROUPS_PER_SLAB
    return jnp.concatenate([a_re.reshape(n_slab, SLAB_STATE), a_im.reshape(n_slab, SLAB_STATE)], axis=1)


def _merge_fwd(yab, zg, attn, *, name):
    T, D = attn.shape
    tr = _divisor(T, ROW_BLOCK, SUBLANES)

    def body(ya_ref, yb_ref, ga_ref, gb_ref, at_ref, o_ref):
        f = lambda r: r[...].astype(F32)
        ssm = f(ya_ref) * jax.nn.sigmoid(f(yb_ref))
        o_ref[...] = (jax.nn.sigmoid(f(ga_ref)) * ssm + jax.nn.sigmoid(f(gb_ref)) * f(at_ref)).astype(o_ref.dtype)

    lo = pl.BlockSpec((tr, D), lambda i: (i, 0))
    hi = pl.BlockSpec((tr, D), lambda i: (i, 1))
    return pl.pallas_call(
        body, name=name, grid=(T // tr,),
        in_specs=[lo, hi, lo, hi, lo],
        out_specs=lo,
        out_shape=jax.ShapeDtypeStruct((T, D), BF16),
        compiler_params=_params(("parallel",)),
    )(yab, yab, zg, zg, attn)


def _merge_bwd(dm, yab, zg, attn, *, name):
    T, D = attn.shape
    tr = _divisor(T, ROW_BLOCK, SUBLANES)

    def body(dm_ref, ya_ref, yb_ref, ga_ref, gb_ref, at_ref, dg_ref, dat_ref, dy_ref):
        f = lambda r: r[...].astype(F32)
        dmv, ya, at = f(dm_ref), f(ya_ref), f(at_ref)
        sa, sb, syb = jax.nn.sigmoid(f(ga_ref)), jax.nn.sigmoid(f(gb_ref)), jax.nn.sigmoid(f(yb_ref))
        ssm = ya * syb
        dssm = dmv * sa
        dg_ref[:, pl.ds(0, D)] = (dmv * ssm * sa * (1.0 - sa)).astype(dg_ref.dtype)
        dg_ref[:, pl.ds(D, D)] = (dmv * at * sb * (1.0 - sb)).astype(dg_ref.dtype)
        dat_ref[...] = (dmv * sb).astype(dat_ref.dtype)
        dy_ref[:, pl.ds(0, D)] = (dssm * syb).astype(dy_ref.dtype)
        dy_ref[:, pl.ds(D, D)] = (dssm * ya * syb * (1.0 - syb)).astype(dy_ref.dtype)

    lo = pl.BlockSpec((tr, D), lambda i: (i, 0))
    hi = pl.BlockSpec((tr, D), lambda i: (i, 1))
    both = pl.BlockSpec((tr, 2 * D), lambda i: (i, 0))
    return pl.pallas_call(
        body, name=name, grid=(T // tr,),
        in_specs=[lo, lo, hi, lo, hi, lo],
        out_specs=[both, lo, both],
        out_shape=[jax.ShapeDtypeStruct((T, 2 * D), BF16), jax.ShapeDtypeStruct((T, D), BF16),
                   jax.ShapeDtypeStruct((T, 2 * D), BF16)],
        compiler_params=_params(("parallel",)),
    )(dm, yab, yab, zg, zg, attn)


def _conv_taps(g_ref, halo_ref, i, tr):
    g0 = g_ref[...].astype(F32)
    halo = jnp.where(i > 0, halo_ref[...].astype(F32), 0.0)
    row = lax.broadcasted_iota(jnp.int32, g0.shape, 0)
    g1 = jnp.where(row == 0, halo[SUBLANES - 1:SUBLANES, :], pltpu.roll(g0, 1, axis=0))
    g2 = pltpu.roll(g0, 2, axis=0)
    g2 = jnp.where(row == 0, halo[SUBLANES - 2:SUBLANES - 1, :], g2)
    g2 = jnp.where(row == 1, halo[SUBLANES - 1:SUBLANES, :], g2)
    return g0, g1, g2


def _conv_blocks(T, FF):
    tr = _divisor(T, CONV_ROW_BLOCK, SUBLANES)
    tc = _divisor(FF, 1024, LANES)
    return tr, tc


def _conv_fwd(gu, conv_w, conv_b, *, name):
    T = gu.shape[0]
    FF = gu.shape[1] // 2
    tr, tc = _conv_blocks(T, FF)
    ncol = FF // tc

    def body(g_ref, halo_ref, u_ref, w_ref, b_ref, o_ref):
        i = pl.program_id(0)
        g0, g1, g2 = _conv_taps(g_ref, halo_ref, i, tr)
        gc = b_ref[...] + w_ref[0:1, :] * g2 + w_ref[1:2, :] * g1 + w_ref[2:3, :] * g0
        o_ref[...] = (gc * jax.nn.sigmoid(gc) * u_ref[...].astype(F32)).astype(o_ref.dtype)

    hb = tr // SUBLANES
    return pl.pallas_call(
        body, name=name, grid=(T // tr, ncol),
        in_specs=[pl.BlockSpec((tr, tc), lambda i, j: (i, j)),
                  pl.BlockSpec((SUBLANES, tc), lambda i, j: (jnp.maximum(i * hb - 1, 0), j)),
                  pl.BlockSpec((tr, tc), lambda i, j: (i, j + ncol)),
                  pl.BlockSpec((SUBLANES, tc), lambda i, j: (0, j)), pl.BlockSpec((1, tc), lambda i, j: (0, j))],
        out_specs=pl.BlockSpec((tr, tc), lambda i, j: (i, j)),
        out_shape=jax.ShapeDtypeStruct((T, FF), BF16),
        compiler_params=_params(("parallel", "parallel")),
    )(gu, gu, gu, conv_w, conv_b)


def _conv_bwd_gate(da, gu, conv_w, conv_b, *, name, riders=()):
    T = gu.shape[0]
    FF = gu.shape[1] // 2
    tr, tc = _conv_blocks(T, FF)
    ncol = FF // tc

    def body(da_ref, g_ref, halo_ref, u_ref, w_ref, b_ref, dgc_ref, du_ref, s_ref):
        i = pl.program_id(1)

        @pl.when(i == 0)
        def _():
            s_ref[...] = jnp.zeros_like(s_ref)

        g0, g1, g2 = _conv_taps(g_ref, halo_ref, i, tr)
        gc = b_ref[...] + w_ref[0:1, :] * g2 + w_ref[1:2, :] * g1 + w_ref[2:3, :] * g0
        sg = jax.nn.sigmoid(gc)
        dav = da_ref[...].astype(F32)
        du_ref[...] = (dav * gc * sg).astype(du_ref.dtype)
        dgc = dav * u_ref[...].astype(F32) * (sg * (1.0 + gc * (1.0 - sg)))
        dgc_ref[...] = dgc.astype(dgc_ref.dtype)
        s_ref[0:1, :] += jnp.sum(dgc * g2, axis=0, keepdims=True)
        s_ref[1:2, :] += jnp.sum(dgc * g1, axis=0, keepdims=True)
        s_ref[2:3, :] += jnp.sum(dgc * g0, axis=0, keepdims=True)
        s_ref[3:4, :] += jnp.sum(dgc, axis=0, keepdims=True)

    hb = tr // SUBLANES
    blk = pl.BlockSpec((tr, tc), lambda j, i: (i, j))
    (dgc, du, sums), carried = _gridded_call(
        name, (ncol, T // tr), ("parallel", "arbitrary"), body,
        [blk, blk,
         pl.BlockSpec((SUBLANES, tc), lambda j, i: (jnp.maximum(i * hb - 1, 0), j)),
         pl.BlockSpec((tr, tc), lambda j, i: (i, j + ncol)),
         pl.BlockSpec((SUBLANES, tc), lambda j, i: (0, j)), pl.BlockSpec((1, tc), lambda j, i: (0, j))],
        [blk, blk, pl.BlockSpec((SUBLANES, tc), lambda j, i: (0, j))],
        [jax.ShapeDtypeStruct((T, FF), BF16), jax.ShapeDtypeStruct((T, FF), BF16),
         jax.ShapeDtypeStruct((SUBLANES, FF), F32)],
        [], [da, gu, gu, gu, conv_w, conv_b], riders)
    return dgc, du, sums, carried


def _conv_bwd_taps(dgc, conv_w, *, name):
    T, FF = dgc.shape
    tr, tc = _conv_blocks(T, FF)
    ncol = FF // tc
    nrow = T // tr

    def body(d_ref, next_ref, w_ref, o_ref):
        i = pl.program_id(0)
        d0 = d_ref[...].astype(F32)
        nxt = jnp.where(i < nrow - 1, next_ref[...].astype(F32), 0.0)
        row = lax.broadcasted_iota(jnp.int32, d0.shape, 0)
        d1 = jnp.where(row == tr - 1, nxt[0:1, :], pltpu.roll(d0, tr - 1, axis=0))
        d2 = pltpu.roll(d0, tr - 2, axis=0)
        d2 = jnp.where(row == tr - 2, nxt[0:1, :], d2)
        d2 = jnp.where(row == tr - 1, nxt[1:2, :], d2)
        dg = w_ref[2:3, :] * d0 + w_ref[1:2, :] * d1 + w_ref[0:1, :] * d2
        o_ref[...] = dg.astype(o_ref.dtype)

    hb = tr // SUBLANES
    last = T // SUBLANES - 1
    return pl.pallas_call(
        body, name=name, grid=(nrow, ncol),
        in_specs=[pl.BlockSpec((tr, tc), lambda i, j: (i, j)),
                  pl.BlockSpec((SUBLANES, tc), lambda i, j: (jnp.minimum((i + 1) * hb, last), j)),
                  pl.BlockSpec((SUBLANES, tc), lambda i, j: (0, j))],
        out_specs=pl.BlockSpec((tr, tc), lambda i, j: (i, j)),
        out_shape=jax.ShapeDtypeStruct((T, FF), BF16),
        compiler_params=_params(("parallel", "parallel")),
    )(dgc, dgc, conv_w)


def _mesh_pos():
    return lax.axis_index("x"), lax.axis_index("y"), lax.axis_index("c")


def _flip(pos, mask):
    x, y, c = pos
    return (x ^ ((mask >> 2) & 1), y ^ ((mask >> 1) & 1), c ^ (mask & 1))


def _index_of(pos):
    x, y, c = pos
    return 4 * x + 2 * y + c


class _Rider(collections.namedtuple("_Rider", "src gather cols R c")):
    def out_shape(self):
        if not self.gather:
            shape = (N_DEV, self.R, self.c)
        elif self.cols:
            shape = (self.R, N_DEV * self.c)
        else:
            shape = (N_DEV * self.R, self.c)
        return jax.ShapeDtypeStruct(shape, self.src.dtype)

    def slab(self, ref, idx):
        if self.cols:
            return ref.at[:, pl.ds(pl.multiple_of(idx * self.c, LANES), self.c)]
        return ref.at[pl.ds(pl.multiple_of(idx * self.R, 2 * SUBLANES), self.R), :]

    def copy(self, src_ref, dst_ref, send_sems, recv_sems, me, k, arriving):
        peer = _flip(me, k)
        owner = _index_of(peer if arriving else me)
        if self.gather:
            src, dst = src_ref, self.slab(dst_ref, owner)
        else:
            src, dst = self.slab(src_ref, _index_of(peer)), dst_ref.at[owner]
        return pltpu.make_async_remote_copy(src_ref=src, dst_ref=dst, send_sem=send_sems.at[k - 1],
                                            recv_sem=recv_sems.at[k - 1], device_id=peer,
                                            device_id_type=pl.DeviceIdType.MESH)

    def own(self, src_ref, dst_ref, local_sem, me):
        my = _index_of(me)
        if self.gather:
            return pltpu.make_async_copy(src_ref, self.slab(dst_ref, my), local_sem)
        return pltpu.make_async_copy(self.slab(src_ref, my), dst_ref.at[my], local_sem)

    def start(self, src_ref, dst_ref, send_sems, recv_sems, local_sem):
        me = _mesh_pos()
        self.own(src_ref, dst_ref, local_sem, me).start()
        for k in range(1, N_DEV):
            self.copy(src_ref, dst_ref, send_sems, recv_sems, me, k, False).start()

    def wait(self, src_ref, dst_ref, send_sems, recv_sems, local_sem):
        me = _mesh_pos()
        for k in range(1, N_DEV):
            self.copy(src_ref, dst_ref, send_sems, recv_sems, me, k, True).wait_recv()
        for k in range(1, N_DEV):
            self.copy(src_ref, dst_ref, send_sems, recv_sems, me, k, False).wait_send()
        self.own(src_ref, dst_ref, local_sem, me).wait()


_RIDER_SEMS = [pltpu.SemaphoreType.DMA((N_DEV - 1,)), pltpu.SemaphoreType.DMA((N_DEV - 1,)), pltpu.SemaphoreType.DMA]
_ANY = pl.BlockSpec(memory_space=pl.ANY)


def _comm_call(riders, *, name):
    n = len(riders)

    def body(*refs):
        srcs, dsts, sems = refs[:n], refs[n:2 * n], refs[2 * n:]
        for r, rider in enumerate(riders):
            rider.start(srcs[r], dsts[r], *sems[3 * r:3 * r + 3])
        for r, rider in enumerate(riders):
            rider.wait(srcs[r], dsts[r], *sems[3 * r:3 * r + 3])

    return pl.pallas_call(
        body, name=name,
        in_specs=[_ANY] * n, out_specs=[_ANY] * n,
        out_shape=[rider.out_shape() for rider in riders],
        scratch_shapes=_RIDER_SEMS * n,
        compiler_params=pltpu.CompilerParams(has_side_effects=True),
    )(*[rider.src for rider in riders])


def _carry(riders, grid, body, in_specs, out_specs, out_shape, scratch_shapes, args):
    n, n_in, n_out, n_scratch = len(riders), len(in_specs), len(out_specs), len(scratch_shapes)

    def carrying(*refs):
        ins, refs = refs[:n_in], refs[n_in:]
        srcs, refs = refs[:n], refs[n:]
        outs, refs = refs[:n_out], refs[n_out:]
        dsts, refs = refs[:n], refs[n:]
        scratch, sems = refs[:n_scratch], refs[n_scratch:]
        ids = [pl.program_id(a) for a in range(len(grid))]
        first = functools.reduce(jnp.logical_and, [i == 0 for i in ids])
        last = functools.reduce(jnp.logical_and, [i == g - 1 for i, g in zip(ids, grid)])

        @pl.when(first)
        def _():
            for r, rider in enumerate(riders):
                rider.start(srcs[r], dsts[r], *sems[3 * r:3 * r + 3])

        body(*ins, *outs, *scratch)

        @pl.when(last)
        def _():
            for r, rider in enumerate(riders):
                rider.wait(srcs[r], dsts[r], *sems[3 * r:3 * r + 3])

    return dict(
        body=carrying,
        in_specs=list(in_specs) + [_ANY] * n,
        out_specs=list(out_specs) + [_ANY] * n,
        out_shape=list(out_shape) + [rider.out_shape() for rider in riders],
        scratch_shapes=list(scratch_shapes) + _RIDER_SEMS * n,
        args=list(args) + [rider.src for rider in riders])


def _gridded_call(name, grid, semantics, body, in_specs, out_specs, out_shape, scratch_shapes, args, riders=()):
    call = dict(body=body, in_specs=in_specs, out_specs=out_specs, out_shape=out_shape,
                scratch_shapes=scratch_shapes, args=args)
    if riders:
        call = _carry(list(riders), grid, **call)
        semantics = ("arbitrary",) * len(grid)
    outs = pl.pallas_call(
        call["body"], name=name, grid=grid, in_specs=call["in_specs"], out_specs=call["out_specs"],
        out_shape=call["out_shape"], scratch_shapes=call["scratch_shapes"],
        compiler_params=_params(semantics),
    )(*call["args"])
    n_out = len(out_shape)
    return list(outs[:n_out]), list(outs[n_out:])


def _all_gather(x, *, name):
    R, C = x.shape
    return _comm_call([_Rider(x, True, False, R, C)], name=name)[0].reshape(N_DEV, R, C)


def _exchange(parts, *, name):
    _, R, C = parts.shape
    return _comm_call([_Rider(parts.reshape(N_DEV * R, C), False, False, R, C)], name=name)[0]


def _sum_adamw(parts, w, m, v, *, name):
    R, C = w.shape
    tr = _divisor(R, max(2 * SUBLANES, ADAMW_BLOCK_ELEMS // C // SUBLANES * SUBLANES), 2 * SUBLANES)
    c1 = 1.0 - ADAM_B1 ** ADAM_STEP
    c2 = 1.0 - ADAM_B2 ** ADAM_STEP

    def body(p_ref, w_ref, m_ref, v_ref, g_ref, d_ref, nm_ref, nv_ref):
        g = p_ref[0].astype(F32)
        for s in range(1, N_DEV):
            g = g + p_ref[s].astype(F32)
        nm = ADAM_B1 * m_ref[...] + (1.0 - ADAM_B1) * g
        nv = ADAM_B2 * v_ref[...] + (1.0 - ADAM_B2) * (g * g)
        g_ref[...] = g
        nm_ref[...] = nm
        nv_ref[...] = nv
        d_ref[...] = -ADAM_LR * ((nm / c1) / (jnp.sqrt(nv / c2) + ADAM_EPS) + ADAM_WD * w_ref[...])

    blk = pl.BlockSpec((tr, C), lambda i: (i, 0))
    return pl.pallas_call(
        body, name=name, grid=(R // tr,),
        in_specs=[pl.BlockSpec((N_DEV, tr, C), lambda i: (0, i, 0)), blk, blk, blk],
        out_specs=[blk] * 4,
        out_shape=[jax.ShapeDtypeStruct((R, C), F32)] * 4,
        compiler_params=_params(("parallel",)),
    )(parts, w, m, v)


def _pad2(a, rows, cols):
    return jnp.pad(a, ((0, rows - a.shape[0]), (0, cols - a.shape[1])))


def _gather_cols(w, dtype, *, name):
    R, c = w.shape
    rp, cp = _round_up(R, 2 * SUBLANES), _round_up(c, LANES)
    g = _all_gather(_pad2(w.astype(dtype), rp, cp), name=name)
    return jnp.transpose(g[:, :R, :c], (1, 0, 2)).reshape(R, N_DEV * c)


def _column_parts(dw, c):
    R = dw.shape[0]
    parts = jnp.transpose(dw.reshape(R, N_DEV, c), (1, 0, 2))
    return jnp.pad(parts, ((0, 0), (0, _round_up(R, 2 * SUBLANES) - R), (0, _round_up(c, LANES) - c)))


def _padded_adamw(got, w, m, v, *, name):
    R, c = w.shape
    rp, cp = got.shape[1:]
    outs = _sum_adamw(got, _pad2(w, rp, cp), _pad2(m, rp, cp), _pad2(v, rp, cp), name=name)
    return [o[:R, :c] for o in outs]


def _update_cols(dw, w, m, v, *, name):
    got = _exchange(_column_parts(dw, w.shape[1]), name=name + "_exchange")
    return _padded_adamw(got, w, m, v, name=name + "_adamw")


def _update_replicated(grads, ws, ms, vs, *, name):
    sizes = [int(g.size) for g in grads]
    total = sum(sizes)
    rows = _round_up(-(-total // LANES), 2 * SUBLANES)

    def pack(arrs):
        flat = jnp.concatenate([a.reshape(-1).astype(F32) for a in arrs])
        return jnp.pad(flat, (0, rows * LANES - total)).reshape(rows, LANES)

    got = _all_gather(pack(grads), name=name + "_gather")
    outs = _sum_adamw(got, pack(ws), pack(ms), pack(vs), name=name + "_adamw")
    result = []
    for o in outs:
        flat = o.reshape(-1)
        arrs, off = [], 0
        for w, n in zip(ws, sizes):
            arrs.append(flat[off:off + n].reshape(w.shape))
            off += n
        result.append(arrs)
    return result


def kernel(x, meta, g_mix, w_in, b_f, lam_re, lam_im, log_dt, b_re, b_im, c_re, c_im, d_skip, w_glu, w_attn_o, w_out, g_ffn, w_up, conv_w, conv_b, w_down, g_final, loss_target, m_meta, m_g_mix, m_w_in, m_b_f, m_lam_re, m_lam_im, m_log_dt, m_b_re, m_b_im, m_c_re, m_c_im, m_d_skip, m_w_glu, m_w_attn_o, m_w_out, m_g_ffn, m_w_up, m_conv_w, m_conv_b, m_w_down, m_g_final, v_meta, v_g_mix, v_w_in, v_b_f, v_lam_re, v_lam_im, v_log_dt, v_b_re, v_b_im, v_c_re, v_c_im, v_d_skip, v_w_glu, v_w_attn_o, v_w_out, v_g_ffn, v_w_up, v_conv_w, v_conv_b, v_w_down, v_g_final):
    seq, D = x.shape[1], x.shape[2]
    L = N_META + seq
    T = _round_up(L, SEQ_BLOCK) if L > SEQ_BLOCK else _round_up(L, LANES)
    DS = d_skip.shape[1]
    H = b_f.shape[1]
    DA = H * HEAD_DIM
    FF = conv_b.shape[1]
    G, P, C = b_re.shape[1:]

    meta_full = _gather_cols(meta, F32, name="gather_meta")
    conv_w_full = _gather_cols(conv_w[0], F32, name="gather_conv_w")
    w_in_full = _gather_cols(w_in[0], BF16, name="gather_w_in")
    gathers = [_Rider(w[0].astype(BF16), True, cols, *w.shape[1:])
               for w, cols in ((w_attn_o, True), (w_glu, True), (w_out, False), (w_up, True), (w_down, False))]
    conv_w8 = jnp.pad(conv_w_full, ((0, SUBLANES - CONV_WIDTH), (0, 0)))

    a_re, a_im, bb_re, bb_im = _ssm_discretise(lam_re[0], lam_im[0], log_dt[0], b_re[0], b_im[0])
    w_b = _slab_in(bb_re, bb_im)
    w_c = _slab_out(c_re[0], c_im[0])
    a_slab = _slab_diag(a_re, a_im)
    w_b16, w_c16 = w_b.astype(BF16), w_c.astype(BF16)
    w_bt16, w_ct16 = jnp.swapaxes(w_b16, 1, 2), jnp.swapaxes(w_c16, 1, 2)

    h0 = jnp.concatenate([meta_full, x[0], jnp.zeros((T - L, D), F32)], axis=0)
    target = jnp.pad(loss_target[0], ((N_META, T - L), (0, 0)))
    n1 = _rms_fwd(h0, g_mix, name="rms_mix")
    o_f, o_u, o_g = 3 * DA, 3 * DA + H, 3 * DA + H + DS
    w_qkv = w_in_full[:, :o_f]
    w_f = jnp.pad(w_in_full[:, o_f:o_u], ((0, 0), (0, LANES - H)))
    w_u = w_in_full[:, o_u:o_g]
    w_g = w_in_full[:, o_g:]
    w_main = jnp.concatenate([w_qkv, w_u, w_g], axis=1)
    zqkv = _matmul(n1, w_qkv, name="mm_qkv")
    zu = _matmul(n1, w_u, name="mm_u")
    zg = _matmul(n1, w_g, name="mm_gates")
    zf = _matmul(n1, w_f, name="mm_forget", out_dtype=F32)
    f_t = zf[:, :H].T
    b_col = b_f.reshape(H, 1)
    fcum = _forget_cumsum(f_t, b_col, name="forget_cumsum") * LOG2_E
    fcol = fcum.T
    o, lse, carried = _attn_fwd(zqkv, fcum, fcol, name="attn_fwd", riders=gathers)
    w_ao_full, w_glu_full, w_out_full, w_up_full, w_down_full = carried
    attn = _matmul(o, w_ao_full, name="mm_attn_o")
    y, gy, hin = _ssm_fwd(zu, w_b16, w_c16, a_slab, d_skip, name="ssm_fwd")
    yab = _matmul(gy, w_glu_full, name="mm_glu")
    merged = _merge_fwd(yab, zg, attn, name="merge_fwd")
    h1 = _matmul(merged, w_out_full, name="mm_out", out_dtype=F32, residual=h0)
    n2 = _rms_fwd(h1, g_ffn, name="rms_ffn")
    gu = _matmul(n2, w_up_full, name="mm_up")
    act = _conv_fwd(gu, conv_w8, conv_b, name="conv_fwd")
    h2 = _matmul(act, w_down_full, name="mm_down", out_dtype=F32, residual=h1)

    dh2, sq, dg_final = _final_loss(h2, g_final.reshape(1, D), target, seq, name="final_loss")
    loss = lax.psum(0.5 * sq[0, 0] / D, ("x", "y", "c"))
    dh2_16 = dh2.astype(BF16)
    d_act = _matmul(dh2_16, w_down_full, name="mm_down_dx", trans_b=True)
    dw_down = _matmul(act, dh2_16, name="mm_down_dw", trans_a=True)
    dgc, du2, conv_sums, (x_w_down,) = _conv_bwd_gate(
        d_act, gu, conv_w8, conv_b, name="conv_bwd_gate",
        riders=[_Rider(dw_down, False, False, *w_down.shape[1:])])
    dgu = jnp.concatenate([_conv_bwd_taps(dgc, conv_w8, name="conv_bwd_taps"), du2], axis=1)
    dn2 = _matmul(dgu, w_up_full, name="mm_up_dx", trans_b=True)
    dw_up = _matmul(n2, dgu, name="mm_up_dw", trans_a=True)
    dh1, dg_ffn = _rms_bwd(dn2, h1, g_ffn, dh2, name="rms_ffn_bwd")
    dh1_16 = dh1.astype(BF16)
    dmerged = _matmul(dh1_16, w_out_full, name="mm_out_dx", trans_b=True)
    dw_out = _matmul(merged, dh1_16, name="mm_out_dw", trans_a=True)
    dzg, dattn, dyab = _merge_bwd(dmerged, yab, zg, attn, name="merge_bwd")
    do = _matmul(dattn, w_ao_full, name="mm_attn_o_dx", trans_b=True)
    dw_ao = _matmul(o, dattn, name="mm_attn_o_dw", trans_a=True)
    dgy = _matmul(dyab, w_glu_full, name="mm_glu_dx", trans_b=True)
    dw_glu = _matmul(gy, dyab, name="mm_glu_dw", trans_a=True)
    dzu, dw_b, dw_c, da_slab, dd_skip = _ssm_bwd(zu, dgy, y, hin, w_b16, w_bt16, w_ct16, a_slab, d_skip,
                                                 name="ssm_bwd")
    dq, delta, dfq = _attn_bwd_dq(zqkv, fcum, fcol, do, o, lse, name="attn_bwd_dq")
    dk, dv, dfcum, (x_w_up, x_w_out, x_w_ao, x_w_glu) = _attn_bwd_dkv(
        zqkv, fcum, fcol, do, lse, delta, name="attn_bwd_dkv",
        riders=[_Rider(dw_up, False, True, *w_up.shape[1:]), _Rider(dw_out, False, False, *w_out.shape[1:]),
                _Rider(dw_ao, False, True, *w_attn_o.shape[1:]), _Rider(dw_glu, False, True, *w_glu.shape[1:])])
    df_t, db_f = _forget_bwd(dfcum.T, dfq, f_t, b_col, name="forget_bwd")
    dzf = jnp.pad(df_t.T, ((0, 0), (0, LANES - H))).astype(BF16)
    dz_main = jnp.concatenate([dq, dk, dv, dzu, dzg], axis=1)
    dw_main = _matmul(n1, dz_main, name="mm_in_dw", trans_a=True)
    dw_f = _matmul(n1, dzf, name="mm_forget_dw", trans_a=True)
    dw_in = jnp.concatenate([dw_main[:, :o_f], dw_f[:, :H], dw_main[:, o_f:]], axis=1)
    parts_in = _column_parts(dw_in, w_in.shape[2])
    dn1 = _matmul(dzf, w_f, name="mm_forget_dx", out_dtype=F32, trans_b=True)
    dn1, (x_w_in,) = _matmul(
        dz_main, w_main, name="mm_in_dx", out_dtype=F32, residual=dn1, trans_b=True,
        riders=[_Rider(parts_in.reshape(-1, parts_in.shape[2]), False, False, *parts_in.shape[1:])])
    dh0, dg_mix = _rms_bwd(dn1, h0, g_mix, dh1, name="rms_mix_bwd")
    grad_x = dh0[N_META:L][None]

    dbb_re, dbb_im = _slab_in_grad(dw_b, G, P, C)
    dc_re, dc_im = _slab_out_grad(dw_c, G, C, P)
    da_re = da_slab[:, :SLAB_STATE].reshape(G, P)
    da_im = da_slab[:, SLAB_STATE:].reshape(G, P)
    _, disc_vjp = jax.vjp(_ssm_discretise, lam_re[0], lam_im[0], log_dt[0], b_re[0], b_im[0])
    dlam_re, dlam_im, dlog_dt, db_re, db_im = disc_vjp((da_re, da_im, dbb_re, dbb_im))

    big = {}
    big["meta"] = _update_cols(dh0[:N_META], meta, m_meta, v_meta, name="meta")
    big["conv_w"] = _update_cols(conv_sums[:CONV_WIDTH], conv_w[0], m_conv_w[0], v_conv_w[0], name="conv_w")
    big["w_down"] = _sum_adamw(x_w_down, w_down[0], m_w_down[0], v_w_down[0], name="w_down_adamw")
    big["w_up"] = _sum_adamw(x_w_up, w_up[0], m_w_up[0], v_w_up[0], name="w_up_adamw")
    big["w_out"] = _sum_adamw(x_w_out, w_out[0], m_w_out[0], v_w_out[0], name="w_out_adamw")
    big["w_attn_o"] = _sum_adamw(x_w_ao, w_attn_o[0], m_w_attn_o[0], v_w_attn_o[0], name="w_attn_o_adamw")
    big["w_glu"] = _sum_adamw(x_w_glu, w_glu[0], m_w_glu[0], v_w_glu[0], name="w_glu_adamw")
    big["w_in"] = _padded_adamw(x_w_in, w_in[0], m_w_in[0], v_w_in[0], name="w_in_adamw")

    rep_names = ["g_mix", "b_f", "lam_re", "lam_im", "log_dt", "b_re", "b_im", "c_re", "c_im", "d_skip", "g_ffn",
                 "conv_b", "g_final"]
    rep_w = [g_mix, b_f, lam_re, lam_im, log_dt, b_re, b_im, c_re, c_im, d_skip, g_ffn, conv_b, g_final]
    rep_m = [m_g_mix, m_b_f, m_lam_re, m_lam_im, m_log_dt, m_b_re, m_b_im, m_c_re, m_c_im, m_d_skip, m_g_ffn,
             m_conv_b, m_g_final]
    rep_v = [v_g_mix, v_b_f, v_lam_re, v_lam_im, v_log_dt, v_b_re, v_b_im, v_c_re, v_c_im, v_d_skip, v_g_ffn,
             v_conv_b, v_g_final]
    rep_g = [dg_mix, db_f[:, 0], dlam_re, dlam_im, dlog_dt, db_re, db_im, dc_re, dc_im, dd_skip, dg_ffn,
             conv_sums[CONV_WIDTH], dg_final]
    rep = _update_replicated(rep_g, rep_w, rep_m, rep_v, name="replicated")
    rep_out = {n: [rep[k][i] for k in range(4)] for i, n in enumerate(rep_names)}

    order = ["meta", "g_mix", "w_in", "b_f", "lam_re", "lam_im", "log_dt", "b_re", "b_im", "c_re", "c_im", "d_skip",
             "w_glu", "w_attn_o", "w_out", "g_ffn", "w_up", "conv_w", "conv_b", "w_down", "g_final"]
    outs = [loss, grad_x]
    for kind in range(4):
        for n in order:
            if n in big:
                outs.append(big[n][kind] if n == "meta" else big[n][kind][None])
            else:
                outs.append(rep_out[n][kind])
    return tuple(outs)
```

```python
import collections
import functools
import math

import jax
import jax.numpy as jnp
from jax import lax
from jax.experimental import pallas as pl
from jax.experimental.pallas import tpu as pltpu

F32 = jnp.float32
BF16 = jnp.bfloat16

N_META = 16
EPS = 1e-6
HEAD_DIM = 128
SSM_GROUP = 16
SSM_STATE = 64
GROUPS_PER_SLAB = 8
SLAB_STATE = GROUPS_PER_SLAB * SSM_STATE
CONV_WIDTH = 3
N_DEV = 8

ADAM_LR = 0.001
ADAM_B1 = 0.9
ADAM_B2 = 0.999
ADAM_EPS = 1e-08
ADAM_WD = 0.01
ADAM_STEP = 10

LANES = 128
SUBLANES = 8
VMEM_LIMIT = 52 * 1024 * 1024

SEQ_BLOCK = 768
ATT_BLOCK = 768
ROW_BLOCK = 256
SSM_CHUNK = 256
CONV_ROW_BLOCK = 768
ADAMW_BLOCK_ELEMS = 1 << 17
WGRAD_BLOCK_M = 1408
MASK_VALUE = -1e30
LOG2_E = math.log2(math.e)


def _round_up(n, m):
    return (n + m - 1) // m * m


def _divisor(n, target, mult):
    if n <= target:
        return n
    best = None
    for d in range(mult, target + 1, mult):
        if n % d == 0:
            best = d
    assert best is not None, (n, target, mult)
    return best


def _params(sem):
    return pltpu.CompilerParams(dimension_semantics=sem, vmem_limit_bytes=VMEM_LIMIT)


def _matmul(a, b, *, name, trans_a=False, trans_b=False, out_dtype=None, residual=None, riders=(),
            tm=768, tn=1024, tk=2816):
    out_dtype = BF16 if out_dtype is None else out_dtype
    assert not (trans_a and trans_b)
    if trans_a:
        K, M = a.shape
    else:
        M, K = a.shape
    if trans_b:
        N, K2 = b.shape
    else:
        K2, N = b.shape
    assert K == K2, (a.shape, b.shape)
    if trans_a:
        tm = max(tm, WGRAD_BLOCK_M)
    tm = _divisor(M, tm, LANES if trans_a else SUBLANES)
    tn = _divisor(N, tn, LANES)
    tk = _divisor(K, tk, LANES if not trans_a else SUBLANES)
    nk = K // tk

    def body(*refs):
        if residual is None:
            a_ref, b_ref, o_ref, acc_ref = refs
        else:
            a_ref, b_ref, r_ref, o_ref, acc_ref = refs
        k = pl.program_id(2)

        @pl.when(k == 0)
        def _():
            acc_ref[...] = jnp.zeros_like(acc_ref)

        contract = (0, 0) if trans_a else (1, 1) if trans_b else (1, 0)
        acc_ref[...] += lax.dot_general(a_ref[...], b_ref[...], (((contract[0],), (contract[1],)), ((), ())),
                                        preferred_element_type=F32)

        @pl.when(k == nk - 1)
        def _():
            r = acc_ref[...]
            if residual is not None:
                r = r + r_ref[...]
            o_ref[...] = r.astype(o_ref.dtype)

    if trans_a:
        a_spec = pl.BlockSpec((tk, tm), lambda i, j, k: (k, i))
    else:
        a_spec = pl.BlockSpec((tm, tk), lambda i, j, k: (i, k))
    if trans_b:
        b_spec = pl.BlockSpec((tn, tk), lambda i, j, k: (j, k))
    else:
        b_spec = pl.BlockSpec((tk, tn), lambda i, j, k: (k, j))
    in_specs = [a_spec, b_spec]
    args = [a, b]
    if residual is not None:
        in_specs.append(pl.BlockSpec((tm, tn), lambda i, j, k: (i, j)))
        args.append(residual)
    (out,), carried = _gridded_call(
        name, (M // tm, N // tn, nk), ("parallel", "parallel", "arbitrary"), body, in_specs,
        [pl.BlockSpec((tm, tn), lambda i, j, k: (i, j))], [jax.ShapeDtypeStruct((M, N), out_dtype)],
        [pltpu.VMEM((tm, tn), F32)], args, riders)
    return (out, carried) if riders else out


def _rms_fwd(h, g, *, name):
    T, D = h.shape
    tr = _divisor(T, ROW_BLOCK, SUBLANES)

    def body(h_ref, g_ref, o_ref):
        x = h_ref[...]
        r = lax.rsqrt(jnp.mean(x * x, axis=-1, keepdims=True) + EPS)
        o_ref[...] = (x * r * g_ref[...]).astype(o_ref.dtype)

    return pl.pallas_call(
        body, name=name, grid=(T // tr,),
        in_specs=[pl.BlockSpec((tr, D), lambda i: (i, 0)), pl.BlockSpec((1, D), lambda i: (0, 0))],
        out_specs=pl.BlockSpec((tr, D), lambda i: (i, 0)),
        out_shape=jax.ShapeDtypeStruct((T, D), BF16),
        compiler_params=_params(("parallel",)),
    )(h, g)


def _rms_bwd(dn, h, g, dres, *, name):
    T, D = h.shape
    tr = _divisor(T, ROW_BLOCK, SUBLANES)

    def body(dn_ref, h_ref, g_ref, dres_ref, dh_ref, dg_ref):
        i = pl.program_id(0)

        @pl.when(i == 0)
        def _():
            dg_ref[...] = jnp.zeros_like(dg_ref)

        x = h_ref[...]
        dn_v = dn_ref[...].astype(F32)
        r = lax.rsqrt(jnp.mean(x * x, axis=-1, keepdims=True) + EPS)
        xh = x * r
        dg_ref[...] += jnp.sum(dn_v * xh, axis=0, keepdims=True)
        dxh = dn_v * g_ref[...]
        dh_ref[...] = dres_ref[...] + r * (dxh - xh * jnp.mean(dxh * xh, axis=-1, keepdims=True))

    return pl.pallas_call(
        body, name=name, grid=(T // tr,),
        in_specs=[pl.BlockSpec((tr, D), lambda i: (i, 0)), pl.BlockSpec((tr, D), lambda i: (i, 0)),
                  pl.BlockSpec((1, D), lambda i: (0, 0)), pl.BlockSpec((tr, D), lambda i: (i, 0))],
        out_specs=[pl.BlockSpec((tr, D), lambda i: (i, 0)), pl.BlockSpec((1, D), lambda i: (0, 0))],
        out_shape=[jax.ShapeDtypeStruct((T, D), F32), jax.ShapeDtypeStruct((1, D), F32)],
        compiler_params=_params(("arbitrary",)),
    )(dn, h, g, dres)


def _final_loss(h, g, target, n_valid, *, name):
    T, D = h.shape
    tr = _divisor(T, ROW_BLOCK, SUBLANES)

    def body(h_ref, g_ref, t_ref, dh_ref, sq_ref, dg_ref):
        i = pl.program_id(0)

        @pl.when(i == 0)
        def _():
            sq_ref[...] = jnp.zeros_like(sq_ref)
            dg_ref[...] = jnp.zeros_like(dg_ref)

        x = h_ref[...]
        r = lax.rsqrt(jnp.mean(x * x, axis=-1, keepdims=True) + EPS)
        xh = x * r
        gv = g_ref[...]
        row = i * tr + lax.broadcasted_iota(jnp.int32, (tr, 1), 0)
        valid = (row >= N_META) & (row < N_META + n_valid)
        err = jnp.where(valid, xh * gv - t_ref[...], 0.0)
        sq_ref[...] += jnp.sum(err * err)
        dy = err * (1.0 / D)
        dg_ref[...] += jnp.sum(dy * xh, axis=0, keepdims=True)
        dxh = dy * gv
        dh_ref[...] = r * (dxh - xh * jnp.mean(dxh * xh, axis=-1, keepdims=True))

    return pl.pallas_call(
        body, name=name, grid=(T // tr,),
        in_specs=[pl.BlockSpec((tr, D), lambda i: (i, 0)), pl.BlockSpec((1, D), lambda i: (0, 0)),
                  pl.BlockSpec((tr, D), lambda i: (i, 0))],
        out_specs=[pl.BlockSpec((tr, D), lambda i: (i, 0)), pl.BlockSpec((SUBLANES, LANES), lambda i: (0, 0)),
                   pl.BlockSpec((1, D), lambda i: (0, 0))],
        out_shape=[jax.ShapeDtypeStruct((T, D), F32), jax.ShapeDtypeStruct((SUBLANES, LANES), F32),
                   jax.ShapeDtypeStruct((1, D), F32)],
        compiler_params=_params(("arbitrary",)),
    )(h, g, target)


def _prefix_sum_lanes(x):
    lane = lax.broadcasted_iota(jnp.int32, x.shape, 1)
    d = 1
    while d < LANES:
        x = x + jnp.where(lane >= d, pltpu.roll(x, d, axis=1), 0.0)
        d *= 2
    return x


def _forget_cumsum(ft, bf, *, name):
    H, T = ft.shape
    nb = T // LANES

    def body(f_ref, b_ref, o_ref):
        carry = jnp.zeros((H, 1), F32)
        for j in range(nb):
            sl = pl.ds(j * LANES, LANES)
            lf = jax.nn.log_sigmoid(f_ref[:, sl] + b_ref[...])
            c = _prefix_sum_lanes(lf) + carry
            o_ref[:, sl] = c
            carry = c[:, LANES - 1:LANES]

    return pl.pallas_call(
        body, name=name,
        in_specs=[pl.BlockSpec(memory_space=pltpu.VMEM), pl.BlockSpec(memory_space=pltpu.VMEM)],
        out_specs=pl.BlockSpec(memory_space=pltpu.VMEM),
        out_shape=jax.ShapeDtypeStruct((H, T), F32),
        compiler_params=pltpu.CompilerParams(vmem_limit_bytes=VMEM_LIMIT),
    )(ft, bf)


def _forget_bwd(dF_key, dF_query, ft, bf, *, name):
    H, T = ft.shape
    nb = T // LANES

    def body(d_ref, dq_ref, f_ref, b_ref, o_ref, s_ref):
        carry = jnp.zeros((H, 1), F32)
        acc = jnp.zeros((H, LANES), F32)
        for j in reversed(range(nb)):
            sl = pl.ds(j * LANES, LANES)
            d = d_ref[:, sl] + dq_ref[:, sl]
            pre = _prefix_sum_lanes(d)
            tot = pre[:, LANES - 1:LANES]
            dlf = tot - pre + d + carry
            carry = carry + tot
            z = f_ref[:, sl] + b_ref[...]
            df = dlf * jax.nn.sigmoid(-z)
            o_ref[:, sl] = df
            acc = acc + df
        s_ref[...] = jnp.broadcast_to(jnp.sum(acc, axis=1, keepdims=True), (H, LANES))

    return pl.pallas_call(
        body, name=name,
        in_specs=[pl.BlockSpec(memory_space=pltpu.VMEM)] * 4,
        out_specs=[pl.BlockSpec(memory_space=pltpu.VMEM)] * 2,
        out_shape=[jax.ShapeDtypeStruct((H, T), F32), jax.ShapeDtypeStruct((H, LANES), F32)],
        compiler_params=pltpu.CompilerParams(vmem_limit_bytes=VMEM_LIMIT),
    )(dF_key, dF_query, ft, bf)


def _scores(q_ref, k_ref, fq_ref, fk_ref, h, blk, scale, diagonal):
    hs = pl.ds(h * HEAD_DIM, HEAD_DIM)
    s = lax.dot_general(k_ref[:, hs], q_ref[:, hs], (((1,), (1,)), ((), ())), preferred_element_type=F32)
    s = s * (scale * LOG2_E) + (fq_ref[h:h + 1, :] - fk_ref[:, h:h + 1])
    if diagonal:
        key = lax.broadcasted_iota(jnp.int32, (blk, blk), 0)
        query = lax.broadcasted_iota(jnp.int32, (blk, blk), 1)
        s = jnp.where(key <= query, s, MASK_VALUE)
    return s


def _causal_blocks(i, j, compute):
    @pl.when(j < i)
    def _():
        compute(False)

    @pl.when(j == i)
    def _():
        compute(True)


def _attn_fwd(zqkv, fcum, fcol, *, name, riders=()):
    T = zqkv.shape[0]
    DA = zqkv.shape[1] // 3
    H = DA // HEAD_DIM
    blk = _divisor(T, ATT_BLOCK, LANES)
    nb = T // blk
    scale = HEAD_DIM ** -0.5

    def body(q_ref, k_ref, v_ref, fq_ref, fk_ref, o_ref, lse_ref, m_ref, l_ref, acc_ref):
        i = pl.program_id(0)
        j = pl.program_id(1)

        @pl.when(j == 0)
        def _():
            m_ref[...] = jnp.full_like(m_ref, MASK_VALUE)
            l_ref[...] = jnp.zeros_like(l_ref)
            acc_ref[...] = jnp.zeros_like(acc_ref)

        def compute(diagonal):
            for h in range(H):
                hs = pl.ds(h * HEAD_DIM, HEAD_DIM)
                s = _scores(q_ref, k_ref, fq_ref, fk_ref, h, blk, scale, diagonal)
                m_prev = m_ref[h:h + 1, :]
                m_new = jnp.maximum(m_prev, jnp.max(s, axis=0, keepdims=True))
                alpha = jnp.exp2(m_prev - m_new)
                p = jnp.exp2(s - m_new)
                l_ref[h:h + 1, :] = alpha * l_ref[h:h + 1, :] + jnp.sum(p, axis=0, keepdims=True)
                acc_ref[hs, :] = alpha * acc_ref[hs, :] + lax.dot_general(
                    v_ref[:, hs], p.astype(BF16), (((0,), (0,)), ((), ())), preferred_element_type=F32)
                m_ref[h:h + 1, :] = m_new

        _causal_blocks(i, j, compute)

        @pl.when(j == nb - 1)
        def _():
            for h in range(H):
                hs = pl.ds(h * HEAD_DIM, HEAD_DIM)
                l = l_ref[h:h + 1, :]
                o_ref[:, hs] = (acc_ref[hs, :] / l).T.astype(o_ref.dtype)
                lse_ref[h:h + 1, :] = m_ref[h:h + 1, :] + jnp.log2(l)

    kv = lambda c: (lambda i, j: (jnp.minimum(j, i), c))
    (o, lse), carried = _gridded_call(
        name, (nb, nb), ("parallel", "arbitrary"), body,
        [pl.BlockSpec((blk, DA), lambda i, j: (i, 0)),
         pl.BlockSpec((blk, DA), kv(1)), pl.BlockSpec((blk, DA), kv(2)),
         pl.BlockSpec((H, blk), lambda i, j: (0, i)),
         pl.BlockSpec((blk, H), lambda i, j: (jnp.minimum(j, i), 0))],
        [pl.BlockSpec((blk, DA), lambda i, j: (i, 0)), pl.BlockSpec((H, blk), lambda i, j: (0, i))],
        [jax.ShapeDtypeStruct((T, DA), BF16), jax.ShapeDtypeStruct((H, T), F32)],
        [pltpu.VMEM((H, blk), F32), pltpu.VMEM((H, blk), F32), pltpu.VMEM((DA, blk), F32)],
        [zqkv, zqkv, zqkv, fcum, fcol], riders)
    return o, lse, carried


def _attn_bwd_dq(zqkv, fcum, fcol, do, o, lse, *, name, riders=()):
    T = zqkv.shape[0]
    DA = zqkv.shape[1] // 3
    H = DA // HEAD_DIM
    blk = _divisor(T, ATT_BLOCK, LANES)
    nb = T // blk
    scale = HEAD_DIM ** -0.5

    def body(q_ref, k_ref, v_ref, fq_ref, fk_ref, do_ref, o_ref, lse_ref, dq_ref, dl_ref, dfq_ref, acc_ref):
        i = pl.program_id(0)
        j = pl.program_id(1)

        @pl.when(j == 0)
        def _():
            acc_ref[...] = jnp.zeros_like(acc_ref)
            dfq_ref[...] = jnp.zeros_like(dfq_ref)
            for h in range(H):
                hs = pl.ds(h * HEAD_DIM, HEAD_DIM)
                prod = do_ref[:, hs].astype(F32) * o_ref[:, hs].astype(F32)
                dl_ref[h:h + 1, :] = jnp.sum(prod.T, axis=0, keepdims=True)

        def compute(diagonal):
            for h in range(H):
                hs = pl.ds(h * HEAD_DIM, HEAD_DIM)
                s = _scores(q_ref, k_ref, fq_ref, fk_ref, h, blk, scale, diagonal)
                p = jnp.exp2(s - lse_ref[h:h + 1, :])
                dp = lax.dot_general(v_ref[:, hs], do_ref[:, hs], (((1,), (1,)), ((), ())),
                                     preferred_element_type=F32)
                ds = p * (dp - dl_ref[h:h + 1, :])
                dfq_ref[h:h + 1, :] += jnp.sum(ds, axis=0, keepdims=True)
                acc_ref[hs, :] += scale * lax.dot_general(k_ref[:, hs], ds.astype(BF16), (((0,), (0,)), ((), ())),
                                                          preferred_element_type=F32)

        _causal_blocks(i, j, compute)

        @pl.when(j == nb - 1)
        def _():
            for h in range(H):
                hs = pl.ds(h * HEAD_DIM, HEAD_DIM)
                dq_ref[:, hs] = acc_ref[hs, :].T.astype(dq_ref.dtype)

    kv = lambda c: (lambda i, j: (jnp.minimum(j, i), c))
    row = lambda i, j: (i, 0)
    lane = lambda i, j: (0, i)
    (dq, delta, dfq), carried = _gridded_call(
        name, (nb, nb), ("parallel", "arbitrary"), body,
        [pl.BlockSpec((blk, DA), row), pl.BlockSpec((blk, DA), kv(1)), pl.BlockSpec((blk, DA), kv(2)),
         pl.BlockSpec((H, blk), lane), pl.BlockSpec((blk, H), lambda i, j: (jnp.minimum(j, i), 0)),
         pl.BlockSpec((blk, DA), row), pl.BlockSpec((blk, DA), row), pl.BlockSpec((H, blk), lane)],
        [pl.BlockSpec((blk, DA), row), pl.BlockSpec((H, blk), lane), pl.BlockSpec((H, blk), lane)],
        [jax.ShapeDtypeStruct((T, DA), BF16), jax.ShapeDtypeStruct((H, T), F32), jax.ShapeDtypeStruct((H, T), F32)],
        [pltpu.VMEM((DA, blk), F32)],
        [zqkv, zqkv, zqkv, fcum, fcol, do, o, lse], riders)
    return dq, delta, dfq, carried


def _attn_bwd_dkv(zqkv, fcum, fcol, do, lse, delta, *, name, riders=()):
    T = zqkv.shape[0]
    DA = zqkv.shape[1] // 3
    H = DA // HEAD_DIM
    blk = _divisor(T, ATT_BLOCK, LANES)
    nb = T // blk
    scale = HEAD_DIM ** -0.5

    def body(q_ref, k_ref, v_ref, fq_ref, fk_ref, do_ref, lse_ref, dl_ref, dk_ref, dv_ref, df_ref,
             dk_acc, dv_acc, df_acc):
        j = pl.program_id(0)
        i = pl.program_id(1)

        @pl.when(i == 0)
        def _():
            dk_acc[...] = jnp.zeros_like(dk_acc)
            dv_acc[...] = jnp.zeros_like(dv_acc)
            df_acc[...] = jnp.zeros_like(df_acc)

        def compute(diagonal):
            for h in range(H):
                hs = pl.ds(h * HEAD_DIM, HEAD_DIM)
                s = _scores(q_ref, k_ref, fq_ref, fk_ref, h, blk, scale, diagonal)
                p = jnp.exp2(s - lse_ref[h:h + 1, :])
                dov = do_ref[:, hs]
                dv_acc[:, hs] += jnp.dot(p.astype(BF16), dov, preferred_element_type=F32)
                dp = lax.dot_general(v_ref[:, hs], dov, (((1,), (1,)), ((), ())), preferred_element_type=F32)
                ds = p * (dp - dl_ref[h:h + 1, :])
                dk_acc[:, hs] += scale * jnp.dot(ds.astype(BF16), q_ref[:, hs], preferred_element_type=F32)
                df_acc[:, h:h + 1] -= jnp.sum(ds, axis=1, keepdims=True)

        _causal_blocks(i, j, compute)

        @pl.when(i == nb - 1)
        def _():
            dk_ref[...] = dk_acc[...].astype(dk_ref.dtype)
            dv_ref[...] = dv_acc[...].astype(dv_ref.dtype)
            df_ref[...] = df_acc[...]

    qrow = lambda j, i: (jnp.maximum(i, j), 0)
    qlane = lambda j, i: (0, jnp.maximum(i, j))
    kcol = lambda c: (lambda j, i: (j, c))
    (dk, dv, df), carried = _gridded_call(
        name, (nb, nb), ("parallel", "arbitrary"), body,
        [pl.BlockSpec((blk, DA), qrow), pl.BlockSpec((blk, DA), kcol(1)), pl.BlockSpec((blk, DA), kcol(2)),
         pl.BlockSpec((H, blk), qlane), pl.BlockSpec((blk, H), kcol(0)),
         pl.BlockSpec((blk, DA), qrow), pl.BlockSpec((H, blk), qlane), pl.BlockSpec((H, blk), qlane)],
        [pl.BlockSpec((blk, DA), kcol(0)), pl.BlockSpec((blk, DA), kcol(0)), pl.BlockSpec((blk, H), kcol(0))],
        [jax.ShapeDtypeStruct((T, DA), BF16), jax.ShapeDtypeStruct((T, DA), BF16),
         jax.ShapeDtypeStruct((T, H), F32)],
        [pltpu.VMEM((blk, DA), F32), pltpu.VMEM((blk, DA), F32), pltpu.VMEM((blk, H), F32)],
        [zqkv, zqkv, zqkv, fcum, fcol, do, lse, delta], riders)
    return dk, dv, df, carried


def _gelu(y):
    c = math.sqrt(2.0 / math.pi)
    return 0.5 * y * (1.0 + jnp.tanh(c * (y + 0.044715 * (y * y * y))))


def _gelu_grad(y):
    c = math.sqrt(2.0 / math.pi)
    th = jnp.tanh(c * (y + 0.044715 * (y * y * y)))
    return 0.5 * (1.0 + th) + 0.5 * y * (1.0 - th * th) * c * (1.0 + 3.0 * 0.044715 * y * y)


STATE_BLOCKS = SLAB_STATE // LANES


def _lane_blocks(ref, lead=()):
    return [ref[lead + (slice(None), pl.ds(b * LANES, LANES))] for b in range(2 * STATE_BLOCKS)]


def _put_lane_blocks(ref, blocks):
    for b, v in enumerate(blocks):
        ref[:, pl.ds(b * LANES, LANES)] = v


def _put_slab(x_ref, first, q, n_slab, chunk, value):
    for b in range(2 * STATE_BLOCKS):
        x_ref[b, pl.ds(first * n_slab + q, chunk, stride=n_slab), :] = value[:, b * LANES:(b + 1) * LANES]


def _get_slab(x_ref, first, q, n_slab, chunk):
    return jnp.concatenate([x_ref[b, pl.ds(first * n_slab + q, chunk, stride=n_slab), :]
                            for b in range(2 * STATE_BLOCKS)], axis=1)


def _ssm_scan_fwd(x_ref, a, h, chunk, n_slab, first=0):
    nb = STATE_BLOCKS

    def step(t, h):
        rows = pl.ds(pl.multiple_of((t + first) * n_slab, n_slab), n_slab)
        out = [None] * (2 * nb)
        for b in range(nb):
            n_re = a[b] * h[b] - a[nb + b] * h[nb + b] + x_ref[b, rows, :]
            n_im = a[b] * h[nb + b] + a[nb + b] * h[b] + x_ref[nb + b, rows, :]
            x_ref[b, rows, :] = n_re
            x_ref[nb + b, rows, :] = n_im
            out[b], out[nb + b] = n_re, n_im
        return tuple(out)

    return lax.fori_loop(0, chunk, step, tuple(h), unroll=4)


def _ssm_fwd(zu, w_b, w_c, a, d_skip, *, name):
    T, DS = zu.shape
    n_slab = DS // LANES
    chunk = _divisor(T, SSM_CHUNK, SUBLANES)
    n_chunk = T // chunk

    def body(u_ref, wb_ref, wc_ref, a_ref, ds_ref, y_ref, gy_ref, hin_ref, x_ref, h_ref):
        k = pl.program_id(0)

        @pl.when(k == 0)
        def _():
            h_ref[...] = jnp.zeros_like(h_ref)

        hin_ref[0] = h_ref[...]
        for q in range(n_slab):
            qs = pl.ds(q * LANES, LANES)
            _put_slab(x_ref, 0, q, n_slab, chunk, jnp.dot(u_ref[:, qs], wb_ref[q], preferred_element_type=F32))
        h = _ssm_scan_fwd(x_ref, _lane_blocks(a_ref), _lane_blocks(h_ref), chunk, n_slab)
        _put_lane_blocks(h_ref, h)
        for q in range(n_slab):
            qs = pl.ds(q * LANES, LANES)
            hq = _get_slab(x_ref, 0, q, n_slab, chunk).astype(BF16)
            y = jnp.dot(hq, wc_ref[q], preferred_element_type=F32) + ds_ref[:, qs] * u_ref[:, qs].astype(F32)
            y_ref[:, qs] = y
            gy_ref[:, qs] = _gelu(y).astype(gy_ref.dtype)

    whole = lambda shape: pl.BlockSpec(shape, lambda k: (0,) * len(shape))
    return pl.pallas_call(
        body, name=name, grid=(n_chunk,),
        in_specs=[pl.BlockSpec((chunk, DS), lambda k: (k, 0)), whole(w_b.shape), whole(w_c.shape),
                  whole(a.shape), whole(d_skip.shape)],
        out_specs=[pl.BlockSpec((chunk, DS), lambda k: (k, 0)), pl.BlockSpec((chunk, DS), lambda k: (k, 0)),
                   pl.BlockSpec((1, n_slab, 2 * SLAB_STATE), lambda k: (k, 0, 0))],
        out_shape=[jax.ShapeDtypeStruct((T, DS), F32), jax.ShapeDtypeStruct((T, DS), BF16),
                   jax.ShapeDtypeStruct((n_chunk, n_slab, 2 * SLAB_STATE), F32)],
        scratch_shapes=[pltpu.VMEM((2 * STATE_BLOCKS, chunk * n_slab, LANES), F32),
                        pltpu.VMEM((n_slab, 2 * SLAB_STATE), F32)],
        compiler_params=_params(("arbitrary",)),
    )(zu, w_b, w_c, a, d_skip)


def _ssm_bwd(zu, dgy, y, hin, w_b, w_bt, w_ct, a, d_skip, *, name):
    T, DS = zu.shape
    n_slab = DS // LANES
    chunk = _divisor(T, SSM_CHUNK, SUBLANES)
    n_chunk = T // chunk
    S = SLAB_STATE

    def body(u_ref, dgy_ref, y_ref, hin_ref, wb_ref, wbt_ref, wct_ref, a_ref, ds_ref,
             du_ref, dwb_ref, dwc_ref, da_ref, dds_ref, hb_ref, gb_ref, dy_ref, g_ref):
        k = pl.program_id(0)

        @pl.when(k == 0)
        def _():
            g_ref[...] = jnp.zeros_like(g_ref)
            dwb_ref[...] = jnp.zeros_like(dwb_ref)
            dwc_ref[...] = jnp.zeros_like(dwc_ref)
            da_ref[...] = jnp.zeros_like(da_ref)
            dds_ref[...] = jnp.zeros_like(dds_ref)

        nb = STATE_BLOCKS
        a = _lane_blocks(a_ref)
        hin = _lane_blocks(hin_ref, lead=(0,))

        for b in range(2 * nb):
            hb_ref[b, pl.ds(0, n_slab), :] = hin[b]
        dy_ref[...] = dgy_ref[...].astype(F32) * _gelu_grad(y_ref[...])
        for q in range(n_slab):
            qs = pl.ds(q * LANES, LANES)
            _put_slab(hb_ref, 1, q, n_slab, chunk, jnp.dot(u_ref[:, qs], wb_ref[q], preferred_element_type=F32))
            _put_slab(gb_ref, 0, q, n_slab, chunk,
                      jnp.dot(dy_ref[:, qs].astype(BF16), wct_ref[q], preferred_element_type=F32))
        _ssm_scan_fwd(hb_ref, a, hin, chunk, n_slab, first=1)

        def step(s, carry):
            g, da = carry[:2 * nb], carry[2 * nb:]
            t = chunk - 1 - s
            rows = pl.ds(pl.multiple_of(t * n_slab, n_slab), n_slab)
            g_out, da_out = [None] * (2 * nb), [None] * (2 * nb)
            for b in range(nb):
                n_re = gb_ref[b, rows, :] + a[b] * g[b] + a[nb + b] * g[nb + b]
                n_im = gb_ref[nb + b, rows, :] + a[b] * g[nb + b] - a[nb + b] * g[b]
                gb_ref[b, rows, :] = n_re
                gb_ref[nb + b, rows, :] = n_im
                p_re = hb_ref[b, rows, :]
                p_im = hb_ref[nb + b, rows, :]
                g_out[b], g_out[nb + b] = n_re, n_im
                da_out[b] = da[b] + n_re * p_re + n_im * p_im
                da_out[nb + b] = da[nb + b] + n_im * p_re - n_re * p_im
            return tuple(g_out) + tuple(da_out)

        zero = jnp.zeros((n_slab, LANES), F32)
        carry = lax.fori_loop(0, chunk, step, tuple(_lane_blocks(g_ref)) + (zero,) * (2 * nb), unroll=4)
        _put_lane_blocks(g_ref, carry[:2 * nb])
        for b in range(2 * nb):
            da_ref[:, pl.ds(b * LANES, LANES)] += carry[2 * nb + b]

        for q in range(n_slab):
            qs = pl.ds(q * LANES, LANES)
            uq = u_ref[:, qs]
            dy = dy_ref[:, qs]
            hq = _get_slab(hb_ref, 1, q, n_slab, chunk).astype(BF16)
            gq = _get_slab(gb_ref, 0, q, n_slab, chunk).astype(BF16)
            dwc_ref[q] += lax.dot_general(hq, dy.astype(BF16), (((0,), (0,)), ((), ())), preferred_element_type=F32)
            dwb_ref[q] += lax.dot_general(uq, gq, (((0,), (0,)), ((), ())), preferred_element_type=F32)
            du_ref[:, qs] = (jnp.dot(gq, wbt_ref[q], preferred_element_type=F32) + ds_ref[:, qs] * dy).astype(du_ref.dtype)
            dds_ref[:, qs] += jnp.sum(dy * uq.astype(F32), axis=0, keepdims=True)

    whole = lambda shape: pl.BlockSpec(shape, lambda k: (0,) * len(shape))
    rev = lambda k: (n_chunk - 1 - k, 0)
    return pl.pallas_call(
        body, name=name, grid=(n_chunk,),
        in_specs=[pl.BlockSpec((chunk, DS), rev), pl.BlockSpec((chunk, DS), rev), pl.BlockSpec((chunk, DS), rev),
                  pl.BlockSpec((1, n_slab, 2 * S), lambda k: (n_chunk - 1 - k, 0, 0)),
                  whole(w_b.shape), whole(w_bt.shape), whole(w_ct.shape), whole(a.shape), whole(d_skip.shape)],
        out_specs=[pl.BlockSpec((chunk, DS), rev), whole(w_b.shape), whole(w_bt.shape), whole(a.shape),
                   whole(d_skip.shape)],
        out_shape=[jax.ShapeDtypeStruct((T, DS), BF16), jax.ShapeDtypeStruct(w_b.shape, F32),
                   jax.ShapeDtypeStruct(w_bt.shape, F32), jax.ShapeDtypeStruct(a.shape, F32),
                   jax.ShapeDtypeStruct(d_skip.shape, F32)],
        scratch_shapes=[pltpu.VMEM((2 * STATE_BLOCKS, (chunk + 1) * n_slab, LANES), F32),
                        pltpu.VMEM((2 * STATE_BLOCKS, chunk * n_slab, LANES), F32),
                        pltpu.VMEM((chunk, DS), F32), pltpu.VMEM((n_slab, 2 * S), F32)],
        compiler_params=_params(("arbitrary",)),
    )(zu, dgy, y, hin, w_b, w_bt, w_ct, a, d_skip)


def _ssm_discretise(lam_re, lam_im, log_dt, b_re, b_im):
    dt = jnp.exp(log_dt)[:, None]
    mag = jnp.exp(lam_re * dt)
    a_re = mag * jnp.cos(lam_im * dt)
    a_im = mag * jnp.sin(lam_im * dt)
    den = lam_re * lam_re + lam_im * lam_im
    nr = a_re - 1.0
    z_re = (nr * lam_re + a_im * lam_im) / den
    z_im = (a_im * lam_re - nr * lam_im) / den
    bb_re = z_re[..., None] * b_re - z_im[..., None] * b_im
    bb_im = z_re[..., None] * b_im + z_im[..., None] * b_re
    return a_re, a_im, bb_re, bb_im


def _slab_in(m_re, m_im):
    G, P, C = m_re.shape
    n_slab = G // GROUPS_PER_SLAB
    eye = jnp.eye(GROUPS_PER_SLAB, dtype=m_re.dtype)

    def one(m):
        m = m.reshape(n_slab, GROUPS_PER_SLAB, P, C)
        w = jnp.einsum('sgpc,gh->sgchp', m, eye)
        return w.reshape(n_slab, GROUPS_PER_SLAB * C, GROUPS_PER_SLAB * P)

    return jnp.concatenate([one(m_re), one(m_im)], axis=2)


def _slab_in_grad(dw, G, P, C):
    n_slab = G // GROUPS_PER_SLAB
    eye = jnp.eye(GROUPS_PER_SLAB, dtype=dw.dtype)

    def one(w):
        w = w.reshape(n_slab, GROUPS_PER_SLAB, C, GROUPS_PER_SLAB, P)
        return jnp.einsum('sgchp,gh->sgpc', w, eye).reshape(G, P, C)

    return one(dw[:, :, :SLAB_STATE]), one(dw[:, :, SLAB_STATE:])


def _slab_out(c_re, c_im):
    G, C, P = c_re.shape
    n_slab = G // GROUPS_PER_SLAB
    eye = jnp.eye(GROUPS_PER_SLAB, dtype=c_re.dtype)

    def one(m):
        m = m.reshape(n_slab, GROUPS_PER_SLAB, C, P)
        w = jnp.einsum('sgcp,gh->shpgc', m, eye)
        return w.reshape(n_slab, GROUPS_PER_SLAB * P, GROUPS_PER_SLAB * C)

    return jnp.concatenate([one(c_re), one(-c_im)], axis=1)


def _slab_out_grad(dw, G, C, P):
    n_slab = G // GROUPS_PER_SLAB
    eye = jnp.eye(GROUPS_PER_SLAB, dtype=dw.dtype)

    def one(w):
        w = w.reshape(n_slab, GROUPS_PER_SLAB, P, GROUPS_PER_SLAB, C)
        return jnp.einsum('shpgc,gh->sgcp', w, eye).reshape(G, C, P)

    return one(dw[:, :SLAB_STATE, :]), -one(dw[:, SLAB_STATE:, :])


def _slab_diag(a_re, a_im):
    G, P = a_re.shape
    n_slab = G // GROUPS_PER_SLAB
    return jnp.concatenate([a_re.reshape(n_slab, SLAB_STATE), a_im.reshape(n_slab, SLAB_STATE)], axis=1)


def _merge_fwd(yab, zg, attn, *, name):
    T, D = attn.shape
    tr = _divisor(T, ROW_BLOCK, SUBLANES)

    def body(ya_ref, yb_ref, ga_ref, gb_ref, at_ref, o_ref):
        f = lambda r: r[...].astype(F32)
        ssm = f(ya_ref) * jax.nn.sigmoid(f(yb_ref))
        o_ref[...] = (jax.nn.sigmoid(f(ga_ref)) * ssm + jax.nn.sigmoid(f(gb_ref)) * f(at_ref)).astype(o_ref.dtype)

    lo = pl.BlockSpec((tr, D), lambda i: (i, 0))
    hi = pl.BlockSpec((tr, D), lambda i: (i, 1))
    return pl.pallas_call(
        body, name=name, grid=(T // tr,),
        in_specs=[lo, hi, lo, hi, lo],
        out_specs=lo,
        out_shape=jax.ShapeDtypeStruct((T, D), BF16),
        compiler_params=_params(("parallel",)),
    )(yab, yab, zg, zg, attn)


def _merge_bwd(dm, yab, zg, attn, *, name):
    T, D = attn.shape
    tr = _divisor(T, ROW_BLOCK, SUBLANES)

    def body(dm_ref, ya_ref, yb_ref, ga_ref, gb_ref, at_ref, dg_ref, dat_ref, dy_ref):
        f = lambda r: r[...].astype(F32)
        dmv, ya, at = f(dm_ref), f(ya_ref), f(at_ref)
        sa, sb, syb = jax.nn.sigmoid(f(ga_ref)), jax.nn.sigmoid(f(gb_ref)), jax.nn.sigmoid(f(yb_ref))
        ssm = ya * syb
        dssm = dmv * sa
        dg_ref[:, pl.ds(0, D)] = (dmv * ssm * sa * (1.0 - sa)).astype(dg_ref.dtype)
        dg_ref[:, pl.ds(D, D)] = (dmv * at * sb * (1.0 - sb)).astype(dg_ref.dtype)
        dat_ref[...] = (dmv * sb).astype(dat_ref.dtype)
        dy_ref[:, pl.ds(0, D)] = (dssm * syb).astype(dy_ref.dtype)
        dy_ref[:, pl.ds(D, D)] = (dssm * ya * syb * (1.0 - syb)).astype(dy_ref.dtype)

    lo = pl.BlockSpec((tr, D), lambda i: (i, 0))
    hi = pl.BlockSpec((tr, D), lambda i: (i, 1))
    both = pl.BlockSpec((tr, 2 * D), lambda i: (i, 0))
    return pl.pallas_call(
        body, name=name, grid=(T // tr,),
        in_specs=[lo, lo, hi, lo, hi, lo],
        out_specs=[both, lo, both],
        out_shape=[jax.ShapeDtypeStruct((T, 2 * D), BF16), jax.ShapeDtypeStruct((T, D), BF16),
                   jax.ShapeDtypeStruct((T, 2 * D), BF16)],
        compiler_params=_params(("parallel",)),
    )(dm, yab, yab, zg, zg, attn)


def _conv_taps(g_ref, halo_ref, i, tr):
    g0 = g_ref[...].astype(F32)
    halo = jnp.where(i > 0, halo_ref[...].astype(F32), 0.0)
    row = lax.broadcasted_iota(jnp.int32, g0.shape, 0)
    g1 = jnp.where(row == 0, halo[SUBLANES - 1:SUBLANES, :], pltpu.roll(g0, 1, axis=0))
    g2 = pltpu.roll(g0, 2, axis=0)
    g2 = jnp.where(row == 0, halo[SUBLANES - 2:SUBLANES - 1, :], g2)
    g2 = jnp.where(row == 1, halo[SUBLANES - 1:SUBLANES, :], g2)
    return g0, g1, g2


def _conv_blocks(T, FF):
    tr = _divisor(T, CONV_ROW_BLOCK, SUBLANES)
    tc = _divisor(FF, 1024, LANES)
    return tr, tc


def _conv_fwd(gu, conv_w, conv_b, *, name):
    T = gu.shape[0]
    FF = gu.shape[1] // 2
    tr, tc = _conv_blocks(T, FF)
    ncol = FF // tc

    def body(g_ref, halo_ref, u_ref, w_ref, b_ref, o_ref):
        i = pl.program_id(0)
        g0, g1, g2 = _conv_taps(g_ref, halo_ref, i, tr)
        gc = b_ref[...] + w_ref[0:1, :] * g2 + w_ref[1:2, :] * g1 + w_ref[2:3, :] * g0
        o_ref[...] = (gc * jax.nn.sigmoid(gc) * u_ref[...].astype(F32)).astype(o_ref.dtype)

    hb = tr // SUBLANES
    return pl.pallas_call(
        body, name=name, grid=(T // tr, ncol),
        in_specs=[pl.BlockSpec((tr, tc), lambda i, j: (i, j)),
                  pl.BlockSpec((SUBLANES, tc), lambda i, j: (jnp.maximum(i * hb - 1, 0), j)),
                  pl.BlockSpec((tr, tc), lambda i, j: (i, j + ncol)),
                  pl.BlockSpec((SUBLANES, tc), lambda i, j: (0, j)), pl.BlockSpec((1, tc), lambda i, j: (0, j))],
        out_specs=pl.BlockSpec((tr, tc), lambda i, j: (i, j)),
        out_shape=jax.ShapeDtypeStruct((T, FF), BF16),
        compiler_params=_params(("parallel", "parallel")),
    )(gu, gu, gu, conv_w, conv_b)


def _conv_bwd_gate(da, gu, conv_w, conv_b, *, name, riders=()):
    T = gu.shape[0]
    FF = gu.shape[1] // 2
    tr, tc = _conv_blocks(T, FF)
    ncol = FF // tc

    def body(da_ref, g_ref, halo_ref, u_ref, w_ref, b_ref, dgc_ref, du_ref, s_ref):
        i = pl.program_id(1)

        @pl.when(i == 0)
        def _():
            s_ref[...] = jnp.zeros_like(s_ref)

        g0, g1, g2 = _conv_taps(g_ref, halo_ref, i, tr)
        gc = b_ref[...] + w_ref[0:1, :] * g2 + w_ref[1:2, :] * g1 + w_ref[2:3, :] * g0
        sg = jax.nn.sigmoid(gc)
        dav = da_ref[...].astype(F32)
        du_ref[...] = (dav * gc * sg).astype(du_ref.dtype)
        dgc = dav * u_ref[...].astype(F32) * (sg * (1.0 + gc * (1.0 - sg)))
        dgc_ref[...] = dgc.astype(dgc_ref.dtype)
        s_ref[0:1, :] += jnp.sum(dgc * g2, axis=0, keepdims=True)
        s_ref[1:2, :] += jnp.sum(dgc * g1, axis=0, keepdims=True)
        s_ref[2:3, :] += jnp.sum(dgc * g0, axis=0, keepdims=True)
        s_ref[3:4, :] += jnp.sum(dgc, axis=0, keepdims=True)

    hb = tr // SUBLANES
    blk = pl.BlockSpec((tr, tc), lambda j, i: (i, j))
    (dgc, du, sums), carried = _gridded_call(
        name, (ncol, T // tr), ("parallel", "arbitrary"), body,
        [blk, blk,
         pl.BlockSpec((SUBLANES, tc), lambda j, i: (jnp.maximum(i * hb - 1, 0), j)),
         pl.BlockSpec((tr, tc), lambda j, i: (i, j + ncol)),
         pl.BlockSpec((SUBLANES, tc), lambda j, i: (0, j)), pl.BlockSpec((1, tc), lambda j, i: (0, j))],
        [blk, blk, pl.BlockSpec((SUBLANES, tc), lambda j, i: (0, j))],
        [jax.ShapeDtypeStruct((T, FF), BF16), jax.ShapeDtypeStruct((T, FF), BF16),
         jax.ShapeDtypeStruct((SUBLANES, FF), F32)],
        [], [da, gu, gu, gu, conv_w, conv_b], riders)
    return dgc, du, sums, carried


def _conv_bwd_taps(dgc, conv_w, *, name):
    T, FF = dgc.shape
    tr, tc = _conv_blocks(T, FF)
    ncol = FF // tc
    nrow = T // tr

    def body(d_ref, next_ref, w_ref, o_ref):
        i = pl.program_id(0)
        d0 = d_ref[...].astype(F32)
        nxt = jnp.where(i < nrow - 1, next_ref[...].astype(F32), 0.0)
        row = lax.broadcasted_iota(jnp.int32, d0.shape, 0)
        d1 = jnp.where(row == tr - 1, nxt[0:1, :], pltpu.roll(d0, tr - 1, axis=0))
        d2 = pltpu.roll(d0, tr - 2, axis=0)
        d2 = jnp.where(row == tr - 2, nxt[0:1, :], d2)
        d2 = jnp.where(row == tr - 1, nxt[1:2, :], d2)
        dg = w_ref[2:3, :] * d0 + w_ref[1:2, :] * d1 + w_ref[0:1, :] * d2
        o_ref[...] = dg.astype(o_ref.dtype)

    hb = tr // SUBLANES
    last = T // SUBLANES - 1
    return pl.pallas_call(
        body, name=name, grid=(nrow, ncol),
        in_specs=[pl.BlockSpec((tr, tc), lambda i, j: (i, j)),
                  pl.BlockSpec((SUBLANES, tc), lambda i, j: (jnp.minimum((i + 1) * hb, last), j)),
                  pl.BlockSpec((SUBLANES, tc), lambda i, j: (0, j))],
        out_specs=pl.BlockSpec((tr, tc), lambda i, j: (i, j)),
        out_shape=jax.ShapeDtypeStruct((T, FF), BF16),
        compiler_params=_params(("parallel", "parallel")),
    )(dgc, dgc, conv_w)


def _mesh_pos():
    return lax.axis_index("x"), lax.axis_index("y"), lax.axis_index("c")


def _flip(pos, mask):
    x, y, c = pos
    return (x ^ ((mask >> 2) & 1), y ^ ((mask >> 1) & 1), c ^ (mask & 1))


def _index_of(pos):
    x, y, c = pos
    return 4 * x + 2 * y + c


class _Rider(collections.namedtuple("_Rider", "src gather cols R c")):
    def out_shape(self):
        if not self.gather:
            shape = (N_DEV, self.R, self.c)
        elif self.cols:
            shape = (self.R, N_DEV * self.c)
        else:
            shape = (N_DEV * self.R, self.c)
        return jax.ShapeDtypeStruct(shape, self.src.dtype)

    def slab(self, ref, idx):
        if self.cols:
            return ref.at[:, pl.ds(pl.multiple_of(idx * self.c, LANES), self.c)]
        return ref.at[pl.ds(pl.multiple_of(idx * self.R, 2 * SUBLANES), self.R), :]

    def copy(self, src_ref, dst_ref, send_sems, recv_sems, me, k, arriving):
        peer = _flip(me, k)
        owner = _index_of(peer if arriving else me)
        if self.gather:
            src, dst = src_ref, self.slab(dst_ref, owner)
        else:
            src, dst = self.slab(src_ref, _index_of(peer)), dst_ref.at[owner]
        return pltpu.make_async_remote_copy(src_ref=src, dst_ref=dst, send_sem=send_sems.at[k - 1],
                                            recv_sem=recv_sems.at[k - 1], device_id=peer,
                                            device_id_type=pl.DeviceIdType.MESH)

    def own(self, src_ref, dst_ref, local_sem, me):
        my = _index_of(me)
        if self.gather:
            return pltpu.make_async_copy(src_ref, self.slab(dst_ref, my), local_sem)
        return pltpu.make_async_copy(self.slab(src_ref, my), dst_ref.at[my], local_sem)

    def start(self, src_ref, dst_ref, send_sems, recv_sems, local_sem):
        me = _mesh_pos()
        self.own(src_ref, dst_ref, local_sem, me).start()
        for k in range(1, N_DEV):
            self.copy(src_ref, dst_ref, send_sems, recv_sems, me, k, False).start()

    def wait(self, src_ref, dst_ref, send_sems, recv_sems, local_sem):
        me = _mesh_pos()
        for k in range(1, N_DEV):
            self.copy(src_ref, dst_ref, send_sems, recv_sems, me, k, True).wait_recv()
        for k in range(1, N_DEV):
            self.copy(src_ref, dst_ref, send_sems, recv_sems, me, k, False).wait_send()
        self.own(src_ref, dst_ref, local_sem, me).wait()


_RIDER_SEMS = [pltpu.SemaphoreType.DMA((N_DEV - 1,)), pltpu.SemaphoreType.DMA((N_DEV - 1,)), pltpu.SemaphoreType.DMA]
_ANY = pl.BlockSpec(memory_space=pl.ANY)


def _comm_call(riders, *, name):
    n = len(riders)

    def body(*refs):
        srcs, dsts, sems = refs[:n], refs[n:2 * n], refs[2 * n:]
        for r, rider in enumerate(riders):
            rider.start(srcs[r], dsts[r], *sems[3 * r:3 * r + 3])
        for r, rider in enumerate(riders):
            rider.wait(srcs[r], dsts[r], *sems[3 * r:3 * r + 3])

    return pl.pallas_call(
        body, name=name,
        in_specs=[_ANY] * n, out_specs=[_ANY] * n,
        out_shape=[rider.out_shape() for rider in riders],
        scratch_shapes=_RIDER_SEMS * n,
        compiler_params=pltpu.CompilerParams(has_side_effects=True),
    )(*[rider.src for rider in riders])


def _carry(riders, grid, body, in_specs, out_specs, out_shape, scratch_shapes, args):
    n, n_in, n_out, n_scratch = len(riders), len(in_specs), len(out_specs), len(scratch_shapes)

    def carrying(*refs):
        ins, refs = refs[:n_in], refs[n_in:]
        srcs, refs = refs[:n], refs[n:]
        outs, refs = refs[:n_out], refs[n_out:]
        dsts, refs = refs[:n], refs[n:]
        scratch, sems = refs[:n_scratch], refs[n_scratch:]
        ids = [pl.program_id(a) for a in range(len(grid))]
        first = functools.reduce(jnp.logical_and, [i == 0 for i in ids])
        last = functools.reduce(jnp.logical_and, [i == g - 1 for i, g in zip(ids, grid)])

        @pl.when(first)
        def _():
            for r, rider in enumerate(riders):
                rider.start(srcs[r], dsts[r], *sems[3 * r:3 * r + 3])

        body(*ins, *outs, *scratch)

        @pl.when(last)
        def _():
            for r, rider in enumerate(riders):
                rider.wait(srcs[r], dsts[r], *sems[3 * r:3 * r + 3])

    return dict(
        body=carrying,
        in_specs=list(in_specs) + [_ANY] * n,
        out_specs=list(out_specs) + [_ANY] * n,
        out_shape=list(out_shape) + [rider.out_shape() for rider in riders],
        scratch_shapes=list(scratch_shapes) + _RIDER_SEMS * n,
        args=list(args) + [rider.src for rider in riders])


def _gridded_call(name, grid, semantics, body, in_specs, out_specs, out_shape, scratch_shapes, args, riders=()):
    call = dict(body=body, in_specs=in_specs, out_specs=out_specs, out_shape=out_shape,
                scratch_shapes=scratch_shapes, args=args)
    if riders:
        call = _carry(list(riders), grid, **call)
        semantics = ("arbitrary",) * len(grid)
    outs = pl.pallas_call(
        call["body"], name=name, grid=grid, in_specs=call["in_specs"], out_specs=call["out_specs"],
        out_shape=call["out_shape"], scratch_shapes=call["scratch_shapes"],
        compiler_params=_params(semantics),
    )(*call["args"])
    n_out = len(out_shape)
    return list(outs[:n_out]), list(outs[n_out:])


def _all_gather(x, *, name):
    R, C = x.shape
    return _comm_call([_Rider(x, True, False, R, C)], name=name)[0].reshape(N_DEV, R, C)


def _exchange(parts, *, name):
    _, R, C = parts.shape
    return _comm_call([_Rider(parts.reshape(N_DEV * R, C), False, False, R, C)], name=name)[0]


def _sum_adamw(parts, w, m, v, *, name):
    R, C = w.shape
    tr = _divisor(R, max(2 * SUBLANES, ADAMW_BLOCK_ELEMS // C // SUBLANES * SUBLANES), 2 * SUBLANES)
    c1 = 1.0 - ADAM_B1 ** ADAM_STEP
    c2 = 1.0 - ADAM_B2 ** ADAM_STEP

    def body(p_ref, w_ref, m_ref, v_ref, g_ref, d_ref, nm_ref, nv_ref):
        g = p_ref[0].astype(F32)
        for s in range(1, N_DEV):
            g = g + p_ref[s].astype(F32)
        nm = ADAM_B1 * m_ref[...] + (1.0 - ADAM_B1) * g
        nv = ADAM_B2 * v_ref[...] + (1.0 - ADAM_B2) * (g * g)
        g_ref[...] = g
        nm_ref[...] = nm
        nv_ref[...] = nv
        d_ref[...] = -ADAM_LR * ((nm / c1) / (jnp.sqrt(nv / c2) + ADAM_EPS) + ADAM_WD * w_ref[...])

    blk = pl.BlockSpec((tr, C), lambda i: (i, 0))
    return pl.pallas_call(
        body, name=name, grid=(R // tr,),
        in_specs=[pl.BlockSpec((N_DEV, tr, C), lambda i: (0, i, 0)), blk, blk, blk],
        out_specs=[blk] * 4,
        out_shape=[jax.ShapeDtypeStruct((R, C), F32)] * 4,
        compiler_params=_params(("parallel",)),
    )(parts, w, m, v)


def _pad2(a, rows, cols):
    return jnp.pad(a, ((0, rows - a.shape[0]), (0, cols - a.shape[1])))


def _gather_cols(w, dtype, *, name):
    R, c = w.shape
    rp, cp = _round_up(R, 2 * SUBLANES), _round_up(c, LANES)
    g = _all_gather(_pad2(w.astype(dtype), rp, cp), name=name)
    return jnp.transpose(g[:, :R, :c], (1, 0, 2)).reshape(R, N_DEV * c)


def _column_parts(dw, c):
    R = dw.shape[0]
    parts = jnp.transpose(dw.reshape(R, N_DEV, c), (1, 0, 2))
    return jnp.pad(parts, ((0, 0), (0, _round_up(R, 2 * SUBLANES) - R), (0, _round_up(c, LANES) - c)))


def _padded_adamw(got, w, m, v, *, name):
    R, c = w.shape
    rp, cp = got.shape[1:]
    outs = _sum_adamw(got, _pad2(w, rp, cp), _pad2(m, rp, cp), _pad2(v, rp, cp), name=name)
    return [o[:R, :c] for o in outs]


def _update_cols(dw, w, m, v, *, name):
    got = _exchange(_column_parts(dw, w.shape[1]), name=name + "_exchange")
    return _padded_adamw(got, w, m, v, name=name + "_adamw")


def _update_replicated(grads, ws, ms, vs, *, name):
    sizes = [int(g.size) for g in grads]
    total = sum(sizes)
    rows = _round_up(-(-total // LANES), 2 * SUBLANES)

    def pack(arrs):
        flat = jnp.concatenate([a.reshape(-1).astype(F32) for a in arrs])
        return jnp.pad(flat, (0, rows * LANES - total)).reshape(rows, LANES)

    got = _all_gather(pack(grads), name=name + "_gather")
    outs = _sum_adamw(got, pack(ws), pack(ms), pack(vs), name=name + "_adamw")
    result = []
    for o in outs:
        flat = o.reshape(-1)
        arrs, off = [], 0
        for w, n in zip(ws, sizes):
            arrs.append(flat[off:off + n].reshape(w.shape))
            off += n
        result.append(arrs)
    return result


def kernel(x, meta, g_mix, w_in, b_f, lam_re, lam_im, log_dt, b_re, b_im, c_re, c_im, d_skip, w_glu, w_attn_o, w_out, g_ffn, w_up, conv_w, conv_b, w_down, g_final, loss_target, m_meta, m_g_mix, m_w_in, m_b_f, m_lam_re, m_lam_im, m_log_dt, m_b_re, m_b_im, m_c_re, m_c_im, m_d_skip, m_w_glu, m_w_attn_o, m_w_out, m_g_ffn, m_w_up, m_conv_w, m_conv_b, m_w_down, m_g_final, v_meta, v_g_mix, v_w_in, v_b_f, v_lam_re, v_lam_im, v_log_dt, v_b_re, v_b_im, v_c_re, v_c_im, v_d_skip, v_w_glu, v_w_attn_o, v_w_out, v_g_ffn, v_w_up, v_conv_w, v_conv_b, v_w_down, v_g_final):
    seq, D = x.shape[1], x.shape[2]
    L = N_META + seq
    T = _round_up(L, SEQ_BLOCK) if L > SEQ_BLOCK else _round_up(L, LANES)
    DS = d_skip.shape[1]
    H = b_f.shape[1]
    DA = H * HEAD_DIM
    FF = conv_b.shape[1]
    G, P, C = b_re.shape[1:]

    meta_full = _gather_cols(meta, F32, name="gather_meta")
    conv_w_full = _gather_cols(conv_w[0], F32, name="gather_conv_w")
    w_in_full = _gather_cols(w_in[0], BF16, name="gather_w_in")
    gathers = [_Rider(w[0].astype(BF16), True, cols, *w.shape[1:])
               for w, cols in ((w_attn_o, True), (w_glu, True), (w_out, False), (w_up, True))]
    gather_w_down = _Rider(w_down[0].astype(BF16), True, False, *w_down.shape[1:])
    conv_w8 = jnp.pad(conv_w_full, ((0, SUBLANES - CONV_WIDTH), (0, 0)))

    a_re, a_im, bb_re, bb_im = _ssm_discretise(lam_re[0], lam_im[0], log_dt[0], b_re[0], b_im[0])
    w_b = _slab_in(bb_re, bb_im)
    w_c = _slab_out(c_re[0], c_im[0])
    a_slab = _slab_diag(a_re, a_im)
    w_b16, w_c16 = w_b.astype(BF16), w_c.astype(BF16)
    w_bt16, w_ct16 = jnp.swapaxes(w_b16, 1, 2), jnp.swapaxes(w_c16, 1, 2)

    h0 = jnp.concatenate([meta_full, x[0], jnp.zeros((T - L, D), F32)], axis=0)
    target = jnp.pad(loss_target[0], ((N_META, T - L), (0, 0)))
    n1 = _rms_fwd(h0, g_mix, name="rms_mix")
    o_f, o_u, o_g = 3 * DA, 3 * DA + H, 3 * DA + H + DS
    w_qkv = w_in_full[:, :o_f]
    w_f = jnp.pad(w_in_full[:, o_f:o_u], ((0, 0), (0, LANES - H)))
    w_u = w_in_full[:, o_u:o_g]
    w_g = w_in_full[:, o_g:]
    w_main = jnp.concatenate([w_qkv, w_u, w_g], axis=1)
    zqkv = _matmul(n1, w_qkv, name="mm_qkv")
    zu = _matmul(n1, w_u, name="mm_u")
    zg = _matmul(n1, w_g, name="mm_gates")
    zf = _matmul(n1, w_f, name="mm_forget", out_dtype=F32)
    f_t = zf[:, :H].T
    b_col = b_f.reshape(H, 1)
    fcum = _forget_cumsum(f_t, b_col, name="forget_cumsum") * LOG2_E
    fcol = fcum.T
    o, lse, carried = _attn_fwd(zqkv, fcum, fcol, name="attn_fwd", riders=gathers)
    w_ao_full, w_glu_full, w_out_full, w_up_full = carried
    attn = _matmul(o, w_ao_full, name="mm_attn_o")
    y, gy, hin = _ssm_fwd(zu, w_b16, w_c16, a_slab, d_skip, name="ssm_fwd")
    yab = _matmul(gy, w_glu_full, name="mm_glu")
    merged = _merge_fwd(yab, zg, attn, name="merge_fwd")
    h1 = _matmul(merged, w_out_full, name="mm_out", out_dtype=F32, residual=h0)
    n2 = _rms_fwd(h1, g_ffn, name="rms_ffn")
    gu, (w_down_full,) = _matmul(n2, w_up_full, name="mm_up", riders=[gather_w_down])
    act = _conv_fwd(gu, conv_w8, conv_b, name="conv_fwd")
    h2 = _matmul(act, w_down_full, name="mm_down", out_dtype=F32, residual=h1)

    dh2, sq, dg_final = _final_loss(h2, g_final.reshape(1, D), target, seq, name="final_loss")
    loss = lax.psum(0.5 * sq[0, 0] / D, ("x", "y", "c"))
    dh2_16 = dh2.astype(BF16)
    d_act = _matmul(dh2_16, w_down_full, name="mm_down_dx", trans_b=True)
    dw_down = _matmul(act, dh2_16, name="mm_down_dw", trans_a=True)
    dgc, du2, conv_sums, (x_w_down,) = _conv_bwd_gate(
        d_act, gu, conv_w8, conv_b, name="conv_bwd_gate",
        riders=[_Rider(dw_down, False, False, *w_down.shape[1:])])
    dgu = jnp.concatenate([_conv_bwd_taps(dgc, conv_w8, name="conv_bwd_taps"), du2], axis=1)
    dn2 = _matmul(dgu, w_up_full, name="mm_up_dx", trans_b=True)
    dw_up = _matmul(n2, dgu, name="mm_up_dw", trans_a=True)
    dh1, dg_ffn = _rms_bwd(dn2, h1, g_ffn, dh2, name="rms_ffn_bwd")
    dh1_16 = dh1.astype(BF16)
    dmerged = _matmul(dh1_16, w_out_full, name="mm_out_dx", trans_b=True)
    dw_out = _matmul(merged, dh1_16, name="mm_out_dw", trans_a=True)
    dzg, dattn, dyab = _merge_bwd(dmerged, yab, zg, attn, name="merge_bwd")
    do = _matmul(dattn, w_ao_full, name="mm_attn_o_dx", trans_b=True)
    dw_ao = _matmul(o, dattn, name="mm_attn_o_dw", trans_a=True)
    dgy = _matmul(dyab, w_glu_full, name="mm_glu_dx", trans_b=True)
    dw_glu = _matmul(gy, dyab, name="mm_glu_dw", trans_a=True)
    dzu, dw_b, dw_c, da_slab, dd_skip = _ssm_bwd(zu, dgy, y, hin, w_b16, w_bt16, w_ct16, a_slab, d_skip,
                                                 name="ssm_bwd")
    dq, delta, dfq, (x_w_up,) = _attn_bwd_dq(
        zqkv, fcum, fcol, do, o, lse, name="attn_bwd_dq", riders=[_Rider(dw_up, False, True, *w_up.shape[1:])])
    dk, dv, dfcum, (x_w_out, x_w_ao, x_w_glu) = _attn_bwd_dkv(
        zqkv, fcum, fcol, do, lse, delta, name="attn_bwd_dkv",
        riders=[_Rider(dw_out, False, False, *w_out.shape[1:]), _Rider(dw_ao, False, True, *w_attn_o.shape[1:]),
                _Rider(dw_glu, False, True, *w_glu.shape[1:])])
    df_t, db_f = _forget_bwd(dfcum.T, dfq, f_t, b_col, name="forget_bwd")
    dzf = jnp.pad(df_t.T, ((0, 0), (0, LANES - H))).astype(BF16)
    dz_main = jnp.concatenate([dq, dk, dv, dzu, dzg], axis=1)
    dw_main = _matmul(n1, dz_main, name="mm_in_dw", trans_a=True)
    dw_f = _matmul(n1, dzf, name="mm_forget_dw", trans_a=True)
    dw_in = jnp.concatenate([dw_main[:, :o_f], dw_f[:, :H], dw_main[:, o_f:]], axis=1)
    parts_in = _column_parts(dw_in, w_in.shape[2])
    dn1 = _matmul(dzf, w_f, name="mm_forget_dx", out_dtype=F32, trans_b=True)
    dn1, (x_w_in,) = _matmul(
        dz_main, w_main, name="mm_in_dx", out_dtype=F32, residual=dn1, trans_b=True,
        riders=[_Rider(parts_in.reshape(-1, parts_in.shape[2]), False, False, *parts_in.shape[1:])])
    dh0, dg_mix = _rms_bwd(dn1, h0, g_mix, dh1, name="rms_mix_bwd")
    grad_x = dh0[N_META:L][None]

    dbb_re, dbb_im = _slab_in_grad(dw_b, G, P, C)
    dc_re, dc_im = _slab_out_grad(dw_c, G, C, P)
    da_re = da_slab[:, :SLAB_STATE].reshape(G, P)
    da_im = da_slab[:, SLAB_STATE:].reshape(G, P)
    _, disc_vjp = jax.vjp(_ssm_discretise, lam_re[0], lam_im[0], log_dt[0], b_re[0], b_im[0])
    dlam_re, dlam_im, dlog_dt, db_re, db_im = disc_vjp((da_re, da_im, dbb_re, dbb_im))

    big = {}
    big["meta"] = _update_cols(dh0[:N_META], meta, m_meta, v_meta, name="meta")
    big["conv_w"] = _update_cols(conv_sums[:CONV_WIDTH], conv_w[0], m_conv_w[0], v_conv_w[0], name="conv_w")
    big["w_down"] = _sum_adamw(x_w_down, w_down[0], m_w_down[0], v_w_down[0], name="w_down_adamw")
    big["w_up"] = _sum_adamw(x_w_up, w_up[0], m_w_up[0], v_w_up[0], name="w_up_adamw")
    big["w_out"] = _sum_adamw(x_w_out, w_out[0], m_w_out[0], v_w_out[0], name="w_out_adamw")
    big["w_attn_o"] = _sum_adamw(x_w_ao, w_attn_o[0], m_w_attn_o[0], v_w_attn_o[0], name="w_attn_o_adamw")
    big["w_glu"] = _sum_adamw(x_w_glu, w_glu[0], m_w_glu[0], v_w_glu[0], name="w_glu_adamw")
    big["w_in"] = _padded_adamw(x_w_in, w_in[0], m_w_in[0], v_w_in[0], name="w_in_adamw")

    rep_names = ["g_mix", "b_f", "lam_re", "lam_im", "log_dt", "b_re", "b_im", "c_re", "c_im", "d_skip", "g_ffn",
                 "conv_b", "g_final"]
    rep_w = [g_mix, b_f, lam_re, lam_im, log_dt, b_re, b_im, c_re, c_im, d_skip, g_ffn, conv_b, g_final]
    rep_m = [m_g_mix, m_b_f, m_lam_re, m_lam_im, m_log_dt, m_b_re, m_b_im, m_c_re, m_c_im, m_d_skip, m_g_ffn,
             m_conv_b, m_g_final]
    rep_v = [v_g_mix, v_b_f, v_lam_re, v_lam_im, v_log_dt, v_b_re, v_b_im, v_c_re, v_c_im, v_d_skip, v_g_ffn,
             v_conv_b, v_g_final]
    rep_g = [dg_mix, db_f[:, 0], dlam_re, dlam_im, dlog_dt, db_re, db_im, dc_re, dc_im, dd_skip, dg_ffn,
             conv_sums[CONV_WIDTH], dg_final]
    rep = _update_replicated(rep_g, rep_w, rep_m, rep_v, name="replicated")
    rep_out = {n: [rep[k][i] for k in range(4)] for i, n in enumerate(rep_names)}

    order = ["meta", "g_mix", "w_in", "b_f", "lam_re", "lam_im", "log_dt", "b_re", "b_im", "c_re", "c_im", "d_skip",
             "w_glu", "w_attn_o", "w_out", "g_ffn", "w_up", "conv_w", "conv_b", "w_down", "g_final"]
    outs = [loss, grad_x]
    for kind in range(4):
        for n in order:
            if n in big:
                outs.append(big[n][kind] if n == "meta" else big[n][kind][None])
            else:
                outs.append(rep_out[n][kind])
    return tuple(outs)
```
